```python
import math
import jax, jax.numpy as jnp
from jax import lax
import numpy as np

D_MODEL = 1024
BATCH = 1
SEQ = 16384
DEPTH = 2

MLSTM_HEADS = 4
MLSTM_HEAD_DIM = 128
MLSTM_WIDTH = MLSTM_HEADS * MLSTM_HEAD_DIM
MLSTM_CONV = 4
MLSTM_CHUNK = 64
S5_WIDTH = 512
S5_GROUP = 16
S5_GROUPS = S5_WIDTH // S5_GROUP
S5_STATE = 64
S5_DT_MIN = 1e-3
S5_DT_MAX = 1e-1
S5_MAX_REAL = -1e-4
HGRN_HEADS = 4
HGRN_KEY_DIM = 128
HGRN_VAL_DIM = 128
HGRN_KEY_WIDTH = HGRN_HEADS * HGRN_KEY_DIM
HGRN_VAL_WIDTH = HGRN_HEADS * HGRN_VAL_DIM
HGRN_CHUNK = 64
N_BRANCHES = 3
IN_WIDTH = 2 * MLSTM_WIDTH + 2 * MLSTM_HEADS + S5_WIDTH + 2 * HGRN_KEY_WIDTH + 2 * HGRN_VAL_WIDTH + N_BRANCHES * D_MODEL
N_EXPERTS = 32
N_GROUPS = 8
EXPERTS_PER_GROUP = N_EXPERTS // N_GROUPS
GROUP_SCORE_K = 2
TOP_K = 2
D_EXPERT = 256
DN_ALPHA = (2 * DEPTH) ** 0.25
DN_BETA = (8 * DEPTH) ** -0.25
LN_EPS = 1e-5
NORM_EPS = 1e-6

kernel_name = "hybrid_mlstm_s5_hgrn2_gmoe"


def layer_norm(x, g, b):
    xf = x.astype(jnp.float32)
    mu = xf.mean(-1, keepdims=True)
    var = jnp.square(xf - mu).mean(-1, keepdims=True)
    return ((xf - mu) * lax.rsqrt(var + LN_EPS)).astype(x.dtype) * g + b


def split_in(z):
    sizes = (MLSTM_WIDTH, MLSTM_WIDTH, MLSTM_HEADS, MLSTM_HEADS, S5_WIDTH,
             HGRN_KEY_WIDTH, HGRN_KEY_WIDTH, HGRN_VAL_WIDTH, HGRN_VAL_WIDTH, N_BRANCHES * D_MODEL)
    parts, off = [], 0
    for s in sizes:
        parts.append(z[..., off:off + s])
        off += s
    return parts


def causal_depthwise_conv(x, w, b):
    k = w.shape[0]
    y = lax.conv_general_dilated(x, w[:, None, :], window_strides=(1,), padding=[(k - 1, 0)],
                                 dimension_numbers=('NWC', 'WIO', 'NWC'), feature_group_count=x.shape[-1])
    return y + b


def mlstm_chunkwise(q, k, v, log_i, log_f):
    bsz, nh, seq, dh = q.shape
    L = MLSTM_CHUNK
    nc = seq // L
    f32 = jnp.float32
    q = (q.astype(f32) * (dh ** -0.5)).reshape(bsz, nh, nc, L, dh)
    k = k.astype(f32).reshape(bsz, nh, nc, L, dh)
    v = v.astype(f32).reshape(bsz, nh, nc, L, dh)
    li = log_i.reshape(bsz, nh, nc, L)
    b = jnp.cumsum(log_f.reshape(bsz, nh, nc, L), axis=-1)
    b_tot = b[..., -1]
    a = b_tot[..., None] - b + li
    a_max = a.max(-1)
    w = jnp.exp(a - a_max[..., None])
    c_loc = jnp.einsum('bhcl,bhcld,bhcle->bhcde', w, k, v)
    n_loc = jnp.einsum('bhcl,bhcld->bhcd', w, k)

    def step(carry, inp):
        c_st, n_st, m_st = carry
        c_l, n_l, am, bt = inp
        m_new = jnp.maximum(bt + m_st, am)
        g_old = jnp.exp(bt + m_st - m_new)
        g_loc = jnp.exp(am - m_new)
        c_new = g_old[..., None, None] * c_st + g_loc[..., None, None] * c_l
        n_new = g_old[..., None] * n_st + g_loc[..., None] * n_l
        return (c_new, n_new, m_new), (c_st, n_st, m_st)

    init = (jnp.zeros((bsz, nh, dh, dh), f32), jnp.zeros((bsz, nh, dh), f32), jnp.zeros((bsz, nh), f32))
    xs = (jnp.moveaxis(c_loc, 2, 0), jnp.moveaxis(n_loc, 2, 0), jnp.moveaxis(a_max, 2, 0), jnp.moveaxis(b_tot, 2, 0))
    _, (c_prev, n_prev, m_prev) = lax.scan(step, init, xs)
    c_prev = jnp.moveaxis(c_prev, 0, 2)
    n_prev = jnp.moveaxis(n_prev, 0, 2)
    m_prev = jnp.moveaxis(m_prev, 0, 2)

    causal = jnp.tril(jnp.ones((L, L), bool))
    d_mat = jnp.where(causal, b[..., :, None] - b[..., None, :] + li[..., None, :], -jnp.inf)
    m_inter = b + m_prev[..., None]
    m_j = jnp.maximum(m_inter, d_mat.max(-1))
    scores = jnp.einsum('bhcjd,bhcsd->bhcjs', q, k) * jnp.exp(d_mat - m_j[..., None])
    g_inter = jnp.exp(m_inter - m_j)
    num = g_inter[..., None] * jnp.einsum('bhcjd,bhcde->bhcje', q, c_prev) + jnp.einsum('bhcjs,bhcse->bhcje', scores, v)
    den = g_inter * jnp.einsum('bhcjd,bhcd->bhcj', q, n_prev) + scores.sum(-1)
    h = num / jnp.maximum(jnp.abs(den), jnp.exp(-m_j))[..., None]
    return h.reshape(bsz, nh, seq, dh)


def mlstm_branch(xm, o_pre, i_pre, f_pre, conv_w, conv_b, wq, wk, wv, b_i, b_f, skip):
    bsz, seq, _ = xm.shape
    xc = jax.nn.silu(causal_depthwise_conv(xm, conv_w, conv_b))
    xc_h = xc.reshape(bsz, seq, MLSTM_HEADS, MLSTM_HEAD_DIM)
    xm_h = xm.reshape(bsz, seq, MLSTM_HEADS, MLSTM_HEAD_DIM)
    q = jnp.einsum('bshd,hde->bhse', xc_h, wq)
    k = jnp.einsum('bshd,hde->bhse', xc_h, wk)
    v = jnp.einsum('bshd,hde->bhse', xm_h, wv)
    log_i = jnp.swapaxes(i_pre + b_i, 1, 2).astype(jnp.float32)
    log_f = jax.nn.log_sigmoid(jnp.swapaxes(f_pre + b_f, 1, 2).astype(jnp.float32))
    h = mlstm_chunkwise(q, k, v, log_i, log_f)
    h = jnp.swapaxes(h, 1, 2).reshape(bsz, seq, MLSTM_WIDTH)
    return (jax.nn.sigmoid(o_pre) * h + skip * xc).astype(xm.dtype)


def s5_branch(u, lam_re, lam_im, log_dt, b_re, b_im, c_re, c_im, d_skip, w_glu):
    bsz, seq, _ = u.shape
    f32 = jnp.float32
    lr = jnp.minimum(lam_re.astype(f32), S5_MAX_REAL)
    li = lam_im.astype(f32)
    dt = jnp.exp(log_dt.astype(f32))[:, None]
    mag = jnp.exp(lr * dt)
    ab_re = mag * jnp.cos(li * dt)
    ab_im = mag * jnp.sin(li * dt)
    nr = ab_re - 1.0
    den = lr * lr + li * li
    cr = (nr * lr + ab_im * li) / den
    ci = (ab_im * lr - nr * li) / den
    bb_re = cr[..., None] * b_re - ci[..., None] * b_im
    bb_im = cr[..., None] * b_im + ci[..., None] * b_re
    ug = u.reshape(bsz, seq, S5_GROUPS, S5_GROUP).astype(f32)
    bu_re = jnp.einsum('blgc,gpc->blgp', ug, bb_re)
    bu_im = jnp.einsum('blgc,gpc->blgp', ug, bb_im)
    a_re = jnp.broadcast_to(ab_re, bu_re.shape)
    a_im = jnp.broadcast_to(ab_im, bu_im.shape)

    def combine(e1, e2):
        a1r, a1i, b1r, b1i = e1
        a2r, a2i, b2r, b2i = e2
        return (a2r * a1r - a2i * a1i, a2r * a1i + a2i * a1r,
                a2r * b1r - a2i * b1i + b2r, a2r * b1i + a2i * b1r + b2i)

    _, _, x_re, x_im = lax.associative_scan(combine, (a_re, a_im, bu_re, bu_im), axis=1)
    y = jnp.einsum('blgp,gcp->blgc', x_re, c_re) - jnp.einsum('blgp,gcp->blgc', x_im, c_im)
    y = jax.nn.gelu(y.reshape(bsz, seq, S5_WIDTH) + d_skip * u)
    return (y * jax.nn.sigmoid(y @ w_glu)).astype(u.dtype)


def hgrn2_chunkwise(q, k, v, log_f):
    bsz, nh, seq, dk = q.shape
    dv = v.shape[-1]
    L = HGRN_CHUNK
    nc = seq // L

    def chunks(t):
        return jnp.moveaxis(t.reshape(bsz, nh, nc, L, t.shape[-1]), 2, 0)

    causal = jnp.tril(jnp.ones((L, L), bool))[:, :, None]

    def step(state, inp):
        qc, kc, vc, lfc = inp
        b = jnp.cumsum(lfc, axis=-2)
        o_inter = jnp.einsum('bhld,bhde->bhle', qc * jnp.exp(b), state)
        rel = jnp.where(causal, b[:, :, :, None, :] - b[:, :, None, :, :], -jnp.inf)
        attn = jnp.einsum('bhjd,bhsd,bhjsd->bhjs', qc, kc, jnp.exp(rel))
        o = o_inter + jnp.einsum('bhjs,bhse->bhje', attn, vc)
        b_last = b[:, :, -1:, :]
        new_state = jnp.exp(b_last[:, :, 0, :])[..., None] * state + jnp.einsum('bhsd,bhse->bhde', kc * jnp.exp(b_last - b), vc)
        return new_state, o

    init = jnp.zeros((bsz, nh, dk, dv), jnp.float32)
    _, o = lax.scan(step, init, (chunks(q), chunks(k), chunks(v), chunks(log_f)))
    return jnp.moveaxis(o, 0, 2).reshape(bsz, nh, seq, dv)


def hgrn2_branch(q_pre, f_pre, i_pre, g_pre, lower_bound, norm_w):
    bsz, seq, _ = q_pre.shape
    f32 = jnp.float32

    def heads(t, dim):
        return jnp.swapaxes(t.reshape(bsz, seq, HGRN_HEADS, dim), 1, 2).astype(f32)

    lb = lower_bound.reshape(HGRN_HEADS, 1, HGRN_KEY_DIM)
    f = lb + (1.0 - lb) * jax.nn.sigmoid(heads(f_pre, HGRN_KEY_DIM))
    q = jax.nn.silu(heads(q_pre, HGRN_KEY_DIM))
    v = heads(i_pre, HGRN_VAL_DIM)
    o = hgrn2_chunkwise(q, 1.0 - f, v, jnp.log(f))
    o = o * lax.rsqrt(jnp.mean(o * o, -1, keepdims=True) + NORM_EPS)
    o = jnp.swapaxes(o, 1, 2).reshape(bsz, seq, HGRN_VAL_WIDTH)
    return (o * norm_w * jax.nn.silu(g_pre)).astype(q_pre.dtype)


def grouped_moe(x, router_w, router_bias, w_gate, w_up, w_down):
    bsz, seq, d = x.shape
    f32 = jnp.float32
    xt = x.reshape(bsz * seq, d)
    s = jax.nn.sigmoid((xt @ router_w).astype(f32))
    sb = s + router_bias
    t = sb.shape[0]
    grp = sb.reshape(t, N_GROUPS, EXPERTS_PER_GROUP)
    g_score = lax.top_k(grp, GROUP_SCORE_K)[0].sum(-1)
    g_idx = jnp.argmax(g_score, axis=-1)
    in_grp = jax.nn.one_hot(g_idx, N_GROUPS, dtype=f32)[:, :, None] > 0
    masked = jnp.where(in_grp, grp, -jnp.inf).reshape(t, N_EXPERTS)
    _, e_idx = lax.top_k(masked, TOP_K)
    s_sel = jnp.take_along_axis(s, e_idx, axis=1)
    gates = s_sel / s_sel.sum(-1, keepdims=True)
    combine = jnp.sum(jax.nn.one_hot(e_idx, N_EXPERTS, dtype=f32) * gates[..., None], axis=1)
    y = jnp.zeros((t, d), f32)
    for g in range(N_GROUPS):
        sl = slice(g * EXPERTS_PER_GROUP, (g + 1) * EXPERTS_PER_GROUP)
        h = jax.nn.silu(jnp.einsum('td,edf->tef', xt, w_gate[sl])) * jnp.einsum('td,edf->tef', xt, w_up[sl])
        y = y + jnp.einsum('tef,efd->td', h * combine[:, sl, None], w_down[sl])
    return y.astype(x.dtype).reshape(bsz, seq, d)


def setup_inputs(seed: int = 0) -> dict:
    key = jax.random.key(seed)
    ks = jax.random.split(key, 40)
    f32 = jnp.float32
    L = DEPTH
    H, Dh = MLSTM_HEADS, MLSTM_HEAD_DIM
    G, P, GC = S5_GROUPS, S5_STATE, S5_GROUP

    def nrm(i, shape, scale):
        return jax.random.normal(ks[i], shape, f32) * scale

    return {
        'x': nrm(0, (BATCH, SEQ, D_MODEL), 1.0),
        'w_in': nrm(1, (L, D_MODEL, IN_WIDTH), D_MODEL ** -0.5),
        'mlstm_conv_w': nrm(2, (L, MLSTM_CONV, MLSTM_WIDTH), MLSTM_CONV ** -0.5),
        'mlstm_conv_b': nrm(3, (L, MLSTM_WIDTH), 0.01),
        'mlstm_wq': nrm(4, (L, H, Dh, Dh), Dh ** -0.5),
        'mlstm_wk': nrm(5, (L, H, Dh, Dh), Dh ** -0.5),
        'mlstm_wv': nrm(6, (L, H, Dh, Dh), Dh ** -0.5),
        'mlstm_b_i': nrm(7, (L, H), 0.1),
        'mlstm_b_f': jnp.linspace(3.0, 6.0, H, dtype=f32)[None, :] + nrm(8, (L, H), 0.01),
        'mlstm_skip': 1.0 + nrm(9, (L, MLSTM_WIDTH), 0.01),
        's5_lambda_re': -0.5 * jnp.exp(nrm(10, (L, G, P), 0.02)),
        's5_lambda_im': math.pi * jnp.arange(P, dtype=f32) + nrm(11, (L, G, P), 0.01),
        's5_log_dt': jax.random.uniform(ks[12], (L, G), f32, math.log(S5_DT_MIN), math.log(S5_DT_MAX)),
        's5_b_re': nrm(13, (L, G, P, GC), (2 * GC) ** -0.5),
        's5_b_im': nrm(14, (L, G, P, GC), (2 * GC) ** -0.5),
        's5_c_re': nrm(15, (L, G, GC, P), 0.5),
        's5_c_im': nrm(16, (L, G, GC, P), 0.5),
        's5_d': nrm(17, (L, S5_WIDTH), 1.0),
        's5_w_glu': nrm(18, (L, S5_WIDTH, S5_WIDTH), S5_WIDTH ** -0.5),
        'hgrn_lower_bounds': nrm(19, (L, HGRN_KEY_WIDTH), 0.1),
        'hgrn_norm_w': 1.0 + nrm(20, (L, HGRN_VAL_WIDTH), 0.01),
        'w_branch_mlstm': nrm(21, (L, MLSTM_WIDTH, D_MODEL), MLSTM_WIDTH ** -0.5 * DN_BETA),
        'w_branch_s5': nrm(22, (L, S5_WIDTH, D_MODEL), S5_WIDTH ** -0.5 * DN_BETA),
        'w_branch_hgrn': nrm(23, (L, HGRN_VAL_WIDTH, D_MODEL), HGRN_VAL_WIDTH ** -0.5 * DN_BETA),
        'w_out': nrm(24, (L, D_MODEL, D_MODEL), D_MODEL ** -0.5 * DN_BETA),
        'ln1_g': 1.0 + nrm(25, (L, D_MODEL), 0.01),
        'ln1_b': nrm(26, (L, D_MODEL), 0.01),
        'ln2_g': 1.0 + nrm(27, (L, D_MODEL), 0.01),
        'ln2_b': nrm(28, (L, D_MODEL), 0.01),
        'router_w': nrm(29, (D_MODEL, N_EXPERTS), D_MODEL ** -0.5),
        'router_bias': nrm(30, (N_EXPERTS,), 0.01),
        'exp_w_gate': nrm(31, (L, N_EXPERTS, D_MODEL, D_EXPERT), D_MODEL ** -0.5),
        'exp_w_up': nrm(32, (L, N_EXPERTS, D_MODEL, D_EXPERT), D_MODEL ** -0.5),
        'exp_w_down': nrm(33, (L, N_EXPERTS, D_EXPERT, D_MODEL), D_EXPERT ** -0.5 * DN_BETA),
    }


def reference(x, w_in, mlstm_conv_w, mlstm_conv_b, mlstm_wq, mlstm_wk, mlstm_wv, mlstm_b_i, mlstm_b_f,
              mlstm_skip, s5_lambda_re, s5_lambda_im, s5_log_dt, s5_b_re, s5_b_im, s5_c_re, s5_c_im, s5_d,
              s5_w_glu, hgrn_lower_bounds, hgrn_norm_w, w_branch_mlstm, w_branch_s5, w_branch_hgrn, w_out,
              ln1_g, ln1_b, ln2_g, ln2_b, router_w, router_bias, exp_w_gate, exp_w_up, exp_w_down):
    lb_cum = jnp.cumsum(jax.nn.softmax(hgrn_lower_bounds.astype(jnp.float32), axis=0), axis=0)
    lb_layers = lb_cum - lb_cum[0]
    for l in range(DEPTH):
        z = x @ w_in[l]
        xm, o_m, i_m, f_m, u_s, q_h, f_h, i_h, g_h, gate_pre = split_in(z)
        y_m = mlstm_branch(xm, o_m, i_m, f_m, mlstm_conv_w[l], mlstm_conv_b[l], mlstm_wq[l], mlstm_wk[l],
                           mlstm_wv[l], mlstm_b_i[l], mlstm_b_f[l], mlstm_skip[l])
        y_s = s5_branch(u_s, s5_lambda_re[l], s5_lambda_im[l], s5_log_dt[l], s5_b_re[l], s5_b_im[l],
                        s5_c_re[l], s5_c_im[l], s5_d[l], s5_w_glu[l])
        y_h = hgrn2_branch(q_h, f_h, i_h, g_h, lb_layers[l], hgrn_norm_w[l])
        gates = jax.nn.sigmoid(gate_pre.reshape(gate_pre.shape[:-1] + (N_BRANCHES, D_MODEL)))
        merged = (gates[..., 0, :] * (y_m @ w_branch_mlstm[l])
                  + gates[..., 1, :] * (y_s @ w_branch_s5[l])
                  + gates[..., 2, :] * (y_h @ w_branch_hgrn[l]))
        x = layer_norm(DN_ALPHA * x + (merged @ w_out[l]).astype(x.dtype), ln1_g[l], ln1_b[l])
        ffn = grouped_moe(x, router_w, router_bias, exp_w_gate[l], exp_w_up[l], exp_w_down[l])
        x = layer_norm(DN_ALPHA * x + ffn, ln2_g[l], ln2_b[l])
    return x
```

```python
import functools
import math

import jax
import jax.numpy as jnp
from jax import lax
from jax.experimental import pallas as pl
from jax.experimental.pallas import tpu as pltpu

F32 = jnp.float32
BF16 = jnp.bfloat16
HIGHEST = lax.Precision.HIGHEST

D_MODEL = 1024
DEPTH = 2
HEADS = 4
HEAD_DIM = 128
WIDTH = HEADS * HEAD_DIM
MLSTM_CONV = 4
CHUNK = 64
SUB = 16
S5_GROUP = 16
S5_GROUPS = 32
S5_STATE = 64
S5_PAIRS = S5_GROUPS // 2
S5_CHUNK = 32
S5_MAX_REAL = -1e-4
N_EXPERTS = 32
N_GROUPS = 8
EXPERTS_PER_GROUP = 4
D_EXPERT = 256
DN_ALPHA = (2 * DEPTH) ** 0.25
LN_EPS = 1e-5
NORM_EPS = 1e-6

LANES = 128
SUBLANES = 8
META = LANES
VMEM_LIMIT = 56 * 1024 * 1024

Z1_DTYPE = F32
Z1_GATE, Z1_XM, Z1_OM, Z1_US, Z1_QH, Z1_IH, Z1_GH = 0, 6, 7, 8, 9, 10, 11
Z1_COLS = 12 * WIDTH
Z2_COLS = WIDTH + LANES


def _cparams(sem):
    return pltpu.CompilerParams(dimension_semantics=sem, vmem_limit_bytes=VMEM_LIMIT)


def _sigmoid(x):
    return 1.0 / (1.0 + jnp.exp(-x))


def _silu(x):
    return x * _sigmoid(x)


def _log_sigmoid(x):
    return jnp.minimum(x, 0.0) - jnp.log(1.0 + jnp.exp(-jnp.abs(x)))


def _gelu_tanh(x):
    return 0.5 * x * (1.0 + jnp.tanh(math.sqrt(2.0 / math.pi) * (x + 0.044715 * (x * x * x))))


def _cumsum_rows(x):
    n = x.shape[0]
    row = lax.broadcasted_iota(jnp.int32, x.shape, 0)
    s = 1
    while s < n:
        x = x + jnp.where(row >= s, pltpu.roll(x, s, 0), 0.0)
        s *= 2
    return x


def _lane_col(x, idx):
    lane = lax.broadcasted_iota(jnp.int32, x.shape, 1)
    return jnp.sum(jnp.where(lane == idx, x, 0.0), axis=1, keepdims=True)


def _dot(a, b):
    return jnp.dot(a.astype(BF16), b.astype(BF16), preferred_element_type=F32)


def _dot_nt(a, b):
    return lax.dot_general(a.astype(BF16), b.astype(BF16), (((1,), (1,)), ((), ())), preferred_element_type=F32)


def _dot_tn(a, b):
    return lax.dot_general(a.astype(BF16), b.astype(BF16), (((0,), (0,)), ((), ())), preferred_element_type=F32)


def _mm_kernel(x_ref, w_ref, o_ref):
    o_ref[...] = jnp.dot(x_ref[...], w_ref[...], preferred_element_type=F32).astype(o_ref.dtype)


def _matmul(x, w, out_dtype, tm, tn):
    m, k = x.shape
    n = w.shape[1]
    return pl.pallas_call(
        _mm_kernel,
        grid=(m // tm, n // tn),
        in_specs=[pl.BlockSpec((tm, k), lambda i, j: (i, 0)), pl.BlockSpec((k, tn), lambda i, j: (0, j))],
        out_specs=pl.BlockSpec((tm, tn), lambda i, j: (i, j)),
        out_shape=jax.ShapeDtypeStruct((m, n), out_dtype),
        compiler_params=_cparams(("parallel", "parallel")),
        name="in_proj_wide",
    )(x, w)


def _inproj_gate_kernel(x_ref, w_ref, wt_ref, o_ref, gt_ref):
    x = x_ref[...]
    o_ref[...] = jnp.dot(x, w_ref[...], preferred_element_type=F32)
    gt_ref[...] = lax.dot_general(wt_ref[...], x, (((1,), (1,)), ((), ())), preferred_element_type=F32)


def _inproj_gate(x, w2, wt, tm):
    m, k = x.shape
    return pl.pallas_call(
        _inproj_gate_kernel,
        grid=(m // tm,),
        in_specs=[pl.BlockSpec((tm, k), lambda i: (i, 0)),
                  pl.BlockSpec((k, Z2_COLS), lambda i: (0, 0)),
                  pl.BlockSpec((SUBLANES, k), lambda i: (0, 0))],
        out_specs=[pl.BlockSpec((tm, Z2_COLS), lambda i: (i, 0)), pl.BlockSpec((SUBLANES, tm), lambda i: (0, i))],
        out_shape=[jax.ShapeDtypeStruct((m, Z2_COLS), F32), jax.ShapeDtypeStruct((SUBLANES, m), F32)],
        compiler_params=_cparams(("parallel",)),
        name="in_proj_gates",
    )(x, w2, wt)


def _split_w_in(w):
    offs, o = [], 0
    for s in (WIDTH, WIDTH, HEADS, HEADS, WIDTH, WIDTH, WIDTH, WIDTH, WIDTH, 3 * D_MODEL):
        offs.append((o, o + s))
        o += s
    seg = [w[:, a:b] for a, b in offs]
    xm, om, im, fm, us, qh, fh, ih, gh, gate = seg
    w1 = jnp.concatenate([gate, xm, om, us, qh, ih, gh], axis=1).astype(BF16)
    pad = jnp.zeros((w.shape[0], LANES - 2 * HEADS), w.dtype)
    w2 = jnp.concatenate([fh, im, fm, pad], axis=1).astype(BF16)
    wt = jnp.concatenate([im, fm], axis=1).T.astype(BF16)
    return w1, w2, wt


def _mlstm_kernel(xm_ref, om_ref, gc_ref, gr_ref, cw_ref, cb_ref, wq_ref, wk_ref, wv_ref, bcol_ref, brow_ref,
                  skip_ref, triu_ref, o_ref, xpad, q_s, k_s, v_s, xc_s, h_s, gcol_s, grow_s, brow_s, ct_s, n_s, m_s,
                  *, tb):
    ncb = tb // CHUNK

    @pl.when(pl.program_id(0) == 0)
    def _():
        xpad[0:SUBLANES, :] = jnp.zeros((SUBLANES, WIDTH), F32)
        ct_s[...] = jnp.zeros_like(ct_s)
        n_s[...] = jnp.zeros_like(n_s)
        m_s[...] = jnp.zeros_like(m_s)

    xm = xm_ref[...].astype(F32)
    xpad[SUBLANES:SUBLANES + tb, :] = xm
    cw = cw_ref[...]
    conv = cb_ref[...] + cw[3:4, :] * xm
    for d in range(1, MLSTM_CONV):
        conv = conv + cw[3 - d:4 - d, :] * xpad[SUBLANES - d:SUBLANES - d + tb, :]
    xpad[0:SUBLANES, :] = xpad[tb:tb + SUBLANES, :]
    xc = _silu(conv)
    xc_s[...] = xc
    for h in range(HEADS):
        sl = slice(h * HEAD_DIM, (h + 1) * HEAD_DIM)
        xch = xc[:, sl].astype(BF16)
        q_s[:, sl] = jnp.dot(xch, wq_ref[h], preferred_element_type=F32) * (HEAD_DIM ** -0.5)
        k_s[:, sl] = jnp.dot(xch, wk_ref[h], preferred_element_type=F32)
        v_s[:, sl] = jnp.dot(xm[:, sl].astype(BF16), wv_ref[h], preferred_element_type=F32)

    gc = gc_ref[...] + bcol_ref[...]
    lane = lax.broadcasted_iota(jnp.int32, gc.shape, 1)
    gcol_s[...] = jnp.where(lane < HEADS, gc, _log_sigmoid(gc))
    gr = gr_ref[...] + brow_ref[...]
    sub = lax.broadcasted_iota(jnp.int32, gr.shape, 1)
    gr = jnp.where(sub < HEADS, gr, _log_sigmoid(gr)).reshape(ncb * SUBLANES, CHUNK)
    grow_s[...] = gr
    brow_s[...] = jnp.dot(gr, triu_ref[...], precision=HIGHEST, preferred_element_type=F32)

    rowi = lax.broadcasted_iota(jnp.int32, (CHUNK, CHUNK), 0)
    coli = lax.broadcasted_iota(jnp.int32, (CHUNK, CHUNK), 1)
    causal = rowi >= coli

    def chunk_body(c, carry):
        r0 = pl.multiple_of(c * CHUNK, CHUNK)
        g0 = pl.multiple_of(c * SUBLANES, SUBLANES)
        gcol = gcol_s[pl.ds(r0, CHUNK), :]
        bcol = _cumsum_rows(gcol)
        grow = grow_s[pl.ds(g0, SUBLANES), :]
        brow = brow_s[pl.ds(g0, SUBLANES), :]
        for h in range(HEADS):
            sl = slice(h * HEAD_DIM, (h + 1) * HEAD_DIM)
            li_c = _lane_col(gcol, h)
            b_c = _lane_col(bcol, HEADS + h)
            li_r = grow[h:h + 1, :]
            b_r = brow[HEADS + h:HEADS + h + 1, :]
            b_tot = b_c[CHUNK - 1:CHUNK, :]
            a_c = b_tot - b_c + li_c
            a_max = jnp.max(a_c, axis=0, keepdims=True)
            q = q_s[pl.ds(r0, CHUNK), sl]
            k = k_s[pl.ds(r0, CHUNK), sl]
            v = v_s[pl.ds(r0, CHUNK), sl]
            wk = jnp.exp(a_c - a_max) * k
            c_loc_t = _dot_tn(v, wk)
            n_loc = jnp.sum(wk, axis=0, keepdims=True)
            ct_prev = ct_s[h]
            n_prev = n_s[h:h + 1, :]
            m_prev = m_s[h:h + 1, 0:1]
            d_mat = jnp.where(causal, b_c - b_r + li_r, -jnp.inf)
            m_inter = b_c + m_prev
            m_j = jnp.maximum(m_inter, jnp.max(d_mat, axis=1, keepdims=True))
            sc = _dot_nt(q, k) * jnp.exp(d_mat - m_j)
            g_inter = jnp.exp(m_inter - m_j)
            num = g_inter * _dot_nt(q, ct_prev) + _dot(sc, v)
            den = g_inter * jnp.sum(q * n_prev, axis=1, keepdims=True) + jnp.sum(sc, axis=1, keepdims=True)
            h_s[pl.ds(r0, CHUNK), sl] = num / jnp.maximum(jnp.abs(den), jnp.exp(-m_j))
            m_new = jnp.maximum(b_tot + m_prev, a_max)
            g_old = jnp.exp(b_tot + m_prev - m_new)
            g_loc = jnp.exp(a_max - m_new)
            ct_s[h] = g_old * ct_prev + g_loc * c_loc_t
            n_s[h:h + 1, :] = g_old * n_prev + g_loc * n_loc
            m_s[h:h + 1, :] = jnp.broadcast_to(m_new, (1, LANES))
        return carry

    lax.fori_loop(0, ncb, chunk_body, 0)
    o_ref[...] = (_sigmoid(om_ref[...].astype(F32)) * h_s[...] + skip_ref[...] * xc_s[...]).astype(o_ref.dtype)


def _mlstm(z1, z2, gt3, conv_w, conv_b, wq, wk, wv, b_i, b_f, skip, tb):
    t = z1.shape[0]
    ncb = tb // CHUNK
    bias = jnp.concatenate([b_i, b_f]).astype(F32)
    bcol = jnp.zeros((1, LANES), F32).at[0, :2 * HEADS].set(bias)
    brow = bias.reshape(2 * HEADS, 1)
    triu = jnp.triu(jnp.ones((CHUNK, CHUNK), F32))
    full = lambda shape: pl.BlockSpec(shape, lambda i: (0,) * len(shape))
    return pl.pallas_call(
        functools.partial(_mlstm_kernel, tb=tb),
        grid=(t // tb,),
        in_specs=[pl.BlockSpec((tb, WIDTH), lambda i: (i, Z1_XM)),
                  pl.BlockSpec((tb, WIDTH), lambda i: (i, Z1_OM)),
                  pl.BlockSpec((tb, LANES), lambda i: (i, WIDTH // LANES)),
                  pl.BlockSpec((ncb, SUBLANES, CHUNK), lambda i: (i, 0, 0)),
                  full((MLSTM_CONV, WIDTH)), full((1, WIDTH)),
                  full((HEADS, HEAD_DIM, HEAD_DIM)), full((HEADS, HEAD_DIM, HEAD_DIM)),
                  full((HEADS, HEAD_DIM, HEAD_DIM)),
                  full((1, LANES)), full((2 * HEADS, 1)), full((1, WIDTH)), full((CHUNK, CHUNK))],
        out_specs=pl.BlockSpec((tb, WIDTH), lambda i: (i, 0)),
        out_shape=jax.ShapeDtypeStruct((t, WIDTH), F32),
        scratch_shapes=[pltpu.VMEM((tb + SUBLANES, WIDTH), F32)]
        + [pltpu.VMEM((tb, WIDTH), F32) for _ in range(5)]
        + [pltpu.VMEM((tb, LANES), F32),
           pltpu.VMEM((ncb * SUBLANES, CHUNK), F32), pltpu.VMEM((ncb * SUBLANES, CHUNK), F32),
           pltpu.VMEM((HEADS, HEAD_DIM, HEAD_DIM), F32), pltpu.VMEM((SUBLANES, LANES), F32),
           pltpu.VMEM((SUBLANES, LANES), F32)],
        compiler_params=_cparams(("arbitrary",)),
        name="mlstm",
    )(z1, z1, z2, gt3, conv_w.astype(F32), conv_b.reshape(1, WIDTH).astype(F32), wq.astype(BF16), wk.astype(BF16),
      wv.astype(BF16), bcol, brow, skip.reshape(1, WIDTH).astype(F32), triu)


def _hgrn_kernel(q_ref, f_ref, i_ref, g_ref, lb_ref, nw_ref, ones_ref, o_ref, st_s, *, tb):
    ncb = tb // CHUNK

    @pl.when(pl.program_id(0) == 0)
    def _():
        st_s[...] = jnp.zeros_like(st_s)

    lb = lb_ref[...]
    nw = nw_ref[...]
    ones_bd = ones_ref[...]
    rowm = lax.broadcasted_iota(jnp.int32, (CHUNK, WIDTH), 0) % SUB

    def chunk_body(c, carry):
        r0 = pl.multiple_of(c * CHUNK, CHUNK)
        f = lb + (1.0 - lb) * _sigmoid(f_ref[pl.ds(r0, CHUNK), :].astype(F32))
        k = 1.0 - f
        q = _silu(q_ref[pl.ds(r0, CHUNK), :].astype(F32))
        v = i_ref[pl.ds(r0, CHUNK), :].astype(F32)
        b = _cumsum_rows(jnp.log(f))
        o = jnp.dot((q * k).astype(BF16), ones_bd, preferred_element_type=F32) * v
        for d in range(1, SUB):
            term = jnp.where(rowm >= d, q * pltpu.roll(k, d, 0) * jnp.exp(b - pltpu.roll(b, d, 0)), 0.0)
            o = o + jnp.dot(term.astype(BF16), ones_bd, preferred_element_type=F32) * pltpu.roll(v, d, 0)
        b_last = b[CHUNK - 1:CHUNK, :]
        qe = q * jnp.exp(b)
        kd = k * jnp.exp(b_last - b)
        e_last = jnp.exp(b_last)
        outs = []
        for h in range(HEADS):
            sl = slice(h * HEAD_DIM, (h + 1) * HEAD_DIM)
            st = st_s[h]
            oh = o[:, sl] + _dot_nt(qe[:, sl], st)
            parts = [oh[0:SUB, :]]
            for j in range(1, CHUNK // SUB):
                lo = j * SUB
                r = b[lo - 1:lo, sl]
                qj = q[lo:lo + SUB, sl] * jnp.exp(b[lo:lo + SUB, sl] - r)
                kj = k[0:lo, sl] * jnp.exp(r - b[0:lo, sl])
                parts.append(oh[lo:lo + SUB, :] + _dot(_dot_nt(qj, kj), v[0:lo, sl]))
            oh = jnp.concatenate(parts, axis=0)
            st_s[h] = e_last[:, sl] * st + _dot_tn(v[:, sl], kd[:, sl])
            ms = jnp.sum(oh * oh, axis=1, keepdims=True) * (1.0 / HEAD_DIM)
            outs.append(oh * lax.rsqrt(ms + NORM_EPS))
        on = jnp.concatenate(outs, axis=1)
        g = g_ref[pl.ds(r0, CHUNK), :].astype(F32)
        o_ref[pl.ds(r0, CHUNK), :] = (on * nw * _silu(g)).astype(o_ref.dtype)
        return carry

    lax.fori_loop(0, ncb, chunk_body, 0)


def _hgrn(z1, z2, lb, norm_w, tb):
    t = z1.shape[0]
    ones_bd = jnp.kron(jnp.eye(HEADS, dtype=F32), jnp.ones((HEAD_DIM, HEAD_DIM), F32)).astype(BF16)
    full = lambda shape: pl.BlockSpec(shape, lambda i: (0,) * len(shape))
    return pl.pallas_call(
        functools.partial(_hgrn_kernel, tb=tb),
        grid=(t // tb,),
        in_specs=[pl.BlockSpec((tb, WIDTH), lambda i: (i, Z1_QH)),
                  pl.BlockSpec((tb, WIDTH), lambda i: (i, 0)),
                  pl.BlockSpec((tb, WIDTH), lambda i: (i, Z1_IH)),
                  pl.BlockSpec((tb, WIDTH), lambda i: (i, Z1_GH)),
                  full((1, WIDTH)), full((1, WIDTH)), full((WIDTH, WIDTH))],
        out_specs=pl.BlockSpec((tb, WIDTH), lambda i: (i, 0)),
        out_shape=jax.ShapeDtypeStruct((t, WIDTH), F32),
        scratch_shapes=[pltpu.VMEM((HEADS, HEAD_DIM, HEAD_DIM), F32)],
        compiler_params=_cparams(("arbitrary",)),
        name="hgrn2",
    )(z1, z2, z1, z1, lb.reshape(1, WIDTH).astype(F32), norm_w.reshape(1, WIDTH).astype(F32), ones_bd)


def _s5_tables(lam_re, lam_im, log_dt, b_re, b_im, c_re, c_im):
    ln = S5_CHUNK
    lr = jnp.minimum(lam_re.astype(F32), S5_MAX_REAL)
    li = lam_im.astype(F32)
    dt = jnp.exp(log_dt.astype(F32))[:, None]
    mag = jnp.exp(lr * dt)
    ab_re = mag * jnp.cos(li * dt)
    ab_im = mag * jnp.sin(li * dt)
    nr = ab_re - 1.0
    den = lr * lr + li * li
    cr = (nr * lr + ab_im * li) / den
    ci = (ab_im * lr - nr * li) / den
    bb_re = cr[..., None] * b_re - ci[..., None] * b_im
    bb_im = cr[..., None] * b_im + ci[..., None] * b_re
    tau = jnp.arange(ln + 1, dtype=F32)[:, None, None]
    pm = jnp.exp(lr * dt * tau)
    pr = pm * jnp.cos(li * dt * tau)
    pi = pm * jnp.sin(li * dt * tau)
    w1 = c_re[:, :, :, None] * bb_re[:, None, :, :] - c_im[:, :, :, None] * bb_im[:, None, :, :]
    w2 = c_re[:, :, :, None] * bb_im[:, None, :, :] + c_im[:, :, :, None] * bb_re[:, None, :, :]
    kk = (jnp.einsum('gopc,tgp->gtco', w1, pr[:ln], precision=HIGHEST)
          - jnp.einsum('gopc,tgp->gtco', w2, pi[:ln], precision=HIGHEST))
    kz = jnp.concatenate([kk, jnp.zeros_like(kk[:, :1])], axis=1)
    s_i = jnp.arange(ln)[:, None]
    t_i = jnp.arange(ln)[None, :]
    idx = jnp.where(t_i >= s_i, t_i - s_i, ln)
    m = kz[:, idx]
    m = m.transpose(0, 1, 3, 2, 4).reshape(S5_GROUPS, ln * S5_GROUP, ln * S5_GROUP)
    pe_r = pr[ln - 1 - jnp.arange(ln)]
    pe_i = pi[ln - 1 - jnp.arange(ln)]
    e_re = pe_r[:, :, :, None] * bb_re[None] - pe_i[:, :, :, None] * bb_im[None]
    e_im = pe_r[:, :, :, None] * bb_im[None] + pe_i[:, :, :, None] * bb_re[None]
    e_re = e_re.transpose(1, 0, 3, 2).reshape(S5_GROUPS, ln * S5_GROUP, S5_STATE)
    e_im = e_im.transpose(1, 0, 3, 2).reshape(S5_GROUPS, ln * S5_GROUP, S5_STATE)
    pf_r = pr[1:].transpose(1, 2, 0)
    pf_i = pi[1:].transpose(1, 2, 0)
    cre = c_re.transpose(0, 2, 1)
    cim = c_im.transpose(0, 2, 1)
    f_re = cre[:, :, None, :] * pf_r[..., None] - cim[:, :, None, :] * pf_i[..., None]
    f_im = -(cre[:, :, None, :] * pf_i[..., None] + cim[:, :, None, :] * pf_r[..., None])
    f_re = f_re.reshape(S5_GROUPS, S5_STATE, ln * S5_GROUP)
    f_im = f_im.reshape(S5_GROUPS, S5_STATE, ln * S5_GROUP)
    rows = ln * S5_GROUP
    zero_e = jnp.zeros((S5_PAIRS, rows, S5_STATE), F32)
    e0r, e1r, e0i, e1i = e_re[0::2], e_re[1::2], e_im[0::2], e_im[1::2]
    e_pair = jnp.concatenate([jnp.concatenate([e0r, zero_e, e0i, zero_e], axis=2),
                              jnp.concatenate([zero_e, e1r, zero_e, e1i], axis=2)], axis=1)
    zero_f = jnp.zeros((S5_PAIRS, S5_STATE, rows), F32)
    fr_pair = jnp.concatenate([jnp.concatenate([f_re[0::2], zero_f], axis=2),
                               jnp.concatenate([zero_f, f_re[1::2]], axis=2)], axis=1)
    fi_pair = jnp.concatenate([jnp.concatenate([f_im[0::2], zero_f], axis=2),
                               jnp.concatenate([zero_f, f_im[1::2]], axis=2)], axis=1)
    al = jnp.zeros((S5_PAIRS, SUBLANES, LANES), F32)
    al = al.at[:, 0, :].set(pr[ln].reshape(S5_PAIRS, 2 * S5_STATE))
    al = al.at[:, 1, :].set(pi[ln].reshape(S5_PAIRS, 2 * S5_STATE))
    return m.astype(BF16), e_pair.astype(BF16), fr_pair.astype(BF16), fi_pair.astype(BF16), al


def _s5_kernel(u_ref, m_ref, e_ref, fr_ref, fi_ref, al_ref, o_ref, sloc, xpr, xpi, *, nc):
    rows = S5_CHUNK * S5_GROUP
    u = u_ref[0]
    sloc[...] = jnp.dot(u, e_ref[0], preferred_element_type=F32)
    ar = al_ref[0, 0:1, :]
    ai = al_ref[0, 1:2, :]

    row = lax.broadcasted_iota(jnp.int32, (SUBLANES, LANES), 0)

    def body(ti, carry):
        xr, xi = carry
        r0 = pl.multiple_of(ti * SUBLANES, SUBLANES)
        sr_t = sloc[pl.ds(r0, SUBLANES), 0:LANES]
        si_t = sloc[pl.ds(r0, SUBLANES), LANES:2 * LANES]
        pr_t = jnp.zeros((SUBLANES, LANES), F32)
        pi_t = jnp.zeros((SUBLANES, LANES), F32)
        for r in range(SUBLANES):
            pr_t = jnp.where(row == r, xr, pr_t)
            pi_t = jnp.where(row == r, xi, pi_t)
            xr, xi = ar * xr - ai * xi + sr_t[r:r + 1, :], ar * xi + ai * xr + si_t[r:r + 1, :]
        xpr[pl.ds(r0, SUBLANES), :] = pr_t
        xpi[pl.ds(r0, SUBLANES), :] = pi_t
        return xr, xi

    zero = jnp.zeros((1, LANES), F32)
    lax.fori_loop(0, nc // SUBLANES, body, (zero, zero))
    y = (jnp.dot(xpr[...].astype(BF16), fr_ref[0], preferred_element_type=F32)
         + jnp.dot(xpi[...].astype(BF16), fi_ref[0], preferred_element_type=F32))
    o_ref[0, :, 0:rows] = y[:, 0:rows] + jnp.dot(u[:, 0:rows], m_ref[0], preferred_element_type=F32)
    o_ref[0, :, rows:2 * rows] = y[:, rows:2 * rows] + jnp.dot(u[:, rows:2 * rows], m_ref[1],
                                                              preferred_element_type=F32)


def _s5(z1, lam_re, lam_im, log_dt, b_re, b_im, c_re, c_im):
    t = z1.shape[0]
    ln = S5_CHUNK
    nc = t // ln
    rows = ln * S5_GROUP
    m, e_pair, fr_pair, fi_pair, al = _s5_tables(lam_re, lam_im, log_dt, b_re, b_im, c_re, c_im)
    u = z1[:, Z1_US * WIDTH:(Z1_US + 1) * WIDTH].astype(BF16)
    u = u.reshape(nc, ln, S5_PAIRS, 2, S5_GROUP).transpose(2, 0, 3, 1, 4).reshape(S5_PAIRS, nc, 2 * rows)
    y = pl.pallas_call(
        functools.partial(_s5_kernel, nc=nc),
        grid=(S5_PAIRS,),
        in_specs=[pl.BlockSpec((1, nc, 2 * rows), lambda j: (j, 0, 0)),
                  pl.BlockSpec((2, rows, rows), lambda j: (j, 0, 0)),
                  pl.BlockSpec((1, 2 * rows, 2 * LANES), lambda j: (j, 0, 0)),
                  pl.BlockSpec((1, LANES, 2 * rows), lambda j: (j, 0, 0)),
                  pl.BlockSpec((1, LANES, 2 * rows), lambda j: (j, 0, 0)),
                  pl.BlockSpec((1, SUBLANES, LANES), lambda j: (j, 0, 0))],
        out_specs=pl.BlockSpec((1, nc, 2 * rows), lambda j: (j, 0, 0)),
        out_shape=jax.ShapeDtypeStruct((S5_PAIRS, nc, 2 * rows), F32),
        scratch_shapes=[pltpu.VMEM((nc, 2 * LANES), F32), pltpu.VMEM((nc, LANES), F32), pltpu.VMEM((nc, LANES), F32)],
        compiler_params=_cparams(("parallel",)),
        name="s5_scan",
    )(u, m, e_pair, fr_pair, fi_pair, al)
    return y.reshape(S5_PAIRS, nc, 2, ln, S5_GROUP).transpose(1, 3, 0, 2, 4).reshape(t, WIDTH)


def _layer_norm(x, g, b):
    mu = jnp.mean(x, axis=-1, keepdims=True)
    xc = x - mu
    var = jnp.mean(xc * xc, axis=-1, keepdims=True)
    return xc * lax.rsqrt(var + LN_EPS) * g + b


def _merge_kernel(x_ref, g0_ref, g1_ref, g2_ref, ym_ref, ys_ref, us_ref, yh_ref, wm_ref, ws_ref, wh_ref, wglu_ref,
                  d_ref, wo_ref, lg_ref, lb_ref, rw_ref, rb_ref, o_ref):
    ys = _gelu_tanh(ys_ref[...] + d_ref[...] * us_ref[...].astype(F32))
    ys = ys * _sigmoid(_dot(ys, wglu_ref[...]))
    merged = (_sigmoid(g0_ref[...].astype(F32)) * _dot(ym_ref[...], wm_ref[...])
              + _sigmoid(g1_ref[...].astype(F32)) * _dot(ys, ws_ref[...])
              + _sigmoid(g2_ref[...].astype(F32)) * _dot(yh_ref[...], wh_ref[...]))
    x1 = _layer_norm(DN_ALPHA * x_ref[...] + _dot(merged, wo_ref[...]), lg_ref[...], lb_ref[...])
    o_ref[:, 0:D_MODEL] = x1

    logits = jnp.dot(x1, rw_ref[...], precision=HIGHEST, preferred_element_type=F32)
    s = [_sigmoid(logits[:, j * LANES:(j + 1) * LANES]) for j in range(EXPERTS_PER_GROUP)]
    sb = [s[j] + rb_ref[:, j * LANES:(j + 1) * LANES] for j in range(EXPERTS_PER_GROUP)]
    hi1, lo1 = jnp.maximum(sb[0], sb[1]), jnp.minimum(sb[0], sb[1])
    hi2, lo2 = jnp.maximum(sb[2], sb[3]), jnp.minimum(sb[2], sb[3])
    top2 = jnp.maximum(hi1, hi2) + jnp.maximum(jnp.minimum(hi1, hi2), jnp.maximum(lo1, lo2))
    lane = lax.broadcasted_iota(jnp.int32, top2.shape, 1)
    top2 = jnp.where(lane < N_GROUPS, top2, -jnp.inf)
    gmax = jnp.max(top2, axis=1, keepdims=True)
    g_idx = jnp.min(jnp.where(top2 == gmax, lane, LANES), axis=1, keepdims=True)
    sel = lane == g_idx
    v = [jnp.sum(jnp.where(sel, sb[j], 0.0), axis=1, keepdims=True) for j in range(EXPERTS_PER_GROUP)]
    sv = [jnp.sum(jnp.where(sel, s[j], 0.0), axis=1, keepdims=True) for j in range(EXPERTS_PER_GROUP)]

    def first_max(vals):
        m = jnp.maximum(jnp.maximum(vals[0], vals[1]), jnp.maximum(vals[2], vals[3]))
        return jnp.where(vals[0] == m, 0, jnp.where(vals[1] == m, 1, jnp.where(vals[2] == m, 2, 3)))

    e1 = first_max(v)
    e2 = first_max([jnp.where(e1 == j, -jnp.inf, v[j]) for j in range(EXPERTS_PER_GROUP)])
    s1 = sum(jnp.where(e1 == j, sv[j], 0.0) for j in range(EXPERTS_PER_GROUP))
    s2 = sum(jnp.where(e2 == j, sv[j], 0.0) for j in range(EXPERTS_PER_GROUP))
    tot = s1 + s2
    meta = jnp.where(lane == 0, g_idx.astype(F32), 0.0)
    for j in range(EXPERTS_PER_GROUP):
        cw = jnp.where(e1 == j, s1 / tot, 0.0) + jnp.where(e2 == j, s2 / tot, 0.0)
        meta = jnp.where(lane == 1 + j, cw, meta)
    o_ref[:, D_MODEL:D_MODEL + META] = meta


def _merge(x, z1, ym, ys, yh, wm, ws, wh, wglu, d, wo, lg, lb, router_w, router_bias, tm):
    t = x.shape[0]
    rw = router_w.astype(F32).reshape(D_MODEL, N_GROUPS, EXPERTS_PER_GROUP).transpose(0, 2, 1)
    rw = jnp.pad(rw, ((0, 0), (0, 0), (0, LANES - N_GROUPS))).reshape(D_MODEL, EXPERTS_PER_GROUP * LANES)
    rb = router_bias.astype(F32).reshape(N_GROUPS, EXPERTS_PER_GROUP).T
    rb = jnp.pad(rb, ((0, 0), (0, LANES - N_GROUPS))).reshape(1, EXPERTS_PER_GROUP * LANES)
    full = lambda shape: pl.BlockSpec(shape, lambda i: (0,) * len(shape))
    row = lambda w: pl.BlockSpec((tm, w), lambda i: (i, 0))
    return pl.pallas_call(
        _merge_kernel,
        grid=(t // tm,),
        in_specs=[row(D_MODEL),
                  pl.BlockSpec((tm, D_MODEL), lambda i: (i, 0)),
                  pl.BlockSpec((tm, D_MODEL), lambda i: (i, 1)),
                  pl.BlockSpec((tm, D_MODEL), lambda i: (i, 2)),
                  row(WIDTH), row(WIDTH), pl.BlockSpec((tm, WIDTH), lambda i: (i, Z1_US)), row(WIDTH),
                  full((WIDTH, D_MODEL)), full((WIDTH, D_MODEL)), full((WIDTH, D_MODEL)), full((WIDTH, WIDTH)),
                  full((1, WIDTH)), full((D_MODEL, D_MODEL)), full((1, D_MODEL)), full((1, D_MODEL)),
                  full((D_MODEL, EXPERTS_PER_GROUP * LANES)), full((1, EXPERTS_PER_GROUP * LANES))],
        out_specs=pl.BlockSpec((tm, D_MODEL + META), lambda i: (i, 0)),
        out_shape=jax.ShapeDtypeStruct((t, D_MODEL + META), F32),
        compiler_params=_cparams(("parallel",)),
        name="merge_router",
    )(x, z1, z1, z1, ym, ys, z1, yh, wm.astype(BF16), ws.astype(BF16), wh.astype(BF16), wglu.astype(BF16),
      d.reshape(1, WIDTH).astype(F32), wo.astype(BF16), lg.reshape(1, D_MODEL).astype(F32),
      lb.reshape(1, D_MODEL).astype(F32), rw, rb)


def _start_row_gather(idx_ref, base, src_hbm, dst, sem, rows):
    def body(r, carry):
        pltpu.make_async_copy(src_hbm.at[pl.ds(idx_ref[base + r], 1)], dst.at[pl.ds(r, 1)], sem).start()
        return carry
    lax.fori_loop(0, rows, body, 0)


def _wait_row_gather(src_hbm, dst, sem, rows):
    pltpu.make_async_copy(src_hbm.at[pl.ds(0, rows)], dst, sem).wait()


def _moe_kernel(tg_ref, src_ref, x_hbm, wg_ref, wu_ref, wd_ref, o_ref, buf, sem, *, tm, ntiles):
    i = pl.program_id(0)
    slot = i % 2

    @pl.when(i == 0)
    def _():
        _start_row_gather(src_ref, 0, x_hbm, buf.at[0], sem.at[0], tm)

    @pl.when(i + 1 < ntiles)
    def _():
        _start_row_gather(src_ref, (i + 1) * tm, x_hbm, buf.at[1 - slot], sem.at[1 - slot], tm)

    _wait_row_gather(x_hbm, buf.at[slot], sem.at[slot], tm)
    xt = buf[slot]
    xb = xt[:, 0:D_MODEL].astype(BF16)
    meta = xt[:, D_MODEL:D_MODEL + META]
    y = jnp.zeros((tm, D_MODEL), F32)
    for e in range(EXPERTS_PER_GROUP):
        hg = jnp.dot(xb, wg_ref[e], preferred_element_type=F32)
        hu = jnp.dot(xb, wu_ref[e], preferred_element_type=F32)
        hh = _silu(hg) * hu * _lane_col(meta, 1 + e)
        y = y + jnp.dot(hh.astype(BF16), wd_ref[e], preferred_element_type=F32)
    o_ref[...] = y


def _ln2_kernel(dst_ref, x_ref, y_hbm, lg_ref, lb_ref, o_ref, buf, sem, *, tm, ntiles):
    i = pl.program_id(0)
    slot = i % 2

    @pl.when(i == 0)
    def _():
        _start_row_gather(dst_ref, 0, y_hbm, buf.at[0], sem.at[0], tm)

    @pl.when(i + 1 < ntiles)
    def _():
        _start_row_gather(dst_ref, (i + 1) * tm, y_hbm, buf.at[1 - slot], sem.at[1 - slot], tm)

    _wait_row_gather(y_hbm, buf.at[slot], sem.at[slot], tm)
    o_ref[...] = _layer_norm(DN_ALPHA * x_ref[...] + buf[slot], lg_ref[...], lb_ref[...])


def _moe_ln(x1e, wg, wu, wd, lg, lb, tm):
    t = x1e.shape[0]
    ntiles = t // tm + N_GROUPS
    p = ntiles * tm
    key = x1e[:, D_MODEL].astype(jnp.int32)
    onehot = (key[:, None] == jnp.arange(N_GROUPS)[None, :]).astype(jnp.int32)
    csum = jnp.cumsum(onehot, axis=0)
    counts = csum[-1]
    rank = jnp.sum(onehot * csum, axis=1) - 1
    pcount = ((counts + tm - 1) // tm) * tm
    pend = jnp.cumsum(pcount)
    dest = (pend - pcount)[key] + rank
    row_src = jnp.zeros((p,), jnp.int32).at[dest].set(jnp.arange(t, dtype=jnp.int32))
    tile_start = jnp.arange(ntiles, dtype=jnp.int32) * tm
    tile_group = jnp.minimum(jnp.sum((tile_start[:, None] >= pend[None, :]).astype(jnp.int32), axis=1),
                             N_GROUPS - 1)
    wspec = lambda a, b: pl.BlockSpec((EXPERTS_PER_GROUP, a, b), lambda i, tg, src: (tg[i], 0, 0))
    y_sorted = pl.pallas_call(
        functools.partial(_moe_kernel, tm=tm, ntiles=ntiles),
        grid_spec=pltpu.PrefetchScalarGridSpec(
            num_scalar_prefetch=2,
            grid=(ntiles,),
            in_specs=[pl.BlockSpec(memory_space=pl.ANY), wspec(D_MODEL, D_EXPERT), wspec(D_MODEL, D_EXPERT),
                      wspec(D_EXPERT, D_MODEL)],
            out_specs=pl.BlockSpec((tm, D_MODEL), lambda i, tg, src: (i, 0)),
            scratch_shapes=[pltpu.VMEM((2, tm, D_MODEL + META), F32), pltpu.SemaphoreType.DMA((2,))]),
        out_shape=jax.ShapeDtypeStruct((p, D_MODEL), F32),
        compiler_params=_cparams(("arbitrary",)),
        name="moe_experts",
    )(tile_group, row_src, x1e, wg.astype(BF16), wu.astype(BF16), wd.astype(BF16))
    nt2 = t // tm
    return pl.pallas_call(
        functools.partial(_ln2_kernel, tm=tm, ntiles=nt2),
        grid_spec=pltpu.PrefetchScalarGridSpec(
            num_scalar_prefetch=1,
            grid=(nt2,),
            in_specs=[pl.BlockSpec((tm, D_MODEL), lambda i, d: (i, 0)), pl.BlockSpec(memory_space=pl.ANY),
                      pl.BlockSpec((1, D_MODEL), lambda i, d: (0, 0)), pl.BlockSpec((1, D_MODEL), lambda i, d: (0, 0))],
            out_specs=pl.BlockSpec((tm, D_MODEL), lambda i, d: (i, 0)),
            scratch_shapes=[pltpu.VMEM((2, tm, D_MODEL), F32), pltpu.SemaphoreType.DMA((2,))]),
        out_shape=jax.ShapeDtypeStruct((t, D_MODEL), F32),
        compiler_params=_cparams(("arbitrary",)),
        name="moe_combine_ln",
    )(dest.astype(jnp.int32), x1e, y_sorted, lg.reshape(1, D_MODEL).astype(F32), lb.reshape(1, D_MODEL).astype(F32))


def _layer(x, l, p, lb_l, cfg):
    t = x.shape[0]
    w1, w2, wt = _split_w_in(p['w_in'][l])
    xb = x.astype(BF16)
    z1 = _matmul(xb, w1, Z1_DTYPE, cfg['tm_in'], cfg['tn_in'])
    z2, gt = _inproj_gate(xb, w2, wt, cfg['tm_in'])
    gt3 = gt.reshape(SUBLANES, t // CHUNK, CHUNK).transpose(1, 0, 2)
    ym = _mlstm(z1, z2, gt3, p['mlstm_conv_w'][l], p['mlstm_conv_b'][l], p['mlstm_wq'][l], p['mlstm_wk'][l],
                p['mlstm_wv'][l], p['mlstm_b_i'][l], p['mlstm_b_f'][l], p['mlstm_skip'][l], cfg['tb'])
    ys = _s5(z1, p['s5_lambda_re'][l], p['s5_lambda_im'][l], p['s5_log_dt'][l], p['s5_b_re'][l], p['s5_b_im'][l],
             p['s5_c_re'][l], p['s5_c_im'][l])
    yh = _hgrn(z1, z2, lb_l, p['hgrn_norm_w'][l], cfg['tb'])
    x1e = _merge(x, z1, ym, ys, yh, p['w_branch_mlstm'][l], p['w_branch_s5'][l], p['w_branch_hgrn'][l],
                 p['s5_w_glu'][l], p['s5_d'][l], p['w_out'][l], p['ln1_g'][l], p['ln1_b'][l], p['router_w'],
                 p['router_bias'], cfg['tm_merge'])
    return _moe_ln(x1e, p['exp_w_gate'][l], p['exp_w_up'][l], p['exp_w_down'][l], p['ln2_g'][l], p['ln2_b'][l],
                   cfg['tm_moe'])


_CFG = dict(tm_in=1024, tn_in=1024, tb=512, tm_merge=256, tm_moe=256)


def kernel(x, w_in, mlstm_conv_w, mlstm_conv_b, mlstm_wq, mlstm_wk, mlstm_wv, mlstm_b_i, mlstm_b_f, mlstm_skip, s5_lambda_re, s5_lambda_im, s5_log_dt, s5_b_re, s5_b_im, s5_c_re, s5_c_im, s5_d, s5_w_glu, hgrn_lower_bounds, hgrn_norm_w, w_branch_mlstm, w_branch_s5, w_branch_hgrn, w_out, ln1_g, ln1_b, ln2_g, ln2_b, router_w, router_bias, exp_w_gate, exp_w_up, exp_w_down):
    p = dict(w_in=w_in, mlstm_conv_w=mlstm_conv_w, mlstm_conv_b=mlstm_conv_b, mlstm_wq=mlstm_wq, mlstm_wk=mlstm_wk,
             mlstm_wv=mlstm_wv, mlstm_b_i=mlstm_b_i, mlstm_b_f=mlstm_b_f, mlstm_skip=mlstm_skip,
             s5_lambda_re=s5_lambda_re, s5_lambda_im=s5_lambda_im, s5_log_dt=s5_log_dt, s5_b_re=s5_b_re,
             s5_b_im=s5_b_im, s5_c_re=s5_c_re, s5_c_im=s5_c_im, s5_d=s5_d, s5_w_glu=s5_w_glu,
             hgrn_norm_w=hgrn_norm_w, w_branch_mlstm=w_branch_mlstm, w_branch_s5=w_branch_s5,
             w_branch_hgrn=w_branch_hgrn, w_out=w_out, ln1_g=ln1_g, ln1_b=ln1_b, ln2_g=ln2_g, ln2_b=ln2_b,
             router_w=router_w, router_bias=router_bias, exp_w_gate=exp_w_gate, exp_w_up=exp_w_up,
             exp_w_down=exp_w_down)
    lb_cum = jnp.cumsum(jax.nn.softmax(hgrn_lower_bounds.astype(F32), axis=0), axis=0)
    lb_layers = lb_cum - lb_cum[0]
    bsz, seq, d = x.shape
    h = x.reshape(bsz * seq, d)
    for l in range(DEPTH):
        h = _layer(h, l, p, lb_layers[l], _CFG)
    return h.reshape(bsz, seq, d)
```

```python
import functools
import math

import jax
import jax.numpy as jnp
from jax import lax
from jax.experimental import pallas as pl
from jax.experimental.pallas import tpu as pltpu

F32 = jnp.float32
BF16 = jnp.bfloat16
HIGHEST = lax.Precision.HIGHEST

D_MODEL = 1024
DEPTH = 2
HEADS = 4
HEAD_DIM = 128
WIDTH = HEADS * HEAD_DIM
MLSTM_CONV = 4
CHUNK = 64
S5_GROUP = 16
S5_GROUPS = 32
S5_STATE = 64
S5_PAIRS = S5_GROUPS // 2
S5_CHUNK = 32
S5_MAX_REAL = -1e-4
N_EXPERTS = 32
N_GROUPS = 8
EXPERTS_PER_GROUP = 4
D_EXPERT = 256
DN_ALPHA = (2 * DEPTH) ** 0.25
LN_EPS = 1e-5
NORM_EPS = 1e-6

LANES = 128
SUBLANES = 8
META = LANES
VMEM_LIMIT = 56 * 1024 * 1024

Z1_DTYPE = F32
Z1_GATE, Z1_XM, Z1_OM, Z1_US, Z1_QH, Z1_IH, Z1_GH = 0, 6, 7, 8, 9, 10, 11
Z1_COLS = 12 * WIDTH
Z2_COLS = WIDTH + LANES


def _cparams(sem):
    return pltpu.CompilerParams(dimension_semantics=sem, vmem_limit_bytes=VMEM_LIMIT)


def _sigmoid(x):
    return 0.5 * (1.0 + jnp.tanh(0.5 * x))


def _silu(x):
    return x * _sigmoid(x)


def _log_sigmoid(x):
    return jnp.minimum(x, 0.0) - jnp.log(1.0 + jnp.exp(-jnp.abs(x)))


def _gelu_tanh(x):
    return 0.5 * x * (1.0 + jnp.tanh(math.sqrt(2.0 / math.pi) * (x + 0.044715 * (x * x * x))))


def _cumsum_rows(x):
    n = x.shape[0]
    row = lax.broadcasted_iota(jnp.int32, x.shape, 0)
    s = 1
    while s < n:
        x = x + jnp.where(row >= s, pltpu.roll(x, s, 0), 0.0)
        s *= 2
    return x


def _lane_col(x, idx):
    lane = lax.broadcasted_iota(jnp.int32, x.shape, 1)
    return jnp.sum(jnp.where(lane == idx, x, 0.0), axis=1, keepdims=True)


def _dot(a, b):
    return jnp.dot(a.astype(BF16), b.astype(BF16), preferred_element_type=F32)


def _dot_nt(a, b):
    return lax.dot_general(a.astype(BF16), b.astype(BF16), (((1,), (1,)), ((), ())), preferred_element_type=F32)


def _dot_tn(a, b):
    return lax.dot_general(a.astype(BF16), b.astype(BF16), (((0,), (0,)), ((), ())), preferred_element_type=F32)


def _mm_kernel(x_ref, w_ref, o_ref):
    o_ref[...] = jnp.dot(x_ref[...], w_ref[...], preferred_element_type=F32).astype(o_ref.dtype)


def _matmul(x, w, out_dtype, tm, tn):
    m, k = x.shape
    n = w.shape[1]
    return pl.pallas_call(
        _mm_kernel,
        grid=(m // tm, n // tn),
        in_specs=[pl.BlockSpec((tm, k), lambda i, j: (i, 0)), pl.BlockSpec((k, tn), lambda i, j: (0, j))],
        out_specs=pl.BlockSpec((tm, tn), lambda i, j: (i, j)),
        out_shape=jax.ShapeDtypeStruct((m, n), out_dtype),
        compiler_params=_cparams(("parallel", "parallel")),
        name="in_proj_wide",
    )(x, w)


def _inproj_gate_kernel(x_ref, w_ref, wt_ref, o_ref, gt_ref):
    x = x_ref[...]
    o_ref[...] = jnp.dot(x, w_ref[...], preferred_element_type=F32)
    gt_ref[...] = lax.dot_general(wt_ref[...], x, (((1,), (1,)), ((), ())), preferred_element_type=F32)


def _inproj_gate(x, w2, wt, tm):
    m, k = x.shape
    return pl.pallas_call(
        _inproj_gate_kernel,
        grid=(m // tm,),
        in_specs=[pl.BlockSpec((tm, k), lambda i: (i, 0)),
                  pl.BlockSpec((k, Z2_COLS), lambda i: (0, 0)),
                  pl.BlockSpec((SUBLANES, k), lambda i: (0, 0))],
        out_specs=[pl.BlockSpec((tm, Z2_COLS), lambda i: (i, 0)), pl.BlockSpec((SUBLANES, tm), lambda i: (0, i))],
        out_shape=[jax.ShapeDtypeStruct((m, Z2_COLS), F32), jax.ShapeDtypeStruct((SUBLANES, m), F32)],
        compiler_params=_cparams(("parallel",)),
        name="in_proj_gates",
    )(x, w2, wt)


def _split_w_in(w):
    offs, o = [], 0
    for s in (WIDTH, WIDTH, HEADS, HEADS, WIDTH, WIDTH, WIDTH, WIDTH, WIDTH, 3 * D_MODEL):
        offs.append((o, o + s))
        o += s
    seg = [w[:, a:b] for a, b in offs]
    xm, om, im, fm, us, qh, fh, ih, gh, gate = seg
    w1 = jnp.concatenate([gate, xm, om, us, qh, ih, gh], axis=1).astype(BF16)
    pad = jnp.zeros((w.shape[0], LANES - 2 * HEADS), w.dtype)
    w2 = jnp.concatenate([fh, im, fm, pad], axis=1).astype(BF16)
    wt = jnp.concatenate([im, fm], axis=1).T.astype(BF16)
    return w1, w2, wt


def _mlstm_kernel(xm_ref, om_ref, gc_ref, gr_ref, cw_ref, cb_ref, wq_ref, wk_ref, wv_ref, bcol_ref, brow_ref,
                  skip_ref, triu_ref, o_ref, xpad, q_s, k_s, v_s, xc_s, h_s, gcol_s, grow_s, brow_s, ct_s, n_s, m_s,
                  *, tb):
    ncb = tb // CHUNK

    @pl.when(pl.program_id(0) == 0)
    def _():
        xpad[0:SUBLANES, :] = jnp.zeros((SUBLANES, WIDTH), F32)
        ct_s[...] = jnp.zeros_like(ct_s)
        n_s[...] = jnp.zeros_like(n_s)
        m_s[...] = jnp.zeros_like(m_s)

    xm = xm_ref[...].astype(F32)
    xpad[SUBLANES:SUBLANES + tb, :] = xm
    cw = cw_ref[...]
    conv = cb_ref[...] + cw[3:4, :] * xm
    for d in range(1, MLSTM_CONV):
        conv = conv + cw[3 - d:4 - d, :] * xpad[SUBLANES - d:SUBLANES - d + tb, :]
    xpad[0:SUBLANES, :] = xpad[tb:tb + SUBLANES, :]
    xc = _silu(conv)
    xc_s[...] = xc
    for h in range(HEADS):
        sl = slice(h * HEAD_DIM, (h + 1) * HEAD_DIM)
        xch = xc[:, sl].astype(BF16)
        q_s[:, sl] = jnp.dot(xch, wq_ref[h], preferred_element_type=F32) * (HEAD_DIM ** -0.5)
        k_s[:, sl] = jnp.dot(xch, wk_ref[h], preferred_element_type=F32)
        v_s[:, sl] = jnp.dot(xm[:, sl].astype(BF16), wv_ref[h], preferred_element_type=F32)

    gc = gc_ref[...] + bcol_ref[...]
    lane = lax.broadcasted_iota(jnp.int32, gc.shape, 1)
    gcol_s[...] = jnp.where(lane < HEADS, gc, _log_sigmoid(gc))
    gr = gr_ref[...] + brow_ref[...]
    sub = lax.broadcasted_iota(jnp.int32, gr.shape, 1)
    gr = jnp.where(sub < HEADS, gr, _log_sigmoid(gr)).reshape(ncb * SUBLANES, CHUNK)
    grow_s[...] = gr
    brow_s[...] = jnp.dot(gr, triu_ref[...], precision=HIGHEST, preferred_element_type=F32)

    rowi = lax.broadcasted_iota(jnp.int32, (CHUNK, CHUNK), 0)
    coli = lax.broadcasted_iota(jnp.int32, (CHUNK, CHUNK), 1)
    causal = rowi >= coli

    def chunk_body(c, carry):
        r0 = pl.multiple_of(c * CHUNK, CHUNK)
        g0 = pl.multiple_of(c * SUBLANES, SUBLANES)
        gcol = gcol_s[pl.ds(r0, CHUNK), :]
        bcol = _cumsum_rows(gcol)
        grow = grow_s[pl.ds(g0, SUBLANES), :]
        brow = brow_s[pl.ds(g0, SUBLANES), :]
        for h in range(HEADS):
            sl = slice(h * HEAD_DIM, (h + 1) * HEAD_DIM)
            li_c = _lane_col(gcol, h)
            b_c = _lane_col(bcol, HEADS + h)
            li_r = grow[h:h + 1, :]
            b_r = brow[HEADS + h:HEADS + h + 1, :]
            b_tot = b_c[CHUNK - 1:CHUNK, :]
            a_c = b_tot - b_c + li_c
            a_max = jnp.max(a_c, axis=0, keepdims=True)
            q = q_s[pl.ds(r0, CHUNK), sl]
            k = k_s[pl.ds(r0, CHUNK), sl]
            v = v_s[pl.ds(r0, CHUNK), sl]
            wk = jnp.exp(a_c - a_max) * k
            c_loc_t = _dot_tn(v, wk)
            n_loc = jnp.sum(wk, axis=0, keepdims=True)
            ct_prev = ct_s[h]
            n_prev = n_s[h:h + 1, :]
            m_prev = m_s[h:h + 1, 0:1]
            d_mat = jnp.where(causal, b_c - b_r + li_r, -jnp.inf)
            m_inter = b_c + m_prev
            m_j = jnp.maximum(m_inter, jnp.max(d_mat, axis=1, keepdims=True))
            sc = _dot_nt(q, k) * jnp.exp(d_mat - m_j)
            g_inter = jnp.exp(m_inter - m_j)
            num = g_inter * _dot_nt(q, ct_prev) + _dot(sc, v)
            den = g_inter * jnp.sum(q * n_prev, axis=1, keepdims=True) + jnp.sum(sc, axis=1, keepdims=True)
            h_s[pl.ds(r0, CHUNK), sl] = num / jnp.maximum(jnp.abs(den), jnp.exp(-m_j))
            m_new = jnp.maximum(b_tot + m_prev, a_max)
            g_old = jnp.exp(b_tot + m_prev - m_new)
            g_loc = jnp.exp(a_max - m_new)
            ct_s[h] = g_old * ct_prev + g_loc * c_loc_t
            n_s[h:h + 1, :] = g_old * n_prev + g_loc * n_loc
            m_s[h:h + 1, :] = jnp.broadcast_to(m_new, (1, LANES))
        return carry

    lax.fori_loop(0, ncb, chunk_body, 0, unroll=2)
    o_ref[...] = (_sigmoid(om_ref[...].astype(F32)) * h_s[...] + skip_ref[...] * xc_s[...]).astype(o_ref.dtype)


def _mlstm(z1, z2, gt3, conv_w, conv_b, wq, wk, wv, b_i, b_f, skip, tb):
    t = z1.shape[0]
    ncb = tb // CHUNK
    bias = jnp.concatenate([b_i, b_f]).astype(F32)
    bcol = jnp.zeros((1, LANES), F32).at[0, :2 * HEADS].set(bias)
    brow = bias.reshape(2 * HEADS, 1)
    triu = jnp.triu(jnp.ones((CHUNK, CHUNK), F32))
    full = lambda shape: pl.BlockSpec(shape, lambda i: (0,) * len(shape))
    return pl.pallas_call(
        functools.partial(_mlstm_kernel, tb=tb),
        grid=(t // tb,),
        in_specs=[pl.BlockSpec((tb, WIDTH), lambda i: (i, Z1_XM)),
                  pl.BlockSpec((tb, WIDTH), lambda i: (i, Z1_OM)),
                  pl.BlockSpec((tb, LANES), lambda i: (i, WIDTH // LANES)),
                  pl.BlockSpec((ncb, SUBLANES, CHUNK), lambda i: (i, 0, 0)),
                  full((MLSTM_CONV, WIDTH)), full((1, WIDTH)),
                  full((HEADS, HEAD_DIM, HEAD_DIM)), full((HEADS, HEAD_DIM, HEAD_DIM)),
                  full((HEADS, HEAD_DIM, HEAD_DIM)),
                  full((1, LANES)), full((2 * HEADS, 1)), full((1, WIDTH)), full((CHUNK, CHUNK))],
        out_specs=pl.BlockSpec((tb, WIDTH), lambda i: (i, 0)),
        out_shape=jax.ShapeDtypeStruct((t, WIDTH), F32),
        scratch_shapes=[pltpu.VMEM((tb + SUBLANES, WIDTH), F32)]
        + [pltpu.VMEM((tb, WIDTH), F32) for _ in range(5)]
        + [pltpu.VMEM((tb, LANES), F32),
           pltpu.VMEM((ncb * SUBLANES, CHUNK), F32), pltpu.VMEM((ncb * SUBLANES, CHUNK), F32),
           pltpu.VMEM((HEADS, HEAD_DIM, HEAD_DIM), F32), pltpu.VMEM((SUBLANES, LANES), F32),
           pltpu.VMEM((SUBLANES, LANES), F32)],
        compiler_params=_cparams(("arbitrary",)),
        name="mlstm",
    )(z1, z1, z2, gt3, conv_w.astype(F32), conv_b.reshape(1, WIDTH).astype(F32), wq.astype(BF16), wk.astype(BF16),
      wv.astype(BF16), bcol, brow, skip.reshape(1, WIDTH).astype(F32), triu)


def _hgrn_kernel(q_ref, f_ref, i_ref, g_ref, lb_ref, nw_ref, tri_ref, o_ref, st_s, *, tb):
    ncb = tb // CHUNK

    @pl.when(pl.program_id(0) == 0)
    def _():
        st_s[...] = jnp.zeros_like(st_s)

    lb = lb_ref[...]
    nw = nw_ref[...]
    tri = tri_ref[...]
    rowi = lax.broadcasted_iota(jnp.int32, (CHUNK, WIDTH), 0)
    sr = lax.broadcasted_iota(jnp.int32, (CHUNK, CHUNK), 0)
    sc = lax.broadcasted_iota(jnp.int32, (CHUNK, CHUNK), 1)
    halves = [1 << p for p in range(CHUNK.bit_length() - 1)]
    upper = {m: (rowi & m) != 0 for m in halves}
    same_blk = {m: (sr // (2 * m)) == (sc // (2 * m)) for m in halves}

    def chunk_body(c, carry):
        r0 = pl.multiple_of(c * CHUNK, CHUNK)
        f = lb + (1.0 - lb) * _sigmoid(f_ref[pl.ds(r0, CHUNK), :].astype(F32))
        k = 1.0 - f
        q = _silu(q_ref[pl.ds(r0, CHUNK), :].astype(F32))
        v = i_ref[pl.ds(r0, CHUNK), :].astype(F32)
        lf = jnp.log(f)
        hi = lf.astype(BF16)
        r1 = lf - hi.astype(F32)
        mid = r1.astype(BF16)
        lo = (r1 - mid.astype(F32)).astype(BF16)
        b = (jnp.dot(tri, hi, preferred_element_type=F32) + jnp.dot(tri, mid, preferred_element_type=F32)
             + jnp.dot(tri, lo, preferred_element_type=F32))
        qs, ks = {}, {}
        for m in halves:
            if m == 1:
                t = jnp.where(upper[m], q * f, k)
            else:
                nblk = CHUNK // (2 * m)
                r = jnp.concatenate([jnp.broadcast_to(b[2 * m * j + m - 1:2 * m * j + m, :], (2 * m, WIDTH))
                                     for j in range(nblk)], axis=0)
                t = jnp.where(upper[m], q, k) * jnp.exp(-jnp.abs(b - r))
            qs[m] = jnp.where(upper[m], t, 0.0).astype(BF16)
            ks[m] = jnp.where(upper[m], 0.0, t).astype(BF16)
        b_last = b[CHUNK - 1:CHUNK, :]
        qe = q * jnp.exp(b)
        kd = k * jnp.exp(b_last - b)
        e_last = jnp.exp(b_last)
        outs = []
        for h in range(HEADS):
            sl = slice(h * HEAD_DIM, (h + 1) * HEAD_DIM)
            st = st_s[h]
            a = jnp.where(sr == sc, _dot_nt(q[:, sl], k[:, sl]), 0.0)
            for m in halves:
                a = a + jnp.where(same_blk[m], _dot_nt(qs[m][:, sl], ks[m][:, sl]), 0.0)
            oh = _dot_nt(qe[:, sl], st) + _dot(a, v[:, sl])
            st_s[h] = e_last[:, sl] * st + _dot_tn(v[:, sl], kd[:, sl])
            ms = jnp.sum(oh * oh, axis=1, keepdims=True) * (1.0 / HEAD_DIM)
            outs.append(oh * lax.rsqrt(ms + NORM_EPS))
        on = jnp.concatenate(outs, axis=1)
        g = g_ref[pl.ds(r0, CHUNK), :].astype(F32)
        o_ref[pl.ds(r0, CHUNK), :] = (on * nw * _silu(g)).astype(o_ref.dtype)
        return carry

    lax.fori_loop(0, ncb, chunk_body, 0, unroll=2)


def _hgrn(z1, z2, lb, norm_w, tb):
    t = z1.shape[0]
    tril = jnp.tril(jnp.ones((CHUNK, CHUNK), F32)).astype(BF16)
    full = lambda shape: pl.BlockSpec(shape, lambda i: (0,) * len(shape))
    return pl.pallas_call(
        functools.partial(_hgrn_kernel, tb=tb),
        grid=(t // tb,),
        in_specs=[pl.BlockSpec((tb, WIDTH), lambda i: (i, Z1_QH)),
                  pl.BlockSpec((tb, WIDTH), lambda i: (i, 0)),
                  pl.BlockSpec((tb, WIDTH), lambda i: (i, Z1_IH)),
                  pl.BlockSpec((tb, WIDTH), lambda i: (i, Z1_GH)),
                  full((1, WIDTH)), full((1, WIDTH)), full((CHUNK, CHUNK))],
        out_specs=pl.BlockSpec((tb, WIDTH), lambda i: (i, 0)),
        out_shape=jax.ShapeDtypeStruct((t, WIDTH), F32),
        scratch_shapes=[pltpu.VMEM((HEADS, HEAD_DIM, HEAD_DIM), F32)],
        compiler_params=_cparams(("arbitrary",)),
        name="hgrn2",
    )(z1, z2, z1, z1, lb.reshape(1, WIDTH).astype(F32), norm_w.reshape(1, WIDTH).astype(F32), tril)


def _s5_tables(lam_re, lam_im, log_dt, b_re, b_im, c_re, c_im):
    ln = S5_CHUNK
    lr = jnp.minimum(lam_re.astype(F32), S5_MAX_REAL)
    li = lam_im.astype(F32)
    dt = jnp.exp(log_dt.astype(F32))[:, None]
    mag = jnp.exp(lr * dt)
    ab_re = mag * jnp.cos(li * dt)
    ab_im = mag * jnp.sin(li * dt)
    nr = ab_re - 1.0
    den = lr * lr + li * li
    cr = (nr * lr + ab_im * li) / den
    ci = (ab_im * lr - nr * li) / den
    bb_re = cr[..., None] * b_re - ci[..., None] * b_im
    bb_im = cr[..., None] * b_im + ci[..., None] * b_re
    tau = jnp.arange(ln + 1, dtype=F32)[:, None, None]
    pm = jnp.exp(lr * dt * tau)
    pr = pm * jnp.cos(li * dt * tau)
    pi = pm * jnp.sin(li * dt * tau)
    w1 = c_re[:, :, :, None] * bb_re[:, None, :, :] - c_im[:, :, :, None] * bb_im[:, None, :, :]
    w2 = c_re[:, :, :, None] * bb_im[:, None, :, :] + c_im[:, :, :, None] * bb_re[:, None, :, :]
    kk = (jnp.einsum('gopc,tgp->gtco', w1, pr[:ln], precision=HIGHEST)
          - jnp.einsum('gopc,tgp->gtco', w2, pi[:ln], precision=HIGHEST))
    kz = jnp.concatenate([kk, jnp.zeros_like(kk[:, :1])], axis=1)
    s_i = jnp.arange(ln)[:, None]
    t_i = jnp.arange(ln)[None, :]
    idx = jnp.where(t_i >= s_i, t_i - s_i, ln)
    m = kz[:, idx]
    m = m.transpose(0, 1, 3, 2, 4).reshape(S5_GROUPS, ln * S5_GROUP, ln * S5_GROUP)
    pe_r = pr[ln - 1 - jnp.arange(ln)]
    pe_i = pi[ln - 1 - jnp.arange(ln)]
    e_re = pe_r[:, :, :, None] * bb_re[None] - pe_i[:, :, :, None] * bb_im[None]
    e_im = pe_r[:, :, :, None] * bb_im[None] + pe_i[:, :, :, None] * bb_re[None]
    e_re = e_re.transpose(1, 0, 3, 2).reshape(S5_GROUPS, ln * S5_GROUP, S5_STATE)
    e_im = e_im.transpose(1, 0, 3, 2).reshape(S5_GROUPS, ln * S5_GROUP, S5_STATE)
    pf_r = pr[1:].transpose(1, 2, 0)
    pf_i = pi[1:].transpose(1, 2, 0)
    cre = c_re.transpose(0, 2, 1)
    cim = c_im.transpose(0, 2, 1)
    f_re = cre[:, :, None, :] * pf_r[..., None] - cim[:, :, None, :] * pf_i[..., None]
    f_im = -(cre[:, :, None, :] * pf_i[..., None] + cim[:, :, None, :] * pf_r[..., None])
    f_re = f_re.reshape(S5_GROUPS, S5_STATE, ln * S5_GROUP)
    f_im = f_im.reshape(S5_GROUPS, S5_STATE, ln * S5_GROUP)
    rows = ln * S5_GROUP
    zero_e = jnp.zeros((S5_PAIRS, rows, S5_STATE), F32)
    e0r, e1r, e0i, e1i = e_re[0::2], e_re[1::2], e_im[0::2], e_im[1::2]
    e_pair = jnp.concatenate([jnp.concatenate([e0r, zero_e, e0i, zero_e], axis=2),
                              jnp.concatenate([zero_e, e1r, zero_e, e1i], axis=2)], axis=1)
    zero_f = jnp.zeros((S5_PAIRS, S5_STATE, rows), F32)
    fr_pair = jnp.concatenate([jnp.concatenate([f_re[0::2], zero_f], axis=2),
                               jnp.concatenate([zero_f, f_re[1::2]], axis=2)], axis=1)
    fi_pair = jnp.concatenate([jnp.concatenate([f_im[0::2], zero_f], axis=2),
                               jnp.concatenate([zero_f, f_im[1::2]], axis=2)], axis=1)
    al = jnp.zeros((S5_PAIRS, SUBLANES, LANES), F32)
    al = al.at[:, 0, :].set(pr[ln].reshape(S5_PAIRS, 2 * S5_STATE))
    al = al.at[:, 1, :].set(pi[ln].reshape(S5_PAIRS, 2 * S5_STATE))
    return m.astype(BF16), e_pair.astype(BF16), fr_pair.astype(BF16), fi_pair.astype(BF16), al


def _s5_kernel(u_ref, m_ref, e_ref, fr_ref, fi_ref, al_ref, o_ref, sloc, xpr, xpi, *, nc):
    rows = S5_CHUNK * S5_GROUP
    u = u_ref[0]
    sloc[...] = jnp.dot(u, e_ref[0], preferred_element_type=F32)
    ar = al_ref[0, 0:1, :]
    ai = al_ref[0, 1:2, :]

    row = lax.broadcasted_iota(jnp.int32, (SUBLANES, LANES), 0)

    def body(ti, carry):
        xr, xi = carry
        r0 = pl.multiple_of(ti * SUBLANES, SUBLANES)
        sr_t = sloc[pl.ds(r0, SUBLANES), 0:LANES]
        si_t = sloc[pl.ds(r0, SUBLANES), LANES:2 * LANES]
        pr_t = jnp.zeros((SUBLANES, LANES), F32)
        pi_t = jnp.zeros((SUBLANES, LANES), F32)
        for r in range(SUBLANES):
            pr_t = jnp.where(row == r, xr, pr_t)
            pi_t = jnp.where(row == r, xi, pi_t)
            xr, xi = ar * xr - ai * xi + sr_t[r:r + 1, :], ar * xi + ai * xr + si_t[r:r + 1, :]
        xpr[pl.ds(r0, SUBLANES), :] = pr_t
        xpi[pl.ds(r0, SUBLANES), :] = pi_t
        return xr, xi

    zero = jnp.zeros((1, LANES), F32)
    lax.fori_loop(0, nc // SUBLANES, body, (zero, zero))
    y = (jnp.dot(xpr[...].astype(BF16), fr_ref[0], preferred_element_type=F32)
         + jnp.dot(xpi[...].astype(BF16), fi_ref[0], preferred_element_type=F32))
    o_ref[0, :, 0:rows] = y[:, 0:rows] + jnp.dot(u[:, 0:rows], m_ref[0], preferred_element_type=F32)
    o_ref[0, :, rows:2 * rows] = y[:, rows:2 * rows] + jnp.dot(u[:, rows:2 * rows], m_ref[1],
                                                              preferred_element_type=F32)


def _s5(z1, lam_re, lam_im, log_dt, b_re, b_im, c_re, c_im):
    t = z1.shape[0]
    ln = S5_CHUNK
    nc = t // ln
    rows = ln * S5_GROUP
    m, e_pair, fr_pair, fi_pair, al = _s5_tables(lam_re, lam_im, log_dt, b_re, b_im, c_re, c_im)
    u = z1[:, Z1_US * WIDTH:(Z1_US + 1) * WIDTH].astype(BF16)
    u = u.reshape(nc, ln, S5_PAIRS, 2, S5_GROUP).transpose(2, 0, 3, 1, 4).reshape(S5_PAIRS, nc, 2 * rows)
    y = pl.pallas_call(
        functools.partial(_s5_kernel, nc=nc),
        grid=(S5_PAIRS,),
        in_specs=[pl.BlockSpec((1, nc, 2 * rows), lambda j: (j, 0, 0)),
                  pl.BlockSpec((2, rows, rows), lambda j: (j, 0, 0)),
                  pl.BlockSpec((1, 2 * rows, 2 * LANES), lambda j: (j, 0, 0)),
                  pl.BlockSpec((1, LANES, 2 * rows), lambda j: (j, 0, 0)),
                  pl.BlockSpec((1, LANES, 2 * rows), lambda j: (j, 0, 0)),
                  pl.BlockSpec((1, SUBLANES, LANES), lambda j: (j, 0, 0))],
        out_specs=pl.BlockSpec((1, nc, 2 * rows), lambda j: (j, 0, 0)),
        out_shape=jax.ShapeDtypeStruct((S5_PAIRS, nc, 2 * rows), F32),
        scratch_shapes=[pltpu.VMEM((nc, 2 * LANES), F32), pltpu.VMEM((nc, LANES), F32), pltpu.VMEM((nc, LANES), F32)],
        compiler_params=_cparams(("parallel",)),
        name="s5_scan",
    )(u, m, e_pair, fr_pair, fi_pair, al)
    return y.reshape(S5_PAIRS, nc, 2, ln, S5_GROUP).transpose(1, 3, 0, 2, 4).reshape(t, WIDTH)


def _layer_norm(x, g, b):
    mu = jnp.mean(x, axis=-1, keepdims=True)
    xc = x - mu
    var = jnp.mean(xc * xc, axis=-1, keepdims=True)
    return xc * lax.rsqrt(var + LN_EPS) * g + b


def _merge_kernel(x_ref, g0_ref, g1_ref, g2_ref, ym_ref, ys_ref, us_ref, yh_ref, wm_ref, ws_ref, wh_ref, wglu_ref,
                  d_ref, wo_ref, lg_ref, lb_ref, rw_ref, rb_ref, o_ref):
    ys = _gelu_tanh(ys_ref[...] + d_ref[...] * us_ref[...].astype(F32))
    ys = ys * _sigmoid(_dot(ys, wglu_ref[...]))
    merged = (_sigmoid(g0_ref[...].astype(F32)) * _dot(ym_ref[...], wm_ref[...])
              + _sigmoid(g1_ref[...].astype(F32)) * _dot(ys, ws_ref[...])
              + _sigmoid(g2_ref[...].astype(F32)) * _dot(yh_ref[...], wh_ref[...]))
    x1 = _layer_norm(DN_ALPHA * x_ref[...] + _dot(merged, wo_ref[...]), lg_ref[...], lb_ref[...])
    o_ref[:, 0:D_MODEL] = x1

    xh = x1.astype(BF16)
    xl = (x1 - xh.astype(F32)).astype(BF16)
    logits = (jnp.dot(xh, rw_ref[0], preferred_element_type=F32) + jnp.dot(xl, rw_ref[0], preferred_element_type=F32)
              + jnp.dot(xh, rw_ref[1], preferred_element_type=F32))
    s0 = _sigmoid(logits)
    sb0 = s0 + rb_ref[...]
    shift = lambda a, j: a if j == 0 else pltpu.roll(a, LANES - N_GROUPS * j, 1)
    s = [shift(s0, j) for j in range(EXPERTS_PER_GROUP)]
    sb = [shift(sb0, j) for j in range(EXPERTS_PER_GROUP)]
    hi1, lo1 = jnp.maximum(sb[0], sb[1]), jnp.minimum(sb[0], sb[1])
    hi2, lo2 = jnp.maximum(sb[2], sb[3]), jnp.minimum(sb[2], sb[3])
    top2 = jnp.maximum(hi1, hi2) + jnp.maximum(jnp.minimum(hi1, hi2), jnp.maximum(lo1, lo2))
    lane = lax.broadcasted_iota(jnp.int32, top2.shape, 1)
    top2 = jnp.where(lane < N_GROUPS, top2, -jnp.inf)
    gmax = jnp.max(top2, axis=1, keepdims=True)
    g_idx = jnp.min(jnp.where(top2 == gmax, lane, LANES), axis=1, keepdims=True)
    sel = lane == g_idx
    v = [jnp.sum(jnp.where(sel, sb[j], 0.0), axis=1, keepdims=True) for j in range(EXPERTS_PER_GROUP)]
    sv = [jnp.sum(jnp.where(sel, s[j], 0.0), axis=1, keepdims=True) for j in range(EXPERTS_PER_GROUP)]

    def first_max(vals):
        m = jnp.maximum(jnp.maximum(vals[0], vals[1]), jnp.maximum(vals[2], vals[3]))
        return jnp.where(vals[0] == m, 0, jnp.where(vals[1] == m, 1, jnp.where(vals[2] == m, 2, 3)))

    e1 = first_max(v)
    e2 = first_max([jnp.where(e1 == j, -jnp.inf, v[j]) for j in range(EXPERTS_PER_GROUP)])
    s1 = sum(jnp.where(e1 == j, sv[j], 0.0) for j in range(EXPERTS_PER_GROUP))
    s2 = sum(jnp.where(e2 == j, sv[j], 0.0) for j in range(EXPERTS_PER_GROUP))
    tot = s1 + s2
    meta = jnp.where(lane == 0, g_idx.astype(F32), 0.0)
    for j in range(EXPERTS_PER_GROUP):
        cw = jnp.where(e1 == j, s1 / tot, 0.0) + jnp.where(e2 == j, s2 / tot, 0.0)
        meta = jnp.where(lane == 1 + j, cw, meta)
    o_ref[:, D_MODEL:D_MODEL + META] = meta


def _merge(x, z1, ym, ys, yh, wm, ws, wh, wglu, d, wo, lg, lb, router_w, router_bias, tm):
    t = x.shape[0]
    rw = router_w.astype(F32).reshape(D_MODEL, N_GROUPS, EXPERTS_PER_GROUP).transpose(0, 2, 1)
    rw = jnp.pad(rw.reshape(D_MODEL, N_EXPERTS), ((0, 0), (0, LANES - N_EXPERTS)))
    rw_hi = rw.astype(BF16)
    rw = jnp.stack([rw_hi, (rw - rw_hi.astype(F32)).astype(BF16)])
    rb = jnp.pad(router_bias.astype(F32).reshape(N_GROUPS, EXPERTS_PER_GROUP).T.reshape(1, N_EXPERTS),
                 ((0, 0), (0, LANES - N_EXPERTS)))
    full = lambda shape: pl.BlockSpec(shape, lambda i: (0,) * len(shape))
    row = lambda w: pl.BlockSpec((tm, w), lambda i: (i, 0))
    return pl.pallas_call(
        _merge_kernel,
        grid=(t // tm,),
        in_specs=[row(D_MODEL),
                  pl.BlockSpec((tm, D_MODEL), lambda i: (i, 0)),
                  pl.BlockSpec((tm, D_MODEL), lambda i: (i, 1)),
                  pl.BlockSpec((tm, D_MODEL), lambda i: (i, 2)),
                  row(WIDTH), row(WIDTH), pl.BlockSpec((tm, WIDTH), lambda i: (i, Z1_US)), row(WIDTH),
                  full((WIDTH, D_MODEL)), full((WIDTH, D_MODEL)), full((WIDTH, D_MODEL)), full((WIDTH, WIDTH)),
                  full((1, WIDTH)), full((D_MODEL, D_MODEL)), full((1, D_MODEL)), full((1, D_MODEL)),
                  full((2, D_MODEL, LANES)), full((1, LANES))],
        out_specs=pl.BlockSpec((tm, D_MODEL + META), lambda i: (i, 0)),
        out_shape=jax.ShapeDtypeStruct((t, D_MODEL + META), F32),
        compiler_params=_cparams(("parallel",)),
        name="merge_router",
    )(x, z1, z1, z1, ym, ys, z1, yh, wm.astype(BF16), ws.astype(BF16), wh.astype(BF16), wglu.astype(BF16),
      d.reshape(1, WIDTH).astype(F32), wo.astype(BF16), lg.reshape(1, D_MODEL).astype(F32),
      lb.reshape(1, D_MODEL).astype(F32), rw, rb)


def _start_row_gather(idx_ref, base, src_hbm, dst, sem, rows):
    def body(r, carry):
        pltpu.make_async_copy(src_hbm.at[pl.ds(idx_ref[base + r], 1)], dst.at[pl.ds(r, 1)], sem).start()
        return carry
    lax.fori_loop(0, rows, body, 0)


def _wait_row_gather(src_hbm, dst, sem, rows):
    pltpu.make_async_copy(src_hbm.at[pl.ds(0, rows)], dst, sem).wait()


def _moe_kernel(tg_ref, src_ref, x_hbm, wg_ref, wu_ref, wd_ref, o_ref, buf, sem, *, tm, ntiles):
    i = pl.program_id(0)
    slot = i % 2

    @pl.when(i == 0)
    def _():
        _start_row_gather(src_ref, 0, x_hbm, buf.at[0], sem.at[0], tm)

    @pl.when(i + 1 < ntiles)
    def _():
        _start_row_gather(src_ref, (i + 1) * tm, x_hbm, buf.at[1 - slot], sem.at[1 - slot], tm)

    _wait_row_gather(x_hbm, buf.at[slot], sem.at[slot], tm)
    xt = buf[slot]
    xb = xt[:, 0:D_MODEL].astype(BF16)
    meta = xt[:, D_MODEL:D_MODEL + META]
    y = jnp.zeros((tm, D_MODEL), F32)
    for e in range(EXPERTS_PER_GROUP):
        hg = jnp.dot(xb, wg_ref[e], preferred_element_type=F32)
        hu = jnp.dot(xb, wu_ref[e], preferred_element_type=F32)
        hh = _silu(hg) * hu * _lane_col(meta, 1 + e)
        y = y + jnp.dot(hh.astype(BF16), wd_ref[e], preferred_element_type=F32)
    o_ref[...] = y


def _ln2_kernel(dst_ref, x_ref, y_hbm, lg_ref, lb_ref, o_ref, buf, sem, *, tm, ntiles):
    i = pl.program_id(0)
    slot = i % 2

    @pl.when(i == 0)
    def _():
        _start_row_gather(dst_ref, 0, y_hbm, buf.at[0], sem.at[0], tm)

    @pl.when(i + 1 < ntiles)
    def _():
        _start_row_gather(dst_ref, (i + 1) * tm, y_hbm, buf.at[1 - slot], sem.at[1 - slot], tm)

    _wait_row_gather(y_hbm, buf.at[slot], sem.at[slot], tm)
    o_ref[...] = _layer_norm(DN_ALPHA * x_ref[...] + buf[slot], lg_ref[...], lb_ref[...])


def _moe_ln(x1e, wg, wu, wd, lg, lb, tm):
    t = x1e.shape[0]
    ntiles = t // tm + N_GROUPS
    p = ntiles * tm
    key = x1e[:, D_MODEL].astype(jnp.int32)
    onehot = (key[:, None] == jnp.arange(N_GROUPS)[None, :]).astype(jnp.int32)
    csum = jnp.cumsum(onehot, axis=0)
    counts = csum[-1]
    rank = jnp.sum(onehot * csum, axis=1) - 1
    pcount = ((counts + tm - 1) // tm) * tm
    pend = jnp.cumsum(pcount)
    dest = (pend - pcount)[key] + rank
    row_src = jnp.zeros((p,), jnp.int32).at[dest].set(jnp.arange(t, dtype=jnp.int32))
    tile_start = jnp.arange(ntiles, dtype=jnp.int32) * tm
    tile_group = jnp.minimum(jnp.sum((tile_start[:, None] >= pend[None, :]).astype(jnp.int32), axis=1),
                             N_GROUPS - 1)
    wspec = lambda a, b: pl.BlockSpec((EXPERTS_PER_GROUP, a, b), lambda i, tg, src: (tg[i], 0, 0))
    y_sorted = pl.pallas_call(
        functools.partial(_moe_kernel, tm=tm, ntiles=ntiles),
        grid_spec=pltpu.PrefetchScalarGridSpec(
            num_scalar_prefetch=2,
            grid=(ntiles,),
            in_specs=[pl.BlockSpec(memory_space=pl.ANY), wspec(D_MODEL, D_EXPERT), wspec(D_MODEL, D_EXPERT),
                      wspec(D_EXPERT, D_MODEL)],
            out_specs=pl.BlockSpec((tm, D_MODEL), lambda i, tg, src: (i, 0)),
            scratch_shapes=[pltpu.VMEM((2, tm, D_MODEL + META), F32), pltpu.SemaphoreType.DMA((2,))]),
        out_shape=jax.ShapeDtypeStruct((p, D_MODEL), F32),
        compiler_params=_cparams(("arbitrary",)),
        name="moe_experts",
    )(tile_group, row_src, x1e, wg.astype(BF16), wu.astype(BF16), wd.astype(BF16))
    nt2 = t // tm
    return pl.pallas_call(
        functools.partial(_ln2_kernel, tm=tm, ntiles=nt2),
        grid_spec=pltpu.PrefetchScalarGridSpec(
            num_scalar_prefetch=1,
            grid=(nt2,),
            in_specs=[pl.BlockSpec((tm, D_MODEL), lambda i, d: (i, 0)), pl.BlockSpec(memory_space=pl.ANY),
                      pl.BlockSpec((1, D_MODEL), lambda i, d: (0, 0)), pl.BlockSpec((1, D_MODEL), lambda i, d: (0, 0))],
            out_specs=pl.BlockSpec((tm, D_MODEL), lambda i, d: (i, 0)),
            scratch_shapes=[pltpu.VMEM((2, tm, D_MODEL), F32), pltpu.SemaphoreType.DMA((2,))]),
        out_shape=jax.ShapeDtypeStruct((t, D_MODEL), F32),
        compiler_params=_cparams(("arbitrary",)),
        name="moe_combine_ln",
    )(dest.astype(jnp.int32), x1e, y_sorted, lg.reshape(1, D_MODEL).astype(F32), lb.reshape(1, D_MODEL).astype(F32))


def _layer(x, l, p, lb_l, cfg):
    t = x.shape[0]
    w1, w2, wt = _split_w_in(p['w_in'][l])
    xb = x.astype(BF16)
    z1 = _matmul(xb, w1, Z1_DTYPE, cfg['tm_in'], cfg['tn_in'])
    z2, gt = _inproj_gate(xb, w2, wt, cfg['tm_in'])
    gt3 = gt.reshape(SUBLANES, t // CHUNK, CHUNK).transpose(1, 0, 2)
    ym = _mlstm(z1, z2, gt3, p['mlstm_conv_w'][l], p['mlstm_conv_b'][l], p['mlstm_wq'][l], p['mlstm_wk'][l],
                p['mlstm_wv'][l], p['mlstm_b_i'][l], p['mlstm_b_f'][l], p['mlstm_skip'][l], cfg['tb'])
    ys = _s5(z1, p['s5_lambda_re'][l], p['s5_lambda_im'][l], p['s5_log_dt'][l], p['s5_b_re'][l], p['s5_b_im'][l],
             p['s5_c_re'][l], p['s5_c_im'][l])
    yh = _hgrn(z1, z2, lb_l, p['hgrn_norm_w'][l], cfg['tb'])
    x1e = _merge(x, z1, ym, ys, yh, p['w_branch_mlstm'][l], p['w_branch_s5'][l], p['w_branch_hgrn'][l],
                 p['s5_w_glu'][l], p['s5_d'][l], p['w_out'][l], p['ln1_g'][l], p['ln1_b'][l], p['router_w'],
                 p['router_bias'], cfg['tm_merge'])
    return _moe_ln(x1e, p['exp_w_gate'][l], p['exp_w_up'][l], p['exp_w_down'][l], p['ln2_g'][l], p['ln2_b'][l],
                   cfg['tm_moe'])


_CFG = dict(tm_in=1024, tn_in=1024, tb=512, tm_merge=256, tm_moe=256)


def kernel(x, w_in, mlstm_conv_w, mlstm_conv_b, mlstm_wq, mlstm_wk, mlstm_wv, mlstm_b_i, mlstm_b_f, mlstm_skip, s5_lambda_re, s5_lambda_im, s5_log_dt, s5_b_re, s5_b_im, s5_c_re, s5_c_im, s5_d, s5_w_glu, hgrn_lower_bounds, hgrn_norm_w, w_branch_mlstm, w_branch_s5, w_branch_hgrn, w_out, ln1_g, ln1_b, ln2_g, ln2_b, router_w, router_bias, exp_w_gate, exp_w_up, exp_w_down):
    p = dict(w_in=w_in, mlstm_conv_w=mlstm_conv_w, mlstm_conv_b=mlstm_conv_b, mlstm_wq=mlstm_wq, mlstm_wk=mlstm_wk,
             mlstm_wv=mlstm_wv, mlstm_b_i=mlstm_b_i, mlstm_b_f=mlstm_b_f, mlstm_skip=mlstm_skip,
             s5_lambda_re=s5_lambda_re, s5_lambda_im=s5_lambda_im, s5_log_dt=s5_log_dt, s5_b_re=s5_b_re,
             s5_b_im=s5_b_im, s5_c_re=s5_c_re, s5_c_im=s5_c_im, s5_d=s5_d, s5_w_glu=s5_w_glu,
             hgrn_norm_w=hgrn_norm_w, w_branch_mlstm=w_branch_mlstm, w_branch_s5=w_branch_s5,
             w_branch_hgrn=w_branch_hgrn, w_out=w_out, ln1_g=ln1_g, ln1_b=ln1_b, ln2_g=ln2_g, ln2_b=ln2_b,
             router_w=router_w, router_bias=router_bias, exp_w_gate=exp_w_gate, exp_w_up=exp_w_up,
             exp_w_down=exp_w_down)
    lb_cum = jnp.cumsum(jax.nn.softmax(hgrn_lower_bounds.astype(F32), axis=0), axis=0)
    lb_layers = lb_cum - lb_cum[0]
    bsz, seq, d = x.shape
    h = x.reshape(bsz * seq, d)
    for l in range(DEPTH):
        h = _layer(h, l, p, lb_layers[l], _CFG)
    return h.reshape(bsz, seq, d)
```

```python
import functools
import math

import jax
import jax.numpy as jnp
from jax import lax
from jax.experimental import pallas as pl
from jax.experimental.pallas import tpu as pltpu

F32 = jnp.float32
BF16 = jnp.bfloat16
HIGHEST = lax.Precision.HIGHEST

D_MODEL = 1024
DEPTH = 2
HEADS = 4
HEAD_DIM = 128
WIDTH = HEADS * HEAD_DIM
MLSTM_CONV = 4
CHUNK = 64
S5_GROUP = 16
S5_GROUPS = 32
S5_STATE = 64
S5_PAIRS = S5_GROUPS // 2
S5_CHUNK = 32
S5_MAX_REAL = -1e-4
N_EXPERTS = 32
N_GROUPS = 8
EXPERTS_PER_GROUP = 4
D_EXPERT = 256
DN_ALPHA = (2 * DEPTH) ** 0.25
LN_EPS = 1e-5
NORM_EPS = 1e-6

LANES = 128
SUBLANES = 8
META = LANES
VMEM_LIMIT = 56 * 1024 * 1024

Z1_DTYPE = BF16
Z1_GATE, Z1_XM, Z1_OM, Z1_US, Z1_QH, Z1_IH, Z1_GH = 0, 6, 7, 8, 9, 10, 11
Z1_COLS = 12 * WIDTH
Z2_COLS = WIDTH + LANES


def _cparams(sem):
    return pltpu.CompilerParams(dimension_semantics=sem, vmem_limit_bytes=VMEM_LIMIT)


def _sigmoid(x):
    return 0.5 * (1.0 + jnp.tanh(0.5 * x))


def _silu(x):
    return x * _sigmoid(x)


def _log_sigmoid(x):
    return jnp.minimum(x, 0.0) - jnp.log(1.0 + jnp.exp(-jnp.abs(x)))


def _gelu_tanh(x):
    return 0.5 * x * (1.0 + jnp.tanh(math.sqrt(2.0 / math.pi) * (x + 0.044715 * (x * x * x))))


def _cumsum_rows(x):
    n = x.shape[0]
    row = lax.broadcasted_iota(jnp.int32, x.shape, 0)
    s = 1
    while s < n:
        x = x + jnp.where(row >= s, pltpu.roll(x, s, 0), 0.0)
        s *= 2
    return x


def _lane_col(x, idx):
    lane = lax.broadcasted_iota(jnp.int32, x.shape, 1)
    return jnp.sum(jnp.where(lane == idx, x, 0.0), axis=1, keepdims=True)


def _dot(a, b):
    return jnp.dot(a.astype(BF16), b.astype(BF16), preferred_element_type=F32)


def _dot_nt(a, b):
    return lax.dot_general(a.astype(BF16), b.astype(BF16), (((1,), (1,)), ((), ())), preferred_element_type=F32)


def _dot_tn(a, b):
    return lax.dot_general(a.astype(BF16), b.astype(BF16), (((0,), (0,)), ((), ())), preferred_element_type=F32)


def _mm_kernel(x_ref, w_ref, o_ref):
    o_ref[...] = jnp.dot(x_ref[...], w_ref[...], preferred_element_type=F32).astype(o_ref.dtype)


def _matmul(x, w, out_dtype, tm, tn):
    m, k = x.shape
    n = w.shape[1]
    return pl.pallas_call(
        _mm_kernel,
        grid=(m // tm, n // tn),
        in_specs=[pl.BlockSpec((tm, k), lambda i, j: (i, 0)), pl.BlockSpec((k, tn), lambda i, j: (0, j))],
        out_specs=pl.BlockSpec((tm, tn), lambda i, j: (i, j)),
        out_shape=jax.ShapeDtypeStruct((m, n), out_dtype),
        compiler_params=_cparams(("parallel", "parallel")),
        name="in_proj_wide",
    )(x, w)


def _inproj_gate_kernel(x_ref, w_ref, wt_ref, o_ref, gt_ref):
    x = x_ref[...]
    o_ref[...] = jnp.dot(x, w_ref[...], preferred_element_type=F32)
    gt_ref[...] = lax.dot_general(wt_ref[...], x, (((1,), (1,)), ((), ())), preferred_element_type=F32)


def _inproj_gate(x, w2, wt, tm):
    m, k = x.shape
    return pl.pallas_call(
        _inproj_gate_kernel,
        grid=(m // tm,),
        in_specs=[pl.BlockSpec((tm, k), lambda i: (i, 0)),
                  pl.BlockSpec((k, Z2_COLS), lambda i: (0, 0)),
                  pl.BlockSpec((SUBLANES, k), lambda i: (0, 0))],
        out_specs=[pl.BlockSpec((tm, Z2_COLS), lambda i: (i, 0)), pl.BlockSpec((SUBLANES, tm), lambda i: (0, i))],
        out_shape=[jax.ShapeDtypeStruct((m, Z2_COLS), F32), jax.ShapeDtypeStruct((SUBLANES, m), F32)],
        compiler_params=_cparams(("parallel",)),
        name="in_proj_gates",
    )(x, w2, wt)


def _split_w_in(w):
    offs, o = [], 0
    for s in (WIDTH, WIDTH, HEADS, HEADS, WIDTH, WIDTH, WIDTH, WIDTH, WIDTH, 3 * D_MODEL):
        offs.append((o, o + s))
        o += s
    seg = [w[:, a:b] for a, b in offs]
    xm, om, im, fm, us, qh, fh, ih, gh, gate = seg
    w1 = jnp.concatenate([gate, xm, om, us, qh, ih, gh], axis=1).astype(BF16)
    pad = jnp.zeros((w.shape[0], LANES - 2 * HEADS), w.dtype)
    w2 = jnp.concatenate([fh, im, fm, pad], axis=1).astype(BF16)
    wt = jnp.concatenate([im, fm], axis=1).T.astype(BF16)
    return w1, w2, wt


def _mlstm_kernel(xm_ref, om_ref, gc_ref, gr_ref, cw_ref, cb_ref, wq_ref, wk_ref, wv_ref, bcol_ref, brow_ref,
                  skip_ref, triu_ref, o_ref, xpad, q_s, k_s, v_s, xc_s, h_s, gcol_s, grow_s, brow_s, ct_s, n_s, m_s,
                  *, tb):
    ncb = tb // CHUNK

    @pl.when(pl.program_id(0) == 0)
    def _():
        xpad[0:SUBLANES, :] = jnp.zeros((SUBLANES, WIDTH), F32)
        ct_s[...] = jnp.zeros_like(ct_s)
        n_s[...] = jnp.zeros_like(n_s)
        m_s[...] = jnp.zeros_like(m_s)

    xm = xm_ref[...].astype(F32)
    xpad[SUBLANES:SUBLANES + tb, :] = xm
    cw = cw_ref[...]
    conv = cb_ref[...] + cw[3:4, :] * xm
    for d in range(1, MLSTM_CONV):
        conv = conv + cw[3 - d:4 - d, :] * xpad[SUBLANES - d:SUBLANES - d + tb, :]
    xpad[0:SUBLANES, :] = xpad[tb:tb + SUBLANES, :]
    xc = _silu(conv)
    xc_s[...] = xc
    for h in range(HEADS):
        sl = slice(h * HEAD_DIM, (h + 1) * HEAD_DIM)
        xch = xc[:, sl].astype(BF16)
        q_s[:, sl] = jnp.dot(xch, wq_ref[h], preferred_element_type=F32) * (HEAD_DIM ** -0.5)
        k_s[:, sl] = jnp.dot(xch, wk_ref[h], preferred_element_type=F32)
        v_s[:, sl] = jnp.dot(xm[:, sl].astype(BF16), wv_ref[h], preferred_element_type=F32)

    gc = gc_ref[...] + bcol_ref[...]
    lane = lax.broadcasted_iota(jnp.int32, gc.shape, 1)
    gcol_s[...] = jnp.where(lane < HEADS, gc, _log_sigmoid(gc))
    gr = gr_ref[...] + brow_ref[...]
    sub = lax.broadcasted_iota(jnp.int32, gr.shape, 1)
    gr = jnp.where(sub < HEADS, gr, _log_sigmoid(gr)).reshape(ncb * SUBLANES, CHUNK)
    grow_s[...] = gr
    brow_s[...] = jnp.dot(gr, triu_ref[...], precision=HIGHEST, preferred_element_type=F32)

    rowi = lax.broadcasted_iota(jnp.int32, (CHUNK, CHUNK), 0)
    coli = lax.broadcasted_iota(jnp.int32, (CHUNK, CHUNK), 1)
    causal = rowi >= coli

    def chunk_body(c, carry):
        r0 = pl.multiple_of(c * CHUNK, CHUNK)
        g0 = pl.multiple_of(c * SUBLANES, SUBLANES)
        gcol = gcol_s[pl.ds(r0, CHUNK), :]
        bcol = _cumsum_rows(gcol)
        grow = grow_s[pl.ds(g0, SUBLANES), :]
        brow = brow_s[pl.ds(g0, SUBLANES), :]
        for h in range(HEADS):
            sl = slice(h * HEAD_DIM, (h + 1) * HEAD_DIM)
            li_c = _lane_col(gcol, h)
            b_c = _lane_col(bcol, HEADS + h)
            li_r = grow[h:h + 1, :]
            b_r = brow[HEADS + h:HEADS + h + 1, :]
            b_tot = b_c[CHUNK - 1:CHUNK, :]
            a_c = b_tot - b_c + li_c
            a_max = jnp.max(a_c, axis=0, keepdims=True)
            q = q_s[pl.ds(r0, CHUNK), sl]
            k = k_s[pl.ds(r0, CHUNK), sl]
            v = v_s[pl.ds(r0, CHUNK), sl]
            wk = jnp.exp(a_c - a_max) * k
            c_loc_t = _dot_tn(v, wk)
            n_loc = jnp.sum(wk, axis=0, keepdims=True)
            ct_prev = ct_s[h]
            n_prev = n_s[h:h + 1, :]
            m_prev = m_s[h:h + 1, 0:1]
            d_mat = jnp.where(causal, b_c - b_r + li_r, -jnp.inf)
            m_inter = b_c + m_prev
            m_j = jnp.maximum(m_inter, jnp.max(d_mat, axis=1, keepdims=True))
            sc = _dot_nt(q, k) * jnp.exp(d_mat - m_j)
            g_inter = jnp.exp(m_inter - m_j)
            num = g_inter * _dot_nt(q, ct_prev) + _dot(sc, v)
            den = g_inter * jnp.sum(q * n_prev, axis=1, keepdims=True) + jnp.sum(sc, axis=1, keepdims=True)
            h_s[pl.ds(r0, CHUNK), sl] = num / jnp.maximum(jnp.abs(den), jnp.exp(-m_j))
            m_new = jnp.maximum(b_tot + m_prev, a_max)
            g_old = jnp.exp(b_tot + m_prev - m_new)
            g_loc = jnp.exp(a_max - m_new)
            ct_s[h] = g_old * ct_prev + g_loc * c_loc_t
            n_s[h:h + 1, :] = g_old * n_prev + g_loc * n_loc
            m_s[h:h + 1, :] = jnp.broadcast_to(m_new, (1, LANES))
        return carry

    lax.fori_loop(0, ncb, chunk_body, 0, unroll=2)
    o_ref[...] = (_sigmoid(om_ref[...].astype(F32)) * h_s[...] + skip_ref[...] * xc_s[...]).astype(o_ref.dtype)


def _mlstm(z1, z2, gt3, conv_w, conv_b, wq, wk, wv, b_i, b_f, skip, tb):
    t = z1.shape[0]
    ncb = tb // CHUNK
    bias = jnp.concatenate([b_i, b_f]).astype(F32)
    bcol = jnp.zeros((1, LANES), F32).at[0, :2 * HEADS].set(bias)
    brow = bias.reshape(2 * HEADS, 1)
    triu = jnp.triu(jnp.ones((CHUNK, CHUNK), F32))
    full = lambda shape: pl.BlockSpec(shape, lambda i: (0,) * len(shape))
    return pl.pallas_call(
        functools.partial(_mlstm_kernel, tb=tb),
        grid=(t // tb,),
        in_specs=[pl.BlockSpec((tb, WIDTH), lambda i: (i, Z1_XM)),
                  pl.BlockSpec((tb, WIDTH), lambda i: (i, Z1_OM)),
                  pl.BlockSpec((tb, LANES), lambda i: (i, WIDTH // LANES)),
                  pl.BlockSpec((ncb, SUBLANES, CHUNK), lambda i: (i, 0, 0)),
                  full((MLSTM_CONV, WIDTH)), full((1, WIDTH)),
                  full((HEADS, HEAD_DIM, HEAD_DIM)), full((HEADS, HEAD_DIM, HEAD_DIM)),
                  full((HEADS, HEAD_DIM, HEAD_DIM)),
                  full((1, LANES)), full((2 * HEADS, 1)), full((1, WIDTH)), full((CHUNK, CHUNK))],
        out_specs=pl.BlockSpec((tb, WIDTH), lambda i: (i, 0)),
        out_shape=jax.ShapeDtypeStruct((t, WIDTH), F32),
        scratch_shapes=[pltpu.VMEM((tb + SUBLANES, WIDTH), F32)]
        + [pltpu.VMEM((tb, WIDTH), F32) for _ in range(5)]
        + [pltpu.VMEM((tb, LANES), F32),
           pltpu.VMEM((ncb * SUBLANES, CHUNK), F32), pltpu.VMEM((ncb * SUBLANES, CHUNK), F32),
           pltpu.VMEM((HEADS, HEAD_DIM, HEAD_DIM), F32), pltpu.VMEM((SUBLANES, LANES), F32),
           pltpu.VMEM((SUBLANES, LANES), F32)],
        compiler_params=_cparams(("arbitrary",)),
        name="mlstm",
    )(z1, z1, z2, gt3, conv_w.astype(F32), conv_b.reshape(1, WIDTH).astype(F32), wq.astype(BF16), wk.astype(BF16),
      wv.astype(BF16), bcol, brow, skip.reshape(1, WIDTH).astype(F32), triu)


def _hgrn_kernel(q_ref, f_ref, i_ref, g_ref, lb_ref, nw_ref, tri_ref, o_ref, st_s, *, tb):
    ncb = tb // CHUNK

    @pl.when(pl.program_id(0) == 0)
    def _():
        st_s[...] = jnp.zeros_like(st_s)

    lb = lb_ref[...]
    nw = nw_ref[...]
    tri = tri_ref[...]
    rowi = lax.broadcasted_iota(jnp.int32, (CHUNK, WIDTH), 0)
    sr = lax.broadcasted_iota(jnp.int32, (CHUNK, CHUNK), 0)
    sc = lax.broadcasted_iota(jnp.int32, (CHUNK, CHUNK), 1)
    halves = [1 << p for p in range(CHUNK.bit_length() - 1)]
    upper = {m: (rowi & m) != 0 for m in halves}
    same_blk = {m: (sr // (2 * m)) == (sc // (2 * m)) for m in halves}

    def chunk_body(c, carry):
        r0 = pl.multiple_of(c * CHUNK, CHUNK)
        f = lb + (1.0 - lb) * _sigmoid(f_ref[pl.ds(r0, CHUNK), :].astype(F32))
        k = 1.0 - f
        q = _silu(q_ref[pl.ds(r0, CHUNK), :].astype(F32))
        v = i_ref[pl.ds(r0, CHUNK), :].astype(F32)
        lf = jnp.log(f)
        hi = lf.astype(BF16)
        r1 = lf - hi.astype(F32)
        mid = r1.astype(BF16)
        lo = (r1 - mid.astype(F32)).astype(BF16)
        b = (jnp.dot(tri, hi, preferred_element_type=F32) + jnp.dot(tri, mid, preferred_element_type=F32)
             + jnp.dot(tri, lo, preferred_element_type=F32))
        qs, ks = {}, {}
        for m in halves:
            if m == 1:
                t = jnp.where(upper[m], q * f, k)
            else:
                nblk = CHUNK // (2 * m)
                r = jnp.concatenate([jnp.broadcast_to(b[2 * m * j + m - 1:2 * m * j + m, :], (2 * m, WIDTH))
                                     for j in range(nblk)], axis=0)
                t = jnp.where(upper[m], q, k) * jnp.exp(-jnp.abs(b - r))
            qs[m] = jnp.where(upper[m], t, 0.0).astype(BF16)
            ks[m] = jnp.where(upper[m], 0.0, t).astype(BF16)
        b_last = b[CHUNK - 1:CHUNK, :]
        qe = q * jnp.exp(b)
        kd = k * jnp.exp(b_last - b)
        e_last = jnp.exp(b_last)
        outs = []
        for h in range(HEADS):
            sl = slice(h * HEAD_DIM, (h + 1) * HEAD_DIM)
            st = st_s[h]
            a = jnp.where(sr == sc, _dot_nt(q[:, sl], k[:, sl]), 0.0)
            for m in halves:
                a = a + jnp.where(same_blk[m], _dot_nt(qs[m][:, sl], ks[m][:, sl]), 0.0)
            oh = _dot_nt(qe[:, sl], st) + _dot(a, v[:, sl])
            st_s[h] = e_last[:, sl] * st + _dot_tn(v[:, sl], kd[:, sl])
            ms = jnp.sum(oh * oh, axis=1, keepdims=True) * (1.0 / HEAD_DIM)
            outs.append(oh * lax.rsqrt(ms + NORM_EPS))
        on = jnp.concatenate(outs, axis=1)
        g = g_ref[pl.ds(r0, CHUNK), :].astype(F32)
        o_ref[pl.ds(r0, CHUNK), :] = (on * nw * _silu(g)).astype(o_ref.dtype)
        return carry

    lax.fori_loop(0, ncb, chunk_body, 0, unroll=2)


def _hgrn(z1, z2, lb, norm_w, tb):
    t = z1.shape[0]
    tril = jnp.tril(jnp.ones((CHUNK, CHUNK), F32)).astype(BF16)
    full = lambda shape: pl.BlockSpec(shape, lambda i: (0,) * len(shape))
    return pl.pallas_call(
        functools.partial(_hgrn_kernel, tb=tb),
        grid=(t // tb,),
        in_specs=[pl.BlockSpec((tb, WIDTH), lambda i: (i, Z1_QH)),
                  pl.BlockSpec((tb, WIDTH), lambda i: (i, 0)),
                  pl.BlockSpec((tb, WIDTH), lambda i: (i, Z1_IH)),
                  pl.BlockSpec((tb, WIDTH), lambda i: (i, Z1_GH)),
                  full((1, WIDTH)), full((1, WIDTH)), full((CHUNK, CHUNK))],
        out_specs=pl.BlockSpec((tb, WIDTH), lambda i: (i, 0)),
        out_shape=jax.ShapeDtypeStruct((t, WIDTH), F32),
        scratch_shapes=[pltpu.VMEM((HEADS, HEAD_DIM, HEAD_DIM), F32)],
        compiler_params=_cparams(("arbitrary",)),
        name="hgrn2",
    )(z1, z2, z1, z1, lb.reshape(1, WIDTH).astype(F32), norm_w.reshape(1, WIDTH).astype(F32), tril)


def _s5_tables(lam_re, lam_im, log_dt, b_re, b_im, c_re, c_im):
    ln = S5_CHUNK
    lr = jnp.minimum(lam_re.astype(F32), S5_MAX_REAL)
    li = lam_im.astype(F32)
    dt = jnp.exp(log_dt.astype(F32))[:, None]
    mag = jnp.exp(lr * dt)
    ab_re = mag * jnp.cos(li * dt)
    ab_im = mag * jnp.sin(li * dt)
    nr = ab_re - 1.0
    den = lr * lr + li * li
    cr = (nr * lr + ab_im * li) / den
    ci = (ab_im * lr - nr * li) / den
    bb_re = cr[..., None] * b_re - ci[..., None] * b_im
    bb_im = cr[..., None] * b_im + ci[..., None] * b_re
    tau = jnp.arange(ln + 1, dtype=F32)[:, None, None]
    pm = jnp.exp(lr * dt * tau)
    pr = pm * jnp.cos(li * dt * tau)
    pi = pm * jnp.sin(li * dt * tau)
    w1 = c_re[:, :, :, None] * bb_re[:, None, :, :] - c_im[:, :, :, None] * bb_im[:, None, :, :]
    w2 = c_re[:, :, :, None] * bb_im[:, None, :, :] + c_im[:, :, :, None] * bb_re[:, None, :, :]
    kk = (jnp.einsum('gopc,tgp->gtco', w1, pr[:ln], precision=HIGHEST)
          - jnp.einsum('gopc,tgp->gtco', w2, pi[:ln], precision=HIGHEST))
    kz = jnp.concatenate([kk, jnp.zeros_like(kk[:, :1])], axis=1)
    s_i = jnp.arange(ln)[:, None]
    t_i = jnp.arange(ln)[None, :]
    idx = jnp.where(t_i >= s_i, t_i - s_i, ln)
    m = kz[:, idx]
    m = m.transpose(0, 1, 3, 2, 4).reshape(S5_GROUPS, ln * S5_GROUP, ln * S5_GROUP)
    pe_r = pr[ln - 1 - jnp.arange(ln)]
    pe_i = pi[ln - 1 - jnp.arange(ln)]
    e_re = pe_r[:, :, :, None] * bb_re[None] - pe_i[:, :, :, None] * bb_im[None]
    e_im = pe_r[:, :, :, None] * bb_im[None] + pe_i[:, :, :, None] * bb_re[None]
    e_re = e_re.transpose(1, 0, 3, 2).reshape(S5_GROUPS, ln * S5_GROUP, S5_STATE)
    e_im = e_im.transpose(1, 0, 3, 2).reshape(S5_GROUPS, ln * S5_GROUP, S5_STATE)
    pf_r = pr[1:].transpose(1, 2, 0)
    pf_i = pi[1:].transpose(1, 2, 0)
    cre = c_re.transpose(0, 2, 1)
    cim = c_im.transpose(0, 2, 1)
    f_re = cre[:, :, None, :] * pf_r[..., None] - cim[:, :, None, :] * pf_i[..., None]
    f_im = -(cre[:, :, None, :] * pf_i[..., None] + cim[:, :, None, :] * pf_r[..., None])
    f_re = f_re.reshape(S5_GROUPS, S5_STATE, ln * S5_GROUP)
    f_im = f_im.reshape(S5_GROUPS, S5_STATE, ln * S5_GROUP)
    rows = ln * S5_GROUP
    zero_e = jnp.zeros((S5_PAIRS, rows, S5_STATE), F32)
    e0r, e1r, e0i, e1i = e_re[0::2], e_re[1::2], e_im[0::2], e_im[1::2]
    e_pair = jnp.concatenate([jnp.concatenate([e0r, zero_e, e0i, zero_e], axis=2),
                              jnp.concatenate([zero_e, e1r, zero_e, e1i], axis=2)], axis=1)
    zero_f = jnp.zeros((S5_PAIRS, S5_STATE, rows), F32)
    fr_pair = jnp.concatenate([jnp.concatenate([f_re[0::2], zero_f], axis=2),
                               jnp.concatenate([zero_f, f_re[1::2]], axis=2)], axis=1)
    fi_pair = jnp.concatenate([jnp.concatenate([f_im[0::2], zero_f], axis=2),
                               jnp.concatenate([zero_f, f_im[1::2]], axis=2)], axis=1)
    al = jnp.zeros((S5_PAIRS, SUBLANES, LANES), F32)
    al = al.at[:, 0, :].set(pr[ln].reshape(S5_PAIRS, 2 * S5_STATE))
    al = al.at[:, 1, :].set(pi[ln].reshape(S5_PAIRS, 2 * S5_STATE))
    return m.astype(BF16), e_pair.astype(BF16), fr_pair.astype(BF16), fi_pair.astype(BF16), al


def _s5_kernel(u_ref, m_ref, e_ref, fr_ref, fi_ref, al_ref, o_ref, sloc, xpr, xpi, *, nc):
    rows = S5_CHUNK * S5_GROUP
    u = u_ref[0]
    sloc[...] = jnp.dot(u, e_ref[0], preferred_element_type=F32)
    ar = al_ref[0, 0:1, :]
    ai = al_ref[0, 1:2, :]

    row = lax.broadcasted_iota(jnp.int32, (SUBLANES, LANES), 0)

    def body(ti, carry):
        xr, xi = carry
        r0 = pl.multiple_of(ti * SUBLANES, SUBLANES)
        sr_t = sloc[pl.ds(r0, SUBLANES), 0:LANES]
        si_t = sloc[pl.ds(r0, SUBLANES), LANES:2 * LANES]
        pr_t = jnp.zeros((SUBLANES, LANES), F32)
        pi_t = jnp.zeros((SUBLANES, LANES), F32)
        for r in range(SUBLANES):
            pr_t = jnp.where(row == r, xr, pr_t)
            pi_t = jnp.where(row == r, xi, pi_t)
            xr, xi = ar * xr - ai * xi + sr_t[r:r + 1, :], ar * xi + ai * xr + si_t[r:r + 1, :]
        xpr[pl.ds(r0, SUBLANES), :] = pr_t
        xpi[pl.ds(r0, SUBLANES), :] = pi_t
        return xr, xi

    zero = jnp.zeros((1, LANES), F32)
    lax.fori_loop(0, nc // SUBLANES, body, (zero, zero))
    y = (jnp.dot(xpr[...].astype(BF16), fr_ref[0], preferred_element_type=F32)
         + jnp.dot(xpi[...].astype(BF16), fi_ref[0], preferred_element_type=F32))
    o_ref[0, :, 0:rows] = y[:, 0:rows] + jnp.dot(u[:, 0:rows], m_ref[0], preferred_element_type=F32)
    o_ref[0, :, rows:2 * rows] = y[:, rows:2 * rows] + jnp.dot(u[:, rows:2 * rows], m_ref[1],
                                                              preferred_element_type=F32)


def _s5(z1, lam_re, lam_im, log_dt, b_re, b_im, c_re, c_im):
    t = z1.shape[0]
    ln = S5_CHUNK
    nc = t // ln
    rows = ln * S5_GROUP
    m, e_pair, fr_pair, fi_pair, al = _s5_tables(lam_re, lam_im, log_dt, b_re, b_im, c_re, c_im)
    u = z1[:, Z1_US * WIDTH:(Z1_US + 1) * WIDTH].astype(BF16)
    u = u.reshape(nc, ln, S5_PAIRS, 2, S5_GROUP).transpose(2, 0, 3, 1, 4).reshape(S5_PAIRS, nc, 2 * rows)
    y = pl.pallas_call(
        functools.partial(_s5_kernel, nc=nc),
        grid=(S5_PAIRS,),
        in_specs=[pl.BlockSpec((1, nc, 2 * rows), lambda j: (j, 0, 0)),
                  pl.BlockSpec((2, rows, rows), lambda j: (j, 0, 0)),
                  pl.BlockSpec((1, 2 * rows, 2 * LANES), lambda j: (j, 0, 0)),
                  pl.BlockSpec((1, LANES, 2 * rows), lambda j: (j, 0, 0)),
                  pl.BlockSpec((1, LANES, 2 * rows), lambda j: (j, 0, 0)),
                  pl.BlockSpec((1, SUBLANES, LANES), lambda j: (j, 0, 0))],
        out_specs=pl.BlockSpec((1, nc, 2 * rows), lambda j: (j, 0, 0)),
        out_shape=jax.ShapeDtypeStruct((S5_PAIRS, nc, 2 * rows), F32),
        scratch_shapes=[pltpu.VMEM((nc, 2 * LANES), F32), pltpu.VMEM((nc, LANES), F32), pltpu.VMEM((nc, LANES), F32)],
        compiler_params=_cparams(("parallel",)),
        name="s5_scan",
    )(u, m, e_pair, fr_pair, fi_pair, al)
    return y.reshape(S5_PAIRS, nc, 2, ln, S5_GROUP).transpose(1, 3, 0, 2, 4).reshape(t, WIDTH)


def _layer_norm(x, g, b):
    mu = jnp.mean(x, axis=-1, keepdims=True)
    xc = x - mu
    var = jnp.mean(xc * xc, axis=-1, keepdims=True)
    return xc * lax.rsqrt(var + LN_EPS) * g + b


def _merge_kernel(x_ref, g0_ref, g1_ref, g2_ref, ym_ref, ys_ref, us_ref, yh_ref, wm_ref, ws_ref, wh_ref, wglu_ref,
                  d_ref, wo_ref, lg_ref, lb_ref, rw_ref, rb_ref, o_ref):
    ys = _gelu_tanh(ys_ref[...] + d_ref[...] * us_ref[...].astype(F32))
    ys = ys * _sigmoid(_dot(ys, wglu_ref[...]))
    merged = (_sigmoid(g0_ref[...].astype(F32)) * _dot(ym_ref[...], wm_ref[...])
              + _sigmoid(g1_ref[...].astype(F32)) * _dot(ys, ws_ref[...])
              + _sigmoid(g2_ref[...].astype(F32)) * _dot(yh_ref[...], wh_ref[...]))
    x1 = _layer_norm(DN_ALPHA * x_ref[...] + _dot(merged, wo_ref[...]), lg_ref[...], lb_ref[...])
    o_ref[:, 0:D_MODEL] = x1

    xh = x1.astype(BF16)
    xl = (x1 - xh.astype(F32)).astype(BF16)
    logits = (jnp.dot(xh, rw_ref[0], preferred_element_type=F32) + jnp.dot(xl, rw_ref[0], preferred_element_type=F32)
              + jnp.dot(xh, rw_ref[1], preferred_element_type=F32))
    s0 = _sigmoid(logits)
    sb0 = s0 + rb_ref[...]
    shift = lambda a, j: a if j == 0 else pltpu.roll(a, LANES - N_GROUPS * j, 1)
    s = [shift(s0, j) for j in range(EXPERTS_PER_GROUP)]
    sb = [shift(sb0, j) for j in range(EXPERTS_PER_GROUP)]
    hi1, lo1 = jnp.maximum(sb[0], sb[1]), jnp.minimum(sb[0], sb[1])
    hi2, lo2 = jnp.maximum(sb[2], sb[3]), jnp.minimum(sb[2], sb[3])
    top2 = jnp.maximum(hi1, hi2) + jnp.maximum(jnp.minimum(hi1, hi2), jnp.maximum(lo1, lo2))
    lane = lax.broadcasted_iota(jnp.int32, top2.shape, 1)
    top2 = jnp.where(lane < N_GROUPS, top2, -jnp.inf)
    gmax = jnp.max(top2, axis=1, keepdims=True)
    g_idx = jnp.min(jnp.where(top2 == gmax, lane, LANES), axis=1, keepdims=True)
    sel = lane == g_idx
    v = [jnp.sum(jnp.where(sel, sb[j], 0.0), axis=1, keepdims=True) for j in range(EXPERTS_PER_GROUP)]
    sv = [jnp.sum(jnp.where(sel, s[j], 0.0), axis=1, keepdims=True) for j in range(EXPERTS_PER_GROUP)]

    def first_max(vals):
        m = jnp.maximum(jnp.maximum(vals[0], vals[1]), jnp.maximum(vals[2], vals[3]))
        return jnp.where(vals[0] == m, 0, jnp.where(vals[1] == m, 1, jnp.where(vals[2] == m, 2, 3)))

    e1 = first_max(v)
    e2 = first_max([jnp.where(e1 == j, -jnp.inf, v[j]) for j in range(EXPERTS_PER_GROUP)])
    s1 = sum(jnp.where(e1 == j, sv[j], 0.0) for j in range(EXPERTS_PER_GROUP))
    s2 = sum(jnp.where(e2 == j, sv[j], 0.0) for j in range(EXPERTS_PER_GROUP))
    tot = s1 + s2
    meta = jnp.where(lane == 0, g_idx.astype(F32), 0.0)
    for j in range(EXPERTS_PER_GROUP):
        cw = jnp.where(e1 == j, s1 / tot, 0.0) + jnp.where(e2 == j, s2 / tot, 0.0)
        meta = jnp.where(lane == 1 + j, cw, meta)
    o_ref[:, D_MODEL:D_MODEL + META] = meta


def _merge(x, z1, ym, ys, yh, wm, ws, wh, wglu, d, wo, lg, lb, router_w, router_bias, tm):
    t = x.shape[0]
    rw = router_w.astype(F32).reshape(D_MODEL, N_GROUPS, EXPERTS_PER_GROUP).transpose(0, 2, 1)
    rw = jnp.pad(rw.reshape(D_MODEL, N_EXPERTS), ((0, 0), (0, LANES - N_EXPERTS)))
    rw_hi = rw.astype(BF16)
    rw = jnp.stack([rw_hi, (rw - rw_hi.astype(F32)).astype(BF16)])
    rb = jnp.pad(router_bias.astype(F32).reshape(N_GROUPS, EXPERTS_PER_GROUP).T.reshape(1, N_EXPERTS),
                 ((0, 0), (0, LANES - N_EXPERTS)))
    full = lambda shape: pl.BlockSpec(shape, lambda i: (0,) * len(shape))
    row = lambda w: pl.BlockSpec((tm, w), lambda i: (i, 0))
    return pl.pallas_call(
        _merge_kernel,
        grid=(t // tm,),
        in_specs=[row(D_MODEL),
                  pl.BlockSpec((tm, D_MODEL), lambda i: (i, 0)),
                  pl.BlockSpec((tm, D_MODEL), lambda i: (i, 1)),
                  pl.BlockSpec((tm, D_MODEL), lambda i: (i, 2)),
                  row(WIDTH), row(WIDTH), pl.BlockSpec((tm, WIDTH), lambda i: (i, Z1_US)), row(WIDTH),
                  full((WIDTH, D_MODEL)), full((WIDTH, D_MODEL)), full((WIDTH, D_MODEL)), full((WIDTH, WIDTH)),
                  full((1, WIDTH)), full((D_MODEL, D_MODEL)), full((1, D_MODEL)), full((1, D_MODEL)),
                  full((2, D_MODEL, LANES)), full((1, LANES))],
        out_specs=pl.BlockSpec((tm, D_MODEL + META), lambda i: (i, 0)),
        out_shape=jax.ShapeDtypeStruct((t, D_MODEL + META), F32),
        compiler_params=_cparams(("parallel",)),
        name="merge_router",
    )(x, z1, z1, z1, ym, ys, z1, yh, wm.astype(BF16), ws.astype(BF16), wh.astype(BF16), wglu.astype(BF16),
      d.reshape(1, WIDTH).astype(F32), wo.astype(BF16), lg.reshape(1, D_MODEL).astype(F32),
      lb.reshape(1, D_MODEL).astype(F32), rw, rb)


def _start_row_gather(idx_ref, base, src_hbm, dst, sem, rows):
    for r in range(rows):
        pltpu.make_async_copy(src_hbm.at[pl.ds(idx_ref[base + r], 1)], dst.at[pl.ds(r, 1)], sem).start(priority=r % 2)


def _wait_row_gather(src_hbm, dst, sem, rows):
    pltpu.make_async_copy(src_hbm.at[pl.ds(0, rows)], dst, sem).wait()


def _moe_kernel(tg_ref, src_ref, x_hbm, wg_ref, wu_ref, wd_ref, o_ref, buf, sem, wg_s, wu_s, wd_s, *, tm, ntiles):
    i = pl.program_id(0)
    slot = i % 2

    @pl.when(i == 0)
    def _():
        _start_row_gather(src_ref, 0, x_hbm, buf.at[0], sem.at[0], tm)

    @pl.when(i + 1 < ntiles)
    def _():
        _start_row_gather(src_ref, (i + 1) * tm, x_hbm, buf.at[1 - slot], sem.at[1 - slot], tm)

    @pl.when(jnp.logical_or(i == 0, tg_ref[i] != tg_ref[jnp.maximum(i - 1, 0)]))
    def _():
        for e in range(EXPERTS_PER_GROUP):
            wg_s[e] = wg_ref[0, e].astype(BF16)
            wu_s[e] = wu_ref[0, e].astype(BF16)
            wd_s[e] = wd_ref[0, e].astype(BF16)

    _wait_row_gather(x_hbm, buf.at[slot], sem.at[slot], tm)
    xt = buf[slot]
    xb = xt[:, 0:D_MODEL].astype(BF16)
    meta = xt[:, D_MODEL:D_MODEL + META]
    y = jnp.zeros((tm, D_MODEL), F32)
    for e in range(EXPERTS_PER_GROUP):
        hg = jnp.dot(xb, wg_s[e], preferred_element_type=F32)
        hu = jnp.dot(xb, wu_s[e], preferred_element_type=F32)
        hh = _silu(hg) * hu * _lane_col(meta, 1 + e)
        y = y + jnp.dot(hh.astype(BF16), wd_s[e], preferred_element_type=F32)
    o_ref[...] = y


def _ln2_kernel(dst_ref, x_ref, y_hbm, lg_ref, lb_ref, o_ref, buf, sem, *, tm, ntiles):
    i = pl.program_id(0)
    slot = i % 2

    @pl.when(i == 0)
    def _():
        _start_row_gather(dst_ref, 0, y_hbm, buf.at[0], sem.at[0], tm)

    @pl.when(i + 1 < ntiles)
    def _():
        _start_row_gather(dst_ref, (i + 1) * tm, y_hbm, buf.at[1 - slot], sem.at[1 - slot], tm)

    _wait_row_gather(y_hbm, buf.at[slot], sem.at[slot], tm)
    o_ref[...] = _layer_norm(DN_ALPHA * x_ref[...] + buf[slot], lg_ref[...], lb_ref[...])


def _moe_ln(x1e, wg, wu, wd, layer, lg, lb, tm):
    t = x1e.shape[0]
    ntiles = t // tm + N_GROUPS
    p = ntiles * tm
    key = x1e[:, D_MODEL].astype(jnp.int32)
    onehot = (key[:, None] == jnp.arange(N_GROUPS)[None, :]).astype(jnp.int32)
    csum = jnp.cumsum(onehot, axis=0)
    counts = csum[-1]
    rank = jnp.sum(onehot * csum, axis=1) - 1
    pcount = ((counts + tm - 1) // tm) * tm
    pend = jnp.cumsum(pcount)
    dest = (pend - pcount)[key] + rank
    row_src = jnp.zeros((p,), jnp.int32).at[dest].set(jnp.arange(t, dtype=jnp.int32))
    tile_start = jnp.arange(ntiles, dtype=jnp.int32) * tm
    tile_group = jnp.minimum(jnp.sum((tile_start[:, None] >= pend[None, :]).astype(jnp.int32), axis=1),
                             N_GROUPS - 1)
    wspec = lambda a, b: pl.BlockSpec((1, EXPERTS_PER_GROUP, a, b), lambda i, tg, src: (layer, tg[i], 0, 0))
    wscr = lambda a, b: pltpu.VMEM((EXPERTS_PER_GROUP, a, b), BF16)
    y_sorted = pl.pallas_call(
        functools.partial(_moe_kernel, tm=tm, ntiles=ntiles),
        grid_spec=pltpu.PrefetchScalarGridSpec(
            num_scalar_prefetch=2,
            grid=(ntiles,),
            in_specs=[pl.BlockSpec(memory_space=pl.ANY), wspec(D_MODEL, D_EXPERT), wspec(D_MODEL, D_EXPERT),
                      wspec(D_EXPERT, D_MODEL)],
            out_specs=pl.BlockSpec((tm, D_MODEL), lambda i, tg, src: (i, 0)),
            scratch_shapes=[pltpu.VMEM((2, tm, D_MODEL + META), F32), pltpu.SemaphoreType.DMA((2,)),
                            wscr(D_MODEL, D_EXPERT), wscr(D_MODEL, D_EXPERT), wscr(D_EXPERT, D_MODEL)]),
        out_shape=jax.ShapeDtypeStruct((p, D_MODEL), F32),
        compiler_params=_cparams(("arbitrary",)),
        name="moe_experts",
    )(tile_group, row_src, x1e, wg, wu, wd)
    nt2 = t // tm
    return pl.pallas_call(
        functools.partial(_ln2_kernel, tm=tm, ntiles=nt2),
        grid_spec=pltpu.PrefetchScalarGridSpec(
            num_scalar_prefetch=1,
            grid=(nt2,),
            in_specs=[pl.BlockSpec((tm, D_MODEL), lambda i, d: (i, 0)), pl.BlockSpec(memory_space=pl.ANY),
                      pl.BlockSpec((1, D_MODEL), lambda i, d: (0, 0)), pl.BlockSpec((1, D_MODEL), lambda i, d: (0, 0))],
            out_specs=pl.BlockSpec((tm, D_MODEL), lambda i, d: (i, 0)),
            scratch_shapes=[pltpu.VMEM((2, tm, D_MODEL), F32), pltpu.SemaphoreType.DMA((2,))]),
        out_shape=jax.ShapeDtypeStruct((t, D_MODEL), F32),
        compiler_params=_cparams(("arbitrary",)),
        name="moe_combine_ln",
    )(dest.astype(jnp.int32), x1e, y_sorted, lg.reshape(1, D_MODEL).astype(F32), lb.reshape(1, D_MODEL).astype(F32))


def _layer(x, l, p, lb_l, cfg):
    t = x.shape[0]
    w1, w2, wt = _split_w_in(p['w_in'][l])
    xb = x.astype(BF16)
    z1 = _matmul(xb, w1, Z1_DTYPE, cfg['tm_in'], cfg['tn_in'])
    z2, gt = _inproj_gate(xb, w2, wt, cfg['tm_in'])
    gt3 = gt.reshape(SUBLANES, t // CHUNK, CHUNK).transpose(1, 0, 2)
    ym = _mlstm(z1, z2, gt3, p['mlstm_conv_w'][l], p['mlstm_conv_b'][l], p['mlstm_wq'][l], p['mlstm_wk'][l],
                p['mlstm_wv'][l], p['mlstm_b_i'][l], p['mlstm_b_f'][l], p['mlstm_skip'][l], cfg['tb'])
    ys = _s5(z1, p['s5_lambda_re'][l], p['s5_lambda_im'][l], p['s5_log_dt'][l], p['s5_b_re'][l], p['s5_b_im'][l],
             p['s5_c_re'][l], p['s5_c_im'][l])
    yh = _hgrn(z1, z2, lb_l, p['hgrn_norm_w'][l], cfg['tb'])
    x1e = _merge(x, z1, ym, ys, yh, p['w_branch_mlstm'][l], p['w_branch_s5'][l], p['w_branch_hgrn'][l],
                 p['s5_w_glu'][l], p['s5_d'][l], p['w_out'][l], p['ln1_g'][l], p['ln1_b'][l], p['router_w'],
                 p['router_bias'], cfg['tm_merge'])
    return _moe_ln(x1e, p['exp_w_gate'], p['exp_w_up'], p['exp_w_down'], l, p['ln2_g'][l], p['ln2_b'][l],
                   cfg['tm_moe'])


_CFG = dict(tm_in=1024, tn_in=1024, tb=512, tm_merge=256, tm_moe=256)


def kernel(x, w_in, mlstm_conv_w, mlstm_conv_b, mlstm_wq, mlstm_wk, mlstm_wv, mlstm_b_i, mlstm_b_f, mlstm_skip, s5_lambda_re, s5_lambda_im, s5_log_dt, s5_b_re, s5_b_im, s5_c_re, s5_c_im, s5_d, s5_w_glu, hgrn_lower_bounds, hgrn_norm_w, w_branch_mlstm, w_branch_s5, w_branch_hgrn, w_out, ln1_g, ln1_b, ln2_g, ln2_b, router_w, router_bias, exp_w_gate, exp_w_up, exp_w_down):
    p = dict(w_in=w_in, mlstm_conv_w=mlstm_conv_w, mlstm_conv_b=mlstm_conv_b, mlstm_wq=mlstm_wq, mlstm_wk=mlstm_wk,
             mlstm_wv=mlstm_wv, mlstm_b_i=mlstm_b_i, mlstm_b_f=mlstm_b_f, mlstm_skip=mlstm_skip,
             s5_lambda_re=s5_lambda_re, s5_lambda_im=s5_lambda_im, s5_log_dt=s5_log_dt, s5_b_re=s5_b_re,
             s5_b_im=s5_b_im, s5_c_re=s5_c_re, s5_c_im=s5_c_im, s5_d=s5_d, s5_w_glu=s5_w_glu,
             hgrn_norm_w=hgrn_norm_w, w_branch_mlstm=w_branch_mlstm, w_branch_s5=w_branch_s5,
             w_branch_hgrn=w_branch_hgrn, w_out=w_out, ln1_g=ln1_g, ln1_b=ln1_b, ln2_g=ln2_g, ln2_b=ln2_b,
             router_w=router_w, router_bias=router_bias, exp_w_gate=exp_w_gate, exp_w_up=exp_w_up,
             exp_w_down=exp_w_down)
    lb_cum = jnp.cumsum(jax.nn.softmax(hgrn_lower_bounds.astype(F32), axis=0), axis=0)
    lb_layers = lb_cum - lb_cum[0]
    bsz, seq, d = x.shape
    h = x.reshape(bsz * seq, d)
    for l in range(DEPTH):
        h = _layer(h, l, p, lb_layers[l], _CFG)
    return h.reshape(bsz, seq, d)
```

```python
import functools
import math

import jax
import jax.numpy as jnp
from jax import lax
from jax.experimental import pallas as pl
from jax.experimental.pallas import tpu as pltpu

F32 = jnp.float32
BF16 = jnp.bfloat16
HIGHEST = lax.Precision.HIGHEST

D_MODEL = 1024
DEPTH = 2
HEADS = 4
HEAD_DIM = 128
WIDTH = HEADS * HEAD_DIM
MLSTM_CONV = 4
CHUNK = 64
S5_GROUP = 16
S5_GROUPS = 32
S5_STATE = 64
S5_PAIRS = S5_GROUPS // 2
S5_CHUNK = 32
S5_MAX_REAL = -1e-4
N_EXPERTS = 32
N_GROUPS = 8
EXPERTS_PER_GROUP = 4
D_EXPERT = 256
DN_ALPHA = (2 * DEPTH) ** 0.25
LN_EPS = 1e-5
NORM_EPS = 1e-6

LANES = 128
SUBLANES = 8
META = LANES
VMEM_LIMIT = 56 * 1024 * 1024

Z1_DTYPE = BF16
Z1_GATE, Z1_XM, Z1_OM, Z1_US, Z1_QH, Z1_IH, Z1_GH = 0, 6, 7, 8, 9, 10, 11
Z1_COLS = 12 * WIDTH
Z2_COLS = WIDTH + LANES


def _cparams(sem):
    return pltpu.CompilerParams(dimension_semantics=sem, vmem_limit_bytes=VMEM_LIMIT)


def _sigmoid(x):
    return 0.5 * (1.0 + jnp.tanh(0.5 * x))


def _silu(x):
    return x * _sigmoid(x)


def _log_sigmoid(x):
    return jnp.minimum(x, 0.0) - jnp.log(1.0 + jnp.exp(-jnp.abs(x)))


def _gelu_tanh(x):
    return 0.5 * x * (1.0 + jnp.tanh(math.sqrt(2.0 / math.pi) * (x + 0.044715 * (x * x * x))))


def _cumsum_rows(x):
    n = x.shape[0]
    row = lax.broadcasted_iota(jnp.int32, x.shape, 0)
    s = 1
    while s < n:
        x = x + jnp.where(row >= s, pltpu.roll(x, s, 0), 0.0)
        s *= 2
    return x


def _lane_col(x, idx):
    lane = lax.broadcasted_iota(jnp.int32, x.shape, 1)
    return jnp.sum(jnp.where(lane == idx, x, 0.0), axis=1, keepdims=True)


def _dot(a, b):
    return jnp.dot(a.astype(BF16), b.astype(BF16), preferred_element_type=F32)


def _dot_nt(a, b):
    return lax.dot_general(a.astype(BF16), b.astype(BF16), (((1,), (1,)), ((), ())), preferred_element_type=F32)


def _dot_tn(a, b):
    return lax.dot_general(a.astype(BF16), b.astype(BF16), (((0,), (0,)), ((), ())), preferred_element_type=F32)


def _mm_kernel(x_ref, w_ref, o_ref):
    o_ref[...] = jnp.dot(x_ref[...], w_ref[...], preferred_element_type=F32).astype(o_ref.dtype)


def _matmul(x, w, out_dtype, tm, tn):
    m, k = x.shape
    n = w.shape[1]
    return pl.pallas_call(
        _mm_kernel,
        grid=(m // tm, n // tn),
        in_specs=[pl.BlockSpec((tm, k), lambda i, j: (i, 0)), pl.BlockSpec((k, tn), lambda i, j: (0, j))],
        out_specs=pl.BlockSpec((tm, tn), lambda i, j: (i, j)),
        out_shape=jax.ShapeDtypeStruct((m, n), out_dtype),
        compiler_params=_cparams(("parallel", "parallel")),
        name="in_proj_wide",
    )(x, w)


def _inproj_gate_kernel(x_ref, w_ref, wt_ref, o_ref, gt_ref):
    x = x_ref[...]
    o_ref[...] = jnp.dot(x, w_ref[...], preferred_element_type=F32)
    gt_ref[...] = lax.dot_general(wt_ref[...], x, (((1,), (1,)), ((), ())), preferred_element_type=F32)


def _inproj_gate(x, w2, wt, tm):
    m, k = x.shape
    return pl.pallas_call(
        _inproj_gate_kernel,
        grid=(m // tm,),
        in_specs=[pl.BlockSpec((tm, k), lambda i: (i, 0)),
                  pl.BlockSpec((k, Z2_COLS), lambda i: (0, 0)),
                  pl.BlockSpec((SUBLANES, k), lambda i: (0, 0))],
        out_specs=[pl.BlockSpec((tm, Z2_COLS), lambda i: (i, 0)), pl.BlockSpec((SUBLANES, tm), lambda i: (0, i))],
        out_shape=[jax.ShapeDtypeStruct((m, Z2_COLS), F32), jax.ShapeDtypeStruct((SUBLANES, m), F32)],
        compiler_params=_cparams(("parallel",)),
        name="in_proj_gates",
    )(x, w2, wt)


def _split_w_in(w):
    offs, o = [], 0
    for s in (WIDTH, WIDTH, HEADS, HEADS, WIDTH, WIDTH, WIDTH, WIDTH, WIDTH, 3 * D_MODEL):
        offs.append((o, o + s))
        o += s
    seg = [w[:, a:b] for a, b in offs]
    xm, om, im, fm, us, qh, fh, ih, gh, gate = seg
    w1 = jnp.concatenate([gate, xm, om, us, qh, ih, gh], axis=1).astype(BF16)
    pad = jnp.zeros((w.shape[0], LANES - 2 * HEADS), w.dtype)
    w2 = jnp.concatenate([fh, im, fm, pad], axis=1).astype(BF16)
    wt = jnp.concatenate([im, fm], axis=1).T.astype(BF16)
    return w1, w2, wt


def _mlstm_kernel(xm_ref, om_ref, gc_ref, gr_ref, cw_ref, cb_ref, wq_ref, wk_ref, wv_ref, bcol_ref, brow_ref,
                  skip_ref, triu_ref, o_ref, xpad, q_s, k_s, v_s, xc_s, h_s, gcol_s, grow_s, brow_s, ct_s, n_s, m_s,
                  *, tb):
    ncb = tb // CHUNK

    @pl.when(pl.program_id(0) == 0)
    def _():
        xpad[0:SUBLANES, :] = jnp.zeros((SUBLANES, WIDTH), F32)
        ct_s[...] = jnp.zeros_like(ct_s)
        n_s[...] = jnp.zeros_like(n_s)
        m_s[...] = jnp.zeros_like(m_s)

    xm = xm_ref[...].astype(F32)
    xpad[SUBLANES:SUBLANES + tb, :] = xm
    cw = cw_ref[...]
    conv = cb_ref[...] + cw[3:4, :] * xm
    for d in range(1, MLSTM_CONV):
        conv = conv + cw[3 - d:4 - d, :] * xpad[SUBLANES - d:SUBLANES - d + tb, :]
    xpad[0:SUBLANES, :] = xpad[tb:tb + SUBLANES, :]
    xc = _silu(conv)
    xc_s[...] = xc
    for h in range(HEADS):
        sl = slice(h * HEAD_DIM, (h + 1) * HEAD_DIM)
        xch = xc[:, sl].astype(BF16)
        q_s[:, sl] = jnp.dot(xch, wq_ref[h], preferred_element_type=F32) * (HEAD_DIM ** -0.5)
        k_s[:, sl] = jnp.dot(xch, wk_ref[h], preferred_element_type=F32)
        v_s[:, sl] = jnp.dot(xm[:, sl].astype(BF16), wv_ref[h], preferred_element_type=F32)

    gc = gc_ref[...] + bcol_ref[...]
    lane = lax.broadcasted_iota(jnp.int32, gc.shape, 1)
    gcol_s[...] = jnp.where(lane < HEADS, gc, _log_sigmoid(gc))
    gr = gr_ref[...] + brow_ref[...]
    sub = lax.broadcasted_iota(jnp.int32, gr.shape, 1)
    gr = jnp.where(sub < HEADS, gr, _log_sigmoid(gr)).reshape(ncb * SUBLANES, CHUNK)
    grow_s[...] = gr
    brow_s[...] = jnp.dot(gr, triu_ref[...], precision=HIGHEST, preferred_element_type=F32)

    rowi = lax.broadcasted_iota(jnp.int32, (CHUNK, CHUNK), 0)
    coli = lax.broadcasted_iota(jnp.int32, (CHUNK, CHUNK), 1)
    causal = rowi >= coli

    def chunk_body(c, carry):
        r0 = pl.multiple_of(c * CHUNK, CHUNK)
        g0 = pl.multiple_of(c * SUBLANES, SUBLANES)
        gcol = gcol_s[pl.ds(r0, CHUNK), :]
        bcol = _cumsum_rows(gcol)
        grow = grow_s[pl.ds(g0, SUBLANES), :]
        brow = brow_s[pl.ds(g0, SUBLANES), :]
        for h in range(HEADS):
            sl = slice(h * HEAD_DIM, (h + 1) * HEAD_DIM)
            li_c = _lane_col(gcol, h)
            b_c = _lane_col(bcol, HEADS + h)
            li_r = grow[h:h + 1, :]
            b_r = brow[HEADS + h:HEADS + h + 1, :]
            b_tot = b_c[CHUNK - 1:CHUNK, :]
            a_c = b_tot - b_c + li_c
            a_max = jnp.max(a_c, axis=0, keepdims=True)
            q = q_s[pl.ds(r0, CHUNK), sl]
            k = k_s[pl.ds(r0, CHUNK), sl]
            v = v_s[pl.ds(r0, CHUNK), sl]
            wk = jnp.exp(a_c - a_max) * k
            c_loc_t = _dot_tn(v, wk)
            n_loc = jnp.sum(wk, axis=0, keepdims=True)
            ct_prev = ct_s[h]
            n_prev = n_s[h:h + 1, :]
            m_prev = m_s[h:h + 1, 0:1]
            d_mat = jnp.where(causal, b_c - b_r + li_r, -jnp.inf)
            m_inter = b_c + m_prev
            m_j = jnp.maximum(m_inter, jnp.max(d_mat, axis=1, keepdims=True))
            sc = _dot_nt(q, k) * jnp.exp(d_mat - m_j)
            g_inter = jnp.exp(m_inter - m_j)
            num = g_inter * _dot_nt(q, ct_prev) + _dot(sc, v)
            den = g_inter * jnp.sum(q * n_prev, axis=1, keepdims=True) + jnp.sum(sc, axis=1, keepdims=True)
            h_s[pl.ds(r0, CHUNK), sl] = num / jnp.maximum(jnp.abs(den), jnp.exp(-m_j))
            m_new = jnp.maximum(b_tot + m_prev, a_max)
            g_old = jnp.exp(b_tot + m_prev - m_new)
            g_loc = jnp.exp(a_max - m_new)
            ct_s[h] = g_old * ct_prev + g_loc * c_loc_t
            n_s[h:h + 1, :] = g_old * n_prev + g_loc * n_loc
            m_s[h:h + 1, :] = jnp.broadcast_to(m_new, (1, LANES))
        return carry

    lax.fori_loop(0, ncb, chunk_body, 0, unroll=2)
    o_ref[...] = (_sigmoid(om_ref[...].astype(F32)) * h_s[...] + skip_ref[...] * xc_s[...]).astype(o_ref.dtype)


def _mlstm(z1, z2, gt3, conv_w, conv_b, wq, wk, wv, b_i, b_f, skip, tb):
    t = z1.shape[0]
    ncb = tb // CHUNK
    bias = jnp.concatenate([b_i, b_f]).astype(F32)
    bcol = jnp.zeros((1, LANES), F32).at[0, :2 * HEADS].set(bias)
    brow = bias.reshape(2 * HEADS, 1)
    triu = jnp.triu(jnp.ones((CHUNK, CHUNK), F32))
    full = lambda shape: pl.BlockSpec(shape, lambda i: (0,) * len(shape))
    return pl.pallas_call(
        functools.partial(_mlstm_kernel, tb=tb),
        grid=(t // tb,),
        in_specs=[pl.BlockSpec((tb, WIDTH), lambda i: (i, Z1_XM)),
                  pl.BlockSpec((tb, WIDTH), lambda i: (i, Z1_OM)),
                  pl.BlockSpec((tb, LANES), lambda i: (i, WIDTH // LANES)),
                  pl.BlockSpec((ncb, SUBLANES, CHUNK), lambda i: (i, 0, 0)),
                  full((MLSTM_CONV, WIDTH)), full((1, WIDTH)),
                  full((HEADS, HEAD_DIM, HEAD_DIM)), full((HEADS, HEAD_DIM, HEAD_DIM)),
                  full((HEADS, HEAD_DIM, HEAD_DIM)),
                  full((1, LANES)), full((2 * HEADS, 1)), full((1, WIDTH)), full((CHUNK, CHUNK))],
        out_specs=pl.BlockSpec((tb, WIDTH), lambda i: (i, 0)),
        out_shape=jax.ShapeDtypeStruct((t, WIDTH), F32),
        scratch_shapes=[pltpu.VMEM((tb + SUBLANES, WIDTH), F32)]
        + [pltpu.VMEM((tb, WIDTH), F32) for _ in range(5)]
        + [pltpu.VMEM((tb, LANES), F32),
           pltpu.VMEM((ncb * SUBLANES, CHUNK), F32), pltpu.VMEM((ncb * SUBLANES, CHUNK), F32),
           pltpu.VMEM((HEADS, HEAD_DIM, HEAD_DIM), F32), pltpu.VMEM((SUBLANES, LANES), F32),
           pltpu.VMEM((SUBLANES, LANES), F32)],
        compiler_params=_cparams(("arbitrary",)),
        name="mlstm",
    )(z1, z1, z2, gt3, conv_w.astype(F32), conv_b.reshape(1, WIDTH).astype(F32), wq.astype(BF16), wk.astype(BF16),
      wv.astype(BF16), bcol, brow, skip.reshape(1, WIDTH).astype(F32), triu)


def _hgrn_kernel(q_ref, f_ref, i_ref, g_ref, lb_ref, nw_ref, tri_ref, o_ref, st_s, *, tb):
    ncb = tb // CHUNK

    @pl.when(pl.program_id(0) == 0)
    def _():
        st_s[...] = jnp.zeros_like(st_s)

    lb = lb_ref[...]
    nw = nw_ref[...]
    tri = tri_ref[...]
    rowi = lax.broadcasted_iota(jnp.int32, (CHUNK, WIDTH), 0)
    sr = lax.broadcasted_iota(jnp.int32, (CHUNK, CHUNK), 0)
    sc = lax.broadcasted_iota(jnp.int32, (CHUNK, CHUNK), 1)
    halves = [1 << p for p in range(CHUNK.bit_length() - 1)]
    upper = {m: (rowi & m) != 0 for m in halves}
    same_blk = {m: (sr // (2 * m)) == (sc // (2 * m)) for m in halves}

    def chunk_body(c, carry):
        r0 = pl.multiple_of(c * CHUNK, CHUNK)
        f = lb + (1.0 - lb) * _sigmoid(f_ref[pl.ds(r0, CHUNK), :].astype(F32))
        k = 1.0 - f
        q = _silu(q_ref[pl.ds(r0, CHUNK), :].astype(F32))
        v = i_ref[pl.ds(r0, CHUNK), :].astype(F32)
        lf = jnp.log(f)
        hi = lf.astype(BF16)
        r1 = lf - hi.astype(F32)
        mid = r1.astype(BF16)
        lo = (r1 - mid.astype(F32)).astype(BF16)
        b = (jnp.dot(tri, hi, preferred_element_type=F32) + jnp.dot(tri, mid, preferred_element_type=F32)
             + jnp.dot(tri, lo, preferred_element_type=F32))
        qs, ks = {}, {}
        for m in halves:
            if m == 1:
                t = jnp.where(upper[m], q * f, k)
            else:
                nblk = CHUNK // (2 * m)
                r = jnp.concatenate([jnp.broadcast_to(b[2 * m * j + m - 1:2 * m * j + m, :], (2 * m, WIDTH))
                                     for j in range(nblk)], axis=0)
                t = jnp.where(upper[m], q, k) * jnp.exp(-jnp.abs(b - r))
            qs[m] = jnp.where(upper[m], t, 0.0).astype(BF16)
            ks[m] = jnp.where(upper[m], 0.0, t).astype(BF16)
        b_last = b[CHUNK - 1:CHUNK, :]
        qe = q * jnp.exp(b)
        kd = k * jnp.exp(b_last - b)
        e_last = jnp.exp(b_last)
        outs = []
        for h in range(HEADS):
            sl = slice(h * HEAD_DIM, (h + 1) * HEAD_DIM)
            st = st_s[h]
            a = jnp.where(sr == sc, _dot_nt(q[:, sl], k[:, sl]), 0.0)
            for m in halves:
                a = a + jnp.where(same_blk[m], _dot_nt(qs[m][:, sl], ks[m][:, sl]), 0.0)
            oh = _dot_nt(qe[:, sl], st) + _dot(a, v[:, sl])
            st_s[h] = e_last[:, sl] * st + _dot_tn(v[:, sl], kd[:, sl])
            ms = jnp.sum(oh * oh, axis=1, keepdims=True) * (1.0 / HEAD_DIM)
            outs.append(oh * lax.rsqrt(ms + NORM_EPS))
        on = jnp.concatenate(outs, axis=1)
        g = g_ref[pl.ds(r0, CHUNK), :].astype(F32)
        o_ref[pl.ds(r0, CHUNK), :] = (on * nw * _silu(g)).astype(o_ref.dtype)
        return carry

    lax.fori_loop(0, ncb, chunk_body, 0, unroll=2)


def _hgrn(z1, z2, lb, norm_w, tb):
    t = z1.shape[0]
    tril = jnp.tril(jnp.ones((CHUNK, CHUNK), F32)).astype(BF16)
    full = lambda shape: pl.BlockSpec(shape, lambda i: (0,) * len(shape))
    return pl.pallas_call(
        functools.partial(_hgrn_kernel, tb=tb),
        grid=(t // tb,),
        in_specs=[pl.BlockSpec((tb, WIDTH), lambda i: (i, Z1_QH)),
                  pl.BlockSpec((tb, WIDTH), lambda i: (i, 0)),
                  pl.BlockSpec((tb, WIDTH), lambda i: (i, Z1_IH)),
                  pl.BlockSpec((tb, WIDTH), lambda i: (i, Z1_GH)),
                  full((1, WIDTH)), full((1, WIDTH)), full((CHUNK, CHUNK))],
        out_specs=pl.BlockSpec((tb, WIDTH), lambda i: (i, 0)),
        out_shape=jax.ShapeDtypeStruct((t, WIDTH), F32),
        scratch_shapes=[pltpu.VMEM((HEADS, HEAD_DIM, HEAD_DIM), F32)],
        compiler_params=_cparams(("arbitrary",)),
        name="hgrn2",
    )(z1, z2, z1, z1, lb.reshape(1, WIDTH).astype(F32), norm_w.reshape(1, WIDTH).astype(F32), tril)


def _s5_toeplitz_kernel(k_ref, m_ref):
    krow = k_ref[0]
    lane = lax.broadcasted_iota(jnp.int32, krow.shape, 1)
    for s in range(S5_CHUNK):
        blk = krow if s == 0 else jnp.where(lane >= s * S5_GROUP, pltpu.roll(krow, s * S5_GROUP, 1), 0.0)
        m_ref[0, s * S5_GROUP:(s + 1) * S5_GROUP, :] = blk.astype(m_ref.dtype)


def _s5_toeplitz(krow):
    rows = S5_CHUNK * S5_GROUP
    return pl.pallas_call(
        _s5_toeplitz_kernel,
        grid=(S5_GROUPS,),
        in_specs=[pl.BlockSpec((1, S5_GROUP, rows), lambda g: (g, 0, 0))],
        out_specs=pl.BlockSpec((1, rows, rows), lambda g: (g, 0, 0)),
        out_shape=jax.ShapeDtypeStruct((S5_GROUPS, rows, rows), BF16),
        compiler_params=_cparams(("parallel",)),
        name="s5_toeplitz",
    )(krow)


def _s5_tables(lam_re, lam_im, log_dt, b_re, b_im, c_re, c_im):
    ln = S5_CHUNK
    lr = jnp.minimum(lam_re.astype(F32), S5_MAX_REAL)
    li = lam_im.astype(F32)
    dt = jnp.exp(log_dt.astype(F32))[:, None]
    mag = jnp.exp(lr * dt)
    ab_re = mag * jnp.cos(li * dt)
    ab_im = mag * jnp.sin(li * dt)
    nr = ab_re - 1.0
    den = lr * lr + li * li
    cr = (nr * lr + ab_im * li) / den
    ci = (ab_im * lr - nr * li) / den
    bb_re = cr[..., None] * b_re - ci[..., None] * b_im
    bb_im = cr[..., None] * b_im + ci[..., None] * b_re
    tau = jnp.arange(ln + 1, dtype=F32)[:, None, None]
    pm = jnp.exp(lr * dt * tau)
    pr = pm * jnp.cos(li * dt * tau)
    pi = pm * jnp.sin(li * dt * tau)
    w1 = c_re[:, :, :, None] * bb_re[:, None, :, :] - c_im[:, :, :, None] * bb_im[:, None, :, :]
    w2 = c_re[:, :, :, None] * bb_im[:, None, :, :] + c_im[:, :, :, None] * bb_re[:, None, :, :]
    kk = (jnp.einsum('gopc,tgp->gtco', w1, pr[:ln], precision=HIGHEST)
          - jnp.einsum('gopc,tgp->gtco', w2, pi[:ln], precision=HIGHEST))
    m = _s5_toeplitz(kk.transpose(0, 2, 1, 3).reshape(S5_GROUPS, S5_GROUP, ln * S5_GROUP))
    pe_r = pr[ln - 1 - jnp.arange(ln)]
    pe_i = pi[ln - 1 - jnp.arange(ln)]
    e_re = pe_r[:, :, :, None] * bb_re[None] - pe_i[:, :, :, None] * bb_im[None]
    e_im = pe_r[:, :, :, None] * bb_im[None] + pe_i[:, :, :, None] * bb_re[None]
    e_re = e_re.transpose(1, 0, 3, 2).reshape(S5_GROUPS, ln * S5_GROUP, S5_STATE)
    e_im = e_im.transpose(1, 0, 3, 2).reshape(S5_GROUPS, ln * S5_GROUP, S5_STATE)
    pf_r = pr[1:].transpose(1, 2, 0)
    pf_i = pi[1:].transpose(1, 2, 0)
    cre = c_re.transpose(0, 2, 1)
    cim = c_im.transpose(0, 2, 1)
    f_re = cre[:, :, None, :] * pf_r[..., None] - cim[:, :, None, :] * pf_i[..., None]
    f_im = -(cre[:, :, None, :] * pf_i[..., None] + cim[:, :, None, :] * pf_r[..., None])
    f_re = f_re.reshape(S5_GROUPS, S5_STATE, ln * S5_GROUP)
    f_im = f_im.reshape(S5_GROUPS, S5_STATE, ln * S5_GROUP)
    rows = ln * S5_GROUP
    zero_e = jnp.zeros((S5_PAIRS, rows, S5_STATE), F32)
    e0r, e1r, e0i, e1i = e_re[0::2], e_re[1::2], e_im[0::2], e_im[1::2]
    e_pair = jnp.concatenate([jnp.concatenate([e0r, zero_e, e0i, zero_e], axis=2),
                              jnp.concatenate([zero_e, e1r, zero_e, e1i], axis=2)], axis=1)
    zero_f = jnp.zeros((S5_PAIRS, S5_STATE, rows), F32)
    fr_pair = jnp.concatenate([jnp.concatenate([f_re[0::2], zero_f], axis=2),
                               jnp.concatenate([zero_f, f_re[1::2]], axis=2)], axis=1)
    fi_pair = jnp.concatenate([jnp.concatenate([f_im[0::2], zero_f], axis=2),
                               jnp.concatenate([zero_f, f_im[1::2]], axis=2)], axis=1)
    al = jnp.zeros((S5_PAIRS, SUBLANES, LANES), F32)
    al = al.at[:, 0, :].set(pr[ln].reshape(S5_PAIRS, 2 * S5_STATE))
    al = al.at[:, 1, :].set(pi[ln].reshape(S5_PAIRS, 2 * S5_STATE))
    return m.astype(BF16), e_pair.astype(BF16), fr_pair.astype(BF16), fi_pair.astype(BF16), al


def _s5_kernel(u_ref, m_ref, e_ref, fr_ref, fi_ref, al_ref, o_ref, sloc, xpr, xpi, *, nc):
    rows = S5_CHUNK * S5_GROUP
    u = u_ref[0]
    sloc[...] = jnp.dot(u, e_ref[0], preferred_element_type=F32)
    ar = al_ref[0, 0:1, :]
    ai = al_ref[0, 1:2, :]

    row = lax.broadcasted_iota(jnp.int32, (SUBLANES, LANES), 0)

    def body(ti, carry):
        xr, xi = carry
        r0 = pl.multiple_of(ti * SUBLANES, SUBLANES)
        sr_t = sloc[pl.ds(r0, SUBLANES), 0:LANES]
        si_t = sloc[pl.ds(r0, SUBLANES), LANES:2 * LANES]
        pr_t = jnp.zeros((SUBLANES, LANES), F32)
        pi_t = jnp.zeros((SUBLANES, LANES), F32)
        for r in range(SUBLANES):
            pr_t = jnp.where(row == r, xr, pr_t)
            pi_t = jnp.where(row == r, xi, pi_t)
            xr, xi = ar * xr - ai * xi + sr_t[r:r + 1, :], ar * xi + ai * xr + si_t[r:r + 1, :]
        xpr[pl.ds(r0, SUBLANES), :] = pr_t
        xpi[pl.ds(r0, SUBLANES), :] = pi_t
        return xr, xi

    zero = jnp.zeros((1, LANES), F32)
    lax.fori_loop(0, nc // SUBLANES, body, (zero, zero))
    y = (jnp.dot(xpr[...].astype(BF16), fr_ref[0], preferred_element_type=F32)
         + jnp.dot(xpi[...].astype(BF16), fi_ref[0], preferred_element_type=F32))
    o_ref[0, :, 0:rows] = y[:, 0:rows] + jnp.dot(u[:, 0:rows], m_ref[0], preferred_element_type=F32)
    o_ref[0, :, rows:2 * rows] = y[:, rows:2 * rows] + jnp.dot(u[:, rows:2 * rows], m_ref[1],
                                                              preferred_element_type=F32)


def _s5(z1, lam_re, lam_im, log_dt, b_re, b_im, c_re, c_im):
    t = z1.shape[0]
    ln = S5_CHUNK
    nc = t // ln
    rows = ln * S5_GROUP
    m, e_pair, fr_pair, fi_pair, al = _s5_tables(lam_re, lam_im, log_dt, b_re, b_im, c_re, c_im)
    u = z1[:, Z1_US * WIDTH:(Z1_US + 1) * WIDTH].astype(BF16)
    u = u.reshape(nc, ln, S5_PAIRS, 2, S5_GROUP).transpose(2, 0, 3, 1, 4).reshape(S5_PAIRS, nc, 2 * rows)
    y = pl.pallas_call(
        functools.partial(_s5_kernel, nc=nc),
        grid=(S5_PAIRS,),
        in_specs=[pl.BlockSpec((1, nc, 2 * rows), lambda j: (j, 0, 0)),
                  pl.BlockSpec((2, rows, rows), lambda j: (j, 0, 0)),
                  pl.BlockSpec((1, 2 * rows, 2 * LANES), lambda j: (j, 0, 0)),
                  pl.BlockSpec((1, LANES, 2 * rows), lambda j: (j, 0, 0)),
                  pl.BlockSpec((1, LANES, 2 * rows), lambda j: (j, 0, 0)),
                  pl.BlockSpec((1, SUBLANES, LANES), lambda j: (j, 0, 0))],
        out_specs=pl.BlockSpec((1, nc, 2 * rows), lambda j: (j, 0, 0)),
        out_shape=jax.ShapeDtypeStruct((S5_PAIRS, nc, 2 * rows), F32),
        scratch_shapes=[pltpu.VMEM((nc, 2 * LANES), F32), pltpu.VMEM((nc, LANES), F32), pltpu.VMEM((nc, LANES), F32)],
        compiler_params=_cparams(("parallel",)),
        name="s5_scan",
    )(u, m, e_pair, fr_pair, fi_pair, al)
    return y.reshape(S5_PAIRS, nc, 2, ln, S5_GROUP).transpose(1, 3, 0, 2, 4).reshape(t, WIDTH)


def _layer_norm(x, g, b):
    mu = jnp.mean(x, axis=-1, keepdims=True)
    xc = x - mu
    var = jnp.mean(xc * xc, axis=-1, keepdims=True)
    return xc * lax.rsqrt(var + LN_EPS) * g + b


def _merge_kernel(x_ref, g0_ref, g1_ref, g2_ref, ym_ref, ys_ref, us_ref, yh_ref, wm_ref, ws_ref, wh_ref, wglu_ref,
                  d_ref, wo_ref, lg_ref, lb_ref, rw_ref, rb_ref, o_ref):
    ys = _gelu_tanh(ys_ref[...] + d_ref[...] * us_ref[...].astype(F32))
    ys = ys * _sigmoid(_dot(ys, wglu_ref[...]))
    merged = (_sigmoid(g0_ref[...].astype(F32)) * _dot(ym_ref[...], wm_ref[...])
              + _sigmoid(g1_ref[...].astype(F32)) * _dot(ys, ws_ref[...])
              + _sigmoid(g2_ref[...].astype(F32)) * _dot(yh_ref[...], wh_ref[...]))
    x1 = _layer_norm(DN_ALPHA * x_ref[...] + _dot(merged, wo_ref[...]), lg_ref[...], lb_ref[...])
    o_ref[:, 0:D_MODEL] = x1

    xh = x1.astype(BF16)
    xl = (x1 - xh.astype(F32)).astype(BF16)
    logits = (jnp.dot(xh, rw_ref[0], preferred_element_type=F32) + jnp.dot(xl, rw_ref[0], preferred_element_type=F32)
              + jnp.dot(xh, rw_ref[1], preferred_element_type=F32))
    s0 = _sigmoid(logits)
    sb0 = s0 + rb_ref[...]
    shift = lambda a, j: a if j == 0 else pltpu.roll(a, LANES - N_GROUPS * j, 1)
    s = [shift(s0, j) for j in range(EXPERTS_PER_GROUP)]
    sb = [shift(sb0, j) for j in range(EXPERTS_PER_GROUP)]
    hi1, lo1 = jnp.maximum(sb[0], sb[1]), jnp.minimum(sb[0], sb[1])
    hi2, lo2 = jnp.maximum(sb[2], sb[3]), jnp.minimum(sb[2], sb[3])
    top2 = jnp.maximum(hi1, hi2) + jnp.maximum(jnp.minimum(hi1, hi2), jnp.maximum(lo1, lo2))
    lane = lax.broadcasted_iota(jnp.int32, top2.shape, 1)
    top2 = jnp.where(lane < N_GROUPS, top2, -jnp.inf)
    gmax = jnp.max(top2, axis=1, keepdims=True)
    g_idx = jnp.min(jnp.where(top2 == gmax, lane, LANES), axis=1, keepdims=True)
    sel = lane == g_idx
    v = [jnp.sum(jnp.where(sel, sb[j], 0.0), axis=1, keepdims=True) for j in range(EXPERTS_PER_GROUP)]
    sv = [jnp.sum(jnp.where(sel, s[j], 0.0), axis=1, keepdims=True) for j in range(EXPERTS_PER_GROUP)]

    def first_max(vals):
        m = jnp.maximum(jnp.maximum(vals[0], vals[1]), jnp.maximum(vals[2], vals[3]))
        return jnp.where(vals[0] == m, 0, jnp.where(vals[1] == m, 1, jnp.where(vals[2] == m, 2, 3)))

    e1 = first_max(v)
    e2 = first_max([jnp.where(e1 == j, -jnp.inf, v[j]) for j in range(EXPERTS_PER_GROUP)])
    s1 = sum(jnp.where(e1 == j, sv[j], 0.0) for j in range(EXPERTS_PER_GROUP))
    s2 = sum(jnp.where(e2 == j, sv[j], 0.0) for j in range(EXPERTS_PER_GROUP))
    tot = s1 + s2
    meta = jnp.where(lane == 0, g_idx.astype(F32), 0.0)
    for j in range(EXPERTS_PER_GROUP):
        cw = jnp.where(e1 == j, s1 / tot, 0.0) + jnp.where(e2 == j, s2 / tot, 0.0)
        meta = jnp.where(lane == 1 + j, cw, meta)
    o_ref[:, D_MODEL:D_MODEL + META] = meta


def _merge(x, z1, ym, ys, yh, wm, ws, wh, wglu, d, wo, lg, lb, router_w, router_bias, tm):
    t = x.shape[0]
    rw = router_w.astype(F32).reshape(D_MODEL, N_GROUPS, EXPERTS_PER_GROUP).transpose(0, 2, 1)
    rw = jnp.pad(rw.reshape(D_MODEL, N_EXPERTS), ((0, 0), (0, LANES - N_EXPERTS)))
    rw_hi = rw.astype(BF16)
    rw = jnp.stack([rw_hi, (rw - rw_hi.astype(F32)).astype(BF16)])
    rb = jnp.pad(router_bias.astype(F32).reshape(N_GROUPS, EXPERTS_PER_GROUP).T.reshape(1, N_EXPERTS),
                 ((0, 0), (0, LANES - N_EXPERTS)))
    full = lambda shape: pl.BlockSpec(shape, lambda i: (0,) * len(shape))
    row = lambda w: pl.BlockSpec((tm, w), lambda i: (i, 0))
    return pl.pallas_call(
        _merge_kernel,
        grid=(t // tm,),
        in_specs=[row(D_MODEL),
                  pl.BlockSpec((tm, D_MODEL), lambda i: (i, 0)),
                  pl.BlockSpec((tm, D_MODEL), lambda i: (i, 1)),
                  pl.BlockSpec((tm, D_MODEL), lambda i: (i, 2)),
                  row(WIDTH), row(WIDTH), pl.BlockSpec((tm, WIDTH), lambda i: (i, Z1_US)), row(WIDTH),
                  full((WIDTH, D_MODEL)), full((WIDTH, D_MODEL)), full((WIDTH, D_MODEL)), full((WIDTH, WIDTH)),
                  full((1, WIDTH)), full((D_MODEL, D_MODEL)), full((1, D_MODEL)), full((1, D_MODEL)),
                  full((2, D_MODEL, LANES)), full((1, LANES))],
        out_specs=pl.BlockSpec((tm, D_MODEL + META), lambda i: (i, 0)),
        out_shape=jax.ShapeDtypeStruct((t, D_MODEL + META), F32),
        compiler_params=_cparams(("parallel",)),
        name="merge_router",
    )(x, z1, z1, z1, ym, ys, z1, yh, wm.astype(BF16), ws.astype(BF16), wh.astype(BF16), wglu.astype(BF16),
      d.reshape(1, WIDTH).astype(F32), wo.astype(BF16), lg.reshape(1, D_MODEL).astype(F32),
      lb.reshape(1, D_MODEL).astype(F32), rw, rb)


def _start_row_gather(idx_ref, base, src_hbm, dst, sem, rows):
    for r in range(rows):
        pltpu.make_async_copy(src_hbm.at[pl.ds(idx_ref[base + r], 1)], dst.at[pl.ds(r, 1)], sem).start(priority=r % 2)


def _wait_row_gather(src_hbm, dst, sem, rows):
    pltpu.make_async_copy(src_hbm.at[pl.ds(0, rows)], dst, sem).wait()


def _moe_kernel(tg_ref, src_ref, x_hbm, wg_ref, wu_ref, wd_ref, o_ref, buf, sem, wg_s, wu_s, wd_s, *, tm, ntiles):
    i = pl.program_id(0)
    slot = i % 2

    @pl.when(i == 0)
    def _():
        _start_row_gather(src_ref, 0, x_hbm, buf.at[0], sem.at[0], tm)

    @pl.when(i + 1 < ntiles)
    def _():
        _start_row_gather(src_ref, (i + 1) * tm, x_hbm, buf.at[1 - slot], sem.at[1 - slot], tm)

    @pl.when(jnp.logical_or(i == 0, tg_ref[i] != tg_ref[jnp.maximum(i - 1, 0)]))
    def _():
        for e in range(EXPERTS_PER_GROUP):
            wg_s[e] = wg_ref[0, e].astype(BF16)
            wu_s[e] = wu_ref[0, e].astype(BF16)
            wd_s[e] = wd_ref[0, e].astype(BF16)

    _wait_row_gather(x_hbm, buf.at[slot], sem.at[slot], tm)
    xt = buf[slot]
    xb = xt[:, 0:D_MODEL].astype(BF16)
    meta = xt[:, D_MODEL:D_MODEL + META]
    y = jnp.zeros((tm, D_MODEL), F32)
    for e in range(EXPERTS_PER_GROUP):
        hg = jnp.dot(xb, wg_s[e], preferred_element_type=F32)
        hu = jnp.dot(xb, wu_s[e], preferred_element_type=F32)
        hh = _silu(hg) * hu * _lane_col(meta, 1 + e)
        y = y + jnp.dot(hh.astype(BF16), wd_s[e], preferred_element_type=F32)
    o_ref[...] = y


def _ln2_kernel(dst_ref, x_ref, y_hbm, lg_ref, lb_ref, o_ref, buf, sem, *, tm, ntiles):
    i = pl.program_id(0)
    slot = i % 2

    @pl.when(i == 0)
    def _():
        _start_row_gather(dst_ref, 0, y_hbm, buf.at[0], sem.at[0], tm)

    @pl.when(i + 1 < ntiles)
    def _():
        _start_row_gather(dst_ref, (i + 1) * tm, y_hbm, buf.at[1 - slot], sem.at[1 - slot], tm)

    _wait_row_gather(y_hbm, buf.at[slot], sem.at[slot], tm)
    o_ref[...] = _layer_norm(DN_ALPHA * x_ref[...] + buf[slot], lg_ref[...], lb_ref[...])


def _moe_ln(x1e, wg, wu, wd, layer, lg, lb, tm):
    t = x1e.shape[0]
    ntiles = t // tm + N_GROUPS
    p = ntiles * tm
    key = x1e[:, D_MODEL].astype(jnp.int32)
    onehot = (key[:, None] == jnp.arange(N_GROUPS)[None, :]).astype(jnp.int32)
    csum = jnp.cumsum(onehot, axis=0)
    counts = csum[-1]
    rank = jnp.sum(onehot * csum, axis=1) - 1
    pcount = ((counts + tm - 1) // tm) * tm
    pend = jnp.cumsum(pcount)
    dest = (pend - pcount)[key] + rank
    row_src = jnp.zeros((p,), jnp.int32).at[dest].set(jnp.arange(t, dtype=jnp.int32))
    tile_start = jnp.arange(ntiles, dtype=jnp.int32) * tm
    tile_group = jnp.minimum(jnp.sum((tile_start[:, None] >= pend[None, :]).astype(jnp.int32), axis=1),
                             N_GROUPS - 1)
    wspec = lambda a, b: pl.BlockSpec((1, EXPERTS_PER_GROUP, a, b), lambda i, tg, src: (layer, tg[i], 0, 0))
    wscr = lambda a, b: pltpu.VMEM((EXPERTS_PER_GROUP, a, b), BF16)
    y_sorted = pl.pallas_call(
        functools.partial(_moe_kernel, tm=tm, ntiles=ntiles),
        grid_spec=pltpu.PrefetchScalarGridSpec(
            num_scalar_prefetch=2,
            grid=(ntiles,),
            in_specs=[pl.BlockSpec(memory_space=pl.ANY), wspec(D_MODEL, D_EXPERT), wspec(D_MODEL, D_EXPERT),
                      wspec(D_EXPERT, D_MODEL)],
            out_specs=pl.BlockSpec((tm, D_MODEL), lambda i, tg, src: (i, 0)),
            scratch_shapes=[pltpu.VMEM((2, tm, D_MODEL + META), F32), pltpu.SemaphoreType.DMA((2,)),
                            wscr(D_MODEL, D_EXPERT), wscr(D_MODEL, D_EXPERT), wscr(D_EXPERT, D_MODEL)]),
        out_shape=jax.ShapeDtypeStruct((p, D_MODEL), F32),
        compiler_params=_cparams(("arbitrary",)),
        name="moe_experts",
    )(tile_group, row_src, x1e, wg, wu, wd)
    nt2 = t // tm
    return pl.pallas_call(
        functools.partial(_ln2_kernel, tm=tm, ntiles=nt2),
        grid_spec=pltpu.PrefetchScalarGridSpec(
            num_scalar_prefetch=1,
            grid=(nt2,),
            in_specs=[pl.BlockSpec((tm, D_MODEL), lambda i, d: (i, 0)), pl.BlockSpec(memory_space=pl.ANY),
                      pl.BlockSpec((1, D_MODEL), lambda i, d: (0, 0)), pl.BlockSpec((1, D_MODEL), lambda i, d: (0, 0))],
            out_specs=pl.BlockSpec((tm, D_MODEL), lambda i, d: (i, 0)),
            scratch_shapes=[pltpu.VMEM((2, tm, D_MODEL), F32), pltpu.SemaphoreType.DMA((2,))]),
        out_shape=jax.ShapeDtypeStruct((t, D_MODEL), F32),
        compiler_params=_cparams(("arbitrary",)),
        name="moe_combine_ln",
    )(dest.astype(jnp.int32), x1e, y_sorted, lg.reshape(1, D_MODEL).astype(F32), lb.reshape(1, D_MODEL).astype(F32))


def _layer(x, l, p, lb_l, cfg):
    t = x.shape[0]
    w1, w2, wt = _split_w_in(p['w_in'][l])
    xb = x.astype(BF16)
    z1 = _matmul(xb, w1, Z1_DTYPE, cfg['tm_in'], cfg['tn_in'])
    z2, gt = _inproj_gate(xb, w2, wt, cfg['tm_in'])
    gt3 = gt.reshape(SUBLANES, t // CHUNK, CHUNK).transpose(1, 0, 2)
    ym = _mlstm(z1, z2, gt3, p['mlstm_conv_w'][l], p['mlstm_conv_b'][l], p['mlstm_wq'][l], p['mlstm_wk'][l],
                p['mlstm_wv'][l], p['mlstm_b_i'][l], p['mlstm_b_f'][l], p['mlstm_skip'][l], cfg['tb'])
    ys = _s5(z1, p['s5_lambda_re'][l], p['s5_lambda_im'][l], p['s5_log_dt'][l], p['s5_b_re'][l], p['s5_b_im'][l],
             p['s5_c_re'][l], p['s5_c_im'][l])
    yh = _hgrn(z1, z2, lb_l, p['hgrn_norm_w'][l], cfg['tb'])
    x1e = _merge(x, z1, ym, ys, yh, p['w_branch_mlstm'][l], p['w_branch_s5'][l], p['w_branch_hgrn'][l],
                 p['s5_w_glu'][l], p['s5_d'][l], p['w_out'][l], p['ln1_g'][l], p['ln1_b'][l], p['router_w'],
                 p['router_bias'], cfg['tm_merge'])
    return _moe_ln(x1e, p['exp_w_gate'], p['exp_w_up'], p['exp_w_down'], l, p['ln2_g'][l], p['ln2_b'][l],
                   cfg['tm_moe'])


_CFG = dict(tm_in=1024, tn_in=1024, tb=512, tm_merge=512, tm_moe=512)


def kernel(x, w_in, mlstm_conv_w, mlstm_conv_b, mlstm_wq, mlstm_wk, mlstm_wv, mlstm_b_i, mlstm_b_f, mlstm_skip, s5_lambda_re, s5_lambda_im, s5_log_dt, s5_b_re, s5_b_im, s5_c_re, s5_c_im, s5_d, s5_w_glu, hgrn_lower_bounds, hgrn_norm_w, w_branch_mlstm, w_branch_s5, w_branch_hgrn, w_out, ln1_g, ln1_b, ln2_g, ln2_b, router_w, router_bias, exp_w_gate, exp_w_up, exp_w_down):
    p = dict(w_in=w_in, mlstm_conv_w=mlstm_conv_w, mlstm_conv_b=mlstm_conv_b, mlstm_wq=mlstm_wq, mlstm_wk=mlstm_wk,
             mlstm_wv=mlstm_wv, mlstm_b_i=mlstm_b_i, mlstm_b_f=mlstm_b_f, mlstm_skip=mlstm_skip,
             s5_lambda_re=s5_lambda_re, s5_lambda_im=s5_lambda_im, s5_log_dt=s5_log_dt, s5_b_re=s5_b_re,
             s5_b_im=s5_b_im, s5_c_re=s5_c_re, s5_c_im=s5_c_im, s5_d=s5_d, s5_w_glu=s5_w_glu,
             hgrn_norm_w=hgrn_norm_w, w_branch_mlstm=w_branch_mlstm, w_branch_s5=w_branch_s5,
             w_branch_hgrn=w_branch_hgrn, w_out=w_out, ln1_g=ln1_g, ln1_b=ln1_b, ln2_g=ln2_g, ln2_b=ln2_b,
             router_w=router_w, router_bias=router_bias, exp_w_gate=exp_w_gate, exp_w_up=exp_w_up,
             exp_w_down=exp_w_down)
    lb_cum = jnp.cumsum(jax.nn.softmax(hgrn_lower_bounds.astype(F32), axis=0), axis=0)
    lb_layers = lb_cum - lb_cum[0]
    bsz, seq, d = x.shape
    h = x.reshape(bsz * seq, d)
    for l in range(DEPTH):
        h = _layer(h, l, p, lb_layers[l], _CFG)
    return h.reshape(bsz, seq, d)
```

```python
import functools
import math

import jax
import jax.numpy as jnp
from jax import lax
from jax.experimental import pallas as pl
from jax.experimental.pallas import tpu as pltpu

F32 = jnp.float32
BF16 = jnp.bfloat16
HIGHEST = lax.Precision.HIGHEST

D_MODEL = 1024
DEPTH = 2
HEADS = 4
HEAD_DIM = 128
WIDTH = HEADS * HEAD_DIM
MLSTM_CONV = 4
CHUNK = 64
S5_GROUP = 16
S5_GROUPS = 32
S5_STATE = 64
S5_PAIRS = S5_GROUPS // 2
S5_CHUNK = 32
S5_MAX_REAL = -1e-4
N_EXPERTS = 32
N_GROUPS = 8
EXPERTS_PER_GROUP = 4
D_EXPERT = 256
DN_ALPHA = (2 * DEPTH) ** 0.25
LN_EPS = 1e-5
NORM_EPS = 1e-6

LANES = 128
SUBLANES = 8
META = LANES
VMEM_LIMIT = 56 * 1024 * 1024

Z1_DTYPE = BF16
Z1_GATE, Z1_XM, Z1_OM, Z1_QH, Z1_IH, Z1_GH = 0, 6, 7, 8, 9, 10
Z1_COLS = 11 * WIDTH
Z2_FH, Z2_US = 0, 1
Z2_COLS = 2 * WIDTH + LANES


def _cparams(sem):
    return pltpu.CompilerParams(dimension_semantics=sem, vmem_limit_bytes=VMEM_LIMIT)


def _sigmoid(x):
    return 0.5 * (1.0 + jnp.tanh(0.5 * x))


def _silu(x):
    return x * _sigmoid(x)


def _log_sigmoid(x):
    return jnp.minimum(x, 0.0) - jnp.log(1.0 + jnp.exp(-jnp.abs(x)))


def _gelu_tanh(x):
    return 0.5 * x * (1.0 + jnp.tanh(math.sqrt(2.0 / math.pi) * (x + 0.044715 * (x * x * x))))


def _cumsum_rows(x):
    n = x.shape[0]
    row = lax.broadcasted_iota(jnp.int32, x.shape, 0)
    s = 1
    while s < n:
        x = x + jnp.where(row >= s, pltpu.roll(x, s, 0), 0.0)
        s *= 2
    return x


def _lane_col(x, idx):
    lane = lax.broadcasted_iota(jnp.int32, x.shape, 1)
    return jnp.sum(jnp.where(lane == idx, x, 0.0), axis=1, keepdims=True)


def _dot(a, b):
    return jnp.dot(a.astype(BF16), b.astype(BF16), preferred_element_type=F32)


def _dot_nt(a, b):
    return lax.dot_general(a.astype(BF16), b.astype(BF16), (((1,), (1,)), ((), ())), preferred_element_type=F32)


def _dot_tn(a, b):
    return lax.dot_general(a.astype(BF16), b.astype(BF16), (((0,), (0,)), ((), ())), preferred_element_type=F32)


def _mm_kernel(x_ref, w_ref, o_ref):
    o_ref[...] = jnp.dot(x_ref[...], w_ref[...], preferred_element_type=F32).astype(o_ref.dtype)


def _matmul(x, w, out_dtype, tm, tn):
    m, k = x.shape
    n = w.shape[1]
    return pl.pallas_call(
        _mm_kernel,
        grid=(m // tm, n // tn),
        in_specs=[pl.BlockSpec((tm, k), lambda i, j: (i, 0)), pl.BlockSpec((k, tn), lambda i, j: (0, j))],
        out_specs=pl.BlockSpec((tm, tn), lambda i, j: (i, j)),
        out_shape=jax.ShapeDtypeStruct((m, n), out_dtype),
        compiler_params=_cparams(("parallel", "parallel")),
        name="in_proj_wide",
    )(x, w)


def _inproj_gate_kernel(x_ref, w_ref, wt_ref, o_ref, gt_ref):
    x = x_ref[...]
    o_ref[...] = jnp.dot(x, w_ref[...], preferred_element_type=F32)
    gt_ref[...] = lax.dot_general(wt_ref[...], x, (((1,), (1,)), ((), ())), preferred_element_type=F32)


def _inproj_gate(x, w2, wt, tm):
    m, k = x.shape
    return pl.pallas_call(
        _inproj_gate_kernel,
        grid=(m // tm,),
        in_specs=[pl.BlockSpec((tm, k), lambda i: (i, 0)),
                  pl.BlockSpec((k, Z2_COLS), lambda i: (0, 0)),
                  pl.BlockSpec((SUBLANES, k), lambda i: (0, 0))],
        out_specs=[pl.BlockSpec((tm, Z2_COLS), lambda i: (i, 0)), pl.BlockSpec((SUBLANES, tm), lambda i: (0, i))],
        out_shape=[jax.ShapeDtypeStruct((m, Z2_COLS), F32), jax.ShapeDtypeStruct((SUBLANES, m), F32)],
        compiler_params=_cparams(("parallel",)),
        name="in_proj_gates",
    )(x, w2, wt)


def _split_w_in(w):
    offs, o = [], 0
    for s in (WIDTH, WIDTH, HEADS, HEADS, WIDTH, WIDTH, WIDTH, WIDTH, WIDTH, 3 * D_MODEL):
        offs.append((o, o + s))
        o += s
    seg = [w[:, a:b] for a, b in offs]
    xm, om, im, fm, us, qh, fh, ih, gh, gate = seg
    w1 = jnp.concatenate([gate, xm, om, qh, ih, gh], axis=1).astype(BF16)
    pad = jnp.zeros((w.shape[0], LANES - 2 * HEADS), w.dtype)
    w2 = jnp.concatenate([fh, us, im, fm, pad], axis=1).astype(BF16)
    wt = jnp.concatenate([im, fm], axis=1).T.astype(BF16)
    return w1, w2, wt


def _mlstm_kernel(xm_ref, om_ref, gc_ref, gr_ref, cw_ref, cb_ref, wq_ref, wk_ref, wv_ref, bcol_ref, brow_ref,
                  skip_ref, triu_ref, o_ref, xpad, q_s, k_s, v_s, xc_s, h_s, gcol_s, grow_s, brow_s, ct_s, n_s, m_s,
                  *, tb):
    ncb = tb // CHUNK

    @pl.when(pl.program_id(0) == 0)
    def _():
        xpad[0:SUBLANES, :] = jnp.zeros((SUBLANES, WIDTH), F32)
        ct_s[...] = jnp.zeros_like(ct_s)
        n_s[...] = jnp.zeros_like(n_s)
        m_s[...] = jnp.zeros_like(m_s)

    xm = xm_ref[...].astype(F32)
    xpad[SUBLANES:SUBLANES + tb, :] = xm
    cw = cw_ref[...]
    conv = cb_ref[...] + cw[3:4, :] * xm
    for d in range(1, MLSTM_CONV):
        conv = conv + cw[3 - d:4 - d, :] * xpad[SUBLANES - d:SUBLANES - d + tb, :]
    xpad[0:SUBLANES, :] = xpad[tb:tb + SUBLANES, :]
    xc = _silu(conv)
    xc_s[...] = xc
    for h in range(HEADS):
        sl = slice(h * HEAD_DIM, (h + 1) * HEAD_DIM)
        xch = xc[:, sl].astype(BF16)
        q_s[:, sl] = jnp.dot(xch, wq_ref[h], preferred_element_type=F32) * (HEAD_DIM ** -0.5)
        k_s[:, sl] = jnp.dot(xch, wk_ref[h], preferred_element_type=F32)
        v_s[:, sl] = jnp.dot(xm[:, sl].astype(BF16), wv_ref[h], preferred_element_type=F32)

    gc = gc_ref[...] + bcol_ref[...]
    lane = lax.broadcasted_iota(jnp.int32, gc.shape, 1)
    gcol_s[...] = jnp.where(lane < HEADS, gc, _log_sigmoid(gc))
    gr = gr_ref[...] + brow_ref[...]
    sub = lax.broadcasted_iota(jnp.int32, gr.shape, 1)
    gr = jnp.where(sub < HEADS, gr, _log_sigmoid(gr)).reshape(ncb * SUBLANES, CHUNK)
    grow_s[...] = gr
    brow_s[...] = jnp.dot(gr, triu_ref[...], precision=HIGHEST, preferred_element_type=F32)

    rowi = lax.broadcasted_iota(jnp.int32, (CHUNK, CHUNK), 0)
    coli = lax.broadcasted_iota(jnp.int32, (CHUNK, CHUNK), 1)
    causal = rowi >= coli

    def chunk_body(c, carry):
        r0 = pl.multiple_of(c * CHUNK, CHUNK)
        g0 = pl.multiple_of(c * SUBLANES, SUBLANES)
        gcol = gcol_s[pl.ds(r0, CHUNK), :]
        bcol = _cumsum_rows(gcol)
        grow = grow_s[pl.ds(g0, SUBLANES), :]
        brow = brow_s[pl.ds(g0, SUBLANES), :]
        for h in range(HEADS):
            sl = slice(h * HEAD_DIM, (h + 1) * HEAD_DIM)
            li_c = _lane_col(gcol, h)
            b_c = _lane_col(bcol, HEADS + h)
            li_r = grow[h:h + 1, :]
            b_r = brow[HEADS + h:HEADS + h + 1, :]
            b_tot = b_c[CHUNK - 1:CHUNK, :]
            a_c = b_tot - b_c + li_c
            a_max = jnp.max(a_c, axis=0, keepdims=True)
            q = q_s[pl.ds(r0, CHUNK), sl]
            k = k_s[pl.ds(r0, CHUNK), sl]
            v = v_s[pl.ds(r0, CHUNK), sl]
            wk = jnp.exp(a_c - a_max) * k
            c_loc_t = _dot_tn(v, wk)
            n_loc = jnp.sum(wk, axis=0, keepdims=True)
            ct_prev = ct_s[h]
            n_prev = n_s[h:h + 1, :]
            m_prev = m_s[h:h + 1, 0:1]
            d_mat = jnp.where(causal, b_c - b_r + li_r, -jnp.inf)
            m_inter = b_c + m_prev
            m_j = jnp.maximum(m_inter, jnp.max(d_mat, axis=1, keepdims=True))
            sc = _dot_nt(q, k) * jnp.exp(d_mat - m_j)
            g_inter = jnp.exp(m_inter - m_j)
            num = g_inter * _dot_nt(q, ct_prev) + _dot(sc, v)
            den = g_inter * jnp.sum(q * n_prev, axis=1, keepdims=True) + jnp.sum(sc, axis=1, keepdims=True)
            h_s[pl.ds(r0, CHUNK), sl] = num / jnp.maximum(jnp.abs(den), jnp.exp(-m_j))
            m_new = jnp.maximum(b_tot + m_prev, a_max)
            g_old = jnp.exp(b_tot + m_prev - m_new)
            g_loc = jnp.exp(a_max - m_new)
            ct_s[h] = g_old * ct_prev + g_loc * c_loc_t
            n_s[h:h + 1, :] = g_old * n_prev + g_loc * n_loc
            m_s[h:h + 1, :] = jnp.broadcast_to(m_new, (1, LANES))
        return carry

    lax.fori_loop(0, ncb, chunk_body, 0, unroll=2)
    o_ref[...] = (_sigmoid(om_ref[...].astype(F32)) * h_s[...] + skip_ref[...] * xc_s[...]).astype(o_ref.dtype)


def _mlstm(z1, z2, gt3, conv_w, conv_b, wq, wk, wv, b_i, b_f, skip, tb):
    t = z1.shape[0]
    ncb = tb // CHUNK
    bias = jnp.concatenate([b_i, b_f]).astype(F32)
    bcol = jnp.zeros((1, LANES), F32).at[0, :2 * HEADS].set(bias)
    brow = bias.reshape(2 * HEADS, 1)
    triu = jnp.triu(jnp.ones((CHUNK, CHUNK), F32))
    full = lambda shape: pl.BlockSpec(shape, lambda i: (0,) * len(shape))
    return pl.pallas_call(
        functools.partial(_mlstm_kernel, tb=tb),
        grid=(t // tb,),
        in_specs=[pl.BlockSpec((tb, WIDTH), lambda i: (i, Z1_XM)),
                  pl.BlockSpec((tb, WIDTH), lambda i: (i, Z1_OM)),
                  pl.BlockSpec((tb, LANES), lambda i: (i, 2 * WIDTH // LANES)),
                  pl.BlockSpec((ncb, SUBLANES, CHUNK), lambda i: (i, 0, 0)),
                  full((MLSTM_CONV, WIDTH)), full((1, WIDTH)),
                  full((HEADS, HEAD_DIM, HEAD_DIM)), full((HEADS, HEAD_DIM, HEAD_DIM)),
                  full((HEADS, HEAD_DIM, HEAD_DIM)),
                  full((1, LANES)), full((2 * HEADS, 1)), full((1, WIDTH)), full((CHUNK, CHUNK))],
        out_specs=pl.BlockSpec((tb, WIDTH), lambda i: (i, 0)),
        out_shape=jax.ShapeDtypeStruct((t, WIDTH), F32),
        scratch_shapes=[pltpu.VMEM((tb + SUBLANES, WIDTH), F32)]
        + [pltpu.VMEM((tb, WIDTH), F32) for _ in range(5)]
        + [pltpu.VMEM((tb, LANES), F32),
           pltpu.VMEM((ncb * SUBLANES, CHUNK), F32), pltpu.VMEM((ncb * SUBLANES, CHUNK), F32),
           pltpu.VMEM((HEADS, HEAD_DIM, HEAD_DIM), F32), pltpu.VMEM((SUBLANES, LANES), F32),
           pltpu.VMEM((SUBLANES, LANES), F32)],
        compiler_params=_cparams(("arbitrary",)),
        name="mlstm",
    )(z1, z1, z2, gt3, conv_w.astype(F32), conv_b.reshape(1, WIDTH).astype(F32), wq.astype(BF16), wk.astype(BF16),
      wv.astype(BF16), bcol, brow, skip.reshape(1, WIDTH).astype(F32), triu)


def _hgrn_kernel(q_ref, f_ref, i_ref, g_ref, lb_ref, nw_ref, tri_ref, o_ref, st_s, *, tb):
    ncb = tb // CHUNK

    @pl.when(pl.program_id(0) == 0)
    def _():
        st_s[...] = jnp.zeros_like(st_s)

    lb = lb_ref[...]
    nw = nw_ref[...]
    tri = tri_ref[...]
    rowi = lax.broadcasted_iota(jnp.int32, (CHUNK, WIDTH), 0)
    sr = lax.broadcasted_iota(jnp.int32, (CHUNK, CHUNK), 0)
    sc = lax.broadcasted_iota(jnp.int32, (CHUNK, CHUNK), 1)
    halves = [1 << p for p in range(CHUNK.bit_length() - 1)]
    upper = {m: (rowi & m) != 0 for m in halves}
    same_blk = {m: (sr // (2 * m)) == (sc // (2 * m)) for m in halves}

    def chunk_body(c, carry):
        r0 = pl.multiple_of(c * CHUNK, CHUNK)
        f = lb + (1.0 - lb) * _sigmoid(f_ref[pl.ds(r0, CHUNK), :].astype(F32))
        k = 1.0 - f
        q = _silu(q_ref[pl.ds(r0, CHUNK), :].astype(F32))
        v = i_ref[pl.ds(r0, CHUNK), :].astype(F32)
        lf = jnp.log(f)
        hi = lf.astype(BF16)
        r1 = lf - hi.astype(F32)
        mid = r1.astype(BF16)
        lo = (r1 - mid.astype(F32)).astype(BF16)
        b = (jnp.dot(tri, hi, preferred_element_type=F32) + jnp.dot(tri, mid, preferred_element_type=F32)
             + jnp.dot(tri, lo, preferred_element_type=F32))
        qs, ks = {}, {}
        for m in halves:
            if m == 1:
                t = jnp.where(upper[m], q * f, k)
            else:
                nblk = CHUNK // (2 * m)
                r = jnp.concatenate([jnp.broadcast_to(b[2 * m * j + m - 1:2 * m * j + m, :], (2 * m, WIDTH))
                                     for j in range(nblk)], axis=0)
                t = jnp.where(upper[m], q, k) * jnp.exp(-jnp.abs(b - r))
            qs[m] = jnp.where(upper[m], t, 0.0).astype(BF16)
            ks[m] = jnp.where(upper[m], 0.0, t).astype(BF16)
        b_last = b[CHUNK - 1:CHUNK, :]
        qe = q * jnp.exp(b)
        kd = k * jnp.exp(b_last - b)
        e_last = jnp.exp(b_last)
        outs = []
        for h in range(HEADS):
            sl = slice(h * HEAD_DIM, (h + 1) * HEAD_DIM)
            st = st_s[h]
            a = jnp.where(sr == sc, _dot_nt(q[:, sl], k[:, sl]), 0.0)
            for m in halves:
                a = a + jnp.where(same_blk[m], _dot_nt(qs[m][:, sl], ks[m][:, sl]), 0.0)
            oh = _dot_nt(qe[:, sl], st) + _dot(a, v[:, sl])
            st_s[h] = e_last[:, sl] * st + _dot_tn(v[:, sl], kd[:, sl])
            ms = jnp.sum(oh * oh, axis=1, keepdims=True) * (1.0 / HEAD_DIM)
            outs.append(oh * lax.rsqrt(ms + NORM_EPS))
        on = jnp.concatenate(outs, axis=1)
        g = g_ref[pl.ds(r0, CHUNK), :].astype(F32)
        o_ref[pl.ds(r0, CHUNK), :] = (on * nw * _silu(g)).astype(o_ref.dtype)
        return carry

    lax.fori_loop(0, ncb, chunk_body, 0, unroll=2)


def _hgrn(z1, z2, lb, norm_w, tb):
    t = z1.shape[0]
    tril = jnp.tril(jnp.ones((CHUNK, CHUNK), F32)).astype(BF16)
    full = lambda shape: pl.BlockSpec(shape, lambda i: (0,) * len(shape))
    return pl.pallas_call(
        functools.partial(_hgrn_kernel, tb=tb),
        grid=(t // tb,),
        in_specs=[pl.BlockSpec((tb, WIDTH), lambda i: (i, Z1_QH)),
                  pl.BlockSpec((tb, WIDTH), lambda i: (i, 0)),
                  pl.BlockSpec((tb, WIDTH), lambda i: (i, Z1_IH)),
                  pl.BlockSpec((tb, WIDTH), lambda i: (i, Z1_GH)),
                  full((1, WIDTH)), full((1, WIDTH)), full((CHUNK, CHUNK))],
        out_specs=pl.BlockSpec((tb, WIDTH), lambda i: (i, 0)),
        out_shape=jax.ShapeDtypeStruct((t, WIDTH), F32),
        scratch_shapes=[pltpu.VMEM((HEADS, HEAD_DIM, HEAD_DIM), F32)],
        compiler_params=_cparams(("arbitrary",)),
        name="hgrn2",
    )(z1, z2, z1, z1, lb.reshape(1, WIDTH).astype(F32), norm_w.reshape(1, WIDTH).astype(F32), tril)


S5_PB = 2 * S5_GROUP
S5_ROWS = S5_CHUNK * S5_PB
S5_SLAB = LANES // S5_PB


def _s5_toeplitz_kernel(k_ref, m_ref):
    krow = k_ref[0]
    lane = lax.broadcasted_iota(jnp.int32, krow.shape, 1)
    for s in range(S5_CHUNK):
        blk = krow if s == 0 else jnp.where(lane >= s * S5_PB, pltpu.roll(krow, s * S5_PB, 1), 0.0)
        m_ref[0, s * S5_PB:(s + 1) * S5_PB, :] = blk.astype(m_ref.dtype)


def _s5_toeplitz(krow):
    return pl.pallas_call(
        _s5_toeplitz_kernel,
        grid=(S5_PAIRS,),
        in_specs=[pl.BlockSpec((1, S5_PB, S5_ROWS), lambda j: (j, 0, 0))],
        out_specs=pl.BlockSpec((1, S5_ROWS, S5_ROWS), lambda j: (j, 0, 0)),
        out_shape=jax.ShapeDtypeStruct((S5_PAIRS, S5_ROWS, S5_ROWS), BF16),
        compiler_params=_cparams(("parallel",)),
        name="s5_toeplitz",
    )(krow)


def _s5_tables(lam_re, lam_im, log_dt, b_re, b_im, c_re, c_im):
    ln = S5_CHUNK
    lr = jnp.minimum(lam_re.astype(F32), S5_MAX_REAL)
    li = lam_im.astype(F32)
    dt = jnp.exp(log_dt.astype(F32))[:, None]
    mag = jnp.exp(lr * dt)
    ab_re = mag * jnp.cos(li * dt)
    ab_im = mag * jnp.sin(li * dt)
    nr = ab_re - 1.0
    den = lr * lr + li * li
    cr = (nr * lr + ab_im * li) / den
    ci = (ab_im * lr - nr * li) / den
    bb_re = cr[..., None] * b_re - ci[..., None] * b_im
    bb_im = cr[..., None] * b_im + ci[..., None] * b_re
    tau = jnp.arange(ln + 1, dtype=F32)[:, None, None]
    pm = jnp.exp(lr * dt * tau)
    pr = pm * jnp.cos(li * dt * tau)
    pi = pm * jnp.sin(li * dt * tau)
    w1 = c_re[:, :, :, None] * bb_re[:, None, :, :] - c_im[:, :, :, None] * bb_im[:, None, :, :]
    w2 = c_re[:, :, :, None] * bb_im[:, None, :, :] + c_im[:, :, :, None] * bb_re[:, None, :, :]
    kk = (jnp.einsum('gopc,tgp->gtco', w1, pr[:ln], precision=HIGHEST)
          - jnp.einsum('gopc,tgp->gtco', w2, pi[:ln], precision=HIGHEST))
    eye2 = jnp.eye(2, dtype=F32)
    kk5 = kk.reshape(S5_PAIRS, 2, ln, S5_GROUP, S5_GROUP).transpose(0, 1, 3, 2, 4)
    m = _s5_toeplitz((kk5[:, :, :, :, None, :] * eye2[None, :, None, None, :, None]).reshape(S5_PAIRS, S5_PB, S5_ROWS))
    pe_r = pr[ln - 1 - jnp.arange(ln)]
    pe_i = pi[ln - 1 - jnp.arange(ln)]
    e_re = pe_r[:, :, :, None] * bb_re[None] - pe_i[:, :, :, None] * bb_im[None]
    e_im = pe_r[:, :, :, None] * bb_im[None] + pe_i[:, :, :, None] * bb_re[None]

    def pair_e(e):
        e5 = e.reshape(ln, S5_PAIRS, 2, S5_STATE, S5_GROUP).transpose(1, 0, 2, 4, 3)
        return (e5[:, :, :, :, None, :] * eye2[None, None, :, None, :, None]).reshape(S5_PAIRS, S5_ROWS, 2 * S5_STATE)

    e_pair = jnp.concatenate([pair_e(e_re), pair_e(e_im)], axis=2)
    pf_r = pr[1:].transpose(1, 2, 0)
    pf_i = pi[1:].transpose(1, 2, 0)
    cre = c_re.transpose(0, 2, 1)
    cim = c_im.transpose(0, 2, 1)
    f_re = cre[:, :, None, :] * pf_r[..., None] - cim[:, :, None, :] * pf_i[..., None]
    f_im = -(cre[:, :, None, :] * pf_i[..., None] + cim[:, :, None, :] * pf_r[..., None])

    def pair_f(f):
        f5 = f.reshape(S5_PAIRS, 2, S5_STATE, ln, S5_GROUP)
        return (f5[:, :, :, :, None, :] * eye2[None, :, None, None, :, None]).reshape(S5_PAIRS, 2 * S5_STATE, S5_ROWS)

    fr_pair, fi_pair = pair_f(f_re), pair_f(f_im)
    al = jnp.zeros((S5_PAIRS, SUBLANES, LANES), F32)
    al = al.at[:, 0, :].set(pr[ln].reshape(S5_PAIRS, 2 * S5_STATE))
    al = al.at[:, 1, :].set(pi[ln].reshape(S5_PAIRS, 2 * S5_STATE))
    return m.astype(BF16), e_pair.astype(BF16), fr_pair.astype(BF16), fi_pair.astype(BF16), al


def _s5_kernel(u_ref, m_ref, e_ref, fr_ref, fi_ref, al_ref, o_ref, ustage, ystage, sloc, xpr, xpi, *, nc):
    jj = pl.program_id(1)

    @pl.when(jj == 0)
    def _():
        for b in range(S5_ROWS // LANES):
            slabs = [u_ref[pl.ds(S5_SLAB * b + a, nc, stride=S5_CHUNK), :] for a in range(S5_SLAB)]
            for q in range(S5_SLAB):
                ustage[q, :, LANES * b:LANES * (b + 1)] = jnp.concatenate(
                    [sl[:, S5_PB * q:S5_PB * (q + 1)] for sl in slabs], axis=1).astype(BF16)

    u = ustage[jj]
    sloc[...] = jnp.dot(u, e_ref[0], preferred_element_type=F32)
    ar = al_ref[0, 0:1, :]
    ai = al_ref[0, 1:2, :]

    row = lax.broadcasted_iota(jnp.int32, (SUBLANES, LANES), 0)

    def body(ti, carry):
        xr, xi = carry
        r0 = pl.multiple_of(ti * SUBLANES, SUBLANES)
        sr_t = sloc[pl.ds(r0, SUBLANES), 0:LANES]
        si_t = sloc[pl.ds(r0, SUBLANES), LANES:2 * LANES]
        pr_t = jnp.zeros((SUBLANES, LANES), F32)
        pi_t = jnp.zeros((SUBLANES, LANES), F32)
        for r in range(SUBLANES):
            pr_t = jnp.where(row == r, xr, pr_t)
            pi_t = jnp.where(row == r, xi, pi_t)
            xr, xi = ar * xr - ai * xi + sr_t[r:r + 1, :], ar * xi + ai * xr + si_t[r:r + 1, :]
        xpr[pl.ds(r0, SUBLANES), :] = pr_t
        xpi[pl.ds(r0, SUBLANES), :] = pi_t
        return xr, xi

    zero = jnp.zeros((1, LANES), F32)
    lax.fori_loop(0, nc // SUBLANES, body, (zero, zero))
    ystage[jj] = (jnp.dot(xpr[...].astype(BF16), fr_ref[0], preferred_element_type=F32)
                  + jnp.dot(xpi[...].astype(BF16), fi_ref[0], preferred_element_type=F32)
                  + jnp.dot(u, m_ref[0], preferred_element_type=F32))

    @pl.when(jj == S5_SLAB - 1)
    def _():
        for t in range(S5_CHUNK):
            o_ref[pl.ds(t, nc, stride=S5_CHUNK), :] = jnp.concatenate(
                [ystage[q, :, S5_PB * t:S5_PB * (t + 1)] for q in range(S5_SLAB)], axis=1)


def _s5(z2, lam_re, lam_im, log_dt, b_re, b_im, c_re, c_im):
    t = z2.shape[0]
    nc = t // S5_CHUNK
    m, e_pair, fr_pair, fi_pair, al = _s5_tables(lam_re, lam_im, log_dt, b_re, b_im, c_re, c_im)
    pair = lambda a, b: pl.BlockSpec((1, a, b), lambda k, jj: (S5_SLAB * k + jj, 0, 0))
    return pl.pallas_call(
        functools.partial(_s5_kernel, nc=nc),
        grid=(WIDTH // LANES, S5_SLAB),
        in_specs=[pl.BlockSpec((t, LANES), lambda k, jj: (0, Z2_US * WIDTH // LANES + k), pipeline_mode=pl.Buffered(1)),
                  pair(S5_ROWS, S5_ROWS), pair(S5_ROWS, 2 * LANES), pair(LANES, S5_ROWS), pair(LANES, S5_ROWS),
                  pair(SUBLANES, LANES)],
        out_specs=pl.BlockSpec((t, LANES), lambda k, jj: (0, k), pipeline_mode=pl.Buffered(1)),
        out_shape=jax.ShapeDtypeStruct((t, WIDTH), F32),
        scratch_shapes=[pltpu.VMEM((S5_SLAB, nc, S5_ROWS), BF16), pltpu.VMEM((S5_SLAB, nc, S5_ROWS), F32),
                        pltpu.VMEM((nc, 2 * LANES), F32), pltpu.VMEM((nc, LANES), F32), pltpu.VMEM((nc, LANES), F32)],
        compiler_params=_cparams(("arbitrary", "arbitrary")),
        name="s5_scan",
    )(z2, m, e_pair, fr_pair, fi_pair, al)


def _layer_norm(x, g, b):
    mu = jnp.mean(x, axis=-1, keepdims=True)
    xc = x - mu
    var = jnp.mean(xc * xc, axis=-1, keepdims=True)
    return xc * lax.rsqrt(var + LN_EPS) * g + b


def _merge_kernel(x_ref, g0_ref, g1_ref, g2_ref, ym_ref, ys_ref, us_ref, yh_ref, wm_ref, ws_ref, wh_ref, wglu_ref,
                  d_ref, wo_ref, lg_ref, lb_ref, rw_ref, rb_ref, o_ref):
    ys = _gelu_tanh(ys_ref[...] + d_ref[...] * us_ref[...].astype(F32))
    ys = ys * _sigmoid(_dot(ys, wglu_ref[...]))
    merged = (_sigmoid(g0_ref[...].astype(F32)) * _dot(ym_ref[...], wm_ref[...])
              + _sigmoid(g1_ref[...].astype(F32)) * _dot(ys, ws_ref[...])
              + _sigmoid(g2_ref[...].astype(F32)) * _dot(yh_ref[...], wh_ref[...]))
    x1 = _layer_norm(DN_ALPHA * x_ref[...] + _dot(merged, wo_ref[...]), lg_ref[...], lb_ref[...])
    o_ref[:, 0:D_MODEL] = x1

    xh = x1.astype(BF16)
    xl = (x1 - xh.astype(F32)).astype(BF16)
    logits = (jnp.dot(xh, rw_ref[0], preferred_element_type=F32) + jnp.dot(xl, rw_ref[0], preferred_element_type=F32)
              + jnp.dot(xh, rw_ref[1], preferred_element_type=F32))
    s0 = _sigmoid(logits)
    sb0 = s0 + rb_ref[...]
    shift = lambda a, j: a if j == 0 else pltpu.roll(a, LANES - N_GROUPS * j, 1)
    s = [shift(s0, j) for j in range(EXPERTS_PER_GROUP)]
    sb = [shift(sb0, j) for j in range(EXPERTS_PER_GROUP)]
    hi1, lo1 = jnp.maximum(sb[0], sb[1]), jnp.minimum(sb[0], sb[1])
    hi2, lo2 = jnp.maximum(sb[2], sb[3]), jnp.minimum(sb[2], sb[3])
    top2 = jnp.maximum(hi1, hi2) + jnp.maximum(jnp.minimum(hi1, hi2), jnp.maximum(lo1, lo2))
    lane = lax.broadcasted_iota(jnp.int32, top2.shape, 1)
    top2 = jnp.where(lane < N_GROUPS, top2, -jnp.inf)
    gmax = jnp.max(top2, axis=1, keepdims=True)
    g_idx = jnp.min(jnp.where(top2 == gmax, lane, LANES), axis=1, keepdims=True)
    sel = lane == g_idx
    v = [jnp.sum(jnp.where(sel, sb[j], 0.0), axis=1, keepdims=True) for j in range(EXPERTS_PER_GROUP)]
    sv = [jnp.sum(jnp.where(sel, s[j], 0.0), axis=1, keepdims=True) for j in range(EXPERTS_PER_GROUP)]

    def first_max(vals):
        m = jnp.maximum(jnp.maximum(vals[0], vals[1]), jnp.maximum(vals[2], vals[3]))
        return jnp.where(vals[0] == m, 0, jnp.where(vals[1] == m, 1, jnp.where(vals[2] == m, 2, 3)))

    e1 = first_max(v)
    e2 = first_max([jnp.where(e1 == j, -jnp.inf, v[j]) for j in range(EXPERTS_PER_GROUP)])
    s1 = sum(jnp.where(e1 == j, sv[j], 0.0) for j in range(EXPERTS_PER_GROUP))
    s2 = sum(jnp.where(e2 == j, sv[j], 0.0) for j in range(EXPERTS_PER_GROUP))
    tot = s1 + s2
    meta = jnp.where(lane == 0, g_idx.astype(F32), 0.0)
    for j in range(EXPERTS_PER_GROUP):
        cw = jnp.where(e1 == j, s1 / tot, 0.0) + jnp.where(e2 == j, s2 / tot, 0.0)
        meta = jnp.where(lane == 1 + j, cw, meta)
    o_ref[:, D_MODEL:D_MODEL + META] = meta


def _merge(x, z1, z2, ym, ys, yh, wm, ws, wh, wglu, d, wo, lg, lb, router_w, router_bias, tm):
    t = x.shape[0]
    rw = router_w.astype(F32).reshape(D_MODEL, N_GROUPS, EXPERTS_PER_GROUP).transpose(0, 2, 1)
    rw = jnp.pad(rw.reshape(D_MODEL, N_EXPERTS), ((0, 0), (0, LANES - N_EXPERTS)))
    rw_hi = rw.astype(BF16)
    rw = jnp.stack([rw_hi, (rw - rw_hi.astype(F32)).astype(BF16)])
    rb = jnp.pad(router_bias.astype(F32).reshape(N_GROUPS, EXPERTS_PER_GROUP).T.reshape(1, N_EXPERTS),
                 ((0, 0), (0, LANES - N_EXPERTS)))
    full = lambda shape: pl.BlockSpec(shape, lambda i: (0,) * len(shape))
    row = lambda w: pl.BlockSpec((tm, w), lambda i: (i, 0))
    return pl.pallas_call(
        _merge_kernel,
        grid=(t // tm,),
        in_specs=[row(D_MODEL),
                  pl.BlockSpec((tm, D_MODEL), lambda i: (i, 0)),
                  pl.BlockSpec((tm, D_MODEL), lambda i: (i, 1)),
                  pl.BlockSpec((tm, D_MODEL), lambda i: (i, 2)),
                  row(WIDTH), row(WIDTH), pl.BlockSpec((tm, WIDTH), lambda i: (i, Z2_US)), row(WIDTH),
                  full((WIDTH, D_MODEL)), full((WIDTH, D_MODEL)), full((WIDTH, D_MODEL)), full((WIDTH, WIDTH)),
                  full((1, WIDTH)), full((D_MODEL, D_MODEL)), full((1, D_MODEL)), full((1, D_MODEL)),
                  full((2, D_MODEL, LANES)), full((1, LANES))],
        out_specs=pl.BlockSpec((tm, D_MODEL + META), lambda i: (i, 0)),
        out_shape=jax.ShapeDtypeStruct((t, D_MODEL + META), F32),
        compiler_params=_cparams(("parallel",)),
        name="merge_router",
    )(x, z1, z1, z1, ym, ys, z2, yh, wm.astype(BF16), ws.astype(BF16), wh.astype(BF16), wglu.astype(BF16),
      d.reshape(1, WIDTH).astype(F32), wo.astype(BF16), lg.reshape(1, D_MODEL).astype(F32),
      lb.reshape(1, D_MODEL).astype(F32), rw, rb)


def _start_row_gather(idx_ref, base, src_hbm, dst, sem, rows):
    for r in range(rows):
        pltpu.make_async_copy(src_hbm.at[pl.ds(idx_ref[base + r], 1)], dst.at[pl.ds(r, 1)], sem).start(priority=r % 2)


def _wait_row_gather(src_hbm, dst, sem, rows):
    pltpu.make_async_copy(src_hbm.at[pl.ds(0, rows)], dst, sem).wait()


def _moe_kernel(tg_ref, src_ref, x_hbm, wg_ref, wu_ref, wd_ref, o_ref, buf, sem, wg_s, wu_s, wd_s, *, tm, ntiles):
    i = pl.program_id(0)
    slot = i % 2

    @pl.when(i == 0)
    def _():
        _start_row_gather(src_ref, 0, x_hbm, buf.at[0], sem.at[0], tm)

    @pl.when(i + 1 < ntiles)
    def _():
        _start_row_gather(src_ref, (i + 1) * tm, x_hbm, buf.at[1 - slot], sem.at[1 - slot], tm)

    @pl.when(jnp.logical_or(i == 0, tg_ref[i] != tg_ref[jnp.maximum(i - 1, 0)]))
    def _():
        for e in range(EXPERTS_PER_GROUP):
            wg_s[e] = wg_ref[0, e].astype(BF16)
            wu_s[e] = wu_ref[0, e].astype(BF16)
            wd_s[e] = wd_ref[0, e].astype(BF16)

    _wait_row_gather(x_hbm, buf.at[slot], sem.at[slot], tm)
    xt = buf[slot]
    xb = xt[:, 0:D_MODEL].astype(BF16)
    meta = xt[:, D_MODEL:D_MODEL + META]
    y = jnp.zeros((tm, D_MODEL), F32)
    for e in range(EXPERTS_PER_GROUP):
        hg = jnp.dot(xb, wg_s[e], preferred_element_type=F32)
        hu = jnp.dot(xb, wu_s[e], preferred_element_type=F32)
        hh = _silu(hg) * hu * _lane_col(meta, 1 + e)
        y = y + jnp.dot(hh.astype(BF16), wd_s[e], preferred_element_type=F32)
    o_ref[...] = y


def _ln2_kernel(dst_ref, x_ref, y_hbm, lg_ref, lb_ref, o_ref, buf, sem, *, tm, ntiles):
    i = pl.program_id(0)
    slot = i % 2

    @pl.when(i == 0)
    def _():
        _start_row_gather(dst_ref, 0, y_hbm, buf.at[0], sem.at[0], tm)

    @pl.when(i + 1 < ntiles)
    def _():
        _start_row_gather(dst_ref, (i + 1) * tm, y_hbm, buf.at[1 - slot], sem.at[1 - slot], tm)

    _wait_row_gather(y_hbm, buf.at[slot], sem.at[slot], tm)
    o_ref[...] = _layer_norm(DN_ALPHA * x_ref[...] + buf[slot], lg_ref[...], lb_ref[...])


def _moe_ln(x1e, wg, wu, wd, layer, lg, lb, tm):
    t = x1e.shape[0]
    ntiles = t // tm + N_GROUPS
    p = ntiles * tm
    key = x1e[:, D_MODEL].astype(jnp.int32)
    onehot = (key[:, None] == jnp.arange(N_GROUPS)[None, :]).astype(jnp.int32)
    csum = jnp.cumsum(onehot, axis=0)
    counts = csum[-1]
    rank = jnp.sum(onehot * csum, axis=1) - 1
    pcount = ((counts + tm - 1) // tm) * tm
    pend = jnp.cumsum(pcount)
    dest = (pend - pcount)[key] + rank
    row_src = jnp.zeros((p,), jnp.int32).at[dest].set(jnp.arange(t, dtype=jnp.int32))
    tile_start = jnp.arange(ntiles, dtype=jnp.int32) * tm
    tile_group = jnp.minimum(jnp.sum((tile_start[:, None] >= pend[None, :]).astype(jnp.int32), axis=1),
                             N_GROUPS - 1)
    wspec = lambda a, b: pl.BlockSpec((1, EXPERTS_PER_GROUP, a, b), lambda i, tg, src: (layer, tg[i], 0, 0))
    wscr = lambda a, b: pltpu.VMEM((EXPERTS_PER_GROUP, a, b), BF16)
    y_sorted = pl.pallas_call(
        functools.partial(_moe_kernel, tm=tm, ntiles=ntiles),
        grid_spec=pltpu.PrefetchScalarGridSpec(
            num_scalar_prefetch=2,
            grid=(ntiles,),
            in_specs=[pl.BlockSpec(memory_space=pl.ANY), wspec(D_MODEL, D_EXPERT), wspec(D_MODEL, D_EXPERT),
                      wspec(D_EXPERT, D_MODEL)],
            out_specs=pl.BlockSpec((tm, D_MODEL), lambda i, tg, src: (i, 0)),
            scratch_shapes=[pltpu.VMEM((2, tm, D_MODEL + META), F32), pltpu.SemaphoreType.DMA((2,)),
                            wscr(D_MODEL, D_EXPERT), wscr(D_MODEL, D_EXPERT), wscr(D_EXPERT, D_MODEL)]),
        out_shape=jax.ShapeDtypeStruct((p, D_MODEL), F32),
        compiler_params=_cparams(("arbitrary",)),
        name="moe_experts",
    )(tile_group, row_src, x1e, wg, wu, wd)
    nt2 = t // tm
    return pl.pallas_call(
        functools.partial(_ln2_kernel, tm=tm, ntiles=nt2),
        grid_spec=pltpu.PrefetchScalarGridSpec(
            num_scalar_prefetch=1,
            grid=(nt2,),
            in_specs=[pl.BlockSpec((tm, D_MODEL), lambda i, d: (i, 0)), pl.BlockSpec(memory_space=pl.ANY),
                      pl.BlockSpec((1, D_MODEL), lambda i, d: (0, 0)), pl.BlockSpec((1, D_MODEL), lambda i, d: (0, 0))],
            out_specs=pl.BlockSpec((tm, D_MODEL), lambda i, d: (i, 0)),
            scratch_shapes=[pltpu.VMEM((2, tm, D_MODEL), F32), pltpu.SemaphoreType.DMA((2,))]),
        out_shape=jax.ShapeDtypeStruct((t, D_MODEL), F32),
        compiler_params=_cparams(("arbitrary",)),
        name="moe_combine_ln",
    )(dest.astype(jnp.int32), x1e, y_sorted, lg.reshape(1, D_MODEL).astype(F32), lb.reshape(1, D_MODEL).astype(F32))


def _layer(x, l, p, lb_l, cfg):
    t = x.shape[0]
    w1, w2, wt = _split_w_in(p['w_in'][l])
    xb = x.astype(BF16)
    z1 = _matmul(xb, w1, Z1_DTYPE, cfg['tm_in'], cfg['tn_in'])
    z2, gt = _inproj_gate(xb, w2, wt, cfg['tm_in'])
    gt3 = gt.reshape(SUBLANES, t // CHUNK, CHUNK).transpose(1, 0, 2)
    ym = _mlstm(z1, z2, gt3, p['mlstm_conv_w'][l], p['mlstm_conv_b'][l], p['mlstm_wq'][l], p['mlstm_wk'][l],
                p['mlstm_wv'][l], p['mlstm_b_i'][l], p['mlstm_b_f'][l], p['mlstm_skip'][l], cfg['tb'])
    ys = _s5(z2, p['s5_lambda_re'][l], p['s5_lambda_im'][l], p['s5_log_dt'][l], p['s5_b_re'][l], p['s5_b_im'][l],
             p['s5_c_re'][l], p['s5_c_im'][l])
    yh = _hgrn(z1, z2, lb_l, p['hgrn_norm_w'][l], cfg['tb'])
    x1e = _merge(x, z1, z2, ym, ys, yh, p['w_branch_mlstm'][l], p['w_branch_s5'][l], p['w_branch_hgrn'][l],
                 p['s5_w_glu'][l], p['s5_d'][l], p['w_out'][l], p['ln1_g'][l], p['ln1_b'][l], p['router_w'],
                 p['router_bias'], cfg['tm_merge'])
    return _moe_ln(x1e, p['exp_w_gate'], p['exp_w_up'], p['exp_w_down'], l, p['ln2_g'][l], p['ln2_b'][l],
                   cfg['tm_moe'])


_CFG = dict(tm_in=1024, tn_in=Z1_COLS // 4, tb=512, tm_merge=256, tm_moe=256)


def kernel(x, w_in, mlstm_conv_w, mlstm_conv_b, mlstm_wq, mlstm_wk, mlstm_wv, mlstm_b_i, mlstm_b_f, mlstm_skip, s5_lambda_re, s5_lambda_im, s5_log_dt, s5_b_re, s5_b_im, s5_c_re, s5_c_im, s5_d, s5_w_glu, hgrn_lower_bounds, hgrn_norm_w, w_branch_mlstm, w_branch_s5, w_branch_hgrn, w_out, ln1_g, ln1_b, ln2_g, ln2_b, router_w, router_bias, exp_w_gate, exp_w_up, exp_w_down):
    p = dict(w_in=w_in, mlstm_conv_w=mlstm_conv_w, mlstm_conv_b=mlstm_conv_b, mlstm_wq=mlstm_wq, mlstm_wk=mlstm_wk,
             mlstm_wv=mlstm_wv, mlstm_b_i=mlstm_b_i, mlstm_b_f=mlstm_b_f, mlstm_skip=mlstm_skip,
             s5_lambda_re=s5_lambda_re, s5_lambda_im=s5_lambda_im, s5_log_dt=s5_log_dt, s5_b_re=s5_b_re,
             s5_b_im=s5_b_im, s5_c_re=s5_c_re, s5_c_im=s5_c_im, s5_d=s5_d, s5_w_glu=s5_w_glu,
             hgrn_norm_w=hgrn_norm_w, w_branch_mlstm=w_branch_mlstm, w_branch_s5=w_branch_s5,
             w_branch_hgrn=w_branch_hgrn, w_out=w_out, ln1_g=ln1_g, ln1_b=ln1_b, ln2_g=ln2_g, ln2_b=ln2_b,
             router_w=router_w, router_bias=router_bias, exp_w_gate=exp_w_gate, exp_w_up=exp_w_up,
             exp_w_down=exp_w_down)
    lb_cum = jnp.cumsum(jax.nn.softmax(hgrn_lower_bounds.astype(F32), axis=0), axis=0)
    lb_layers = lb_cum - lb_cum[0]
    bsz, seq, d = x.shape
    h = x.reshape(bsz * seq, d)
    for l in range(DEPTH):
        h = _layer(h, l, p, lb_layers[l], _CFG)
    return h.reshape(bsz, seq, d)
```

```python
import functools
import math

import jax
import jax.numpy as jnp
from jax import lax
from jax.experimental import pallas as pl
from jax.experimental.pallas import tpu as pltpu

F32 = jnp.float32
BF16 = jnp.bfloat16
HIGHEST = lax.Precision.HIGHEST

D_MODEL = 1024
DEPTH = 2
HEADS = 4
HEAD_DIM = 128
WIDTH = HEADS * HEAD_DIM
MLSTM_CONV = 4
CHUNK = 64
S5_GROUP = 16
S5_GROUPS = 32
S5_STATE = 64
S5_PAIRS = S5_GROUPS // 2
S5_CHUNK = 32
S5_MAX_REAL = -1e-4
N_EXPERTS = 32
N_GROUPS = 8
EXPERTS_PER_GROUP = 4
D_EXPERT = 256
DN_ALPHA = (2 * DEPTH) ** 0.25
LN_EPS = 1e-5
NORM_EPS = 1e-6

LANES = 128
SUBLANES = 8
META = LANES
VMEM_LIMIT = 56 * 1024 * 1024

Z1_DTYPE = BF16
Z1_GATE, Z1_XM, Z1_OM, Z1_QH, Z1_IH, Z1_GH = 0, 6, 7, 8, 9, 10
Z1_COLS = 11 * WIDTH
Z2_FH, Z2_US = 0, 1
Z2_COLS = 2 * WIDTH + LANES


def _cparams(sem):
    return pltpu.CompilerParams(dimension_semantics=sem, vmem_limit_bytes=VMEM_LIMIT)


def _sigmoid(x):
    return 0.5 * (1.0 + jnp.tanh(0.5 * x))


def _silu(x):
    return x * _sigmoid(x)


def _log_sigmoid(x):
    return jnp.minimum(x, 0.0) - jnp.log(1.0 + jnp.exp(-jnp.abs(x)))


def _gelu_tanh(x):
    return 0.5 * x * (1.0 + jnp.tanh(math.sqrt(2.0 / math.pi) * (x + 0.044715 * (x * x * x))))


def _cumsum_rows(x):
    n = x.shape[0]
    row = lax.broadcasted_iota(jnp.int32, x.shape, 0)
    s = 1
    while s < n:
        x = x + jnp.where(row >= s, pltpu.roll(x, s, 0), 0.0)
        s *= 2
    return x


def _lane_col(x, idx):
    lane = lax.broadcasted_iota(jnp.int32, x.shape, 1)
    return jnp.sum(jnp.where(lane == idx, x, 0.0), axis=1, keepdims=True)


def _dot(a, b):
    return jnp.dot(a.astype(BF16), b.astype(BF16), preferred_element_type=F32)


def _dot_nt(a, b):
    return lax.dot_general(a.astype(BF16), b.astype(BF16), (((1,), (1,)), ((), ())), preferred_element_type=F32)


def _dot_tn(a, b):
    return lax.dot_general(a.astype(BF16), b.astype(BF16), (((0,), (0,)), ((), ())), preferred_element_type=F32)


def _mm_kernel(x_ref, w_ref, o_ref):
    o_ref[...] = jnp.dot(x_ref[...], w_ref[...], preferred_element_type=F32).astype(o_ref.dtype)


def _matmul(x, w, out_dtype, tm, tn):
    m, k = x.shape
    n = w.shape[1]
    return pl.pallas_call(
        _mm_kernel,
        grid=(m // tm, n // tn),
        in_specs=[pl.BlockSpec((tm, k), lambda i, j: (i, 0)), pl.BlockSpec((k, tn), lambda i, j: (0, j))],
        out_specs=pl.BlockSpec((tm, tn), lambda i, j: (i, j)),
        out_shape=jax.ShapeDtypeStruct((m, n), out_dtype),
        compiler_params=_cparams(("parallel", "parallel")),
        name="in_proj_wide",
    )(x, w)


def _inproj_gate_kernel(x_ref, w_ref, wt_ref, o_ref, gt_ref):
    x = x_ref[...]
    o_ref[...] = jnp.dot(x, w_ref[...], preferred_element_type=F32)
    gt_ref[...] = lax.dot_general(wt_ref[...], x, (((1,), (1,)), ((), ())), preferred_element_type=F32)


def _inproj_gate(x, w2, wt, tm):
    m, k = x.shape
    return pl.pallas_call(
        _inproj_gate_kernel,
        grid=(m // tm,),
        in_specs=[pl.BlockSpec((tm, k), lambda i: (i, 0)),
                  pl.BlockSpec((k, Z2_COLS), lambda i: (0, 0)),
                  pl.BlockSpec((SUBLANES, k), lambda i: (0, 0))],
        out_specs=[pl.BlockSpec((tm, Z2_COLS), lambda i: (i, 0)), pl.BlockSpec((SUBLANES, tm), lambda i: (0, i))],
        out_shape=[jax.ShapeDtypeStruct((m, Z2_COLS), F32), jax.ShapeDtypeStruct((SUBLANES, m), F32)],
        compiler_params=_cparams(("parallel",)),
        name="in_proj_gates",
    )(x, w2, wt)


def _split_w_in(w):
    offs, o = [], 0
    for s in (WIDTH, WIDTH, HEADS, HEADS, WIDTH, WIDTH, WIDTH, WIDTH, WIDTH, 3 * D_MODEL):
        offs.append((o, o + s))
        o += s
    seg = [w[:, a:b] for a, b in offs]
    xm, om, im, fm, us, qh, fh, ih, gh, gate = seg
    w1 = jnp.concatenate([gate, xm, om, qh, ih, gh], axis=1).astype(BF16)
    pad = jnp.zeros((w.shape[0], LANES - 2 * HEADS), w.dtype)
    w2 = jnp.concatenate([fh, us, im, fm, pad], axis=1).astype(BF16)
    wt = jnp.concatenate([im, fm], axis=1).T.astype(BF16)
    return w1, w2, wt


def _mlstm_kernel(xm_ref, om_ref, gc_ref, gr_ref, cw_ref, cb_ref, wq_ref, wk_ref, wv_ref, bcol_ref, brow_ref,
                  skip_ref, triu_ref, o_ref, xpad, q_s, k_s, v_s, xc_s, h_s, gcol_s, grow_s, brow_s, ct_s, n_s, m_s,
                  *, tb):
    ncb = tb // CHUNK

    @pl.when(pl.program_id(0) == 0)
    def _():
        xpad[0:SUBLANES, :] = jnp.zeros((SUBLANES, WIDTH), F32)
        ct_s[...] = jnp.zeros_like(ct_s)
        n_s[...] = jnp.zeros_like(n_s)
        m_s[...] = jnp.zeros_like(m_s)

    xm = xm_ref[...].astype(F32)
    xpad[SUBLANES:SUBLANES + tb, :] = xm
    cw = cw_ref[...]
    conv = cb_ref[...] + cw[3:4, :] * xm
    for d in range(1, MLSTM_CONV):
        conv = conv + cw[3 - d:4 - d, :] * xpad[SUBLANES - d:SUBLANES - d + tb, :]
    xpad[0:SUBLANES, :] = xpad[tb:tb + SUBLANES, :]
    xc = _silu(conv)
    xc_s[...] = xc
    for h in range(HEADS):
        sl = slice(h * HEAD_DIM, (h + 1) * HEAD_DIM)
        xch = xc[:, sl].astype(BF16)
        q_s[:, sl] = jnp.dot(xch, wq_ref[h], preferred_element_type=F32) * (HEAD_DIM ** -0.5)
        k_s[:, sl] = jnp.dot(xch, wk_ref[h], preferred_element_type=F32)
        v_s[:, sl] = jnp.dot(xm[:, sl].astype(BF16), wv_ref[h], preferred_element_type=F32)

    gc = gc_ref[...] + bcol_ref[...]
    lane = lax.broadcasted_iota(jnp.int32, gc.shape, 1)
    gcol_s[...] = jnp.where(lane < HEADS, gc, _log_sigmoid(gc))
    gr = gr_ref[...] + brow_ref[...]
    sub = lax.broadcasted_iota(jnp.int32, gr.shape, 1)
    gr = jnp.where(sub < HEADS, gr, _log_sigmoid(gr)).reshape(ncb * SUBLANES, CHUNK)
    grow_s[...] = gr
    brow_s[...] = jnp.dot(gr, triu_ref[...], precision=HIGHEST, preferred_element_type=F32)

    rowi = lax.broadcasted_iota(jnp.int32, (CHUNK, CHUNK), 0)
    coli = lax.broadcasted_iota(jnp.int32, (CHUNK, CHUNK), 1)
    causal = rowi >= coli

    def chunk_body(c, carry):
        r0 = pl.multiple_of(c * CHUNK, CHUNK)
        g0 = pl.multiple_of(c * SUBLANES, SUBLANES)
        gcol = gcol_s[pl.ds(r0, CHUNK), :]
        bcol = _cumsum_rows(gcol)
        grow = grow_s[pl.ds(g0, SUBLANES), :]
        brow = brow_s[pl.ds(g0, SUBLANES), :]
        for h in range(HEADS):
            sl = slice(h * HEAD_DIM, (h + 1) * HEAD_DIM)
            li_c = _lane_col(gcol, h)
            b_c = _lane_col(bcol, HEADS + h)
            li_r = grow[h:h + 1, :]
            b_r = brow[HEADS + h:HEADS + h + 1, :]
            b_tot = b_c[CHUNK - 1:CHUNK, :]
            a_c = b_tot - b_c + li_c
            a_max = jnp.max(a_c, axis=0, keepdims=True)
            q = q_s[pl.ds(r0, CHUNK), sl]
            k = k_s[pl.ds(r0, CHUNK), sl]
            v = v_s[pl.ds(r0, CHUNK), sl]
            wk = jnp.exp(a_c - a_max) * k
            c_loc_t = _dot_tn(v, wk)
            n_loc = jnp.sum(wk, axis=0, keepdims=True)
            ct_prev = ct_s[h]
            n_prev = n_s[h:h + 1, :]
            m_prev = m_s[h:h + 1, 0:1]
            d_mat = jnp.where(causal, b_c - b_r + li_r, -jnp.inf)
            m_inter = b_c + m_prev
            m_j = jnp.maximum(m_inter, jnp.max(d_mat, axis=1, keepdims=True))
            sc = _dot_nt(q, k) * jnp.exp(d_mat - m_j)
            g_inter = jnp.exp(m_inter - m_j)
            num = g_inter * _dot_nt(q, ct_prev) + _dot(sc, v)
            den = g_inter * jnp.sum(q * n_prev, axis=1, keepdims=True) + jnp.sum(sc, axis=1, keepdims=True)
            h_s[pl.ds(r0, CHUNK), sl] = num / jnp.maximum(jnp.abs(den), jnp.exp(-m_j))
            m_new = jnp.maximum(b_tot + m_prev, a_max)
            g_old = jnp.exp(b_tot + m_prev - m_new)
            g_loc = jnp.exp(a_max - m_new)
            ct_s[h] = g_old * ct_prev + g_loc * c_loc_t
            n_s[h:h + 1, :] = g_old * n_prev + g_loc * n_loc
            m_s[h:h + 1, :] = jnp.broadcast_to(m_new, (1, LANES))
        return carry

    lax.fori_loop(0, ncb, chunk_body, 0, unroll=2)
    o_ref[...] = (_sigmoid(om_ref[...].astype(F32)) * h_s[...] + skip_ref[...] * xc_s[...]).astype(o_ref.dtype)


def _mlstm(z1, z2, gt3, conv_w, conv_b, wq, wk, wv, b_i, b_f, skip, tb):
    t = z1.shape[0]
    ncb = tb // CHUNK
    bias = jnp.concatenate([b_i, b_f]).astype(F32)
    bcol = jnp.zeros((1, LANES), F32).at[0, :2 * HEADS].set(bias)
    brow = bias.reshape(2 * HEADS, 1)
    triu = jnp.triu(jnp.ones((CHUNK, CHUNK), F32))
    full = lambda shape: pl.BlockSpec(shape, lambda i: (0,) * len(shape))
    return pl.pallas_call(
        functools.partial(_mlstm_kernel, tb=tb),
        grid=(t // tb,),
        in_specs=[pl.BlockSpec((tb, WIDTH), lambda i: (i, Z1_XM)),
                  pl.BlockSpec((tb, WIDTH), lambda i: (i, Z1_OM)),
                  pl.BlockSpec((tb, LANES), lambda i: (i, 2 * WIDTH // LANES)),
                  pl.BlockSpec((ncb, SUBLANES, CHUNK), lambda i: (i, 0, 0)),
                  full((MLSTM_CONV, WIDTH)), full((1, WIDTH)),
                  full((HEADS, HEAD_DIM, HEAD_DIM)), full((HEADS, HEAD_DIM, HEAD_DIM)),
                  full((HEADS, HEAD_DIM, HEAD_DIM)),
                  full((1, LANES)), full((2 * HEADS, 1)), full((1, WIDTH)), full((CHUNK, CHUNK))],
        out_specs=pl.BlockSpec((tb, WIDTH), lambda i: (i, 0)),
        out_shape=jax.ShapeDtypeStruct((t, WIDTH), F32),
        scratch_shapes=[pltpu.VMEM((tb + SUBLANES, WIDTH), F32)]
        + [pltpu.VMEM((tb, WIDTH), F32) for _ in range(5)]
        + [pltpu.VMEM((tb, LANES), F32),
           pltpu.VMEM((ncb * SUBLANES, CHUNK), F32), pltpu.VMEM((ncb * SUBLANES, CHUNK), F32),
           pltpu.VMEM((HEADS, HEAD_DIM, HEAD_DIM), F32), pltpu.VMEM((SUBLANES, LANES), F32),
           pltpu.VMEM((SUBLANES, LANES), F32)],
        compiler_params=_cparams(("arbitrary",)),
        name="mlstm",
    )(z1, z1, z2, gt3, conv_w.astype(F32), conv_b.reshape(1, WIDTH).astype(F32), wq.astype(BF16), wk.astype(BF16),
      wv.astype(BF16), bcol, brow, skip.reshape(1, WIDTH).astype(F32), triu)


def _hgrn_kernel(q_ref, f_ref, i_ref, g_ref, lb_ref, nw_ref, tri_ref, o_ref, st_s, *, tb):
    ncb = tb // CHUNK

    @pl.when(pl.program_id(0) == 0)
    def _():
        st_s[...] = jnp.zeros_like(st_s)

    lb = lb_ref[...]
    nw = nw_ref[...]
    tri = tri_ref[...]
    rowi = lax.broadcasted_iota(jnp.int32, (CHUNK, WIDTH), 0)
    sr = lax.broadcasted_iota(jnp.int32, (CHUNK, CHUNK), 0)
    sc = lax.broadcasted_iota(jnp.int32, (CHUNK, CHUNK), 1)
    halves = [1 << p for p in range(CHUNK.bit_length() - 1)]
    upper = {m: (rowi & m) != 0 for m in halves}
    same_blk = {m: (sr // (2 * m)) == (sc // (2 * m)) for m in halves}

    def chunk_body(c, carry):
        r0 = pl.multiple_of(c * CHUNK, CHUNK)
        f = lb + (1.0 - lb) * _sigmoid(f_ref[pl.ds(r0, CHUNK), :].astype(F32))
        k = 1.0 - f
        q = _silu(q_ref[pl.ds(r0, CHUNK), :].astype(F32))
        v = i_ref[pl.ds(r0, CHUNK), :].astype(F32)
        lf = jnp.log(f)
        hi = lf.astype(BF16)
        r1 = lf - hi.astype(F32)
        mid = r1.astype(BF16)
        lo = (r1 - mid.astype(F32)).astype(BF16)
        b = (jnp.dot(tri, hi, preferred_element_type=F32) + jnp.dot(tri, mid, preferred_element_type=F32)
             + jnp.dot(tri, lo, preferred_element_type=F32))
        qs, ks = {}, {}
        for m in halves:
            if m == 1:
                t = jnp.where(upper[m], q * f, k)
            else:
                nblk = CHUNK // (2 * m)
                r = jnp.concatenate([jnp.broadcast_to(b[2 * m * j + m - 1:2 * m * j + m, :], (2 * m, WIDTH))
                                     for j in range(nblk)], axis=0)
                t = jnp.where(upper[m], q, k) * jnp.exp(-jnp.abs(b - r))
            qs[m] = jnp.where(upper[m], t, 0.0).astype(BF16)
            ks[m] = jnp.where(upper[m], 0.0, t).astype(BF16)
        b_last = b[CHUNK - 1:CHUNK, :]
        qe = q * jnp.exp(b)
        kd = k * jnp.exp(b_last - b)
        e_last = jnp.exp(b_last)
        outs = []
        for h in range(HEADS):
            sl = slice(h * HEAD_DIM, (h + 1) * HEAD_DIM)
            st = st_s[h]
            a = jnp.where(sr == sc, _dot_nt(q[:, sl], k[:, sl]), 0.0)
            for m in halves:
                a = a + jnp.where(same_blk[m], _dot_nt(qs[m][:, sl], ks[m][:, sl]), 0.0)
            oh = _dot_nt(qe[:, sl], st) + _dot(a, v[:, sl])
            st_s[h] = e_last[:, sl] * st + _dot_tn(v[:, sl], kd[:, sl])
            ms = jnp.sum(oh * oh, axis=1, keepdims=True) * (1.0 / HEAD_DIM)
            outs.append(oh * lax.rsqrt(ms + NORM_EPS))
        on = jnp.concatenate(outs, axis=1)
        g = g_ref[pl.ds(r0, CHUNK), :].astype(F32)
        o_ref[pl.ds(r0, CHUNK), :] = (on * nw * _silu(g)).astype(o_ref.dtype)
        return carry

    lax.fori_loop(0, ncb, chunk_body, 0, unroll=2)


def _hgrn(z1, z2, lb, norm_w, tb):
    t = z1.shape[0]
    tril = jnp.tril(jnp.ones((CHUNK, CHUNK), F32)).astype(BF16)
    full = lambda shape: pl.BlockSpec(shape, lambda i: (0,) * len(shape))
    return pl.pallas_call(
        functools.partial(_hgrn_kernel, tb=tb),
        grid=(t // tb,),
        in_specs=[pl.BlockSpec((tb, WIDTH), lambda i: (i, Z1_QH)),
                  pl.BlockSpec((tb, WIDTH), lambda i: (i, 0)),
                  pl.BlockSpec((tb, WIDTH), lambda i: (i, Z1_IH)),
                  pl.BlockSpec((tb, WIDTH), lambda i: (i, Z1_GH)),
                  full((1, WIDTH)), full((1, WIDTH)), full((CHUNK, CHUNK))],
        out_specs=pl.BlockSpec((tb, WIDTH), lambda i: (i, 0)),
        out_shape=jax.ShapeDtypeStruct((t, WIDTH), F32),
        scratch_shapes=[pltpu.VMEM((HEADS, HEAD_DIM, HEAD_DIM), F32)],
        compiler_params=_cparams(("arbitrary",)),
        name="hgrn2",
    )(z1, z2, z1, z1, lb.reshape(1, WIDTH).astype(F32), norm_w.reshape(1, WIDTH).astype(F32), tril)


S5_PB = 2 * S5_GROUP
S5_ROWS = S5_CHUNK * S5_PB
S5_SLAB = LANES // S5_PB


def _s5_toeplitz_kernel(k_ref, m_ref):
    krow = k_ref[0]
    lane = lax.broadcasted_iota(jnp.int32, krow.shape, 1)
    for s in range(S5_CHUNK):
        blk = krow if s == 0 else jnp.where(lane >= s * S5_PB, pltpu.roll(krow, s * S5_PB, 1), 0.0)
        m_ref[0, s * S5_PB:(s + 1) * S5_PB, :] = blk.astype(m_ref.dtype)


def _s5_toeplitz(krow):
    return pl.pallas_call(
        _s5_toeplitz_kernel,
        grid=(S5_PAIRS,),
        in_specs=[pl.BlockSpec((1, S5_PB, S5_ROWS), lambda j: (j, 0, 0))],
        out_specs=pl.BlockSpec((1, S5_ROWS, S5_ROWS), lambda j: (j, 0, 0)),
        out_shape=jax.ShapeDtypeStruct((S5_PAIRS, S5_ROWS, S5_ROWS), BF16),
        compiler_params=_cparams(("parallel",)),
        name="s5_toeplitz",
    )(krow)


def _s5_tables(lam_re, lam_im, log_dt, b_re, b_im, c_re, c_im):
    ln = S5_CHUNK
    lr = jnp.minimum(lam_re.astype(F32), S5_MAX_REAL)
    li = lam_im.astype(F32)
    dt = jnp.exp(log_dt.astype(F32))[:, None]
    mag = jnp.exp(lr * dt)
    ab_re = mag * jnp.cos(li * dt)
    ab_im = mag * jnp.sin(li * dt)
    nr = ab_re - 1.0
    den = lr * lr + li * li
    cr = (nr * lr + ab_im * li) / den
    ci = (ab_im * lr - nr * li) / den
    bb_re = cr[..., None] * b_re - ci[..., None] * b_im
    bb_im = cr[..., None] * b_im + ci[..., None] * b_re
    tau = jnp.arange(ln + 1, dtype=F32)[:, None, None]
    pm = jnp.exp(lr * dt * tau)
    pr = pm * jnp.cos(li * dt * tau)
    pi = pm * jnp.sin(li * dt * tau)
    def block_diag(a):
        a0, a1 = a[0::2], a[1::2]
        z = jnp.zeros_like(a0)
        return jnp.concatenate([jnp.concatenate([a0, z], axis=2), jnp.concatenate([z, a1], axis=2)], axis=1)

    def pair_pow(p):
        return p.reshape(p.shape[0], S5_PAIRS, 2 * S5_STATE).transpose(1, 0, 2)

    bbr = block_diag(bb_re.transpose(0, 2, 1))
    bbi = block_diag(bb_im.transpose(0, 2, 1))
    ccr = block_diag(c_re.transpose(0, 2, 1))
    cci = block_diag(c_im.transpose(0, 2, 1))
    col = jnp.arange(S5_ROWS)
    rep_t = (col[None, :] // S5_PB == jnp.arange(ln)[:, None]).astype(F32)
    rep_c = (col[None, :] % S5_PB == jnp.arange(S5_PB)[:, None]).astype(F32)
    lanes_c = lambda a: jnp.einsum('jrq,qc->jrc', a, rep_c, precision=HIGHEST)
    lanes_t = lambda p: jnp.einsum('jtr,tc->jrc', pair_pow(p), rep_t, precision=HIGHEST)
    ccr_l, cci_l = lanes_c(ccr), lanes_c(cci)

    def out_tables(p_r, p_i):
        pr_l, pi_l = lanes_t(p_r), lanes_t(p_i)
        return ccr_l * pr_l - cci_l * pi_l, -(ccr_l * pi_l + cci_l * pr_l)

    fr_pair, fi_pair = out_tables(pr[1:], pi[1:])
    fk_re, fk_im = out_tables(pr[:ln], pi[:ln])
    krow = jnp.einsum('jrk,jkc->jrc', jnp.concatenate([bbr, bbi], axis=2), jnp.concatenate([fk_re, fk_im], axis=1),
                      precision=HIGHEST)
    m = _s5_toeplitz(krow)
    rows_of = lambda a: jnp.broadcast_to(a[:, :, None, :], (S5_PAIRS, a.shape[1], S5_PB, a.shape[2])).reshape(
        S5_PAIRS, a.shape[1] * S5_PB, a.shape[2])
    pe_r = rows_of(pair_pow(pr[ln - 1 - jnp.arange(ln)]))
    pe_i = rows_of(pair_pow(pi[ln - 1 - jnp.arange(ln)]))
    tile_s = lambda a: jnp.broadcast_to(a[:, None], (S5_PAIRS, ln) + a.shape[1:]).reshape(S5_PAIRS, S5_ROWS, a.shape[2])
    bbr_s, bbi_s = tile_s(bbr), tile_s(bbi)
    e_pair = jnp.concatenate([pe_r * bbr_s - pe_i * bbi_s, pe_r * bbi_s + pe_i * bbr_s], axis=2)
    al = jnp.zeros((S5_PAIRS, SUBLANES, LANES), F32)
    al = al.at[:, 0, :].set(pr[ln].reshape(S5_PAIRS, 2 * S5_STATE))
    al = al.at[:, 1, :].set(pi[ln].reshape(S5_PAIRS, 2 * S5_STATE))
    return m.astype(BF16), e_pair.astype(BF16), fr_pair.astype(BF16), fi_pair.astype(BF16), al


def _s5_kernel(u_ref, m_ref, e_ref, fr_ref, fi_ref, al_ref, o_ref, ustage, ystage, sloc, xpr, xpi, *, nc):
    jj = pl.program_id(1)

    @pl.when(jj == 0)
    def _():
        for b in range(S5_ROWS // LANES):
            slabs = [u_ref[pl.ds(S5_SLAB * b + a, nc, stride=S5_CHUNK), :] for a in range(S5_SLAB)]
            for q in range(S5_SLAB):
                ustage[q, :, LANES * b:LANES * (b + 1)] = jnp.concatenate(
                    [sl[:, S5_PB * q:S5_PB * (q + 1)] for sl in slabs], axis=1).astype(BF16)

    u = ustage[jj]
    sloc[...] = jnp.dot(u, e_ref[0], preferred_element_type=F32)
    ar = al_ref[0, 0:1, :]
    ai = al_ref[0, 1:2, :]

    row = lax.broadcasted_iota(jnp.int32, (SUBLANES, LANES), 0)

    def body(ti, carry):
        xr, xi = carry
        r0 = pl.multiple_of(ti * SUBLANES, SUBLANES)
        sr_t = sloc[pl.ds(r0, SUBLANES), 0:LANES]
        si_t = sloc[pl.ds(r0, SUBLANES), LANES:2 * LANES]
        pr_t = jnp.zeros((SUBLANES, LANES), F32)
        pi_t = jnp.zeros((SUBLANES, LANES), F32)
        for r in range(SUBLANES):
            pr_t = jnp.where(row == r, xr, pr_t)
            pi_t = jnp.where(row == r, xi, pi_t)
            xr, xi = ar * xr - ai * xi + sr_t[r:r + 1, :], ar * xi + ai * xr + si_t[r:r + 1, :]
        xpr[pl.ds(r0, SUBLANES), :] = pr_t
        xpi[pl.ds(r0, SUBLANES), :] = pi_t
        return xr, xi

    zero = jnp.zeros((1, LANES), F32)
    lax.fori_loop(0, nc // SUBLANES, body, (zero, zero))
    ystage[jj] = (jnp.dot(xpr[...].astype(BF16), fr_ref[0], preferred_element_type=F32)
                  + jnp.dot(xpi[...].astype(BF16), fi_ref[0], preferred_element_type=F32)
                  + jnp.dot(u, m_ref[0], preferred_element_type=F32))

    @pl.when(jj == S5_SLAB - 1)
    def _():
        for t in range(S5_CHUNK):
            o_ref[pl.ds(t, nc, stride=S5_CHUNK), :] = jnp.concatenate(
                [ystage[q, :, S5_PB * t:S5_PB * (t + 1)] for q in range(S5_SLAB)], axis=1)


def _s5(z2, lam_re, lam_im, log_dt, b_re, b_im, c_re, c_im):
    t = z2.shape[0]
    nc = t // S5_CHUNK
    m, e_pair, fr_pair, fi_pair, al = _s5_tables(lam_re, lam_im, log_dt, b_re, b_im, c_re, c_im)
    pair = lambda a, b: pl.BlockSpec((1, a, b), lambda k, jj: (S5_SLAB * k + jj, 0, 0))
    return pl.pallas_call(
        functools.partial(_s5_kernel, nc=nc),
        grid=(WIDTH // LANES, S5_SLAB),
        in_specs=[pl.BlockSpec((t, LANES), lambda k, jj: (0, Z2_US * WIDTH // LANES + k), pipeline_mode=pl.Buffered(1)),
                  pair(S5_ROWS, S5_ROWS), pair(S5_ROWS, 2 * LANES), pair(LANES, S5_ROWS), pair(LANES, S5_ROWS),
                  pair(SUBLANES, LANES)],
        out_specs=pl.BlockSpec((t, LANES), lambda k, jj: (0, k), pipeline_mode=pl.Buffered(1)),
        out_shape=jax.ShapeDtypeStruct((t, WIDTH), F32),
        scratch_shapes=[pltpu.VMEM((S5_SLAB, nc, S5_ROWS), BF16), pltpu.VMEM((S5_SLAB, nc, S5_ROWS), F32),
                        pltpu.VMEM((nc, 2 * LANES), F32), pltpu.VMEM((nc, LANES), F32), pltpu.VMEM((nc, LANES), F32)],
        compiler_params=_cparams(("arbitrary", "arbitrary")),
        name="s5_scan",
    )(z2, m, e_pair, fr_pair, fi_pair, al)


def _layer_norm(x, g, b):
    mu = jnp.mean(x, axis=-1, keepdims=True)
    xc = x - mu
    var = jnp.mean(xc * xc, axis=-1, keepdims=True)
    return xc * lax.rsqrt(var + LN_EPS) * g + b


def _merge_kernel(x_ref, g0_ref, g1_ref, g2_ref, ym_ref, ys_ref, us_ref, yh_ref, wm_ref, ws_ref, wh_ref, wglu_ref,
                  d_ref, wo_ref, lg_ref, lb_ref, rw_ref, rb_ref, o_ref):
    ys = _gelu_tanh(ys_ref[...] + d_ref[...] * us_ref[...].astype(F32))
    ys = ys * _sigmoid(_dot(ys, wglu_ref[...]))
    merged = (_sigmoid(g0_ref[...].astype(F32)) * _dot(ym_ref[...], wm_ref[...])
              + _sigmoid(g1_ref[...].astype(F32)) * _dot(ys, ws_ref[...])
              + _sigmoid(g2_ref[...].astype(F32)) * _dot(yh_ref[...], wh_ref[...]))
    x1 = _layer_norm(DN_ALPHA * x_ref[...] + _dot(merged, wo_ref[...]), lg_ref[...], lb_ref[...])
    o_ref[:, 0:D_MODEL] = x1

    xh = x1.astype(BF16)
    xl = (x1 - xh.astype(F32)).astype(BF16)
    logits = (jnp.dot(xh, rw_ref[0], preferred_element_type=F32) + jnp.dot(xl, rw_ref[0], preferred_element_type=F32)
              + jnp.dot(xh, rw_ref[1], preferred_element_type=F32))
    s0 = _sigmoid(logits)
    sb0 = s0 + rb_ref[...]
    shift = lambda a, j: a if j == 0 else pltpu.roll(a, LANES - N_GROUPS * j, 1)
    s = [shift(s0, j) for j in range(EXPERTS_PER_GROUP)]
    sb = [shift(sb0, j) for j in range(EXPERTS_PER_GROUP)]
    hi1, lo1 = jnp.maximum(sb[0], sb[1]), jnp.minimum(sb[0], sb[1])
    hi2, lo2 = jnp.maximum(sb[2], sb[3]), jnp.minimum(sb[2], sb[3])
    top2 = jnp.maximum(hi1, hi2) + jnp.maximum(jnp.minimum(hi1, hi2), jnp.maximum(lo1, lo2))
    lane = lax.broadcasted_iota(jnp.int32, top2.shape, 1)
    top2 = jnp.where(lane < N_GROUPS, top2, -jnp.inf)
    gmax = jnp.max(top2, axis=1, keepdims=True)
    g_idx = jnp.min(jnp.where(top2 == gmax, lane, LANES), axis=1, keepdims=True)
    sel = lane == g_idx
    v = [jnp.sum(jnp.where(sel, sb[j], 0.0), axis=1, keepdims=True) for j in range(EXPERTS_PER_GROUP)]
    sv = [jnp.sum(jnp.where(sel, s[j], 0.0), axis=1, keepdims=True) for j in range(EXPERTS_PER_GROUP)]

    def first_max(vals):
        m = jnp.maximum(jnp.maximum(vals[0], vals[1]), jnp.maximum(vals[2], vals[3]))
        return jnp.where(vals[0] == m, 0, jnp.where(vals[1] == m, 1, jnp.where(vals[2] == m, 2, 3)))

    e1 = first_max(v)
    e2 = first_max([jnp.where(e1 == j, -jnp.inf, v[j]) for j in range(EXPERTS_PER_GROUP)])
    s1 = sum(jnp.where(e1 == j, sv[j], 0.0) for j in range(EXPERTS_PER_GROUP))
    s2 = sum(jnp.where(e2 == j, sv[j], 0.0) for j in range(EXPERTS_PER_GROUP))
    tot = s1 + s2
    meta = jnp.where(lane == 0, g_idx.astype(F32), 0.0)
    for j in range(EXPERTS_PER_GROUP):
        cw = jnp.where(e1 == j, s1 / tot, 0.0) + jnp.where(e2 == j, s2 / tot, 0.0)
        meta = jnp.where(lane == 1 + j, cw, meta)
    o_ref[:, D_MODEL:D_MODEL + META] = meta


def _merge(x, z1, z2, ym, ys, yh, wm, ws, wh, wglu, d, wo, lg, lb, router_w, router_bias, tm):
    t = x.shape[0]
    rw = router_w.astype(F32).reshape(D_MODEL, N_GROUPS, EXPERTS_PER_GROUP).transpose(0, 2, 1)
    rw = jnp.pad(rw.reshape(D_MODEL, N_EXPERTS), ((0, 0), (0, LANES - N_EXPERTS)))
    rw_hi = rw.astype(BF16)
    rw = jnp.stack([rw_hi, (rw - rw_hi.astype(F32)).astype(BF16)])
    rb = jnp.pad(router_bias.astype(F32).reshape(N_GROUPS, EXPERTS_PER_GROUP).T.reshape(1, N_EXPERTS),
                 ((0, 0), (0, LANES - N_EXPERTS)))
    full = lambda shape: pl.BlockSpec(shape, lambda i: (0,) * len(shape))
    row = lambda w: pl.BlockSpec((tm, w), lambda i: (i, 0))
    return pl.pallas_call(
        _merge_kernel,
        grid=(t // tm,),
        in_specs=[row(D_MODEL),
                  pl.BlockSpec((tm, D_MODEL), lambda i: (i, 0)),
                  pl.BlockSpec((tm, D_MODEL), lambda i: (i, 1)),
                  pl.BlockSpec((tm, D_MODEL), lambda i: (i, 2)),
                  row(WIDTH), row(WIDTH), pl.BlockSpec((tm, WIDTH), lambda i: (i, Z2_US)), row(WIDTH),
                  full((WIDTH, D_MODEL)), full((WIDTH, D_MODEL)), full((WIDTH, D_MODEL)), full((WIDTH, WIDTH)),
                  full((1, WIDTH)), full((D_MODEL, D_MODEL)), full((1, D_MODEL)), full((1, D_MODEL)),
                  full((2, D_MODEL, LANES)), full((1, LANES))],
        out_specs=pl.BlockSpec((tm, D_MODEL + META), lambda i: (i, 0)),
        out_shape=jax.ShapeDtypeStruct((t, D_MODEL + META), F32),
        compiler_params=_cparams(("parallel",)),
        name="merge_router",
    )(x, z1, z1, z1, ym, ys, z2, yh, wm.astype(BF16), ws.astype(BF16), wh.astype(BF16), wglu.astype(BF16),
      d.reshape(1, WIDTH).astype(F32), wo.astype(BF16), lg.reshape(1, D_MODEL).astype(F32),
      lb.reshape(1, D_MODEL).astype(F32), rw, rb)


def _start_row_gather(idx_ref, base, src_hbm, dst, sem, rows):
    for r in range(rows):
        pltpu.make_async_copy(src_hbm.at[pl.ds(idx_ref[base + r], 1)], dst.at[pl.ds(r, 1)], sem).start(priority=r % 2)


def _wait_row_gather(src_hbm, dst, sem, rows):
    pltpu.make_async_copy(src_hbm.at[pl.ds(0, rows)], dst, sem).wait()


def _moe_kernel(tg_ref, src_ref, x_hbm, wg_ref, wu_ref, wd_ref, o_ref, buf, sem, wg_s, wu_s, wd_s, *, tm, ntiles):
    i = pl.program_id(0)
    slot = i % 2

    @pl.when(i == 0)
    def _():
        _start_row_gather(src_ref, 0, x_hbm, buf.at[0], sem.at[0], tm)

    @pl.when(i + 1 < ntiles)
    def _():
        _start_row_gather(src_ref, (i + 1) * tm, x_hbm, buf.at[1 - slot], sem.at[1 - slot], tm)

    @pl.when(jnp.logical_or(i == 0, tg_ref[i] != tg_ref[jnp.maximum(i - 1, 0)]))
    def _():
        for e in range(EXPERTS_PER_GROUP):
            wg_s[e] = wg_ref[0, e].astype(BF16)
            wu_s[e] = wu_ref[0, e].astype(BF16)
            wd_s[e] = wd_ref[0, e].astype(BF16)

    _wait_row_gather(x_hbm, buf.at[slot], sem.at[slot], tm)
    xt = buf[slot]
    xb = xt[:, 0:D_MODEL].astype(BF16)
    meta = xt[:, D_MODEL:D_MODEL + META]
    y = jnp.zeros((tm, D_MODEL), F32)
    for e in range(EXPERTS_PER_GROUP):
        hg = jnp.dot(xb, wg_s[e], preferred_element_type=F32)
        hu = jnp.dot(xb, wu_s[e], preferred_element_type=F32)
        hh = _silu(hg) * hu * _lane_col(meta, 1 + e)
        y = y + jnp.dot(hh.astype(BF16), wd_s[e], preferred_element_type=F32)
    o_ref[...] = y


def _ln2_kernel(dst_ref, x_ref, y_hbm, lg_ref, lb_ref, o_ref, buf, sem, *, tm, ntiles):
    i = pl.program_id(0)
    slot = i % 2

    @pl.when(i == 0)
    def _():
        _start_row_gather(dst_ref, 0, y_hbm, buf.at[0], sem.at[0], tm)

    @pl.when(i + 1 < ntiles)
    def _():
        _start_row_gather(dst_ref, (i + 1) * tm, y_hbm, buf.at[1 - slot], sem.at[1 - slot], tm)

    _wait_row_gather(y_hbm, buf.at[slot], sem.at[slot], tm)
    o_ref[...] = _layer_norm(DN_ALPHA * x_ref[...] + buf[slot], lg_ref[...], lb_ref[...])


def _moe_ln(x1e, wg, wu, wd, layer, lg, lb, tm):
    t = x1e.shape[0]
    ntiles = t // tm + N_GROUPS
    p = ntiles * tm
    key = x1e[:, D_MODEL].astype(jnp.int32)
    onehot = (key[:, None] == jnp.arange(N_GROUPS)[None, :]).astype(jnp.int32)
    csum = jnp.cumsum(onehot, axis=0)
    counts = csum[-1]
    rank = jnp.sum(onehot * csum, axis=1) - 1
    pcount = ((counts + tm - 1) // tm) * tm
    pend = jnp.cumsum(pcount)
    dest = (pend - pcount)[key] + rank
    row_src = jnp.zeros((p,), jnp.int32).at[dest].set(jnp.arange(t, dtype=jnp.int32))
    tile_start = jnp.arange(ntiles, dtype=jnp.int32) * tm
    tile_group = jnp.minimum(jnp.sum((tile_start[:, None] >= pend[None, :]).astype(jnp.int32), axis=1),
                             N_GROUPS - 1)
    wspec = lambda a, b: pl.BlockSpec((1, EXPERTS_PER_GROUP, a, b), lambda i, tg, src: (layer, tg[i], 0, 0))
    wscr = lambda a, b: pltpu.VMEM((EXPERTS_PER_GROUP, a, b), BF16)
    y_sorted = pl.pallas_call(
        functools.partial(_moe_kernel, tm=tm, ntiles=ntiles),
        grid_spec=pltpu.PrefetchScalarGridSpec(
            num_scalar_prefetch=2,
            grid=(ntiles,),
            in_specs=[pl.BlockSpec(memory_space=pl.ANY), wspec(D_MODEL, D_EXPERT), wspec(D_MODEL, D_EXPERT),
                      wspec(D_EXPERT, D_MODEL)],
            out_specs=pl.BlockSpec((tm, D_MODEL), lambda i, tg, src: (i, 0)),
            scratch_shapes=[pltpu.VMEM((2, tm, D_MODEL + META), F32), pltpu.SemaphoreType.DMA((2,)),
                            wscr(D_MODEL, D_EXPERT), wscr(D_MODEL, D_EXPERT), wscr(D_EXPERT, D_MODEL)]),
        out_shape=jax.ShapeDtypeStruct((p, D_MODEL), F32),
        compiler_params=_cparams(("arbitrary",)),
        name="moe_experts",
    )(tile_group, row_src, x1e, wg, wu, wd)
    nt2 = t // tm
    return pl.pallas_call(
        functools.partial(_ln2_kernel, tm=tm, ntiles=nt2),
        grid_spec=pltpu.PrefetchScalarGridSpec(
            num_scalar_prefetch=1,
            grid=(nt2,),
            in_specs=[pl.BlockSpec((tm, D_MODEL), lambda i, d: (i, 0)), pl.BlockSpec(memory_space=pl.ANY),
                      pl.BlockSpec((1, D_MODEL), lambda i, d: (0, 0)), pl.BlockSpec((1, D_MODEL), lambda i, d: (0, 0))],
            out_specs=pl.BlockSpec((tm, D_MODEL), lambda i, d: (i, 0)),
            scratch_shapes=[pltpu.VMEM((2, tm, D_MODEL), F32), pltpu.SemaphoreType.DMA((2,))]),
        out_shape=jax.ShapeDtypeStruct((t, D_MODEL), F32),
        compiler_params=_cparams(("arbitrary",)),
        name="moe_combine_ln",
    )(dest.astype(jnp.int32), x1e, y_sorted, lg.reshape(1, D_MODEL).astype(F32), lb.reshape(1, D_MODEL).astype(F32))


def _layer(x, l, p, lb_l, cfg):
    t = x.shape[0]
    w1, w2, wt = _split_w_in(p['w_in'][l])
    xb = x.astype(BF16)
    z1 = _matmul(xb, w1, Z1_DTYPE, cfg['tm_in'], cfg['tn_in'])
    z2, gt = _inproj_gate(xb, w2, wt, cfg['tm_in'])
    gt3 = gt.reshape(SUBLANES, t // CHUNK, CHUNK).transpose(1, 0, 2)
    ym = _mlstm(z1, z2, gt3, p['mlstm_conv_w'][l], p['mlstm_conv_b'][l], p['mlstm_wq'][l], p['mlstm_wk'][l],
                p['mlstm_wv'][l], p['mlstm_b_i'][l], p['mlstm_b_f'][l], p['mlstm_skip'][l], cfg['tb'])
    ys = _s5(z2, p['s5_lambda_re'][l], p['s5_lambda_im'][l], p['s5_log_dt'][l], p['s5_b_re'][l], p['s5_b_im'][l],
             p['s5_c_re'][l], p['s5_c_im'][l])
    yh = _hgrn(z1, z2, lb_l, p['hgrn_norm_w'][l], cfg['tb'])
    x1e = _merge(x, z1, z2, ym, ys, yh, p['w_branch_mlstm'][l], p['w_branch_s5'][l], p['w_branch_hgrn'][l],
                 p['s5_w_glu'][l], p['s5_d'][l], p['w_out'][l], p['ln1_g'][l], p['ln1_b'][l], p['router_w'],
                 p['router_bias'], cfg['tm_merge'])
    return _moe_ln(x1e, p['exp_w_gate'], p['exp_w_up'], p['exp_w_down'], l, p['ln2_g'][l], p['ln2_b'][l],
                   cfg['tm_moe'])


_CFG = dict(tm_in=1024, tn_in=Z1_COLS // 4, tb=512, tm_merge=256, tm_moe=256)


def kernel(x, w_in, mlstm_conv_w, mlstm_conv_b, mlstm_wq, mlstm_wk, mlstm_wv, mlstm_b_i, mlstm_b_f, mlstm_skip, s5_lambda_re, s5_lambda_im, s5_log_dt, s5_b_re, s5_b_im, s5_c_re, s5_c_im, s5_d, s5_w_glu, hgrn_lower_bounds, hgrn_norm_w, w_branch_mlstm, w_branch_s5, w_branch_hgrn, w_out, ln1_g, ln1_b, ln2_g, ln2_b, router_w, router_bias, exp_w_gate, exp_w_up, exp_w_down):
    p = dict(w_in=w_in, mlstm_conv_w=mlstm_conv_w, mlstm_conv_b=mlstm_conv_b, mlstm_wq=mlstm_wq, mlstm_wk=mlstm_wk,
             mlstm_wv=mlstm_wv, mlstm_b_i=mlstm_b_i, mlstm_b_f=mlstm_b_f, mlstm_skip=mlstm_skip,
             s5_lambda_re=s5_lambda_re, s5_lambda_im=s5_lambda_im, s5_log_dt=s5_log_dt, s5_b_re=s5_b_re,
             s5_b_im=s5_b_im, s5_c_re=s5_c_re, s5_c_im=s5_c_im, s5_d=s5_d, s5_w_glu=s5_w_glu,
             hgrn_norm_w=hgrn_norm_w, w_branch_mlstm=w_branch_mlstm, w_branch_s5=w_branch_s5,
             w_branch_hgrn=w_branch_hgrn, w_out=w_out, ln1_g=ln1_g, ln1_b=ln1_b, ln2_g=ln2_g, ln2_b=ln2_b,
             router_w=router_w, router_bias=router_bias, exp_w_gate=exp_w_gate, exp_w_up=exp_w_up,
             exp_w_down=exp_w_down)
    lb_cum = jnp.cumsum(jax.nn.softmax(hgrn_lower_bounds.astype(F32), axis=0), axis=0)
    lb_layers = lb_cum - lb_cum[0]
    bsz, seq, d = x.shape
    h = x.reshape(bsz * seq, d)
    for l in range(DEPTH):
        h = _layer(h, l, p, lb_layers[l], _CFG)
    return h.reshape(bsz, seq, d)
```

```python
import functools
import math

import jax
import jax.numpy as jnp
from jax import lax
from jax.experimental import pallas as pl
from jax.experimental.pallas import tpu as pltpu

F32 = jnp.float32
BF16 = jnp.bfloat16
HIGHEST = lax.Precision.HIGHEST

D_MODEL = 1024
DEPTH = 2
HEADS = 4
HEAD_DIM = 128
WIDTH = HEADS * HEAD_DIM
MLSTM_CONV = 4
CHUNK = 64
S5_GROUP = 16
S5_GROUPS = 32
S5_STATE = 64
S5_PAIRS = S5_GROUPS // 2
S5_CHUNK = 32
S5_MAX_REAL = -1e-4
N_EXPERTS = 32
N_GROUPS = 8
EXPERTS_PER_GROUP = 4
D_EXPERT = 256
DN_ALPHA = (2 * DEPTH) ** 0.25
LN_EPS = 1e-5
NORM_EPS = 1e-6

LANES = 128
SUBLANES = 8
META = LANES
VMEM_LIMIT = 56 * 1024 * 1024

Z1_DTYPE = BF16
Z1_GATE, Z1_XM, Z1_OM, Z1_QH, Z1_IH, Z1_GH = 0, 6, 7, 8, 9, 10
Z1_COLS = 11 * WIDTH
Z2_FH, Z2_US = 0, 1
Z2_COLS = 2 * WIDTH + LANES


def _cparams(sem):
    return pltpu.CompilerParams(dimension_semantics=sem, vmem_limit_bytes=VMEM_LIMIT)


def _sigmoid(x):
    return 0.5 * (1.0 + jnp.tanh(0.5 * x))


def _silu(x):
    return x * _sigmoid(x)


def _log_sigmoid(x):
    return jnp.minimum(x, 0.0) - jnp.log(1.0 + jnp.exp(-jnp.abs(x)))


def _gelu_tanh(x):
    return 0.5 * x * (1.0 + jnp.tanh(math.sqrt(2.0 / math.pi) * (x + 0.044715 * (x * x * x))))


def _cumsum_rows(x):
    n = x.shape[0]
    row = lax.broadcasted_iota(jnp.int32, x.shape, 0)
    s = 1
    while s < n:
        x = x + jnp.where(row >= s, pltpu.roll(x, s, 0), 0.0)
        s *= 2
    return x


def _lane_col(x, idx):
    lane = lax.broadcasted_iota(jnp.int32, x.shape, 1)
    return jnp.sum(jnp.where(lane == idx, x, 0.0), axis=1, keepdims=True)


def _dot(a, b):
    return jnp.dot(a.astype(BF16), b.astype(BF16), preferred_element_type=F32)


def _dot_nt(a, b):
    return lax.dot_general(a.astype(BF16), b.astype(BF16), (((1,), (1,)), ((), ())), preferred_element_type=F32)


def _dot_tn(a, b):
    return lax.dot_general(a.astype(BF16), b.astype(BF16), (((0,), (0,)), ((), ())), preferred_element_type=F32)


def _mm_kernel(x_ref, w_ref, o_ref):
    o_ref[...] = jnp.dot(x_ref[...], w_ref[...], preferred_element_type=F32).astype(o_ref.dtype)


def _matmul(x, w, out_dtype, tm, tn):
    m, k = x.shape
    n = w.shape[1]
    return pl.pallas_call(
        _mm_kernel,
        grid=(m // tm, n // tn),
        in_specs=[pl.BlockSpec((tm, k), lambda i, j: (i, 0)), pl.BlockSpec((k, tn), lambda i, j: (0, j))],
        out_specs=pl.BlockSpec((tm, tn), lambda i, j: (i, j)),
        out_shape=jax.ShapeDtypeStruct((m, n), out_dtype),
        compiler_params=_cparams(("parallel", "parallel")),
        name="in_proj_wide",
    )(x, w)


def _inproj_gate_kernel(x_ref, w_ref, wt_ref, o_ref, gt_ref):
    x = x_ref[...]
    o_ref[...] = jnp.dot(x, w_ref[...], preferred_element_type=F32)
    gt_ref[...] = lax.dot_general(wt_ref[...], x, (((1,), (1,)), ((), ())), preferred_element_type=F32)


def _inproj_gate(x, w2, wt, tm):
    m, k = x.shape
    return pl.pallas_call(
        _inproj_gate_kernel,
        grid=(m // tm,),
        in_specs=[pl.BlockSpec((tm, k), lambda i: (i, 0)),
                  pl.BlockSpec((k, Z2_COLS), lambda i: (0, 0)),
                  pl.BlockSpec((SUBLANES, k), lambda i: (0, 0))],
        out_specs=[pl.BlockSpec((tm, Z2_COLS), lambda i: (i, 0)), pl.BlockSpec((SUBLANES, tm), lambda i: (0, i))],
        out_shape=[jax.ShapeDtypeStruct((m, Z2_COLS), F32), jax.ShapeDtypeStruct((SUBLANES, m), F32)],
        compiler_params=_cparams(("parallel",)),
        name="in_proj_gates",
    )(x, w2, wt)


def _split_w_in(w):
    offs, o = [], 0
    for s in (WIDTH, WIDTH, HEADS, HEADS, WIDTH, WIDTH, WIDTH, WIDTH, WIDTH, 3 * D_MODEL):
        offs.append((o, o + s))
        o += s
    seg = [w[:, a:b] for a, b in offs]
    xm, om, im, fm, us, qh, fh, ih, gh, gate = seg
    w1 = jnp.concatenate([gate, xm, om, qh, ih, gh], axis=1).astype(BF16)
    pad = jnp.zeros((w.shape[0], LANES - 2 * HEADS), w.dtype)
    w2 = jnp.concatenate([fh, us, im, fm, pad], axis=1).astype(BF16)
    wt = jnp.concatenate([im, fm], axis=1).T.astype(BF16)
    return w1, w2, wt


def _mlstm_kernel(xm_ref, om_ref, gc_ref, gr_ref, cw_ref, cb_ref, wq_ref, wk_ref, wv_ref, bcol_ref, brow_ref,
                  skip_ref, triu_ref, o_ref, xpad, q_s, k_s, v_s, xc_s, h_s, gcol_s, grow_s, brow_s, ct_s, n_s, m_s,
                  *, tb):
    ncb = tb // CHUNK

    @pl.when(pl.program_id(0) == 0)
    def _():
        xpad[0:SUBLANES, :] = jnp.zeros((SUBLANES, WIDTH), F32)
        ct_s[...] = jnp.zeros_like(ct_s)
        n_s[...] = jnp.zeros_like(n_s)
        m_s[...] = jnp.zeros_like(m_s)

    xm = xm_ref[...].astype(F32)
    xpad[SUBLANES:SUBLANES + tb, :] = xm
    cw = cw_ref[...]
    conv = cb_ref[...] + cw[3:4, :] * xm
    for d in range(1, MLSTM_CONV):
        conv = conv + cw[3 - d:4 - d, :] * xpad[SUBLANES - d:SUBLANES - d + tb, :]
    xpad[0:SUBLANES, :] = xpad[tb:tb + SUBLANES, :]
    xc = _silu(conv)
    xc_s[...] = xc
    for h in range(HEADS):
        sl = slice(h * HEAD_DIM, (h + 1) * HEAD_DIM)
        xch = xc[:, sl].astype(BF16)
        q_s[:, sl] = jnp.dot(xch, wq_ref[h], preferred_element_type=F32) * (HEAD_DIM ** -0.5)
        k_s[:, sl] = jnp.dot(xch, wk_ref[h], preferred_element_type=F32)
        v_s[:, sl] = jnp.dot(xm[:, sl].astype(BF16), wv_ref[h], preferred_element_type=F32)

    gc = gc_ref[...] + bcol_ref[...]
    lane = lax.broadcasted_iota(jnp.int32, gc.shape, 1)
    gcol_s[...] = jnp.where(lane < HEADS, gc, _log_sigmoid(gc))
    gr = gr_ref[...] + brow_ref[...]
    sub = lax.broadcasted_iota(jnp.int32, gr.shape, 1)
    gr = jnp.where(sub < HEADS, gr, _log_sigmoid(gr)).reshape(ncb * SUBLANES, CHUNK)
    grow_s[...] = gr
    brow_s[...] = jnp.dot(gr, triu_ref[...], precision=HIGHEST, preferred_element_type=F32)

    rowi = lax.broadcasted_iota(jnp.int32, (CHUNK, CHUNK), 0)
    coli = lax.broadcasted_iota(jnp.int32, (CHUNK, CHUNK), 1)
    causal = rowi >= coli

    def chunk_body(c, carry):
        r0 = pl.multiple_of(c * CHUNK, CHUNK)
        g0 = pl.multiple_of(c * SUBLANES, SUBLANES)
        gcol = gcol_s[pl.ds(r0, CHUNK), :]
        bcol = _cumsum_rows(gcol)
        grow = grow_s[pl.ds(g0, SUBLANES), :]
        brow = brow_s[pl.ds(g0, SUBLANES), :]
        for h in range(HEADS):
            sl = slice(h * HEAD_DIM, (h + 1) * HEAD_DIM)
            li_c = _lane_col(gcol, h)
            b_c = _lane_col(bcol, HEADS + h)
            li_r = grow[h:h + 1, :]
            b_r = brow[HEADS + h:HEADS + h + 1, :]
            b_tot = b_c[CHUNK - 1:CHUNK, :]
            a_c = b_tot - b_c + li_c
            a_max = jnp.max(a_c, axis=0, keepdims=True)
            q = q_s[pl.ds(r0, CHUNK), sl]
            k = k_s[pl.ds(r0, CHUNK), sl]
            v = v_s[pl.ds(r0, CHUNK), sl]
            wk = jnp.exp(a_c - a_max) * k
            c_loc_t = _dot_tn(v, wk)
            n_loc = jnp.sum(wk, axis=0, keepdims=True)
            ct_prev = ct_s[h]
            n_prev = n_s[h:h + 1, :]
            m_prev = m_s[h:h + 1, 0:1]
            d_mat = jnp.where(causal, b_c - b_r + li_r, -jnp.inf)
            m_inter = b_c + m_prev
            m_j = jnp.maximum(m_inter, jnp.max(d_mat, axis=1, keepdims=True))
            sc = _dot_nt(q, k) * jnp.exp(d_mat - m_j)
            g_inter = jnp.exp(m_inter - m_j)
            num = g_inter * _dot_nt(q, ct_prev) + _dot(sc, v)
            den = g_inter * jnp.sum(q * n_prev, axis=1, keepdims=True) + jnp.sum(sc, axis=1, keepdims=True)
            h_s[pl.ds(r0, CHUNK), sl] = num / jnp.maximum(jnp.abs(den), jnp.exp(-m_j))
            m_new = jnp.maximum(b_tot + m_prev, a_max)
            g_old = jnp.exp(b_tot + m_prev - m_new)
            g_loc = jnp.exp(a_max - m_new)
            ct_s[h] = g_old * ct_prev + g_loc * c_loc_t
            n_s[h:h + 1, :] = g_old * n_prev + g_loc * n_loc
            m_s[h:h + 1, :] = jnp.broadcast_to(m_new, (1, LANES))
        return carry

    lax.fori_loop(0, ncb, chunk_body, 0, unroll=2)
    o_ref[...] = (_sigmoid(om_ref[...].astype(F32)) * h_s[...] + skip_ref[...] * xc_s[...]).astype(o_ref.dtype)


def _mlstm(z1, z2, gt3, conv_w, conv_b, wq, wk, wv, b_i, b_f, skip, tb):
    t = z1.shape[0]
    ncb = tb // CHUNK
    bias = jnp.concatenate([b_i, b_f]).astype(F32)
    bcol = jnp.zeros((1, LANES), F32).at[0, :2 * HEADS].set(bias)
    brow = bias.reshape(2 * HEADS, 1)
    triu = jnp.triu(jnp.ones((CHUNK, CHUNK), F32))
    full = lambda shape: pl.BlockSpec(shape, lambda i: (0,) * len(shape))
    return pl.pallas_call(
        functools.partial(_mlstm_kernel, tb=tb),
        grid=(t // tb,),
        in_specs=[pl.BlockSpec((tb, WIDTH), lambda i: (i, Z1_XM)),
                  pl.BlockSpec((tb, WIDTH), lambda i: (i, Z1_OM)),
                  pl.BlockSpec((tb, LANES), lambda i: (i, 2 * WIDTH // LANES)),
                  pl.BlockSpec((ncb, SUBLANES, CHUNK), lambda i: (i, 0, 0)),
                  full((MLSTM_CONV, WIDTH)), full((1, WIDTH)),
                  full((HEADS, HEAD_DIM, HEAD_DIM)), full((HEADS, HEAD_DIM, HEAD_DIM)),
                  full((HEADS, HEAD_DIM, HEAD_DIM)),
                  full((1, LANES)), full((2 * HEADS, 1)), full((1, WIDTH)), full((CHUNK, CHUNK))],
        out_specs=pl.BlockSpec((tb, WIDTH), lambda i: (i, 0)),
        out_shape=jax.ShapeDtypeStruct((t, WIDTH), F32),
        scratch_shapes=[pltpu.VMEM((tb + SUBLANES, WIDTH), F32)]
        + [pltpu.VMEM((tb, WIDTH), F32) for _ in range(5)]
        + [pltpu.VMEM((tb, LANES), F32),
           pltpu.VMEM((ncb * SUBLANES, CHUNK), F32), pltpu.VMEM((ncb * SUBLANES, CHUNK), F32),
           pltpu.VMEM((HEADS, HEAD_DIM, HEAD_DIM), F32), pltpu.VMEM((SUBLANES, LANES), F32),
           pltpu.VMEM((SUBLANES, LANES), F32)],
        compiler_params=_cparams(("arbitrary",)),
        name="mlstm",
    )(z1, z1, z2, gt3, conv_w.astype(F32), conv_b.reshape(1, WIDTH).astype(F32), wq.astype(BF16), wk.astype(BF16),
      wv.astype(BF16), bcol, brow, skip.reshape(1, WIDTH).astype(F32), triu)


def _hgrn_kernel(q_ref, f_ref, i_ref, g_ref, lb_ref, nw_ref, tri_ref, o_ref, st_s, *, tb):
    ncb = tb // CHUNK

    @pl.when(pl.program_id(0) == 0)
    def _():
        st_s[...] = jnp.zeros_like(st_s)

    lb = lb_ref[...]
    nw = nw_ref[...]
    tri = tri_ref[...]
    rowi = lax.broadcasted_iota(jnp.int32, (CHUNK, WIDTH), 0)
    sr = lax.broadcasted_iota(jnp.int32, (CHUNK, CHUNK), 0)
    sc = lax.broadcasted_iota(jnp.int32, (CHUNK, CHUNK), 1)
    halves = [1 << p for p in range(CHUNK.bit_length() - 1)]
    upper = {m: (rowi & m) != 0 for m in halves}
    same_blk = {m: (sr // (2 * m)) == (sc // (2 * m)) for m in halves}

    def chunk_body(c, carry):
        r0 = pl.multiple_of(c * CHUNK, CHUNK)
        f = lb + (1.0 - lb) * _sigmoid(f_ref[pl.ds(r0, CHUNK), :].astype(F32))
        k = 1.0 - f
        q = _silu(q_ref[pl.ds(r0, CHUNK), :].astype(F32))
        v = i_ref[pl.ds(r0, CHUNK), :].astype(F32)
        lf = jnp.log(f)
        hi = lf.astype(BF16)
        r1 = lf - hi.astype(F32)
        mid = r1.astype(BF16)
        lo = (r1 - mid.astype(F32)).astype(BF16)
        b = (jnp.dot(tri, hi, preferred_element_type=F32) + jnp.dot(tri, mid, preferred_element_type=F32)
             + jnp.dot(tri, lo, preferred_element_type=F32))
        qs, ks = {}, {}
        for m in halves:
            if m == 1:
                t = jnp.where(upper[m], q * f, k)
            else:
                nblk = CHUNK // (2 * m)
                r = jnp.concatenate([jnp.broadcast_to(b[2 * m * j + m - 1:2 * m * j + m, :], (2 * m, WIDTH))
                                     for j in range(nblk)], axis=0)
                t = jnp.where(upper[m], q, k) * jnp.exp(-jnp.abs(b - r))
            qs[m] = jnp.where(upper[m], t, 0.0).astype(BF16)
            ks[m] = jnp.where(upper[m], 0.0, t).astype(BF16)
        b_last = b[CHUNK - 1:CHUNK, :]
        qe = q * jnp.exp(b)
        kd = k * jnp.exp(b_last - b)
        e_last = jnp.exp(b_last)
        outs = []
        for h in range(HEADS):
            sl = slice(h * HEAD_DIM, (h + 1) * HEAD_DIM)
            st = st_s[h]
            a = jnp.where(sr == sc, _dot_nt(q[:, sl], k[:, sl]), 0.0)
            for m in halves:
                a = a + jnp.where(same_blk[m], _dot_nt(qs[m][:, sl], ks[m][:, sl]), 0.0)
            oh = _dot_nt(qe[:, sl], st) + _dot(a, v[:, sl])
            st_s[h] = e_last[:, sl] * st + _dot_tn(v[:, sl], kd[:, sl])
            ms = jnp.sum(oh * oh, axis=1, keepdims=True) * (1.0 / HEAD_DIM)
            outs.append(oh * lax.rsqrt(ms + NORM_EPS))
        on = jnp.concatenate(outs, axis=1)
        g = g_ref[pl.ds(r0, CHUNK), :].astype(F32)
        o_ref[pl.ds(r0, CHUNK), :] = (on * nw * _silu(g)).astype(o_ref.dtype)
        return carry

    lax.fori_loop(0, ncb, chunk_body, 0, unroll=2)


def _hgrn(z1, z2, lb, norm_w, tb):
    t = z1.shape[0]
    tril = jnp.tril(jnp.ones((CHUNK, CHUNK), F32)).astype(BF16)
    full = lambda shape: pl.BlockSpec(shape, lambda i: (0,) * len(shape))
    return pl.pallas_call(
        functools.partial(_hgrn_kernel, tb=tb),
        grid=(t // tb,),
        in_specs=[pl.BlockSpec((tb, WIDTH), lambda i: (i, Z1_QH)),
                  pl.BlockSpec((tb, WIDTH), lambda i: (i, 0)),
                  pl.BlockSpec((tb, WIDTH), lambda i: (i, Z1_IH)),
                  pl.BlockSpec((tb, WIDTH), lambda i: (i, Z1_GH)),
                  full((1, WIDTH)), full((1, WIDTH)), full((CHUNK, CHUNK))],
        out_specs=pl.BlockSpec((tb, WIDTH), lambda i: (i, 0)),
        out_shape=jax.ShapeDtypeStruct((t, WIDTH), F32),
        scratch_shapes=[pltpu.VMEM((HEADS, HEAD_DIM, HEAD_DIM), F32)],
        compiler_params=_cparams(("arbitrary",)),
        name="hgrn2",
    )(z1, z2, z1, z1, lb.reshape(1, WIDTH).astype(F32), norm_w.reshape(1, WIDTH).astype(F32), tril)


S5_PB = 2 * S5_GROUP
S5_ROWS = S5_CHUNK * S5_PB
S5_SLAB = LANES // S5_PB


def _s5_toeplitz_kernel(k_ref, m_ref):
    krow = k_ref[0]
    lane = lax.broadcasted_iota(jnp.int32, krow.shape, 1)
    for s in range(S5_CHUNK):
        blk = krow if s == 0 else jnp.where(lane >= s * S5_PB, pltpu.roll(krow, s * S5_PB, 1), 0.0)
        m_ref[0, s * S5_PB:(s + 1) * S5_PB, :] = blk.astype(m_ref.dtype)


def _s5_toeplitz(krow):
    return pl.pallas_call(
        _s5_toeplitz_kernel,
        grid=(S5_PAIRS,),
        in_specs=[pl.BlockSpec((1, S5_PB, S5_ROWS), lambda j: (j, 0, 0))],
        out_specs=pl.BlockSpec((1, S5_ROWS, S5_ROWS), lambda j: (j, 0, 0)),
        out_shape=jax.ShapeDtypeStruct((S5_PAIRS, S5_ROWS, S5_ROWS), BF16),
        compiler_params=_cparams(("parallel",)),
        name="s5_toeplitz",
    )(krow)


def _s5_tables(lam_re, lam_im, log_dt, b_re, b_im, c_re, c_im):
    ln = S5_CHUNK
    lr = jnp.minimum(lam_re.astype(F32), S5_MAX_REAL)
    li = lam_im.astype(F32)
    dt = jnp.exp(log_dt.astype(F32))[:, None]
    mag = jnp.exp(lr * dt)
    ab_re = mag * jnp.cos(li * dt)
    ab_im = mag * jnp.sin(li * dt)
    nr = ab_re - 1.0
    den = lr * lr + li * li
    cr = (nr * lr + ab_im * li) / den
    ci = (ab_im * lr - nr * li) / den
    bb_re = cr[..., None] * b_re - ci[..., None] * b_im
    bb_im = cr[..., None] * b_im + ci[..., None] * b_re
    tau = jnp.arange(ln + 1, dtype=F32)[:, None, None]
    pm = jnp.exp(lr * dt * tau)
    pr = pm * jnp.cos(li * dt * tau)
    pi = pm * jnp.sin(li * dt * tau)
    def block_diag(a):
        a0, a1 = a[0::2], a[1::2]
        z = jnp.zeros_like(a0)
        return jnp.concatenate([jnp.concatenate([a0, z], axis=2), jnp.concatenate([z, a1], axis=2)], axis=1)

    def pair_pow(p):
        return p.reshape(p.shape[0], S5_PAIRS, 2 * S5_STATE).transpose(1, 0, 2)

    bbr = block_diag(bb_re.transpose(0, 2, 1))
    bbi = block_diag(bb_im.transpose(0, 2, 1))
    ccr = block_diag(c_re.transpose(0, 2, 1))
    cci = block_diag(c_im.transpose(0, 2, 1))
    col = jnp.arange(S5_ROWS)
    rep_t = (col[None, :] // S5_PB == jnp.arange(ln)[:, None]).astype(F32)
    rep_c = (col[None, :] % S5_PB == jnp.arange(S5_PB)[:, None]).astype(F32)
    lanes_c = lambda a: jnp.einsum('jrq,qc->jrc', a, rep_c, precision=HIGHEST)
    lanes_t = lambda p: jnp.einsum('jtr,tc->jrc', pair_pow(p), rep_t, precision=HIGHEST)
    ccr_l, cci_l = lanes_c(ccr), lanes_c(cci)

    def out_tables(p_r, p_i):
        pr_l, pi_l = lanes_t(p_r), lanes_t(p_i)
        return ccr_l * pr_l - cci_l * pi_l, -(ccr_l * pi_l + cci_l * pr_l)

    fr_pair, fi_pair = out_tables(pr[1:], pi[1:])
    fk_re, fk_im = out_tables(pr[:ln], pi[:ln])
    krow = jnp.einsum('jrk,jkc->jrc', jnp.concatenate([bbr, bbi], axis=2), jnp.concatenate([fk_re, fk_im], axis=1),
                      precision=HIGHEST)
    m = _s5_toeplitz(krow)
    rows_of = lambda a: jnp.broadcast_to(a[:, :, None, :], (S5_PAIRS, a.shape[1], S5_PB, a.shape[2])).reshape(
        S5_PAIRS, a.shape[1] * S5_PB, a.shape[2])
    pe_r = rows_of(pair_pow(pr[ln - 1 - jnp.arange(ln)]))
    pe_i = rows_of(pair_pow(pi[ln - 1 - jnp.arange(ln)]))
    tile_s = lambda a: jnp.broadcast_to(a[:, None], (S5_PAIRS, ln) + a.shape[1:]).reshape(S5_PAIRS, S5_ROWS, a.shape[2])
    bbr_s, bbi_s = tile_s(bbr), tile_s(bbi)
    e_pair = jnp.concatenate([pe_r * bbr_s - pe_i * bbi_s, pe_r * bbi_s + pe_i * bbr_s], axis=2)
    al = jnp.zeros((S5_PAIRS, SUBLANES, LANES), F32)
    al = al.at[:, 0, :].set(pr[ln].reshape(S5_PAIRS, 2 * S5_STATE))
    al = al.at[:, 1, :].set(pi[ln].reshape(S5_PAIRS, 2 * S5_STATE))
    return m.astype(BF16), e_pair.astype(BF16), fr_pair.astype(BF16), fi_pair.astype(BF16), al


def _s5_kernel(u_ref, m_ref, e_ref, fr_ref, fi_ref, al_ref, o_ref, ustage, ystage, sloc, xpr, xpi, *, nc):
    jj = pl.program_id(1)

    @pl.when(jj == 0)
    def _():
        for b in range(S5_ROWS // LANES):
            slabs = [u_ref[pl.ds(S5_SLAB * b + a, nc, stride=S5_CHUNK), :] for a in range(S5_SLAB)]
            for q in range(S5_SLAB):
                ustage[q, :, LANES * b:LANES * (b + 1)] = jnp.concatenate(
                    [sl[:, S5_PB * q:S5_PB * (q + 1)] for sl in slabs], axis=1).astype(BF16)

    u = ustage[jj]
    sloc[...] = jnp.dot(u, e_ref[0], preferred_element_type=F32)
    ar = al_ref[0, 0:1, :]
    ai = al_ref[0, 1:2, :]

    row = lax.broadcasted_iota(jnp.int32, (SUBLANES, LANES), 0)

    def body(ti, carry):
        xr, xi = carry
        r0 = pl.multiple_of(ti * SUBLANES, SUBLANES)
        sr_t = sloc[pl.ds(r0, SUBLANES), 0:LANES]
        si_t = sloc[pl.ds(r0, SUBLANES), LANES:2 * LANES]
        pr_t = jnp.zeros((SUBLANES, LANES), F32)
        pi_t = jnp.zeros((SUBLANES, LANES), F32)
        for r in range(SUBLANES):
            pr_t = jnp.where(row == r, xr, pr_t)
            pi_t = jnp.where(row == r, xi, pi_t)
            xr, xi = ar * xr - ai * xi + sr_t[r:r + 1, :], ar * xi + ai * xr + si_t[r:r + 1, :]
        xpr[pl.ds(r0, SUBLANES), :] = pr_t
        xpi[pl.ds(r0, SUBLANES), :] = pi_t
        return xr, xi

    zero = jnp.zeros((1, LANES), F32)
    lax.fori_loop(0, nc // SUBLANES, body, (zero, zero))
    ystage[jj] = (jnp.dot(xpr[...].astype(BF16), fr_ref[0], preferred_element_type=F32)
                  + jnp.dot(xpi[...].astype(BF16), fi_ref[0], preferred_element_type=F32)
                  + jnp.dot(u, m_ref[0], preferred_element_type=F32))

    @pl.when(jj == S5_SLAB - 1)
    def _():
        for t in range(S5_CHUNK):
            o_ref[pl.ds(t, nc, stride=S5_CHUNK), :] = jnp.concatenate(
                [ystage[q, :, S5_PB * t:S5_PB * (t + 1)] for q in range(S5_SLAB)], axis=1)


def _s5(z2, lam_re, lam_im, log_dt, b_re, b_im, c_re, c_im):
    t = z2.shape[0]
    nc = t // S5_CHUNK
    m, e_pair, fr_pair, fi_pair, al = _s5_tables(lam_re, lam_im, log_dt, b_re, b_im, c_re, c_im)
    pair = lambda a, b: pl.BlockSpec((1, a, b), lambda k, jj: (S5_SLAB * k + jj, 0, 0))
    return pl.pallas_call(
        functools.partial(_s5_kernel, nc=nc),
        grid=(WIDTH // LANES, S5_SLAB),
        in_specs=[pl.BlockSpec((t, LANES), lambda k, jj: (0, Z2_US * WIDTH // LANES + k), pipeline_mode=pl.Buffered(1)),
                  pair(S5_ROWS, S5_ROWS), pair(S5_ROWS, 2 * LANES), pair(LANES, S5_ROWS), pair(LANES, S5_ROWS),
                  pair(SUBLANES, LANES)],
        out_specs=pl.BlockSpec((t, LANES), lambda k, jj: (0, k), pipeline_mode=pl.Buffered(1)),
        out_shape=jax.ShapeDtypeStruct((t, WIDTH), F32),
        scratch_shapes=[pltpu.VMEM((S5_SLAB, nc, S5_ROWS), BF16), pltpu.VMEM((S5_SLAB, nc, S5_ROWS), F32),
                        pltpu.VMEM((nc, 2 * LANES), F32), pltpu.VMEM((nc, LANES), F32), pltpu.VMEM((nc, LANES), F32)],
        compiler_params=_cparams(("arbitrary", "arbitrary")),
        name="s5_scan",
    )(z2, m, e_pair, fr_pair, fi_pair, al)


def _layer_norm(x, g, b):
    mu = jnp.mean(x, axis=-1, keepdims=True)
    xc = x - mu
    var = jnp.mean(xc * xc, axis=-1, keepdims=True)
    return xc * lax.rsqrt(var + LN_EPS) * g + b


def _merge_kernel(x_ref, g0_ref, g1_ref, g2_ref, ym_ref, ys_ref, us_ref, yh_ref, wm_ref, ws_ref, wh_ref, wglu_ref,
                  d_ref, wo_ref, lg_ref, lb_ref, rw_ref, rb_ref, o_ref):
    ys = _gelu_tanh(ys_ref[...] + d_ref[...] * us_ref[...].astype(F32))
    ys = ys * _sigmoid(_dot(ys, wglu_ref[...]))
    merged = (_sigmoid(g0_ref[...].astype(F32)) * _dot(ym_ref[...], wm_ref[...])
              + _sigmoid(g1_ref[...].astype(F32)) * _dot(ys, ws_ref[...])
              + _sigmoid(g2_ref[...].astype(F32)) * _dot(yh_ref[...], wh_ref[...]))
    x1 = _layer_norm(DN_ALPHA * x_ref[...] + _dot(merged, wo_ref[...]), lg_ref[...], lb_ref[...])
    o_ref[:, 0:D_MODEL] = x1

    xh = x1.astype(BF16)
    xl = (x1 - xh.astype(F32)).astype(BF16)
    logits = (jnp.dot(xh, rw_ref[0], preferred_element_type=F32) + jnp.dot(xl, rw_ref[0], preferred_element_type=F32)
              + jnp.dot(xh, rw_ref[1], preferred_element_type=F32))
    s0 = _sigmoid(logits)
    sb0 = s0 + rb_ref[...]
    shift = lambda a, j: a if j == 0 else pltpu.roll(a, LANES - N_GROUPS * j, 1)
    s = [shift(s0, j) for j in range(EXPERTS_PER_GROUP)]
    sb = [shift(sb0, j) for j in range(EXPERTS_PER_GROUP)]
    hi1, lo1 = jnp.maximum(sb[0], sb[1]), jnp.minimum(sb[0], sb[1])
    hi2, lo2 = jnp.maximum(sb[2], sb[3]), jnp.minimum(sb[2], sb[3])
    top2 = jnp.maximum(hi1, hi2) + jnp.maximum(jnp.minimum(hi1, hi2), jnp.maximum(lo1, lo2))
    lane = lax.broadcasted_iota(jnp.int32, top2.shape, 1)
    top2 = jnp.where(lane < N_GROUPS, top2, -jnp.inf)
    gmax = jnp.max(top2, axis=1, keepdims=True)
    g_idx = jnp.min(jnp.where(top2 == gmax, lane, LANES), axis=1, keepdims=True)
    sel = lane == g_idx
    v = [jnp.sum(jnp.where(sel, sb[j], 0.0), axis=1, keepdims=True) for j in range(EXPERTS_PER_GROUP)]
    sv = [jnp.sum(jnp.where(sel, s[j], 0.0), axis=1, keepdims=True) for j in range(EXPERTS_PER_GROUP)]

    def first_max(vals):
        m = jnp.maximum(jnp.maximum(vals[0], vals[1]), jnp.maximum(vals[2], vals[3]))
        return jnp.where(vals[0] == m, 0, jnp.where(vals[1] == m, 1, jnp.where(vals[2] == m, 2, 3)))

    e1 = first_max(v)
    e2 = first_max([jnp.where(e1 == j, -jnp.inf, v[j]) for j in range(EXPERTS_PER_GROUP)])
    s1 = sum(jnp.where(e1 == j, sv[j], 0.0) for j in range(EXPERTS_PER_GROUP))
    s2 = sum(jnp.where(e2 == j, sv[j], 0.0) for j in range(EXPERTS_PER_GROUP))
    tot = s1 + s2
    meta = jnp.where(lane == 0, g_idx.astype(F32), 0.0)
    for j in range(EXPERTS_PER_GROUP):
        cw = jnp.where(e1 == j, s1 / tot, 0.0) + jnp.where(e2 == j, s2 / tot, 0.0)
        meta = jnp.where(lane == 1 + j, cw, meta)
    o_ref[:, D_MODEL:D_MODEL + META] = meta


def _merge(x, z1, z2, ym, ys, yh, wm, ws, wh, wglu, d, wo, lg, lb, router_w, router_bias, tm):
    t = x.shape[0]
    rw = router_w.astype(F32).reshape(D_MODEL, N_GROUPS, EXPERTS_PER_GROUP).transpose(0, 2, 1)
    rw = jnp.pad(rw.reshape(D_MODEL, N_EXPERTS), ((0, 0), (0, LANES - N_EXPERTS)))
    rw_hi = rw.astype(BF16)
    rw = jnp.stack([rw_hi, (rw - rw_hi.astype(F32)).astype(BF16)])
    rb = jnp.pad(router_bias.astype(F32).reshape(N_GROUPS, EXPERTS_PER_GROUP).T.reshape(1, N_EXPERTS),
                 ((0, 0), (0, LANES - N_EXPERTS)))
    full = lambda shape: pl.BlockSpec(shape, lambda i: (0,) * len(shape))
    row = lambda w: pl.BlockSpec((tm, w), lambda i: (i, 0))
    return pl.pallas_call(
        _merge_kernel,
        grid=(t // tm,),
        in_specs=[row(D_MODEL),
                  pl.BlockSpec((tm, D_MODEL), lambda i: (i, 0)),
                  pl.BlockSpec((tm, D_MODEL), lambda i: (i, 1)),
                  pl.BlockSpec((tm, D_MODEL), lambda i: (i, 2)),
                  row(WIDTH), row(WIDTH), pl.BlockSpec((tm, WIDTH), lambda i: (i, Z2_US)), row(WIDTH),
                  full((WIDTH, D_MODEL)), full((WIDTH, D_MODEL)), full((WIDTH, D_MODEL)), full((WIDTH, WIDTH)),
                  full((1, WIDTH)), full((D_MODEL, D_MODEL)), full((1, D_MODEL)), full((1, D_MODEL)),
                  full((2, D_MODEL, LANES)), full((1, LANES))],
        out_specs=pl.BlockSpec((tm, D_MODEL + META), lambda i: (i, 0)),
        out_shape=jax.ShapeDtypeStruct((t, D_MODEL + META), F32),
        compiler_params=_cparams(("parallel",)),
        name="merge_router",
    )(x, z1, z1, z1, ym, ys, z2, yh, wm.astype(BF16), ws.astype(BF16), wh.astype(BF16), wglu.astype(BF16),
      d.reshape(1, WIDTH).astype(F32), wo.astype(BF16), lg.reshape(1, D_MODEL).astype(F32),
      lb.reshape(1, D_MODEL).astype(F32), rw, rb)


def _start_row_gather(idx_ref, base, src_hbm, dst, sem, rows):
    for r in range(rows):
        pltpu.make_async_copy(src_hbm.at[pl.ds(idx_ref[base + r], 1)], dst.at[pl.ds(r, 1)], sem).start(priority=r % 2)


def _wait_row_gather(src_hbm, dst, sem, rows):
    pltpu.make_async_copy(src_hbm.at[pl.ds(0, rows)], dst, sem).wait()


def _dispatch_kernel(dst_ref, x_ref, init_hbm, o_hbm, sem, *, tm):
    del init_hbm
    base = pl.program_id(0) * tm
    for r in range(tm):
        pltpu.make_async_copy(x_ref.at[pl.ds(r, 1)], o_hbm.at[pl.ds(dst_ref[base + r], 1)], sem).start(priority=r % 2)
    pltpu.make_async_copy(x_ref, o_hbm.at[pl.ds(0, tm)], sem).wait()


def _moe_kernel(tg_ref, x_ref, wg_ref, wu_ref, wd_ref, o_ref, wg_s, wu_s, wd_s, *, tm):
    i = pl.program_id(0)

    @pl.when(jnp.logical_or(i == 0, tg_ref[i] != tg_ref[jnp.maximum(i - 1, 0)]))
    def _():
        for e in range(EXPERTS_PER_GROUP):
            wg_s[e] = wg_ref[0, e].astype(BF16)
            wu_s[e] = wu_ref[0, e].astype(BF16)
            wd_s[e] = wd_ref[0, e].astype(BF16)

    xb = x_ref[:, 0:D_MODEL].astype(BF16)
    meta = x_ref[:, D_MODEL:D_MODEL + META]
    y = jnp.zeros((tm, D_MODEL), F32)
    for e in range(EXPERTS_PER_GROUP):
        hg = jnp.dot(xb, wg_s[e], preferred_element_type=F32)
        hu = jnp.dot(xb, wu_s[e], preferred_element_type=F32)
        hh = _silu(hg) * hu * _lane_col(meta, 1 + e)
        y = y + jnp.dot(hh.astype(BF16), wd_s[e], preferred_element_type=F32)
    o_ref[...] = y


def _ln2_kernel(dst_ref, x_ref, y_hbm, lg_ref, lb_ref, o_ref, buf, sem, *, tm, ntiles):
    i = pl.program_id(0)
    slot = i % 2

    @pl.when(i == 0)
    def _():
        _start_row_gather(dst_ref, 0, y_hbm, buf.at[0], sem.at[0], tm)

    @pl.when(i + 1 < ntiles)
    def _():
        _start_row_gather(dst_ref, (i + 1) * tm, y_hbm, buf.at[1 - slot], sem.at[1 - slot], tm)

    _wait_row_gather(y_hbm, buf.at[slot], sem.at[slot], tm)
    o_ref[...] = _layer_norm(DN_ALPHA * x_ref[...] + buf[slot], lg_ref[...], lb_ref[...])


def _moe_ln(x1e, wg, wu, wd, layer, lg, lb, tm):
    t = x1e.shape[0]
    ntiles = t // tm + N_GROUPS
    p = ntiles * tm
    key = x1e[:, D_MODEL].astype(jnp.int32)
    onehot = (key[:, None] == jnp.arange(N_GROUPS)[None, :]).astype(jnp.int32)
    csum = jnp.cumsum(onehot, axis=0)
    counts = csum[-1]
    rank = jnp.sum(onehot * csum, axis=1) - 1
    pcount = ((counts + tm - 1) // tm) * tm
    pend = jnp.cumsum(pcount)
    dest = (pend - pcount)[key] + rank
    dest = dest.astype(jnp.int32)
    tile_start = jnp.arange(ntiles, dtype=jnp.int32) * tm
    tile_group = jnp.minimum(jnp.sum((tile_start[:, None] >= pend[None, :]).astype(jnp.int32), axis=1),
                             N_GROUPS - 1)
    nt2 = t // tm
    x_sorted = pl.pallas_call(
        functools.partial(_dispatch_kernel, tm=tm),
        grid_spec=pltpu.PrefetchScalarGridSpec(
            num_scalar_prefetch=1,
            grid=(nt2,),
            in_specs=[pl.BlockSpec((tm, D_MODEL + META), lambda i, d: (i, 0)), pl.BlockSpec(memory_space=pl.ANY)],
            out_specs=pl.BlockSpec(memory_space=pl.ANY),
            scratch_shapes=[pltpu.SemaphoreType.DMA(())]),
        out_shape=jax.ShapeDtypeStruct((p, D_MODEL + META), F32),
        input_output_aliases={2: 0},
        compiler_params=_cparams(("arbitrary",)),
        name="moe_dispatch",
    )(dest, x1e, jnp.zeros((p, D_MODEL + META), F32))
    wspec = lambda a, b: pl.BlockSpec((1, EXPERTS_PER_GROUP, a, b), lambda i, tg: (layer, tg[i], 0, 0))
    wscr = lambda a, b: pltpu.VMEM((EXPERTS_PER_GROUP, a, b), BF16)
    y_sorted = pl.pallas_call(
        functools.partial(_moe_kernel, tm=tm),
        grid_spec=pltpu.PrefetchScalarGridSpec(
            num_scalar_prefetch=1,
            grid=(ntiles,),
            in_specs=[pl.BlockSpec((tm, D_MODEL + META), lambda i, tg: (i, 0)), wspec(D_MODEL, D_EXPERT),
                      wspec(D_MODEL, D_EXPERT), wspec(D_EXPERT, D_MODEL)],
            out_specs=pl.BlockSpec((tm, D_MODEL), lambda i, tg: (i, 0)),
            scratch_shapes=[wscr(D_MODEL, D_EXPERT), wscr(D_MODEL, D_EXPERT), wscr(D_EXPERT, D_MODEL)]),
        out_shape=jax.ShapeDtypeStruct((p, D_MODEL), F32),
        compiler_params=_cparams(("arbitrary",)),
        name="moe_experts",
    )(tile_group, x_sorted, wg, wu, wd)
    return pl.pallas_call(
        functools.partial(_ln2_kernel, tm=tm, ntiles=nt2),
        grid_spec=pltpu.PrefetchScalarGridSpec(
            num_scalar_prefetch=1,
            grid=(nt2,),
            in_specs=[pl.BlockSpec((tm, D_MODEL), lambda i, d: (i, 0)), pl.BlockSpec(memory_space=pl.ANY),
                      pl.BlockSpec((1, D_MODEL), lambda i, d: (0, 0)), pl.BlockSpec((1, D_MODEL), lambda i, d: (0, 0))],
            out_specs=pl.BlockSpec((tm, D_MODEL), lambda i, d: (i, 0)),
            scratch_shapes=[pltpu.VMEM((2, tm, D_MODEL), F32), pltpu.SemaphoreType.DMA((2,))]),
        out_shape=jax.ShapeDtypeStruct((t, D_MODEL), F32),
        compiler_params=_cparams(("arbitrary",)),
        name="moe_combine_ln",
    )(dest.astype(jnp.int32), x1e, y_sorted, lg.reshape(1, D_MODEL).astype(F32), lb.reshape(1, D_MODEL).astype(F32))


def _layer(x, l, p, lb_l, cfg):
    t = x.shape[0]
    w1, w2, wt = _split_w_in(p['w_in'][l])
    xb = x.astype(BF16)
    z1 = _matmul(xb, w1, Z1_DTYPE, cfg['tm_in'], cfg['tn_in'])
    z2, gt = _inproj_gate(xb, w2, wt, cfg['tm_in'])
    gt3 = gt.reshape(SUBLANES, t // CHUNK, CHUNK).transpose(1, 0, 2)
    ym = _mlstm(z1, z2, gt3, p['mlstm_conv_w'][l], p['mlstm_conv_b'][l], p['mlstm_wq'][l], p['mlstm_wk'][l],
                p['mlstm_wv'][l], p['mlstm_b_i'][l], p['mlstm_b_f'][l], p['mlstm_skip'][l], cfg['tb'])
    ys = _s5(z2, p['s5_lambda_re'][l], p['s5_lambda_im'][l], p['s5_log_dt'][l], p['s5_b_re'][l], p['s5_b_im'][l],
             p['s5_c_re'][l], p['s5_c_im'][l])
    yh = _hgrn(z1, z2, lb_l, p['hgrn_norm_w'][l], cfg['tb'])
    x1e = _merge(x, z1, z2, ym, ys, yh, p['w_branch_mlstm'][l], p['w_branch_s5'][l], p['w_branch_hgrn'][l],
                 p['s5_w_glu'][l], p['s5_d'][l], p['w_out'][l], p['ln1_g'][l], p['ln1_b'][l], p['router_w'],
                 p['router_bias'], cfg['tm_merge'])
    return _moe_ln(x1e, p['exp_w_gate'], p['exp_w_up'], p['exp_w_down'], l, p['ln2_g'][l], p['ln2_b'][l],
                   cfg['tm_moe'])


_CFG = dict(tm_in=1024, tn_in=Z1_COLS // 4, tb=512, tm_merge=256, tm_moe=256)


def kernel(x, w_in, mlstm_conv_w, mlstm_conv_b, mlstm_wq, mlstm_wk, mlstm_wv, mlstm_b_i, mlstm_b_f, mlstm_skip, s5_lambda_re, s5_lambda_im, s5_log_dt, s5_b_re, s5_b_im, s5_c_re, s5_c_im, s5_d, s5_w_glu, hgrn_lower_bounds, hgrn_norm_w, w_branch_mlstm, w_branch_s5, w_branch_hgrn, w_out, ln1_g, ln1_b, ln2_g, ln2_b, router_w, router_bias, exp_w_gate, exp_w_up, exp_w_down):
    p = dict(w_in=w_in, mlstm_conv_w=mlstm_conv_w, mlstm_conv_b=mlstm_conv_b, mlstm_wq=mlstm_wq, mlstm_wk=mlstm_wk,
             mlstm_wv=mlstm_wv, mlstm_b_i=mlstm_b_i, mlstm_b_f=mlstm_b_f, mlstm_skip=mlstm_skip,
             s5_lambda_re=s5_lambda_re, s5_lambda_im=s5_lambda_im, s5_log_dt=s5_log_dt, s5_b_re=s5_b_re,
             s5_b_im=s5_b_im, s5_c_re=s5_c_re, s5_c_im=s5_c_im, s5_d=s5_d, s5_w_glu=s5_w_glu,
             hgrn_norm_w=hgrn_norm_w, w_branch_mlstm=w_branch_mlstm, w_branch_s5=w_branch_s5,
             w_branch_hgrn=w_branch_hgrn, w_out=w_out, ln1_g=ln1_g, ln1_b=ln1_b, ln2_g=ln2_g, ln2_b=ln2_b,
             router_w=router_w, router_bias=router_bias, exp_w_gate=exp_w_gate, exp_w_up=exp_w_up,
             exp_w_down=exp_w_down)
    lb_cum = jnp.cumsum(jax.nn.softmax(hgrn_lower_bounds.astype(F32), axis=0), axis=0)
    lb_layers = lb_cum - lb_cum[0]
    bsz, seq, d = x.shape
    h = x.reshape(bsz * seq, d)
    for l in range(DEPTH):
        h = _layer(h, l, p, lb_layers[l], _CFG)
    return h.reshape(bsz, seq, d)
```

```python
import functools
import math

import jax
import jax.numpy as jnp
from jax import lax
from jax.experimental import pallas as pl
from jax.experimental.pallas import tpu as pltpu

F32 = jnp.float32
BF16 = jnp.bfloat16
HIGHEST = lax.Precision.HIGHEST

D_MODEL = 1024
DEPTH = 2
HEADS = 4
HEAD_DIM = 128
WIDTH = HEADS * HEAD_DIM
MLSTM_CONV = 4
CHUNK = 64
S5_GROUP = 16
S5_GROUPS = 32
S5_STATE = 64
S5_PAIRS = S5_GROUPS // 2
S5_CHUNK = 32
S5_MAX_REAL = -1e-4
N_EXPERTS = 32
N_GROUPS = 8
EXPERTS_PER_GROUP = 4
D_EXPERT = 256
DN_ALPHA = (2 * DEPTH) ** 0.25
LN_EPS = 1e-5
NORM_EPS = 1e-6

LANES = 128
SUBLANES = 8
META = LANES
VMEM_LIMIT = 56 * 1024 * 1024

Z1_DTYPE = BF16
Z1_GATE, Z1_XM, Z1_OM, Z1_QH, Z1_IH, Z1_GH = 0, 6, 7, 8, 9, 10
Z1_COLS = 11 * WIDTH
Z2_FH, Z2_US = 0, 1
Z2_COLS = 2 * WIDTH + LANES


def _cparams(sem):
    return pltpu.CompilerParams(dimension_semantics=sem, vmem_limit_bytes=VMEM_LIMIT)


def _sigmoid(x):
    return 0.5 * (1.0 + jnp.tanh(0.5 * x))


def _silu(x):
    return x * _sigmoid(x)


def _log_sigmoid(x):
    return jnp.minimum(x, 0.0) - jnp.log(1.0 + jnp.exp(-jnp.abs(x)))


def _gelu_tanh(x):
    return 0.5 * x * (1.0 + jnp.tanh(math.sqrt(2.0 / math.pi) * (x + 0.044715 * (x * x * x))))


def _cumsum_rows(x):
    n = x.shape[0]
    row = lax.broadcasted_iota(jnp.int32, x.shape, 0)
    s = 1
    while s < n:
        x = x + jnp.where(row >= s, pltpu.roll(x, s, 0), 0.0)
        s *= 2
    return x


def _lane_col(x, idx):
    lane = lax.broadcasted_iota(jnp.int32, x.shape, 1)
    return jnp.sum(jnp.where(lane == idx, x, 0.0), axis=1, keepdims=True)


def _dot(a, b):
    return jnp.dot(a.astype(BF16), b.astype(BF16), preferred_element_type=F32)


def _dot_nt(a, b):
    return lax.dot_general(a.astype(BF16), b.astype(BF16), (((1,), (1,)), ((), ())), preferred_element_type=F32)


def _dot_tn(a, b):
    return lax.dot_general(a.astype(BF16), b.astype(BF16), (((0,), (0,)), ((), ())), preferred_element_type=F32)


def _mm_kernel(x_ref, w_ref, o_ref):
    o_ref[...] = jnp.dot(x_ref[...], w_ref[...], preferred_element_type=F32).astype(o_ref.dtype)


def _matmul(x, w, out_dtype, tm, tn):
    m, k = x.shape
    n = w.shape[1]
    return pl.pallas_call(
        _mm_kernel,
        grid=(m // tm, n // tn),
        in_specs=[pl.BlockSpec((tm, k), lambda i, j: (i, 0)), pl.BlockSpec((k, tn), lambda i, j: (0, j))],
        out_specs=pl.BlockSpec((tm, tn), lambda i, j: (i, j)),
        out_shape=jax.ShapeDtypeStruct((m, n), out_dtype),
        compiler_params=_cparams(("parallel", "parallel")),
        name="in_proj_wide",
    )(x, w)


def _inproj_gate_kernel(x_ref, w_ref, wt_ref, o_ref, gt_ref):
    x = x_ref[...]
    o_ref[...] = jnp.dot(x, w_ref[...], preferred_element_type=F32)
    gt_ref[...] = lax.dot_general(wt_ref[...], x, (((1,), (1,)), ((), ())), preferred_element_type=F32)


def _inproj_gate(x, w2, wt, tm):
    m, k = x.shape
    return pl.pallas_call(
        _inproj_gate_kernel,
        grid=(m // tm,),
        in_specs=[pl.BlockSpec((tm, k), lambda i: (i, 0)),
                  pl.BlockSpec((k, Z2_COLS), lambda i: (0, 0)),
                  pl.BlockSpec((SUBLANES, k), lambda i: (0, 0))],
        out_specs=[pl.BlockSpec((tm, Z2_COLS), lambda i: (i, 0)), pl.BlockSpec((SUBLANES, tm), lambda i: (0, i))],
        out_shape=[jax.ShapeDtypeStruct((m, Z2_COLS), F32), jax.ShapeDtypeStruct((SUBLANES, m), F32)],
        compiler_params=_cparams(("parallel",)),
        name="in_proj_gates",
    )(x, w2, wt)


def _split_w_in(w):
    offs, o = [], 0
    for s in (WIDTH, WIDTH, HEADS, HEADS, WIDTH, WIDTH, WIDTH, WIDTH, WIDTH, 3 * D_MODEL):
        offs.append((o, o + s))
        o += s
    seg = [w[:, a:b] for a, b in offs]
    xm, om, im, fm, us, qh, fh, ih, gh, gate = seg
    w1 = jnp.concatenate([gate, xm, om, qh, ih, gh], axis=1).astype(BF16)
    pad = jnp.zeros((w.shape[0], LANES - 2 * HEADS), w.dtype)
    w2 = jnp.concatenate([fh, us, im, fm, pad], axis=1).astype(BF16)
    wt = jnp.concatenate([im, fm], axis=1).T.astype(BF16)
    return w1, w2, wt


def _mlstm_kernel(xm_ref, om_ref, gc_ref, gr_ref, cw_ref, cb_ref, wq_ref, wk_ref, wv_ref, bcol_ref, brow_ref,
                  skip_ref, triu_ref, o_ref, xpad, q_s, k_s, v_s, xc_s, h_s, gcol_s, grow_s, brow_s, ct_s, n_s, m_s,
                  *, tb):
    ncb = tb // CHUNK

    @pl.when(pl.program_id(0) == 0)
    def _():
        xpad[0:SUBLANES, :] = jnp.zeros((SUBLANES, WIDTH), F32)
        ct_s[...] = jnp.zeros_like(ct_s)
        n_s[...] = jnp.zeros_like(n_s)
        m_s[...] = jnp.zeros_like(m_s)

    xm = xm_ref[...].astype(F32)
    xpad[SUBLANES:SUBLANES + tb, :] = xm
    cw = cw_ref[...]
    conv = cb_ref[...] + cw[3:4, :] * xm
    for d in range(1, MLSTM_CONV):
        conv = conv + cw[3 - d:4 - d, :] * xpad[SUBLANES - d:SUBLANES - d + tb, :]
    xpad[0:SUBLANES, :] = xpad[tb:tb + SUBLANES, :]
    xc = _silu(conv)
    xc_s[...] = xc
    for h in range(HEADS):
        sl = slice(h * HEAD_DIM, (h + 1) * HEAD_DIM)
        xch = xc[:, sl].astype(BF16)
        q_s[:, sl] = jnp.dot(xch, wq_ref[h], preferred_element_type=F32) * (HEAD_DIM ** -0.5)
        k_s[:, sl] = jnp.dot(xch, wk_ref[h], preferred_element_type=F32)
        v_s[:, sl] = jnp.dot(xm[:, sl].astype(BF16), wv_ref[h], preferred_element_type=F32)

    gc = gc_ref[...] + bcol_ref[...]
    lane = lax.broadcasted_iota(jnp.int32, gc.shape, 1)
    gcol_s[...] = jnp.where(lane < HEADS, gc, _log_sigmoid(gc))
    gr = gr_ref[...] + brow_ref[...]
    sub = lax.broadcasted_iota(jnp.int32, gr.shape, 1)
    gr = jnp.where(sub < HEADS, gr, _log_sigmoid(gr)).reshape(ncb * SUBLANES, CHUNK)
    grow_s[...] = gr
    brow_s[...] = jnp.dot(gr, triu_ref[...], precision=HIGHEST, preferred_element_type=F32)

    rowi = lax.broadcasted_iota(jnp.int32, (CHUNK, CHUNK), 0)
    coli = lax.broadcasted_iota(jnp.int32, (CHUNK, CHUNK), 1)
    causal = rowi >= coli

    def chunk_body(c, carry):
        r0 = pl.multiple_of(c * CHUNK, CHUNK)
        g0 = pl.multiple_of(c * SUBLANES, SUBLANES)
        gcol = gcol_s[pl.ds(r0, CHUNK), :]
        bcol = _cumsum_rows(gcol)
        grow = grow_s[pl.ds(g0, SUBLANES), :]
        brow = brow_s[pl.ds(g0, SUBLANES), :]
        for h in range(HEADS):
            sl = slice(h * HEAD_DIM, (h + 1) * HEAD_DIM)
            li_c = _lane_col(gcol, h)
            b_c = _lane_col(bcol, HEADS + h)
            li_r = grow[h:h + 1, :]
            b_r = brow[HEADS + h:HEADS + h + 1, :]
            b_tot = b_c[CHUNK - 1:CHUNK, :]
            a_c = b_tot - b_c + li_c
            a_max = jnp.max(a_c, axis=0, keepdims=True)
            q = q_s[pl.ds(r0, CHUNK), sl]
            k = k_s[pl.ds(r0, CHUNK), sl]
            v = v_s[pl.ds(r0, CHUNK), sl]
            wk = jnp.exp(a_c - a_max) * k
            c_loc_t = _dot_tn(v, wk)
            n_loc = jnp.sum(wk, axis=0, keepdims=True)
            ct_prev = ct_s[h]
            n_prev = n_s[h:h + 1, :]
            m_prev = m_s[h:h + 1, 0:1]
            d_mat = jnp.where(causal, b_c - b_r + li_r, -jnp.inf)
            m_inter = b_c + m_prev
            m_j = jnp.maximum(m_inter, jnp.max(d_mat, axis=1, keepdims=True))
            sc = _dot_nt(q, k) * jnp.exp(d_mat - m_j)
            g_inter = jnp.exp(m_inter - m_j)
            num = g_inter * _dot_nt(q, ct_prev) + _dot(sc, v)
            den = g_inter * jnp.sum(q * n_prev, axis=1, keepdims=True) + jnp.sum(sc, axis=1, keepdims=True)
            h_s[pl.ds(r0, CHUNK), sl] = num / jnp.maximum(jnp.abs(den), jnp.exp(-m_j))
            m_new = jnp.maximum(b_tot + m_prev, a_max)
            g_old = jnp.exp(b_tot + m_prev - m_new)
            g_loc = jnp.exp(a_max - m_new)
            ct_s[h] = g_old * ct_prev + g_loc * c_loc_t
            n_s[h:h + 1, :] = g_old * n_prev + g_loc * n_loc
            m_s[h:h + 1, :] = jnp.broadcast_to(m_new, (1, LANES))
        return carry

    lax.fori_loop(0, ncb, chunk_body, 0, unroll=4)
    o_ref[...] = (_sigmoid(om_ref[...].astype(F32)) * h_s[...] + skip_ref[...] * xc_s[...]).astype(o_ref.dtype)


def _mlstm(z1, z2, gt3, conv_w, conv_b, wq, wk, wv, b_i, b_f, skip, tb):
    t = z1.shape[0]
    ncb = tb // CHUNK
    bias = jnp.concatenate([b_i, b_f]).astype(F32)
    bcol = jnp.zeros((1, LANES), F32).at[0, :2 * HEADS].set(bias)
    brow = bias.reshape(2 * HEADS, 1)
    triu = jnp.triu(jnp.ones((CHUNK, CHUNK), F32))
    full = lambda shape: pl.BlockSpec(shape, lambda i: (0,) * len(shape))
    return pl.pallas_call(
        functools.partial(_mlstm_kernel, tb=tb),
        grid=(t // tb,),
        in_specs=[pl.BlockSpec((tb, WIDTH), lambda i: (i, Z1_XM)),
                  pl.BlockSpec((tb, WIDTH), lambda i: (i, Z1_OM)),
                  pl.BlockSpec((tb, LANES), lambda i: (i, 2 * WIDTH // LANES)),
                  pl.BlockSpec((ncb, SUBLANES, CHUNK), lambda i: (i, 0, 0)),
                  full((MLSTM_CONV, WIDTH)), full((1, WIDTH)),
                  full((HEADS, HEAD_DIM, HEAD_DIM)), full((HEADS, HEAD_DIM, HEAD_DIM)),
                  full((HEADS, HEAD_DIM, HEAD_DIM)),
                  full((1, LANES)), full((2 * HEADS, 1)), full((1, WIDTH)), full((CHUNK, CHUNK))],
        out_specs=pl.BlockSpec((tb, WIDTH), lambda i: (i, 0)),
        out_shape=jax.ShapeDtypeStruct((t, WIDTH), F32),
        scratch_shapes=[pltpu.VMEM((tb + SUBLANES, WIDTH), F32)]
        + [pltpu.VMEM((tb, WIDTH), F32) for _ in range(5)]
        + [pltpu.VMEM((tb, LANES), F32),
           pltpu.VMEM((ncb * SUBLANES, CHUNK), F32), pltpu.VMEM((ncb * SUBLANES, CHUNK), F32),
           pltpu.VMEM((HEADS, HEAD_DIM, HEAD_DIM), F32), pltpu.VMEM((SUBLANES, LANES), F32),
           pltpu.VMEM((SUBLANES, LANES), F32)],
        compiler_params=_cparams(("arbitrary",)),
        name="mlstm",
    )(z1, z1, z2, gt3, conv_w.astype(F32), conv_b.reshape(1, WIDTH).astype(F32), wq.astype(BF16), wk.astype(BF16),
      wv.astype(BF16), bcol, brow, skip.reshape(1, WIDTH).astype(F32), triu)


def _hgrn_kernel(q_ref, f_ref, i_ref, g_ref, lb_ref, nw_ref, tri_ref, o_ref, st_s, *, tb):
    ncb = tb // CHUNK

    @pl.when(pl.program_id(0) == 0)
    def _():
        st_s[...] = jnp.zeros_like(st_s)

    lb = lb_ref[...]
    nw = nw_ref[...]
    tri = tri_ref[...]
    rowi = lax.broadcasted_iota(jnp.int32, (CHUNK, WIDTH), 0)
    sr = lax.broadcasted_iota(jnp.int32, (CHUNK, CHUNK), 0)
    sc = lax.broadcasted_iota(jnp.int32, (CHUNK, CHUNK), 1)
    halves = [1 << p for p in range(CHUNK.bit_length() - 1)]
    upper = {m: (rowi & m) != 0 for m in halves}
    same_blk = {m: (sr // (2 * m)) == (sc // (2 * m)) for m in halves}

    def chunk_body(c, carry):
        r0 = pl.multiple_of(c * CHUNK, CHUNK)
        f = lb + (1.0 - lb) * _sigmoid(f_ref[pl.ds(r0, CHUNK), :].astype(F32))
        k = 1.0 - f
        q = _silu(q_ref[pl.ds(r0, CHUNK), :].astype(F32))
        v = i_ref[pl.ds(r0, CHUNK), :].astype(F32)
        lf = jnp.log(f)
        hi = lf.astype(BF16)
        r1 = lf - hi.astype(F32)
        mid = r1.astype(BF16)
        lo = (r1 - mid.astype(F32)).astype(BF16)
        b = (jnp.dot(tri, hi, preferred_element_type=F32) + jnp.dot(tri, mid, preferred_element_type=F32)
             + jnp.dot(tri, lo, preferred_element_type=F32))
        qs, ks = {}, {}
        for m in halves:
            if m == 1:
                t = jnp.where(upper[m], q * f, k)
            else:
                nblk = CHUNK // (2 * m)
                r = jnp.concatenate([jnp.broadcast_to(b[2 * m * j + m - 1:2 * m * j + m, :], (2 * m, WIDTH))
                                     for j in range(nblk)], axis=0)
                t = jnp.where(upper[m], q, k) * jnp.exp(-jnp.abs(b - r))
            qs[m] = jnp.where(upper[m], t, 0.0).astype(BF16)
            ks[m] = jnp.where(upper[m], 0.0, t).astype(BF16)
        b_last = b[CHUNK - 1:CHUNK, :]
        qe = q * jnp.exp(b)
        kd = k * jnp.exp(b_last - b)
        e_last = jnp.exp(b_last)
        outs = []
        for h in range(HEADS):
            sl = slice(h * HEAD_DIM, (h + 1) * HEAD_DIM)
            st = st_s[h]
            a = jnp.where(sr == sc, _dot_nt(q[:, sl], k[:, sl]), 0.0)
            for m in halves:
                a = a + jnp.where(same_blk[m], _dot_nt(qs[m][:, sl], ks[m][:, sl]), 0.0)
            oh = _dot_nt(qe[:, sl], st) + _dot(a, v[:, sl])
            st_s[h] = e_last[:, sl] * st + _dot_tn(v[:, sl], kd[:, sl])
            ms = jnp.sum(oh * oh, axis=1, keepdims=True) * (1.0 / HEAD_DIM)
            outs.append(oh * lax.rsqrt(ms + NORM_EPS))
        on = jnp.concatenate(outs, axis=1)
        g = g_ref[pl.ds(r0, CHUNK), :].astype(F32)
        o_ref[pl.ds(r0, CHUNK), :] = (on * nw * _silu(g)).astype(o_ref.dtype)
        return carry

    lax.fori_loop(0, ncb, chunk_body, 0, unroll=4)


def _hgrn(z1, z2, lb, norm_w, tb):
    t = z1.shape[0]
    tril = jnp.tril(jnp.ones((CHUNK, CHUNK), F32)).astype(BF16)
    full = lambda shape: pl.BlockSpec(shape, lambda i: (0,) * len(shape))
    return pl.pallas_call(
        functools.partial(_hgrn_kernel, tb=tb),
        grid=(t // tb,),
        in_specs=[pl.BlockSpec((tb, WIDTH), lambda i: (i, Z1_QH)),
                  pl.BlockSpec((tb, WIDTH), lambda i: (i, 0)),
                  pl.BlockSpec((tb, WIDTH), lambda i: (i, Z1_IH)),
                  pl.BlockSpec((tb, WIDTH), lambda i: (i, Z1_GH)),
                  full((1, WIDTH)), full((1, WIDTH)), full((CHUNK, CHUNK))],
        out_specs=pl.BlockSpec((tb, WIDTH), lambda i: (i, 0)),
        out_shape=jax.ShapeDtypeStruct((t, WIDTH), F32),
        scratch_shapes=[pltpu.VMEM((HEADS, HEAD_DIM, HEAD_DIM), F32)],
        compiler_params=_cparams(("arbitrary",)),
        name="hgrn2",
    )(z1, z2, z1, z1, lb.reshape(1, WIDTH).astype(F32), norm_w.reshape(1, WIDTH).astype(F32), tril)


S5_PB = 2 * S5_GROUP
S5_ROWS = S5_CHUNK * S5_PB
S5_SLAB = LANES // S5_PB


def _s5_toeplitz_kernel(k_ref, m_ref):
    krow = k_ref[0]
    lane = lax.broadcasted_iota(jnp.int32, krow.shape, 1)
    for s in range(S5_CHUNK):
        blk = krow if s == 0 else jnp.where(lane >= s * S5_PB, pltpu.roll(krow, s * S5_PB, 1), 0.0)
        m_ref[0, s * S5_PB:(s + 1) * S5_PB, :] = blk.astype(m_ref.dtype)


def _s5_toeplitz(krow):
    return pl.pallas_call(
        _s5_toeplitz_kernel,
        grid=(S5_PAIRS,),
        in_specs=[pl.BlockSpec((1, S5_PB, S5_ROWS), lambda j: (j, 0, 0))],
        out_specs=pl.BlockSpec((1, S5_ROWS, S5_ROWS), lambda j: (j, 0, 0)),
        out_shape=jax.ShapeDtypeStruct((S5_PAIRS, S5_ROWS, S5_ROWS), BF16),
        compiler_params=_cparams(("parallel",)),
        name="s5_toeplitz",
    )(krow)


def _s5_tables(lam_re, lam_im, log_dt, b_re, b_im, c_re, c_im):
    ln = S5_CHUNK
    lr = jnp.minimum(lam_re.astype(F32), S5_MAX_REAL)
    li = lam_im.astype(F32)
    dt = jnp.exp(log_dt.astype(F32))[:, None]
    mag = jnp.exp(lr * dt)
    ab_re = mag * jnp.cos(li * dt)
    ab_im = mag * jnp.sin(li * dt)
    nr = ab_re - 1.0
    den = lr * lr + li * li
    cr = (nr * lr + ab_im * li) / den
    ci = (ab_im * lr - nr * li) / den
    bb_re = cr[..., None] * b_re - ci[..., None] * b_im
    bb_im = cr[..., None] * b_im + ci[..., None] * b_re
    tau = jnp.arange(ln + 1, dtype=F32)[:, None, None]
    pm = jnp.exp(lr * dt * tau)
    pr = pm * jnp.cos(li * dt * tau)
    pi = pm * jnp.sin(li * dt * tau)
    def block_diag(a):
        a0, a1 = a[0::2], a[1::2]
        z = jnp.zeros_like(a0)
        return jnp.concatenate([jnp.concatenate([a0, z], axis=2), jnp.concatenate([z, a1], axis=2)], axis=1)

    def pair_pow(p):
        return p.reshape(p.shape[0], S5_PAIRS, 2 * S5_STATE).transpose(1, 0, 2)

    bbr = block_diag(bb_re.transpose(0, 2, 1))
    bbi = block_diag(bb_im.transpose(0, 2, 1))
    ccr = block_diag(c_re.transpose(0, 2, 1))
    cci = block_diag(c_im.transpose(0, 2, 1))
    col = jnp.arange(S5_ROWS)
    rep_t = (col[None, :] // S5_PB == jnp.arange(ln)[:, None]).astype(F32)
    rep_c = (col[None, :] % S5_PB == jnp.arange(S5_PB)[:, None]).astype(F32)
    lanes_c = lambda a: jnp.einsum('jrq,qc->jrc', a, rep_c, precision=HIGHEST)
    lanes_t = lambda p: jnp.einsum('jtr,tc->jrc', pair_pow(p), rep_t, precision=HIGHEST)
    ccr_l, cci_l = lanes_c(ccr), lanes_c(cci)

    def out_tables(p_r, p_i):
        pr_l, pi_l = lanes_t(p_r), lanes_t(p_i)
        return ccr_l * pr_l - cci_l * pi_l, -(ccr_l * pi_l + cci_l * pr_l)

    fr_pair, fi_pair = out_tables(pr[1:], pi[1:])
    fk_re, fk_im = out_tables(pr[:ln], pi[:ln])
    krow = jnp.einsum('jrk,jkc->jrc', jnp.concatenate([bbr, bbi], axis=2), jnp.concatenate([fk_re, fk_im], axis=1),
                      precision=HIGHEST)
    m = _s5_toeplitz(krow)
    rows_of = lambda a: jnp.broadcast_to(a[:, :, None, :], (S5_PAIRS, a.shape[1], S5_PB, a.shape[2])).reshape(
        S5_PAIRS, a.shape[1] * S5_PB, a.shape[2])
    pe_r = rows_of(pair_pow(pr[ln - 1 - jnp.arange(ln)]))
    pe_i = rows_of(pair_pow(pi[ln - 1 - jnp.arange(ln)]))
    tile_s = lambda a: jnp.broadcast_to(a[:, None], (S5_PAIRS, ln) + a.shape[1:]).reshape(S5_PAIRS, S5_ROWS, a.shape[2])
    bbr_s, bbi_s = tile_s(bbr), tile_s(bbi)
    e_pair = jnp.concatenate([pe_r * bbr_s - pe_i * bbi_s, pe_r * bbi_s + pe_i * bbr_s], axis=2)
    al = jnp.zeros((S5_PAIRS, SUBLANES, LANES), F32)
    al = al.at[:, 0, :].set(pr[ln].reshape(S5_PAIRS, 2 * S5_STATE))
    al = al.at[:, 1, :].set(pi[ln].reshape(S5_PAIRS, 2 * S5_STATE))
    return m.astype(BF16), e_pair.astype(BF16), fr_pair.astype(BF16), fi_pair.astype(BF16), al


def _s5_kernel(u_ref, m_ref, e_ref, fr_ref, fi_ref, al_ref, o_ref, ustage, ystage, sloc, xpr, xpi, *, nc):
    jj = pl.program_id(1)

    @pl.when(jj == 0)
    def _():
        for b in range(S5_ROWS // LANES):
            slabs = [u_ref[pl.ds(S5_SLAB * b + a, nc, stride=S5_CHUNK), :] for a in range(S5_SLAB)]
            for q in range(S5_SLAB):
                ustage[q, :, LANES * b:LANES * (b + 1)] = jnp.concatenate(
                    [sl[:, S5_PB * q:S5_PB * (q + 1)] for sl in slabs], axis=1).astype(BF16)

    u = ustage[jj]
    sloc[...] = jnp.dot(u, e_ref[0], preferred_element_type=F32)
    ar = al_ref[0, 0:1, :]
    ai = al_ref[0, 1:2, :]

    row = lax.broadcasted_iota(jnp.int32, (SUBLANES, LANES), 0)

    def body(ti, carry):
        xr, xi = carry
        r0 = pl.multiple_of(ti * SUBLANES, SUBLANES)
        sr_t = sloc[pl.ds(r0, SUBLANES), 0:LANES]
        si_t = sloc[pl.ds(r0, SUBLANES), LANES:2 * LANES]
        pr_t = jnp.zeros((SUBLANES, LANES), F32)
        pi_t = jnp.zeros((SUBLANES, LANES), F32)
        for r in range(SUBLANES):
            pr_t = jnp.where(row == r, xr, pr_t)
            pi_t = jnp.where(row == r, xi, pi_t)
            xr, xi = ar * xr - ai * xi + sr_t[r:r + 1, :], ar * xi + ai * xr + si_t[r:r + 1, :]
        xpr[pl.ds(r0, SUBLANES), :] = pr_t
        xpi[pl.ds(r0, SUBLANES), :] = pi_t
        return xr, xi

    zero = jnp.zeros((1, LANES), F32)
    lax.fori_loop(0, nc // SUBLANES, body, (zero, zero))
    ystage[jj] = (jnp.dot(xpr[...].astype(BF16), fr_ref[0], preferred_element_type=F32)
                  + jnp.dot(xpi[...].astype(BF16), fi_ref[0], preferred_element_type=F32)
                  + jnp.dot(u, m_ref[0], preferred_element_type=F32))

    @pl.when(jj == S5_SLAB - 1)
    def _():
        for t in range(S5_CHUNK):
            o_ref[pl.ds(t, nc, stride=S5_CHUNK), :] = jnp.concatenate(
                [ystage[q, :, S5_PB * t:S5_PB * (t + 1)] for q in range(S5_SLAB)], axis=1)


def _s5(z2, lam_re, lam_im, log_dt, b_re, b_im, c_re, c_im):
    t = z2.shape[0]
    nc = t // S5_CHUNK
    m, e_pair, fr_pair, fi_pair, al = _s5_tables(lam_re, lam_im, log_dt, b_re, b_im, c_re, c_im)
    pair = lambda a, b: pl.BlockSpec((1, a, b), lambda k, jj: (S5_SLAB * k + jj, 0, 0))
    return pl.pallas_call(
        functools.partial(_s5_kernel, nc=nc),
        grid=(WIDTH // LANES, S5_SLAB),
        in_specs=[pl.BlockSpec((t, LANES), lambda k, jj: (0, Z2_US * WIDTH // LANES + k), pipeline_mode=pl.Buffered(1)),
                  pair(S5_ROWS, S5_ROWS), pair(S5_ROWS, 2 * LANES), pair(LANES, S5_ROWS), pair(LANES, S5_ROWS),
                  pair(SUBLANES, LANES)],
        out_specs=pl.BlockSpec((t, LANES), lambda k, jj: (0, k), pipeline_mode=pl.Buffered(1)),
        out_shape=jax.ShapeDtypeStruct((t, WIDTH), F32),
        scratch_shapes=[pltpu.VMEM((S5_SLAB, nc, S5_ROWS), BF16), pltpu.VMEM((S5_SLAB, nc, S5_ROWS), F32),
                        pltpu.VMEM((nc, 2 * LANES), F32), pltpu.VMEM((nc, LANES), F32), pltpu.VMEM((nc, LANES), F32)],
        compiler_params=_cparams(("arbitrary", "arbitrary")),
        name="s5_scan",
    )(z2, m, e_pair, fr_pair, fi_pair, al)


def _layer_norm(x, g, b):
    mu = jnp.mean(x, axis=-1, keepdims=True)
    xc = x - mu
    var = jnp.mean(xc * xc, axis=-1, keepdims=True)
    return xc * lax.rsqrt(var + LN_EPS) * g + b


def _merge_rows(sl, x_ref, g0_ref, g1_ref, g2_ref, ym_ref, ys_ref, us_ref, yh_ref, wm_ref, ws_ref, wh_ref, wglu_ref,
                d_ref, wo_ref, lg_ref, lb_ref, rw_ref, rb_ref, o_ref):
    ys = _gelu_tanh(ys_ref[sl, :] + d_ref[...] * us_ref[sl, :].astype(F32))
    ys = ys * _sigmoid(_dot(ys, wglu_ref[...]))
    merged = (_sigmoid(g0_ref[sl, :].astype(F32)) * _dot(ym_ref[sl, :], wm_ref[...])
              + _sigmoid(g1_ref[sl, :].astype(F32)) * _dot(ys, ws_ref[...])
              + _sigmoid(g2_ref[sl, :].astype(F32)) * _dot(yh_ref[sl, :], wh_ref[...]))
    x1 = _layer_norm(DN_ALPHA * x_ref[sl, :] + _dot(merged, wo_ref[...]), lg_ref[...], lb_ref[...])
    o_ref[sl, 0:D_MODEL] = x1

    xh = x1.astype(BF16)
    xl = (x1 - xh.astype(F32)).astype(BF16)
    logits = (jnp.dot(xh, rw_ref[0], preferred_element_type=F32) + jnp.dot(xl, rw_ref[0], preferred_element_type=F32)
              + jnp.dot(xh, rw_ref[1], preferred_element_type=F32))
    s0 = _sigmoid(logits)
    sb0 = s0 + rb_ref[...]
    shift = lambda a, j: a if j == 0 else pltpu.roll(a, LANES - N_GROUPS * j, 1)
    s = [shift(s0, j) for j in range(EXPERTS_PER_GROUP)]
    sb = [shift(sb0, j) for j in range(EXPERTS_PER_GROUP)]
    hi1, lo1 = jnp.maximum(sb[0], sb[1]), jnp.minimum(sb[0], sb[1])
    hi2, lo2 = jnp.maximum(sb[2], sb[3]), jnp.minimum(sb[2], sb[3])
    top2 = jnp.maximum(hi1, hi2) + jnp.maximum(jnp.minimum(hi1, hi2), jnp.maximum(lo1, lo2))
    lane = lax.broadcasted_iota(jnp.int32, top2.shape, 1)
    top2 = jnp.where(lane < N_GROUPS, top2, -jnp.inf)
    gmax = jnp.max(top2, axis=1, keepdims=True)
    g_idx = jnp.min(jnp.where(top2 == gmax, lane, LANES), axis=1, keepdims=True)
    sel = lane == g_idx
    v = [jnp.sum(jnp.where(sel, sb[j], 0.0), axis=1, keepdims=True) for j in range(EXPERTS_PER_GROUP)]
    sv = [jnp.sum(jnp.where(sel, s[j], 0.0), axis=1, keepdims=True) for j in range(EXPERTS_PER_GROUP)]

    def first_max(vals):
        m = jnp.maximum(jnp.maximum(vals[0], vals[1]), jnp.maximum(vals[2], vals[3]))
        return jnp.where(vals[0] == m, 0, jnp.where(vals[1] == m, 1, jnp.where(vals[2] == m, 2, 3)))

    e1 = first_max(v)
    e2 = first_max([jnp.where(e1 == j, -jnp.inf, v[j]) for j in range(EXPERTS_PER_GROUP)])
    s1 = sum(jnp.where(e1 == j, sv[j], 0.0) for j in range(EXPERTS_PER_GROUP))
    s2 = sum(jnp.where(e2 == j, sv[j], 0.0) for j in range(EXPERTS_PER_GROUP))
    tot = s1 + s2
    meta = jnp.where(lane == 0, g_idx.astype(F32), 0.0)
    for j in range(EXPERTS_PER_GROUP):
        cw = jnp.where(e1 == j, s1 / tot, 0.0) + jnp.where(e2 == j, s2 / tot, 0.0)
        meta = jnp.where(lane == 1 + j, cw, meta)
    o_ref[sl, D_MODEL:D_MODEL + META] = meta


MERGE_ROWS = 256


def _merge_kernel(*refs):
    for lo in range(0, refs[0].shape[0], MERGE_ROWS):
        _merge_rows(slice(lo, lo + MERGE_ROWS), *refs)


def _merge(x, z1, z2, ym, ys, yh, wm, ws, wh, wglu, d, wo, lg, lb, router_w, router_bias, tm):
    t = x.shape[0]
    rw = router_w.astype(F32).reshape(D_MODEL, N_GROUPS, EXPERTS_PER_GROUP).transpose(0, 2, 1)
    rw = jnp.pad(rw.reshape(D_MODEL, N_EXPERTS), ((0, 0), (0, LANES - N_EXPERTS)))
    rw_hi = rw.astype(BF16)
    rw = jnp.stack([rw_hi, (rw - rw_hi.astype(F32)).astype(BF16)])
    rb = jnp.pad(router_bias.astype(F32).reshape(N_GROUPS, EXPERTS_PER_GROUP).T.reshape(1, N_EXPERTS),
                 ((0, 0), (0, LANES - N_EXPERTS)))
    full = lambda shape: pl.BlockSpec(shape, lambda i: (0,) * len(shape))
    row = lambda w: pl.BlockSpec((tm, w), lambda i: (i, 0))
    return pl.pallas_call(
        _merge_kernel,
        grid=(t // tm,),
        in_specs=[row(D_MODEL),
                  pl.BlockSpec((tm, D_MODEL), lambda i: (i, 0)),
                  pl.BlockSpec((tm, D_MODEL), lambda i: (i, 1)),
                  pl.BlockSpec((tm, D_MODEL), lambda i: (i, 2)),
                  row(WIDTH), row(WIDTH), pl.BlockSpec((tm, WIDTH), lambda i: (i, Z2_US)), row(WIDTH),
                  full((WIDTH, D_MODEL)), full((WIDTH, D_MODEL)), full((WIDTH, D_MODEL)), full((WIDTH, WIDTH)),
                  full((1, WIDTH)), full((D_MODEL, D_MODEL)), full((1, D_MODEL)), full((1, D_MODEL)),
                  full((2, D_MODEL, LANES)), full((1, LANES))],
        out_specs=pl.BlockSpec((tm, D_MODEL + META), lambda i: (i, 0)),
        out_shape=jax.ShapeDtypeStruct((t, D_MODEL + META), F32),
        compiler_params=_cparams(("parallel",)),
        name="merge_router",
    )(x, z1, z1, z1, ym, ys, z2, yh, wm.astype(BF16), ws.astype(BF16), wh.astype(BF16), wglu.astype(BF16),
      d.reshape(1, WIDTH).astype(F32), wo.astype(BF16), lg.reshape(1, D_MODEL).astype(F32),
      lb.reshape(1, D_MODEL).astype(F32), rw, rb)


def _start_row_gather(idx_ref, base, src_hbm, dst, sem, rows):
    for r in range(rows):
        pltpu.make_async_copy(src_hbm.at[pl.ds(idx_ref[base + r], 1)], dst.at[pl.ds(r, 1)], sem).start(priority=r % 2)


def _wait_row_gather(src_hbm, dst, sem, rows):
    pltpu.make_async_copy(src_hbm.at[pl.ds(0, rows)], dst, sem).wait()


def _dispatch_kernel(dst_ref, x_ref, init_hbm, o_hbm, sem, *, tm):
    del init_hbm
    base = pl.program_id(0) * tm
    for r in range(tm):
        pltpu.make_async_copy(x_ref.at[pl.ds(r, 1)], o_hbm.at[pl.ds(dst_ref[base + r], 1)], sem).start(priority=r % 2)
    pltpu.make_async_copy(x_ref, o_hbm.at[pl.ds(0, tm)], sem).wait()


def _moe_kernel(tg_ref, x_ref, wg_ref, wu_ref, wd_ref, o_ref, wg_s, wu_s, wd_s, *, tm):
    i = pl.program_id(0)

    @pl.when(jnp.logical_or(i == 0, tg_ref[i] != tg_ref[jnp.maximum(i - 1, 0)]))
    def _():
        for e in range(EXPERTS_PER_GROUP):
            wg_s[e] = wg_ref[0, e].astype(BF16)
            wu_s[e] = wu_ref[0, e].astype(BF16)
            wd_s[e] = wd_ref[0, e].astype(BF16)

    xb = x_ref[:, 0:D_MODEL].astype(BF16)
    meta = x_ref[:, D_MODEL:D_MODEL + META]
    y = jnp.zeros((tm, D_MODEL), F32)
    for e in range(EXPERTS_PER_GROUP):
        hg = jnp.dot(xb, wg_s[e], preferred_element_type=F32)
        hu = jnp.dot(xb, wu_s[e], preferred_element_type=F32)
        hh = _silu(hg) * hu * _lane_col(meta, 1 + e)
        y = y + jnp.dot(hh.astype(BF16), wd_s[e], preferred_element_type=F32)
    o_ref[...] = y


def _ln2_kernel(dst_ref, x_ref, y_hbm, lg_ref, lb_ref, o_ref, buf, sem, *, tm, ntiles):
    i = pl.program_id(0)
    slot = i % 2

    @pl.when(i == 0)
    def _():
        _start_row_gather(dst_ref, 0, y_hbm, buf.at[0], sem.at[0], tm)

    @pl.when(i + 1 < ntiles)
    def _():
        _start_row_gather(dst_ref, (i + 1) * tm, y_hbm, buf.at[1 - slot], sem.at[1 - slot], tm)

    _wait_row_gather(y_hbm, buf.at[slot], sem.at[slot], tm)
    o_ref[...] = _layer_norm(DN_ALPHA * x_ref[...] + buf[slot], lg_ref[...], lb_ref[...])


def _moe_ln(x1e, wg, wu, wd, layer, lg, lb, tm):
    t = x1e.shape[0]
    ntiles = t // tm + N_GROUPS
    p = ntiles * tm
    key = x1e[:, D_MODEL].astype(jnp.int32)
    onehot = (key[:, None] == jnp.arange(N_GROUPS)[None, :]).astype(jnp.int32)
    csum = jnp.cumsum(onehot, axis=0)
    counts = csum[-1]
    rank = jnp.sum(onehot * csum, axis=1) - 1
    pcount = ((counts + tm - 1) // tm) * tm
    pend = jnp.cumsum(pcount)
    dest = (pend - pcount)[key] + rank
    dest = dest.astype(jnp.int32)
    tile_start = jnp.arange(ntiles, dtype=jnp.int32) * tm
    tile_group = jnp.minimum(jnp.sum((tile_start[:, None] >= pend[None, :]).astype(jnp.int32), axis=1),
                             N_GROUPS - 1)
    nt2 = t // tm
    x_sorted = pl.pallas_call(
        functools.partial(_dispatch_kernel, tm=tm),
        grid_spec=pltpu.PrefetchScalarGridSpec(
            num_scalar_prefetch=1,
            grid=(nt2,),
            in_specs=[pl.BlockSpec((tm, D_MODEL + META), lambda i, d: (i, 0)), pl.BlockSpec(memory_space=pl.ANY)],
            out_specs=pl.BlockSpec(memory_space=pl.ANY),
            scratch_shapes=[pltpu.SemaphoreType.DMA(())]),
        out_shape=jax.ShapeDtypeStruct((p, D_MODEL + META), F32),
        input_output_aliases={2: 0},
        compiler_params=_cparams(("arbitrary",)),
        name="moe_dispatch",
    )(dest, x1e, jnp.zeros((p, D_MODEL + META), F32))
    wspec = lambda a, b: pl.BlockSpec((1, EXPERTS_PER_GROUP, a, b), lambda i, tg: (layer, tg[i], 0, 0))
    wscr = lambda a, b: pltpu.VMEM((EXPERTS_PER_GROUP, a, b), BF16)
    y_sorted = pl.pallas_call(
        functools.partial(_moe_kernel, tm=tm),
        grid_spec=pltpu.PrefetchScalarGridSpec(
            num_scalar_prefetch=1,
            grid=(ntiles,),
            in_specs=[pl.BlockSpec((tm, D_MODEL + META), lambda i, tg: (i, 0)), wspec(D_MODEL, D_EXPERT),
                      wspec(D_MODEL, D_EXPERT), wspec(D_EXPERT, D_MODEL)],
            out_specs=pl.BlockSpec((tm, D_MODEL), lambda i, tg: (i, 0)),
            scratch_shapes=[wscr(D_MODEL, D_EXPERT), wscr(D_MODEL, D_EXPERT), wscr(D_EXPERT, D_MODEL)]),
        out_shape=jax.ShapeDtypeStruct((p, D_MODEL), F32),
        compiler_params=_cparams(("arbitrary",)),
        name="moe_experts",
    )(tile_group, x_sorted, wg, wu, wd)
    return pl.pallas_call(
        functools.partial(_ln2_kernel, tm=tm, ntiles=nt2),
        grid_spec=pltpu.PrefetchScalarGridSpec(
            num_scalar_prefetch=1,
            grid=(nt2,),
            in_specs=[pl.BlockSpec((tm, D_MODEL), lambda i, d: (i, 0)), pl.BlockSpec(memory_space=pl.ANY),
                      pl.BlockSpec((1, D_MODEL), lambda i, d: (0, 0)), pl.BlockSpec((1, D_MODEL), lambda i, d: (0, 0))],
            out_specs=pl.BlockSpec((tm, D_MODEL), lambda i, d: (i, 0)),
            scratch_shapes=[pltpu.VMEM((2, tm, D_MODEL), F32), pltpu.SemaphoreType.DMA((2,))]),
        out_shape=jax.ShapeDtypeStruct((t, D_MODEL), F32),
        compiler_params=_cparams(("arbitrary",)),
        name="moe_combine_ln",
    )(dest.astype(jnp.int32), x1e, y_sorted, lg.reshape(1, D_MODEL).astype(F32), lb.reshape(1, D_MODEL).astype(F32))


def _layer(x, l, p, lb_l, cfg):
    t = x.shape[0]
    w1, w2, wt = _split_w_in(p['w_in'][l])
    xb = x.astype(BF16)
    z1 = _matmul(xb, w1, Z1_DTYPE, cfg['tm_in'], cfg['tn_in'])
    z2, gt = _inproj_gate(xb, w2, wt, cfg['tm_in'])
    gt3 = gt.reshape(SUBLANES, t // CHUNK, CHUNK).transpose(1, 0, 2)
    ym = _mlstm(z1, z2, gt3, p['mlstm_conv_w'][l], p['mlstm_conv_b'][l], p['mlstm_wq'][l], p['mlstm_wk'][l],
                p['mlstm_wv'][l], p['mlstm_b_i'][l], p['mlstm_b_f'][l], p['mlstm_skip'][l], cfg['tb'])
    ys = _s5(z2, p['s5_lambda_re'][l], p['s5_lambda_im'][l], p['s5_log_dt'][l], p['s5_b_re'][l], p['s5_b_im'][l],
             p['s5_c_re'][l], p['s5_c_im'][l])
    yh = _hgrn(z1, z2, lb_l, p['hgrn_norm_w'][l], cfg['tb'])
    x1e = _merge(x, z1, z2, ym, ys, yh, p['w_branch_mlstm'][l], p['w_branch_s5'][l], p['w_branch_hgrn'][l],
                 p['s5_w_glu'][l], p['s5_d'][l], p['w_out'][l], p['ln1_g'][l], p['ln1_b'][l], p['router_w'],
                 p['router_bias'], cfg['tm_merge'])
    return _moe_ln(x1e, p['exp_w_gate'], p['exp_w_up'], p['exp_w_down'], l, p['ln2_g'][l], p['ln2_b'][l],
                   cfg['tm_moe'])


_CFG = dict(tm_in=1024, tn_in=Z1_COLS // 4, tb=512, tm_merge=2 * MERGE_ROWS, tm_moe=256)


def kernel(x, w_in, mlstm_conv_w, mlstm_conv_b, mlstm_wq, mlstm_wk, mlstm_wv, mlstm_b_i, mlstm_b_f, mlstm_skip, s5_lambda_re, s5_lambda_im, s5_log_dt, s5_b_re, s5_b_im, s5_c_re, s5_c_im, s5_d, s5_w_glu, hgrn_lower_bounds, hgrn_norm_w, w_branch_mlstm, w_branch_s5, w_branch_hgrn, w_out, ln1_g, ln1_b, ln2_g, ln2_b, router_w, router_bias, exp_w_gate, exp_w_up, exp_w_down):
    p = dict(w_in=w_in, mlstm_conv_w=mlstm_conv_w, mlstm_conv_b=mlstm_conv_b, mlstm_wq=mlstm_wq, mlstm_wk=mlstm_wk,
             mlstm_wv=mlstm_wv, mlstm_b_i=mlstm_b_i, mlstm_b_f=mlstm_b_f, mlstm_skip=mlstm_skip,
             s5_lambda_re=s5_lambda_re, s5_lambda_im=s5_lambda_im, s5_log_dt=s5_log_dt, s5_b_re=s5_b_re,
             s5_b_im=s5_b_im, s5_c_re=s5_c_re, s5_c_im=s5_c_im, s5_d=s5_d, s5_w_glu=s5_w_glu,
             hgrn_norm_w=hgrn_norm_w, w_branch_mlstm=w_branch_mlstm, w_branch_s5=w_branch_s5,
             w_branch_hgrn=w_branch_hgrn, w_out=w_out, ln1_g=ln1_g, ln1_b=ln1_b, ln2_g=ln2_g, ln2_b=ln2_b,
             router_w=router_w, router_bias=router_bias, exp_w_gate=exp_w_gate, exp_w_up=exp_w_up,
             exp_w_down=exp_w_down)
    lb_cum = jnp.cumsum(jax.nn.softmax(hgrn_lower_bounds.astype(F32), axis=0), axis=0)
    lb_layers = lb_cum - lb_cum[0]
    bsz, seq, d = x.shape
    h = x.reshape(bsz * seq, d)
    for l in range(DEPTH):
        h = _layer(h, l, p, lb_layers[l], _CFG)
    return h.reshape(bsz, seq, d)
```

```python
import functools
import math

import jax
import jax.numpy as jnp
from jax import lax
from jax.experimental import pallas as pl
from jax.experimental.pallas import tpu as pltpu

F32 = jnp.float32
BF16 = jnp.bfloat16
HIGHEST = lax.Precision.HIGHEST

D_MODEL = 1024
DEPTH = 2
HEADS = 4
HEAD_DIM = 128
WIDTH = HEADS * HEAD_DIM
MLSTM_CONV = 4
CHUNK = 64
S5_GROUP = 16
S5_GROUPS = 32
S5_STATE = 64
S5_PAIRS = S5_GROUPS // 2
S5_CHUNK = 32
S5_MAX_REAL = -1e-4
N_EXPERTS = 32
N_GROUPS = 8
EXPERTS_PER_GROUP = 4
D_EXPERT = 256
DN_ALPHA = (2 * DEPTH) ** 0.25
LN_EPS = 1e-5
NORM_EPS = 1e-6

LANES = 128
SUBLANES = 8
META = LANES
VMEM_LIMIT = 56 * 1024 * 1024

Z1_DTYPE = BF16
Z1_GATE, Z1_XM, Z1_OM, Z1_QH, Z1_IH, Z1_GH = 0, 6, 7, 8, 9, 10
Z1_COLS = 11 * WIDTH
Z2_FH, Z2_US = 0, 1
Z2_COLS = 2 * WIDTH + LANES


def _cparams(sem):
    return pltpu.CompilerParams(dimension_semantics=sem, vmem_limit_bytes=VMEM_LIMIT)


def _sigmoid(x):
    return 0.5 * (1.0 + jnp.tanh(0.5 * x))


def _silu(x):
    return x * _sigmoid(x)


def _log_sigmoid(x):
    return jnp.minimum(x, 0.0) - jnp.log(1.0 + jnp.exp(-jnp.abs(x)))


def _gelu_tanh(x):
    return 0.5 * x * (1.0 + jnp.tanh(math.sqrt(2.0 / math.pi) * (x + 0.044715 * (x * x * x))))


def _cumsum_rows(x):
    n = x.shape[0]
    row = lax.broadcasted_iota(jnp.int32, x.shape, 0)
    s = 1
    while s < n:
        x = x + jnp.where(row >= s, pltpu.roll(x, s, 0), 0.0)
        s *= 2
    return x


def _lane_col(x, idx):
    lane = lax.broadcasted_iota(jnp.int32, x.shape, 1)
    return jnp.sum(jnp.where(lane == idx, x, 0.0), axis=1, keepdims=True)


def _dot(a, b):
    return jnp.dot(a.astype(BF16), b.astype(BF16), preferred_element_type=F32)


def _dot_nt(a, b):
    return lax.dot_general(a.astype(BF16), b.astype(BF16), (((1,), (1,)), ((), ())), preferred_element_type=F32)


def _dot_tn(a, b):
    return lax.dot_general(a.astype(BF16), b.astype(BF16), (((0,), (0,)), ((), ())), preferred_element_type=F32)


def _mm_kernel(x_ref, w_ref, o_ref):
    o_ref[...] = jnp.dot(x_ref[...], w_ref[...], preferred_element_type=F32).astype(o_ref.dtype)


def _matmul(x, w, out_dtype, tm, tn):
    m, k = x.shape
    n = w.shape[1]
    return pl.pallas_call(
        _mm_kernel,
        grid=(m // tm, n // tn),
        in_specs=[pl.BlockSpec((tm, k), lambda i, j: (i, 0)), pl.BlockSpec((k, tn), lambda i, j: (0, j))],
        out_specs=pl.BlockSpec((tm, tn), lambda i, j: (i, j)),
        out_shape=jax.ShapeDtypeStruct((m, n), out_dtype),
        compiler_params=_cparams(("parallel", "parallel")),
        name="in_proj_wide",
    )(x, w)


def _inproj_gate_kernel(x_ref, w_ref, wt_ref, o_ref, gt_ref):
    x = x_ref[...]
    o_ref[...] = jnp.dot(x, w_ref[...], preferred_element_type=F32)
    gt_ref[...] = lax.dot_general(wt_ref[...], x, (((1,), (1,)), ((), ())), preferred_element_type=F32)


def _inproj_gate(x, w2, wt, tm):
    m, k = x.shape
    return pl.pallas_call(
        _inproj_gate_kernel,
        grid=(m // tm,),
        in_specs=[pl.BlockSpec((tm, k), lambda i: (i, 0)),
                  pl.BlockSpec((k, Z2_COLS), lambda i: (0, 0)),
                  pl.BlockSpec((SUBLANES, k), lambda i: (0, 0))],
        out_specs=[pl.BlockSpec((tm, Z2_COLS), lambda i: (i, 0)), pl.BlockSpec((SUBLANES, tm), lambda i: (0, i))],
        out_shape=[jax.ShapeDtypeStruct((m, Z2_COLS), F32), jax.ShapeDtypeStruct((SUBLANES, m), F32)],
        compiler_params=_cparams(("parallel",)),
        name="in_proj_gates",
    )(x, w2, wt)


def _split_w_in(w):
    offs, o = [], 0
    for s in (WIDTH, WIDTH, HEADS, HEADS, WIDTH, WIDTH, WIDTH, WIDTH, WIDTH, 3 * D_MODEL):
        offs.append((o, o + s))
        o += s
    seg = [w[:, a:b] for a, b in offs]
    xm, om, im, fm, us, qh, fh, ih, gh, gate = seg
    w1 = jnp.concatenate([gate, xm, om, qh, ih, gh], axis=1).astype(BF16)
    pad = jnp.zeros((w.shape[0], LANES - 2 * HEADS), w.dtype)
    w2 = jnp.concatenate([fh, us, im, fm, pad], axis=1).astype(BF16)
    wt = jnp.concatenate([im, fm], axis=1).T.astype(BF16)
    return w1, w2, wt


def _mlstm_kernel(xm_ref, om_ref, gc_ref, gr_ref, cw_ref, cb_ref, wq_ref, wk_ref, wv_ref, bcol_ref, brow_ref,
                  skip_ref, triu_ref, o_ref, xpad, q_s, k_s, v_s, xc_s, h_s, gcol_s, grow_s, brow_s, ct_s, n_s, m_s,
                  *, tb):
    ncb = tb // CHUNK

    @pl.when(pl.program_id(0) == 0)
    def _():
        xpad[0:SUBLANES, :] = jnp.zeros((SUBLANES, WIDTH), F32)
        ct_s[...] = jnp.zeros_like(ct_s)
        n_s[...] = jnp.zeros_like(n_s)
        m_s[...] = jnp.zeros_like(m_s)

    xm = xm_ref[...].astype(F32)
    xpad[SUBLANES:SUBLANES + tb, :] = xm
    cw = cw_ref[...]
    conv = cb_ref[...] + cw[3:4, :] * xm
    for d in range(1, MLSTM_CONV):
        conv = conv + cw[3 - d:4 - d, :] * xpad[SUBLANES - d:SUBLANES - d + tb, :]
    xpad[0:SUBLANES, :] = xpad[tb:tb + SUBLANES, :]
    xc = _silu(conv)
    xc_s[...] = xc
    for h in range(HEADS):
        sl = slice(h * HEAD_DIM, (h + 1) * HEAD_DIM)
        xch = xc[:, sl].astype(BF16)
        q_s[:, sl] = jnp.dot(xch, wq_ref[h], preferred_element_type=F32) * (HEAD_DIM ** -0.5)
        k_s[:, sl] = jnp.dot(xch, wk_ref[h], preferred_element_type=F32)
        v_s[:, sl] = jnp.dot(xm[:, sl].astype(BF16), wv_ref[h], preferred_element_type=F32)

    gc = gc_ref[...] + bcol_ref[...]
    lane = lax.broadcasted_iota(jnp.int32, gc.shape, 1)
    gcol_s[...] = jnp.where(lane < HEADS, gc, _log_sigmoid(gc))
    gr = gr_ref[...] + brow_ref[...]
    sub = lax.broadcasted_iota(jnp.int32, gr.shape, 1)
    gr = jnp.where(sub < HEADS, gr, _log_sigmoid(gr)).reshape(ncb * SUBLANES, CHUNK)
    grow_s[...] = gr
    brow_s[...] = jnp.dot(gr, triu_ref[...], precision=HIGHEST, preferred_element_type=F32)

    rowi = lax.broadcasted_iota(jnp.int32, (CHUNK, CHUNK), 0)
    coli = lax.broadcasted_iota(jnp.int32, (CHUNK, CHUNK), 1)
    causal = rowi >= coli

    def chunk_body(c, carry):
        r0 = pl.multiple_of(c * CHUNK, CHUNK)
        g0 = pl.multiple_of(c * SUBLANES, SUBLANES)
        gcol = gcol_s[pl.ds(r0, CHUNK), :]
        bcol = _cumsum_rows(gcol)
        grow = grow_s[pl.ds(g0, SUBLANES), :]
        brow = brow_s[pl.ds(g0, SUBLANES), :]
        for h in range(HEADS):
            sl = slice(h * HEAD_DIM, (h + 1) * HEAD_DIM)
            li_c = _lane_col(gcol, h)
            b_c = _lane_col(bcol, HEADS + h)
            li_r = grow[h:h + 1, :]
            b_r = brow[HEADS + h:HEADS + h + 1, :]
            b_tot = b_c[CHUNK - 1:CHUNK, :]
            a_c = b_tot - b_c + li_c
            a_max = jnp.max(a_c, axis=0, keepdims=True)
            q = q_s[pl.ds(r0, CHUNK), sl]
            k = k_s[pl.ds(r0, CHUNK), sl]
            v = v_s[pl.ds(r0, CHUNK), sl]
            wk = jnp.exp(a_c - a_max) * k
            c_loc_t = _dot_tn(v, wk)
            n_loc = jnp.sum(wk, axis=0, keepdims=True)
            ct_prev = ct_s[h]
            n_prev = n_s[h:h + 1, :]
            m_prev = m_s[h:h + 1, 0:1]
            d_mat = jnp.where(causal, b_c - b_r + li_r, -jnp.inf)
            m_inter = b_c + m_prev
            m_j = jnp.maximum(m_inter, jnp.max(d_mat, axis=1, keepdims=True))
            sc = _dot_nt(q, k) * jnp.exp(d_mat - m_j)
            g_inter = jnp.exp(m_inter - m_j)
            num = g_inter * _dot_nt(q, ct_prev) + _dot(sc, v)
            den = g_inter * jnp.sum(q * n_prev, axis=1, keepdims=True) + jnp.sum(sc, axis=1, keepdims=True)
            h_s[pl.ds(r0, CHUNK), sl] = num / jnp.maximum(jnp.abs(den), jnp.exp(-m_j))
            m_new = jnp.maximum(b_tot + m_prev, a_max)
            g_old = jnp.exp(b_tot + m_prev - m_new)
            g_loc = jnp.exp(a_max - m_new)
            ct_s[h] = g_old * ct_prev + g_loc * c_loc_t
            n_s[h:h + 1, :] = g_old * n_prev + g_loc * n_loc
            m_s[h:h + 1, :] = jnp.broadcast_to(m_new, (1, LANES))
        return carry

    lax.fori_loop(0, ncb, chunk_body, 0, unroll=4)
    o_ref[...] = (_sigmoid(om_ref[...].astype(F32)) * h_s[...] + skip_ref[...] * xc_s[...]).astype(o_ref.dtype)


def _mlstm(z1, z2, gt3, conv_w, conv_b, wq, wk, wv, b_i, b_f, skip, tb):
    t = z1.shape[0]
    ncb = tb // CHUNK
    bias = jnp.concatenate([b_i, b_f]).astype(F32)
    bcol = jnp.zeros((1, LANES), F32).at[0, :2 * HEADS].set(bias)
    brow = bias.reshape(2 * HEADS, 1)
    triu = jnp.triu(jnp.ones((CHUNK, CHUNK), F32))
    full = lambda shape: pl.BlockSpec(shape, lambda i: (0,) * len(shape))
    return pl.pallas_call(
        functools.partial(_mlstm_kernel, tb=tb),
        grid=(t // tb,),
        in_specs=[pl.BlockSpec((tb, WIDTH), lambda i: (i, Z1_XM)),
                  pl.BlockSpec((tb, WIDTH), lambda i: (i, Z1_OM)),
                  pl.BlockSpec((tb, LANES), lambda i: (i, 2 * WIDTH // LANES)),
                  pl.BlockSpec((ncb, SUBLANES, CHUNK), lambda i: (i, 0, 0)),
                  full((MLSTM_CONV, WIDTH)), full((1, WIDTH)),
                  full((HEADS, HEAD_DIM, HEAD_DIM)), full((HEADS, HEAD_DIM, HEAD_DIM)),
                  full((HEADS, HEAD_DIM, HEAD_DIM)),
                  full((1, LANES)), full((2 * HEADS, 1)), full((1, WIDTH)), full((CHUNK, CHUNK))],
        out_specs=pl.BlockSpec((tb, WIDTH), lambda i: (i, 0)),
        out_shape=jax.ShapeDtypeStruct((t, WIDTH), F32),
        scratch_shapes=[pltpu.VMEM((tb + SUBLANES, WIDTH), F32)]
        + [pltpu.VMEM((tb, WIDTH), F32) for _ in range(5)]
        + [pltpu.VMEM((tb, LANES), F32),
           pltpu.VMEM((ncb * SUBLANES, CHUNK), F32), pltpu.VMEM((ncb * SUBLANES, CHUNK), F32),
           pltpu.VMEM((HEADS, HEAD_DIM, HEAD_DIM), F32), pltpu.VMEM((SUBLANES, LANES), F32),
           pltpu.VMEM((SUBLANES, LANES), F32)],
        compiler_params=_cparams(("arbitrary",)),
        name="mlstm",
    )(z1, z1, z2, gt3, conv_w.astype(F32), conv_b.reshape(1, WIDTH).astype(F32), wq.astype(BF16), wk.astype(BF16),
      wv.astype(BF16), bcol, brow, skip.reshape(1, WIDTH).astype(F32), triu)


def _hgrn_kernel(q_ref, f_ref, i_ref, g_ref, lb_ref, nw_ref, tri_ref, o_ref, st_s, *, tb):
    ncb = tb // CHUNK

    @pl.when(pl.program_id(0) == 0)
    def _():
        st_s[...] = jnp.zeros_like(st_s)

    lb = lb_ref[...]
    nw = nw_ref[...]
    tri = tri_ref[...]
    rowi = lax.broadcasted_iota(jnp.int32, (CHUNK, WIDTH), 0)
    sr = lax.broadcasted_iota(jnp.int32, (CHUNK, CHUNK), 0)
    sc = lax.broadcasted_iota(jnp.int32, (CHUNK, CHUNK), 1)
    halves = [1 << p for p in range(CHUNK.bit_length() - 1)]
    upper = {m: (rowi & m) != 0 for m in halves}
    same_blk = {m: (sr // (2 * m)) == (sc // (2 * m)) for m in halves}

    def chunk_body(c, carry):
        r0 = pl.multiple_of(c * CHUNK, CHUNK)
        f = lb + (1.0 - lb) * _sigmoid(f_ref[pl.ds(r0, CHUNK), :].astype(F32))
        k = 1.0 - f
        q = _silu(q_ref[pl.ds(r0, CHUNK), :].astype(F32))
        v = i_ref[pl.ds(r0, CHUNK), :].astype(F32)
        lf = jnp.log(f)
        hi = lf.astype(BF16)
        r1 = lf - hi.astype(F32)
        mid = r1.astype(BF16)
        lo = (r1 - mid.astype(F32)).astype(BF16)
        b = (jnp.dot(tri, hi, preferred_element_type=F32) + jnp.dot(tri, mid, preferred_element_type=F32)
             + jnp.dot(tri, lo, preferred_element_type=F32))
        qs, ks = {}, {}
        for m in halves:
            if m == 1:
                t = jnp.where(upper[m], q * f, k)
            else:
                nblk = CHUNK // (2 * m)
                r = jnp.concatenate([jnp.broadcast_to(b[2 * m * j + m - 1:2 * m * j + m, :], (2 * m, WIDTH))
                                     for j in range(nblk)], axis=0)
                t = jnp.where(upper[m], q, k) * jnp.exp(-jnp.abs(b - r))
            qs[m] = jnp.where(upper[m], t, 0.0).astype(BF16)
            ks[m] = jnp.where(upper[m], 0.0, t).astype(BF16)
        b_last = b[CHUNK - 1:CHUNK, :]
        qe = q * jnp.exp(b)
        kd = k * jnp.exp(b_last - b)
        e_last = jnp.exp(b_last)
        outs = []
        for h in range(HEADS):
            sl = slice(h * HEAD_DIM, (h + 1) * HEAD_DIM)
            st = st_s[h]
            a = jnp.where(sr == sc, _dot_nt(q[:, sl], k[:, sl]), 0.0)
            for m in halves:
                a = a + jnp.where(same_blk[m], _dot_nt(qs[m][:, sl], ks[m][:, sl]), 0.0)
            oh = _dot_nt(qe[:, sl], st) + _dot(a, v[:, sl])
            st_s[h] = e_last[:, sl] * st + _dot_tn(v[:, sl], kd[:, sl])
            ms = jnp.sum(oh * oh, axis=1, keepdims=True) * (1.0 / HEAD_DIM)
            outs.append(oh * lax.rsqrt(ms + NORM_EPS))
        on = jnp.concatenate(outs, axis=1)
        g = g_ref[pl.ds(r0, CHUNK), :].astype(F32)
        o_ref[pl.ds(r0, CHUNK), :] = (on * nw * _silu(g)).astype(o_ref.dtype)
        return carry

    lax.fori_loop(0, ncb, chunk_body, 0, unroll=4)


def _hgrn(z1, z2, lb, norm_w, tb):
    t = z1.shape[0]
    tril = jnp.tril(jnp.ones((CHUNK, CHUNK), F32)).astype(BF16)
    full = lambda shape: pl.BlockSpec(shape, lambda i: (0,) * len(shape))
    return pl.pallas_call(
        functools.partial(_hgrn_kernel, tb=tb),
        grid=(t // tb,),
        in_specs=[pl.BlockSpec((tb, WIDTH), lambda i: (i, Z1_QH)),
                  pl.BlockSpec((tb, WIDTH), lambda i: (i, 0)),
                  pl.BlockSpec((tb, WIDTH), lambda i: (i, Z1_IH)),
                  pl.BlockSpec((tb, WIDTH), lambda i: (i, Z1_GH)),
                  full((1, WIDTH)), full((1, WIDTH)), full((CHUNK, CHUNK))],
        out_specs=pl.BlockSpec((tb, WIDTH), lambda i: (i, 0)),
        out_shape=jax.ShapeDtypeStruct((t, WIDTH), F32),
        scratch_shapes=[pltpu.VMEM((HEADS, HEAD_DIM, HEAD_DIM), F32)],
        compiler_params=_cparams(("arbitrary",)),
        name="hgrn2",
    )(z1, z2, z1, z1, lb.reshape(1, WIDTH).astype(F32), norm_w.reshape(1, WIDTH).astype(F32), tril)


S5_PB = 2 * S5_GROUP
S5_ROWS = S5_CHUNK * S5_PB
S5_SLAB = LANES // S5_PB


def _s5_toeplitz_kernel(k_ref, m_ref):
    krow = k_ref[0]
    lane = lax.broadcasted_iota(jnp.int32, krow.shape, 1)
    for s in range(S5_CHUNK):
        blk = krow if s == 0 else jnp.where(lane >= s * S5_PB, pltpu.roll(krow, s * S5_PB, 1), 0.0)
        m_ref[0, s * S5_PB:(s + 1) * S5_PB, :] = blk.astype(m_ref.dtype)


def _s5_toeplitz(krow):
    return pl.pallas_call(
        _s5_toeplitz_kernel,
        grid=(S5_PAIRS,),
        in_specs=[pl.BlockSpec((1, S5_PB, S5_ROWS), lambda j: (j, 0, 0))],
        out_specs=pl.BlockSpec((1, S5_ROWS, S5_ROWS), lambda j: (j, 0, 0)),
        out_shape=jax.ShapeDtypeStruct((S5_PAIRS, S5_ROWS, S5_ROWS), BF16),
        compiler_params=_cparams(("parallel",)),
        name="s5_toeplitz",
    )(krow)


def _s5_tables(lam_re, lam_im, log_dt, b_re, b_im, c_re, c_im):
    ln = S5_CHUNK
    lr = jnp.minimum(lam_re.astype(F32), S5_MAX_REAL)
    li = lam_im.astype(F32)
    dt = jnp.exp(log_dt.astype(F32))[:, None]
    mag = jnp.exp(lr * dt)
    ab_re = mag * jnp.cos(li * dt)
    ab_im = mag * jnp.sin(li * dt)
    nr = ab_re - 1.0
    den = lr * lr + li * li
    cr = (nr * lr + ab_im * li) / den
    ci = (ab_im * lr - nr * li) / den
    bb_re = cr[..., None] * b_re - ci[..., None] * b_im
    bb_im = cr[..., None] * b_im + ci[..., None] * b_re
    tau = jnp.arange(ln + 1, dtype=F32)[:, None, None]
    pm = jnp.exp(lr * dt * tau)
    pr = pm * jnp.cos(li * dt * tau)
    pi = pm * jnp.sin(li * dt * tau)
    def block_diag(a):
        a0, a1 = a[0::2], a[1::2]
        z = jnp.zeros_like(a0)
        return jnp.concatenate([jnp.concatenate([a0, z], axis=2), jnp.concatenate([z, a1], axis=2)], axis=1)

    def pair_pow(p):
        return p.reshape(p.shape[0], S5_PAIRS, 2 * S5_STATE).transpose(1, 0, 2)

    bbr = block_diag(bb_re.transpose(0, 2, 1))
    bbi = block_diag(bb_im.transpose(0, 2, 1))
    ccr = block_diag(c_re.transpose(0, 2, 1))
    cci = block_diag(c_im.transpose(0, 2, 1))
    col = jnp.arange(S5_ROWS)
    rep_t = (col[None, :] // S5_PB == jnp.arange(ln)[:, None]).astype(F32)
    rep_c = (col[None, :] % S5_PB == jnp.arange(S5_PB)[:, None]).astype(F32)
    lanes_c = lambda a: jnp.einsum('jrq,qc->jrc', a, rep_c, precision=HIGHEST)
    lanes_t = lambda p: jnp.einsum('jtr,tc->jrc', pair_pow(p), rep_t, precision=HIGHEST)
    ccr_l, cci_l = lanes_c(ccr), lanes_c(cci)

    def out_tables(p_r, p_i):
        pr_l, pi_l = lanes_t(p_r), lanes_t(p_i)
        return ccr_l * pr_l - cci_l * pi_l, -(ccr_l * pi_l + cci_l * pr_l)

    fr_pair, fi_pair = out_tables(pr[1:], pi[1:])
    fk_re, fk_im = out_tables(pr[:ln], pi[:ln])
    krow = jnp.einsum('jrk,jkc->jrc', jnp.concatenate([bbr, bbi], axis=2), jnp.concatenate([fk_re, fk_im], axis=1),
                      precision=HIGHEST)
    m = _s5_toeplitz(krow)
    rows_of = lambda a: jnp.broadcast_to(a[:, :, None, :], (S5_PAIRS, a.shape[1], S5_PB, a.shape[2])).reshape(
        S5_PAIRS, a.shape[1] * S5_PB, a.shape[2])
    pe_r = rows_of(pair_pow(pr[ln - 1 - jnp.arange(ln)]))
    pe_i = rows_of(pair_pow(pi[ln - 1 - jnp.arange(ln)]))
    tile_s = lambda a: jnp.broadcast_to(a[:, None], (S5_PAIRS, ln) + a.shape[1:]).reshape(S5_PAIRS, S5_ROWS, a.shape[2])
    bbr_s, bbi_s = tile_s(bbr), tile_s(bbi)
    e_pair = jnp.concatenate([pe_r * bbr_s - pe_i * bbi_s, pe_r * bbi_s + pe_i * bbr_s], axis=2)
    al = jnp.zeros((S5_PAIRS, SUBLANES, LANES), F32)
    al = al.at[:, 0, :].set(pr[ln].reshape(S5_PAIRS, 2 * S5_STATE))
    al = al.at[:, 1, :].set(pi[ln].reshape(S5_PAIRS, 2 * S5_STATE))
    return m.astype(BF16), e_pair.astype(BF16), fr_pair.astype(BF16), fi_pair.astype(BF16), al


def _s5_kernel(u_ref, m_ref, e_ref, fr_ref, fi_ref, al_ref, o_ref, ustage, ystage, sloc, xpr, xpi, *, nc):
    jj = pl.program_id(1)

    @pl.when(jj == 0)
    def _():
        for b in range(S5_ROWS // LANES):
            slabs = [u_ref[pl.ds(S5_SLAB * b + a, nc, stride=S5_CHUNK), :] for a in range(S5_SLAB)]
            for q in range(S5_SLAB):
                ustage[q, :, LANES * b:LANES * (b + 1)] = jnp.concatenate(
                    [sl[:, S5_PB * q:S5_PB * (q + 1)] for sl in slabs], axis=1).astype(BF16)

    u = ustage[jj]
    sloc[...] = jnp.dot(u, e_ref[0], preferred_element_type=F32)
    ar = al_ref[0, 0:1, :]
    ai = al_ref[0, 1:2, :]

    row = lax.broadcasted_iota(jnp.int32, (SUBLANES, LANES), 0)

    def body(ti, carry):
        xr, xi = carry
        r0 = pl.multiple_of(ti * SUBLANES, SUBLANES)
        sr_t = sloc[pl.ds(r0, SUBLANES), 0:LANES]
        si_t = sloc[pl.ds(r0, SUBLANES), LANES:2 * LANES]
        pr_t = jnp.zeros((SUBLANES, LANES), F32)
        pi_t = jnp.zeros((SUBLANES, LANES), F32)
        for r in range(SUBLANES):
            pr_t = jnp.where(row == r, xr, pr_t)
            pi_t = jnp.where(row == r, xi, pi_t)
            xr, xi = ar * xr - ai * xi + sr_t[r:r + 1, :], ar * xi + ai * xr + si_t[r:r + 1, :]
        xpr[pl.ds(r0, SUBLANES), :] = pr_t
        xpi[pl.ds(r0, SUBLANES), :] = pi_t
        return xr, xi

    zero = jnp.zeros((1, LANES), F32)
    lax.fori_loop(0, nc // SUBLANES, body, (zero, zero))
    ystage[jj] = (jnp.dot(xpr[...].astype(BF16), fr_ref[0], preferred_element_type=F32)
                  + jnp.dot(xpi[...].astype(BF16), fi_ref[0], preferred_element_type=F32)
                  + jnp.dot(u, m_ref[0], preferred_element_type=F32))

    @pl.when(jj == S5_SLAB - 1)
    def _():
        for t in range(S5_CHUNK):
            o_ref[pl.ds(t, nc, stride=S5_CHUNK), :] = jnp.concatenate(
                [ystage[q, :, S5_PB * t:S5_PB * (t + 1)] for q in range(S5_SLAB)], axis=1)


def _s5(z2, lam_re, lam_im, log_dt, b_re, b_im, c_re, c_im):
    t = z2.shape[0]
    nc = t // S5_CHUNK
    m, e_pair, fr_pair, fi_pair, al = _s5_tables(lam_re, lam_im, log_dt, b_re, b_im, c_re, c_im)
    pair = lambda a, b: pl.BlockSpec((1, a, b), lambda k, jj: (S5_SLAB * k + jj, 0, 0))
    return pl.pallas_call(
        functools.partial(_s5_kernel, nc=nc),
        grid=(WIDTH // LANES, S5_SLAB),
        in_specs=[pl.BlockSpec((t, LANES), lambda k, jj: (0, Z2_US * WIDTH // LANES + k), pipeline_mode=pl.Buffered(1)),
                  pair(S5_ROWS, S5_ROWS), pair(S5_ROWS, 2 * LANES), pair(LANES, S5_ROWS), pair(LANES, S5_ROWS),
                  pair(SUBLANES, LANES)],
        out_specs=pl.BlockSpec((t, LANES), lambda k, jj: (0, k), pipeline_mode=pl.Buffered(1)),
        out_shape=jax.ShapeDtypeStruct((t, WIDTH), F32),
        scratch_shapes=[pltpu.VMEM((S5_SLAB, nc, S5_ROWS), BF16), pltpu.VMEM((S5_SLAB, nc, S5_ROWS), F32),
                        pltpu.VMEM((nc, 2 * LANES), F32), pltpu.VMEM((nc, LANES), F32), pltpu.VMEM((nc, LANES), F32)],
        compiler_params=_cparams(("arbitrary", "arbitrary")),
        name="s5_scan",
    )(z2, m, e_pair, fr_pair, fi_pair, al)


def _layer_norm(x, g, b):
    mu = jnp.mean(x, axis=-1, keepdims=True)
    xc = x - mu
    var = jnp.mean(xc * xc, axis=-1, keepdims=True)
    return xc * lax.rsqrt(var + LN_EPS) * g + b


def _merge_rows(sl, x_ref, g0_ref, g1_ref, g2_ref, ym_ref, ys_ref, us_ref, yh_ref, wm_ref, ws_ref, wh_ref, wglu_ref,
                d_ref, wo_ref, lg_ref, lb_ref, rw_ref, rb_ref, o_ref):
    ys = _gelu_tanh(ys_ref[sl, :] + d_ref[...] * us_ref[sl, :].astype(F32))
    ys = ys * _sigmoid(_dot(ys, wglu_ref[...]))
    merged = (_sigmoid(g0_ref[sl, :].astype(F32)) * _dot(ym_ref[sl, :], wm_ref[...])
              + _sigmoid(g1_ref[sl, :].astype(F32)) * _dot(ys, ws_ref[...])
              + _sigmoid(g2_ref[sl, :].astype(F32)) * _dot(yh_ref[sl, :], wh_ref[...]))
    x1 = _layer_norm(DN_ALPHA * x_ref[sl, :] + _dot(merged, wo_ref[...]), lg_ref[...], lb_ref[...])
    o_ref[sl, 0:D_MODEL] = x1

    xh = x1.astype(BF16)
    xl = (x1 - xh.astype(F32)).astype(BF16)
    logits = (jnp.dot(xh, rw_ref[0], preferred_element_type=F32) + jnp.dot(xl, rw_ref[0], preferred_element_type=F32)
              + jnp.dot(xh, rw_ref[1], preferred_element_type=F32))
    s0 = _sigmoid(logits)
    sb0 = s0 + rb_ref[...]
    shift = lambda a, j: a if j == 0 else pltpu.roll(a, LANES - N_GROUPS * j, 1)
    s = [shift(s0, j) for j in range(EXPERTS_PER_GROUP)]
    sb = [shift(sb0, j) for j in range(EXPERTS_PER_GROUP)]
    hi1, lo1 = jnp.maximum(sb[0], sb[1]), jnp.minimum(sb[0], sb[1])
    hi2, lo2 = jnp.maximum(sb[2], sb[3]), jnp.minimum(sb[2], sb[3])
    top2 = jnp.maximum(hi1, hi2) + jnp.maximum(jnp.minimum(hi1, hi2), jnp.maximum(lo1, lo2))
    lane = lax.broadcasted_iota(jnp.int32, top2.shape, 1)
    top2 = jnp.where(lane < N_GROUPS, top2, -jnp.inf)
    gmax = jnp.max(top2, axis=1, keepdims=True)
    g_idx = jnp.min(jnp.where(top2 == gmax, lane, LANES), axis=1, keepdims=True)
    sel = lane == g_idx
    v = [jnp.sum(jnp.where(sel, sb[j], 0.0), axis=1, keepdims=True) for j in range(EXPERTS_PER_GROUP)]
    sv = [jnp.sum(jnp.where(sel, s[j], 0.0), axis=1, keepdims=True) for j in range(EXPERTS_PER_GROUP)]

    def first_max(vals):
        m = jnp.maximum(jnp.maximum(vals[0], vals[1]), jnp.maximum(vals[2], vals[3]))
        return jnp.where(vals[0] == m, 0, jnp.where(vals[1] == m, 1, jnp.where(vals[2] == m, 2, 3)))

    e1 = first_max(v)
    e2 = first_max([jnp.where(e1 == j, -jnp.inf, v[j]) for j in range(EXPERTS_PER_GROUP)])
    s1 = sum(jnp.where(e1 == j, sv[j], 0.0) for j in range(EXPERTS_PER_GROUP))
    s2 = sum(jnp.where(e2 == j, sv[j], 0.0) for j in range(EXPERTS_PER_GROUP))
    tot = s1 + s2
    meta = jnp.where(lane == 0, g_idx.astype(F32), 0.0)
    for j in range(EXPERTS_PER_GROUP):
        cw = jnp.where(e1 == j, s1 / tot, 0.0) + jnp.where(e2 == j, s2 / tot, 0.0)
        meta = jnp.where(lane == 1 + j, cw, meta)
    o_ref[sl, D_MODEL:D_MODEL + META] = meta


MERGE_ROWS = 256


def _merge_kernel(*refs):
    for lo in range(0, refs[0].shape[0], MERGE_ROWS):
        _merge_rows(slice(lo, lo + MERGE_ROWS), *refs)


def _merge(x, z1, z2, ym, ys, yh, wm, ws, wh, wglu, d, wo, lg, lb, router_w, router_bias, tm):
    t = x.shape[0]
    rw = router_w.astype(F32).reshape(D_MODEL, N_GROUPS, EXPERTS_PER_GROUP).transpose(0, 2, 1)
    rw = jnp.pad(rw.reshape(D_MODEL, N_EXPERTS), ((0, 0), (0, LANES - N_EXPERTS)))
    rw_hi = rw.astype(BF16)
    rw = jnp.stack([rw_hi, (rw - rw_hi.astype(F32)).astype(BF16)])
    rb = jnp.pad(router_bias.astype(F32).reshape(N_GROUPS, EXPERTS_PER_GROUP).T.reshape(1, N_EXPERTS),
                 ((0, 0), (0, LANES - N_EXPERTS)))
    full = lambda shape: pl.BlockSpec(shape, lambda i: (0,) * len(shape))
    row = lambda w: pl.BlockSpec((tm, w), lambda i: (i, 0))
    return pl.pallas_call(
        _merge_kernel,
        grid=(t // tm,),
        in_specs=[row(D_MODEL),
                  pl.BlockSpec((tm, D_MODEL), lambda i: (i, 0)),
                  pl.BlockSpec((tm, D_MODEL), lambda i: (i, 1)),
                  pl.BlockSpec((tm, D_MODEL), lambda i: (i, 2)),
                  row(WIDTH), row(WIDTH), pl.BlockSpec((tm, WIDTH), lambda i: (i, Z2_US)), row(WIDTH),
                  full((WIDTH, D_MODEL)), full((WIDTH, D_MODEL)), full((WIDTH, D_MODEL)), full((WIDTH, WIDTH)),
                  full((1, WIDTH)), full((D_MODEL, D_MODEL)), full((1, D_MODEL)), full((1, D_MODEL)),
                  full((2, D_MODEL, LANES)), full((1, LANES))],
        out_specs=pl.BlockSpec((tm, D_MODEL + META), lambda i: (i, 0)),
        out_shape=jax.ShapeDtypeStruct((t, D_MODEL + META), F32),
        compiler_params=_cparams(("parallel",)),
        name="merge_router",
    )(x, z1, z1, z1, ym, ys, z2, yh, wm.astype(BF16), ws.astype(BF16), wh.astype(BF16), wglu.astype(BF16),
      d.reshape(1, WIDTH).astype(F32), wo.astype(BF16), lg.reshape(1, D_MODEL).astype(F32),
      lb.reshape(1, D_MODEL).astype(F32), rw, rb)


def _start_row_gather(idx_ref, base, src_hbm, dst, sem, rows):
    for r in range(rows):
        pltpu.make_async_copy(src_hbm.at[pl.ds(idx_ref[base + r], 1)], dst.at[pl.ds(r, 1)], sem).start(priority=r % 2)


def _wait_row_gather(src_hbm, dst, sem, rows):
    pltpu.make_async_copy(src_hbm.at[pl.ds(0, rows)], dst, sem).wait()


def _dispatch_kernel(dst_ref, x_ref, init_hbm, o_hbm, sem, *, tm):
    del init_hbm
    base = pl.program_id(0) * tm
    for r in range(tm):
        pltpu.make_async_copy(x_ref.at[pl.ds(r, 1)], o_hbm.at[pl.ds(dst_ref[base + r], 1)], sem).start(priority=r % 2)
    pltpu.make_async_copy(x_ref, o_hbm.at[pl.ds(0, tm)], sem).wait()


def _moe_kernel(tg_ref, x_ref, wg_ref, wu_ref, wd_ref, o_ref, wg_s, wu_s, wd_s, *, tm, ntiles):
    i = pl.program_id(0)

    @pl.when(jnp.logical_or(i == 0, tg_ref[i] != tg_ref[jnp.maximum(i - 1, 0)]))
    def _():
        for e in range(EXPERTS_PER_GROUP):
            wg_s[e] = wg_ref[0, e].astype(BF16)
            wu_s[e] = wu_ref[0, e].astype(BF16)
            wd_s[e] = wd_ref[0, e].astype(BF16)

    used = i < tg_ref[ntiles]

    @pl.when(used)
    def _():
        xb = x_ref[:, 0:D_MODEL].astype(BF16)
        meta = x_ref[:, D_MODEL:D_MODEL + META]
        y = jnp.zeros((tm, D_MODEL), F32)
        for e in range(EXPERTS_PER_GROUP):
            hg = jnp.dot(xb, wg_s[e], preferred_element_type=F32)
            hu = jnp.dot(xb, wu_s[e], preferred_element_type=F32)
            hh = _silu(hg) * hu * _lane_col(meta, 1 + e)
            y = y + jnp.dot(hh.astype(BF16), wd_s[e], preferred_element_type=F32)
        o_ref[...] = y

    @pl.when(jnp.logical_not(used))
    def _():
        o_ref[...] = jnp.zeros_like(o_ref)


def _ln2_kernel(dst_ref, x_ref, y_hbm, lg_ref, lb_ref, o_ref, buf, sem, *, tm, ntiles):
    i = pl.program_id(0)
    slot = i % 2

    @pl.when(i == 0)
    def _():
        _start_row_gather(dst_ref, 0, y_hbm, buf.at[0], sem.at[0], tm)

    @pl.when(i + 1 < ntiles)
    def _():
        _start_row_gather(dst_ref, (i + 1) * tm, y_hbm, buf.at[1 - slot], sem.at[1 - slot], tm)

    _wait_row_gather(y_hbm, buf.at[slot], sem.at[slot], tm)
    o_ref[...] = _layer_norm(DN_ALPHA * x_ref[...] + buf[slot], lg_ref[...], lb_ref[...])


def _moe_ln(x1e, wg, wu, wd, layer, lg, lb, tm):
    t = x1e.shape[0]
    ntiles = t // tm + N_GROUPS
    p = ntiles * tm
    key = x1e[:, D_MODEL].astype(jnp.int32)
    onehot = (key[:, None] == jnp.arange(N_GROUPS)[None, :]).astype(jnp.int32)
    csum = jnp.cumsum(onehot, axis=0)
    counts = csum[-1]
    rank = jnp.sum(onehot * csum, axis=1) - 1
    pcount = ((counts + tm - 1) // tm) * tm
    pend = jnp.cumsum(pcount)
    dest = (pend - pcount)[key] + rank
    dest = dest.astype(jnp.int32)
    tile_start = jnp.arange(ntiles, dtype=jnp.int32) * tm
    tile_group = jnp.minimum(jnp.sum((tile_start[:, None] >= pend[None, :]).astype(jnp.int32), axis=1),
                             N_GROUPS - 1)
    tile_group = jnp.concatenate([tile_group, (pend[-1:] // tm).astype(jnp.int32)])
    nt2 = t // tm
    x_sorted = pl.pallas_call(
        functools.partial(_dispatch_kernel, tm=tm),
        grid_spec=pltpu.PrefetchScalarGridSpec(
            num_scalar_prefetch=1,
            grid=(nt2,),
            in_specs=[pl.BlockSpec((tm, D_MODEL + META), lambda i, d: (i, 0)), pl.BlockSpec(memory_space=pl.ANY)],
            out_specs=pl.BlockSpec(memory_space=pl.ANY),
            scratch_shapes=[pltpu.SemaphoreType.DMA(())]),
        out_shape=jax.ShapeDtypeStruct((p, D_MODEL + META), F32),
        input_output_aliases={2: 0},
        compiler_params=_cparams(("arbitrary",)),
        name="moe_dispatch",
    )(dest, x1e, jnp.zeros((p, D_MODEL + META), F32))
    wspec = lambda a, b: pl.BlockSpec((1, EXPERTS_PER_GROUP, a, b), lambda i, tg: (layer, tg[i], 0, 0))
    wscr = lambda a, b: pltpu.VMEM((EXPERTS_PER_GROUP, a, b), BF16)
    y_sorted = pl.pallas_call(
        functools.partial(_moe_kernel, tm=tm, ntiles=ntiles),
        grid_spec=pltpu.PrefetchScalarGridSpec(
            num_scalar_prefetch=1,
            grid=(ntiles,),
            in_specs=[pl.BlockSpec((tm, D_MODEL + META), lambda i, tg: (i, 0)), wspec(D_MODEL, D_EXPERT),
                      wspec(D_MODEL, D_EXPERT), wspec(D_EXPERT, D_MODEL)],
            out_specs=pl.BlockSpec((tm, D_MODEL), lambda i, tg: (i, 0)),
            scratch_shapes=[wscr(D_MODEL, D_EXPERT), wscr(D_MODEL, D_EXPERT), wscr(D_EXPERT, D_MODEL)]),
        out_shape=jax.ShapeDtypeStruct((p, D_MODEL), F32),
        compiler_params=_cparams(("arbitrary",)),
        name="moe_experts",
    )(tile_group, x_sorted, wg, wu, wd)
    return pl.pallas_call(
        functools.partial(_ln2_kernel, tm=tm, ntiles=nt2),
        grid_spec=pltpu.PrefetchScalarGridSpec(
            num_scalar_prefetch=1,
            grid=(nt2,),
            in_specs=[pl.BlockSpec((tm, D_MODEL), lambda i, d: (i, 0)), pl.BlockSpec(memory_space=pl.ANY),
                      pl.BlockSpec((1, D_MODEL), lambda i, d: (0, 0)), pl.BlockSpec((1, D_MODEL), lambda i, d: (0, 0))],
            out_specs=pl.BlockSpec((tm, D_MODEL), lambda i, d: (i, 0)),
            scratch_shapes=[pltpu.VMEM((2, tm, D_MODEL), F32), pltpu.SemaphoreType.DMA((2,))]),
        out_shape=jax.ShapeDtypeStruct((t, D_MODEL), F32),
        compiler_params=_cparams(("arbitrary",)),
        name="moe_combine_ln",
    )(dest.astype(jnp.int32), x1e, y_sorted, lg.reshape(1, D_MODEL).astype(F32), lb.reshape(1, D_MODEL).astype(F32))


def _layer(x, l, p, lb_l, cfg):
    t = x.shape[0]
    w1, w2, wt = _split_w_in(p['w_in'][l])
    xb = x.astype(BF16)
    z1 = _matmul(xb, w1, Z1_DTYPE, cfg['tm_in'], cfg['tn_in'])
    z2, gt = _inproj_gate(xb, w2, wt, cfg['tm_in'])
    gt3 = gt.reshape(SUBLANES, t // CHUNK, CHUNK).transpose(1, 0, 2)
    ym = _mlstm(z1, z2, gt3, p['mlstm_conv_w'][l], p['mlstm_conv_b'][l], p['mlstm_wq'][l], p['mlstm_wk'][l],
                p['mlstm_wv'][l], p['mlstm_b_i'][l], p['mlstm_b_f'][l], p['mlstm_skip'][l], cfg['tb'])
    ys = _s5(z2, p['s5_lambda_re'][l], p['s5_lambda_im'][l], p['s5_log_dt'][l], p['s5_b_re'][l], p['s5_b_im'][l],
             p['s5_c_re'][l], p['s5_c_im'][l])
    yh = _hgrn(z1, z2, lb_l, p['hgrn_norm_w'][l], cfg['tb'])
    x1e = _merge(x, z1, z2, ym, ys, yh, p['w_branch_mlstm'][l], p['w_branch_s5'][l], p['w_branch_hgrn'][l],
                 p['s5_w_glu'][l], p['s5_d'][l], p['w_out'][l], p['ln1_g'][l], p['ln1_b'][l], p['router_w'],
                 p['router_bias'], cfg['tm_merge'])
    return _moe_ln(x1e, p['exp_w_gate'], p['exp_w_up'], p['exp_w_down'], l, p['ln2_g'][l], p['ln2_b'][l],
                   cfg['tm_moe'])


_CFG = dict(tm_in=1024, tn_in=Z1_COLS // 4, tb=1024, tm_merge=2 * MERGE_ROWS, tm_moe=256)


def kernel(x, w_in, mlstm_conv_w, mlstm_conv_b, mlstm_wq, mlstm_wk, mlstm_wv, mlstm_b_i, mlstm_b_f, mlstm_skip, s5_lambda_re, s5_lambda_im, s5_log_dt, s5_b_re, s5_b_im, s5_c_re, s5_c_im, s5_d, s5_w_glu, hgrn_lower_bounds, hgrn_norm_w, w_branch_mlstm, w_branch_s5, w_branch_hgrn, w_out, ln1_g, ln1_b, ln2_g, ln2_b, router_w, router_bias, exp_w_gate, exp_w_up, exp_w_down):
    p = dict(w_in=w_in, mlstm_conv_w=mlstm_conv_w, mlstm_conv_b=mlstm_conv_b, mlstm_wq=mlstm_wq, mlstm_wk=mlstm_wk,
             mlstm_wv=mlstm_wv, mlstm_b_i=mlstm_b_i, mlstm_b_f=mlstm_b_f, mlstm_skip=mlstm_skip,
             s5_lambda_re=s5_lambda_re, s5_lambda_im=s5_lambda_im, s5_log_dt=s5_log_dt, s5_b_re=s5_b_re,
             s5_b_im=s5_b_im, s5_c_re=s5_c_re, s5_c_im=s5_c_im, s5_d=s5_d, s5_w_glu=s5_w_glu,
             hgrn_norm_w=hgrn_norm_w, w_branch_mlstm=w_branch_mlstm, w_branch_s5=w_branch_s5,
             w_branch_hgrn=w_branch_hgrn, w_out=w_out, ln1_g=ln1_g, ln1_b=ln1_b, ln2_g=ln2_g, ln2_b=ln2_b,
             router_w=router_w, router_bias=router_bias, exp_w_gate=exp_w_gate, exp_w_up=exp_w_up,
             exp_w_down=exp_w_down)
    lb_cum = jnp.cumsum(jax.nn.softmax(hgrn_lower_bounds.astype(F32), axis=0), axis=0)
    lb_layers = lb_cum - lb_cum[0]
    bsz, seq, d = x.shape
    h = x.reshape(bsz * seq, d)
    for l in range(DEPTH):
        h = _layer(h, l, p, lb_layers[l], _CFG)
    return h.reshape(bsz, seq, d)
```

```python
import functools
import math

import jax
import jax.numpy as jnp
from jax import lax
from jax.experimental import pallas as pl
from jax.experimental.pallas import tpu as pltpu

F32 = jnp.float32
BF16 = jnp.bfloat16
HIGHEST = lax.Precision.HIGHEST

D_MODEL = 1024
DEPTH = 2
HEADS = 4
HEAD_DIM = 128
WIDTH = HEADS * HEAD_DIM
MLSTM_CONV = 4
CHUNK = 64
S5_GROUP = 16
S5_GROUPS = 32
S5_STATE = 64
S5_PAIRS = S5_GROUPS // 2
S5_CHUNK = 32
S5_MAX_REAL = -1e-4
N_EXPERTS = 32
N_GROUPS = 8
EXPERTS_PER_GROUP = 4
D_EXPERT = 256
DN_ALPHA = (2 * DEPTH) ** 0.25
LN_EPS = 1e-5
NORM_EPS = 1e-6

LANES = 128
SUBLANES = 8
META = LANES
V7X_VMEM_BYTES = 64 * 1024 * 1024
VMEM_LIMIT = V7X_VMEM_BYTES * 7 // 8

Z1_DTYPE = BF16
Z1_GATE, Z1_XM, Z1_OM, Z1_QH, Z1_IH, Z1_GH = 0, 6, 7, 8, 9, 10
Z1_COLS = 11 * WIDTH
Z2_FH, Z2_US = 0, 1
Z2_COLS = 2 * WIDTH + LANES


def _cparams(sem):
    return pltpu.CompilerParams(dimension_semantics=sem, vmem_limit_bytes=VMEM_LIMIT)


def _sigmoid(x):
    return 0.5 * (1.0 + jnp.tanh(0.5 * x))


def _silu(x):
    return x * _sigmoid(x)


def _log_sigmoid(x):
    return jnp.minimum(x, 0.0) - jnp.log(1.0 + jnp.exp(-jnp.abs(x)))


def _gelu_tanh(x):
    return 0.5 * x * (1.0 + jnp.tanh(math.sqrt(2.0 / math.pi) * (x + 0.044715 * (x * x * x))))


def _cumsum_rows(x):
    n = x.shape[0]
    row = lax.broadcasted_iota(jnp.int32, x.shape, 0)
    s = 1
    while s < n:
        x = x + jnp.where(row >= s, pltpu.roll(x, s, 0), 0.0)
        s *= 2
    return x


def _lane_col(x, idx):
    lane = lax.broadcasted_iota(jnp.int32, x.shape, 1)
    return jnp.sum(jnp.where(lane == idx, x, 0.0), axis=1, keepdims=True)


def _dot(a, b):
    return jnp.dot(a.astype(BF16), b.astype(BF16), preferred_element_type=F32)


def _dot_nt(a, b):
    return lax.dot_general(a.astype(BF16), b.astype(BF16), (((1,), (1,)), ((), ())), preferred_element_type=F32)


def _dot_tn(a, b):
    return lax.dot_general(a.astype(BF16), b.astype(BF16), (((0,), (0,)), ((), ())), preferred_element_type=F32)


def _mm_kernel(x_ref, w_ref, o_ref):
    o_ref[...] = jnp.dot(x_ref[...], w_ref[...], preferred_element_type=F32).astype(o_ref.dtype)


def _matmul(x, w, out_dtype, tm, tn):
    m, k = x.shape
    n = w.shape[1]
    return pl.pallas_call(
        _mm_kernel,
        grid=(m // tm, n // tn),
        in_specs=[pl.BlockSpec((tm, k), lambda i, j: (i, 0)), pl.BlockSpec((k, tn), lambda i, j: (0, j))],
        out_specs=pl.BlockSpec((tm, tn), lambda i, j: (i, j)),
        out_shape=jax.ShapeDtypeStruct((m, n), out_dtype),
        compiler_params=_cparams(("parallel", "parallel")),
        name="in_proj_wide",
    )(x, w)


def _inproj_gate_kernel(x_ref, w_ref, wt_ref, o_ref, gt_ref):
    x = x_ref[...]
    o_ref[...] = jnp.dot(x, w_ref[...], preferred_element_type=F32)
    gt_ref[...] = lax.dot_general(wt_ref[...], x, (((1,), (1,)), ((), ())), preferred_element_type=F32)


def _inproj_gate(x, w2, wt, tm):
    m, k = x.shape
    return pl.pallas_call(
        _inproj_gate_kernel,
        grid=(m // tm,),
        in_specs=[pl.BlockSpec((tm, k), lambda i: (i, 0)),
                  pl.BlockSpec((k, Z2_COLS), lambda i: (0, 0)),
                  pl.BlockSpec((SUBLANES, k), lambda i: (0, 0))],
        out_specs=[pl.BlockSpec((tm, Z2_COLS), lambda i: (i, 0)), pl.BlockSpec((SUBLANES, tm), lambda i: (0, i))],
        out_shape=[jax.ShapeDtypeStruct((m, Z2_COLS), F32), jax.ShapeDtypeStruct((SUBLANES, m), F32)],
        compiler_params=_cparams(("parallel",)),
        name="in_proj_gates",
    )(x, w2, wt)


def _split_w_in(w):
    offs, o = [], 0
    for s in (WIDTH, WIDTH, HEADS, HEADS, WIDTH, WIDTH, WIDTH, WIDTH, WIDTH, 3 * D_MODEL):
        offs.append((o, o + s))
        o += s
    seg = [w[:, a:b] for a, b in offs]
    xm, om, im, fm, us, qh, fh, ih, gh, gate = seg
    w1 = jnp.concatenate([gate, xm, om, qh, ih, gh], axis=1).astype(BF16)
    pad = jnp.zeros((w.shape[0], LANES - 2 * HEADS), w.dtype)
    w2 = jnp.concatenate([fh, us, im, fm, pad], axis=1).astype(BF16)
    wt = jnp.concatenate([im, fm], axis=1).T.astype(BF16)
    return w1, w2, wt


def _mlstm_kernel(xm_ref, om_ref, gc_ref, gr_ref, cw_ref, cb_ref, wq_ref, wk_ref, wv_ref, bcol_ref, brow_ref,
                  skip_ref, triu_ref, o_ref, xpad, q_s, k_s, v_s, xc_s, h_s, gcol_s, grow_s, brow_s, ct_s, n_s, m_s,
                  *, tb):
    ncb = tb // CHUNK

    @pl.when(pl.program_id(0) == 0)
    def _():
        xpad[0:SUBLANES, :] = jnp.zeros((SUBLANES, WIDTH), F32)
        ct_s[...] = jnp.zeros_like(ct_s)
        n_s[...] = jnp.zeros_like(n_s)
        m_s[...] = jnp.zeros_like(m_s)

    xm = xm_ref[...].astype(F32)
    xpad[SUBLANES:SUBLANES + tb, :] = xm
    cw = cw_ref[...]
    conv = cb_ref[...] + cw[3:4, :] * xm
    for d in range(1, MLSTM_CONV):
        conv = conv + cw[3 - d:4 - d, :] * xpad[SUBLANES - d:SUBLANES - d + tb, :]
    xpad[0:SUBLANES, :] = xpad[tb:tb + SUBLANES, :]
    xc = _silu(conv)
    xc_s[...] = xc
    for h in range(HEADS):
        sl = slice(h * HEAD_DIM, (h + 1) * HEAD_DIM)
        xch = xc[:, sl].astype(BF16)
        q_s[:, sl] = jnp.dot(xch, wq_ref[h], preferred_element_type=F32) * (HEAD_DIM ** -0.5)
        k_s[:, sl] = jnp.dot(xch, wk_ref[h], preferred_element_type=F32)
        v_s[:, sl] = jnp.dot(xm[:, sl].astype(BF16), wv_ref[h], preferred_element_type=F32)

    gc = gc_ref[...] + bcol_ref[...]
    lane = lax.broadcasted_iota(jnp.int32, gc.shape, 1)
    gcol_s[...] = jnp.where(lane < HEADS, gc, _log_sigmoid(gc))
    gr = gr_ref[...] + brow_ref[...]
    sub = lax.broadcasted_iota(jnp.int32, gr.shape, 1)
    gr = jnp.where(sub < HEADS, gr, _log_sigmoid(gr)).reshape(ncb * SUBLANES, CHUNK)
    grow_s[...] = gr
    brow_s[...] = jnp.dot(gr, triu_ref[...], precision=HIGHEST, preferred_element_type=F32)

    rowi = lax.broadcasted_iota(jnp.int32, (CHUNK, CHUNK), 0)
    coli = lax.broadcasted_iota(jnp.int32, (CHUNK, CHUNK), 1)
    causal = rowi >= coli

    def chunk_body(c, carry):
        r0 = pl.multiple_of(c * CHUNK, CHUNK)
        g0 = pl.multiple_of(c * SUBLANES, SUBLANES)
        gcol = gcol_s[pl.ds(r0, CHUNK), :]
        bcol = _cumsum_rows(gcol)
        grow = grow_s[pl.ds(g0, SUBLANES), :]
        brow = brow_s[pl.ds(g0, SUBLANES), :]
        for h in range(HEADS):
            sl = slice(h * HEAD_DIM, (h + 1) * HEAD_DIM)
            li_c = _lane_col(gcol, h)
            b_c = _lane_col(bcol, HEADS + h)
            li_r = grow[h:h + 1, :]
            b_r = brow[HEADS + h:HEADS + h + 1, :]
            b_tot = b_c[CHUNK - 1:CHUNK, :]
            a_c = b_tot - b_c + li_c
            a_max = jnp.max(a_c, axis=0, keepdims=True)
            q = q_s[pl.ds(r0, CHUNK), sl]
            k = k_s[pl.ds(r0, CHUNK), sl]
            v = v_s[pl.ds(r0, CHUNK), sl]
            wk = jnp.exp(a_c - a_max) * k
            c_loc_t = _dot_tn(v, wk)
            n_loc = jnp.sum(wk, axis=0, keepdims=True)
            ct_prev = ct_s[h]
            n_prev = n_s[h:h + 1, :]
            m_prev = m_s[h:h + 1, 0:1]
            d_mat = jnp.where(causal, b_c - b_r + li_r, -jnp.inf)
            m_inter = b_c + m_prev
            m_j = jnp.maximum(m_inter, jnp.max(d_mat, axis=1, keepdims=True))
            sc = _dot_nt(q, k) * jnp.exp(d_mat - m_j)
            g_inter = jnp.exp(m_inter - m_j)
            num = g_inter * _dot_nt(q, ct_prev) + _dot(sc, v)
            den = g_inter * jnp.sum(q * n_prev, axis=1, keepdims=True) + jnp.sum(sc, axis=1, keepdims=True)
            h_s[pl.ds(r0, CHUNK), sl] = num / jnp.maximum(jnp.abs(den), jnp.exp(-m_j))
            m_new = jnp.maximum(b_tot + m_prev, a_max)
            g_old = jnp.exp(b_tot + m_prev - m_new)
            g_loc = jnp.exp(a_max - m_new)
            ct_s[h] = g_old * ct_prev + g_loc * c_loc_t
            n_s[h:h + 1, :] = g_old * n_prev + g_loc * n_loc
            m_s[h:h + 1, :] = jnp.broadcast_to(m_new, (1, LANES))
        return carry

    lax.fori_loop(0, ncb, chunk_body, 0, unroll=4)
    o_ref[...] = (_sigmoid(om_ref[...].astype(F32)) * h_s[...] + skip_ref[...] * xc_s[...]).astype(o_ref.dtype)


def _mlstm(z1, z2, gt3, conv_w, conv_b, wq, wk, wv, b_i, b_f, skip, tb):
    t = z1.shape[0]
    ncb = tb // CHUNK
    bias = jnp.concatenate([b_i, b_f]).astype(F32)
    bcol = jnp.zeros((1, LANES), F32).at[0, :2 * HEADS].set(bias)
    brow = bias.reshape(2 * HEADS, 1)
    triu = jnp.triu(jnp.ones((CHUNK, CHUNK), F32))
    full = lambda shape: pl.BlockSpec(shape, lambda i: (0,) * len(shape))
    return pl.pallas_call(
        functools.partial(_mlstm_kernel, tb=tb),
        grid=(t // tb,),
        in_specs=[pl.BlockSpec((tb, WIDTH), lambda i: (i, Z1_XM)),
                  pl.BlockSpec((tb, WIDTH), lambda i: (i, Z1_OM)),
                  pl.BlockSpec((tb, LANES), lambda i: (i, 2 * WIDTH // LANES)),
                  pl.BlockSpec((ncb, SUBLANES, CHUNK), lambda i: (i, 0, 0)),
                  full((MLSTM_CONV, WIDTH)), full((1, WIDTH)),
                  full((HEADS, HEAD_DIM, HEAD_DIM)), full((HEADS, HEAD_DIM, HEAD_DIM)),
                  full((HEADS, HEAD_DIM, HEAD_DIM)),
                  full((1, LANES)), full((2 * HEADS, 1)), full((1, WIDTH)), full((CHUNK, CHUNK))],
        out_specs=pl.BlockSpec((tb, WIDTH), lambda i: (i, 0)),
        out_shape=jax.ShapeDtypeStruct((t, WIDTH), F32),
        scratch_shapes=[pltpu.VMEM((tb + SUBLANES, WIDTH), F32)]
        + [pltpu.VMEM((tb, WIDTH), F32) for _ in range(5)]
        + [pltpu.VMEM((tb, LANES), F32),
           pltpu.VMEM((ncb * SUBLANES, CHUNK), F32), pltpu.VMEM((ncb * SUBLANES, CHUNK), F32),
           pltpu.VMEM((HEADS, HEAD_DIM, HEAD_DIM), F32), pltpu.VMEM((SUBLANES, LANES), F32),
           pltpu.VMEM((SUBLANES, LANES), F32)],
        compiler_params=_cparams(("arbitrary",)),
        name="mlstm",
    )(z1, z1, z2, gt3, conv_w.astype(F32), conv_b.reshape(1, WIDTH).astype(F32), wq.astype(BF16), wk.astype(BF16),
      wv.astype(BF16), bcol, brow, skip.reshape(1, WIDTH).astype(F32), triu)


def _hgrn_kernel(q_ref, f_ref, i_ref, g_ref, lb_ref, nw_ref, tri_ref, o_ref, st_s, *, tb):
    ncb = tb // CHUNK

    @pl.when(pl.program_id(0) == 0)
    def _():
        st_s[...] = jnp.zeros_like(st_s)

    lb = lb_ref[...]
    nw = nw_ref[...]
    tri = tri_ref[...]
    rowi = lax.broadcasted_iota(jnp.int32, (CHUNK, WIDTH), 0)
    sr = lax.broadcasted_iota(jnp.int32, (CHUNK, CHUNK), 0)
    sc = lax.broadcasted_iota(jnp.int32, (CHUNK, CHUNK), 1)
    halves = [1 << p for p in range(CHUNK.bit_length() - 1)]
    upper = {m: (rowi & m) != 0 for m in halves}
    same_blk = {m: (sr // (2 * m)) == (sc // (2 * m)) for m in halves}

    def chunk_body(c, carry):
        r0 = pl.multiple_of(c * CHUNK, CHUNK)
        f = lb + (1.0 - lb) * _sigmoid(f_ref[pl.ds(r0, CHUNK), :].astype(F32))
        k = 1.0 - f
        q = _silu(q_ref[pl.ds(r0, CHUNK), :].astype(F32))
        v = i_ref[pl.ds(r0, CHUNK), :].astype(F32)
        lf = jnp.log(f)
        hi = lf.astype(BF16)
        r1 = lf - hi.astype(F32)
        mid = r1.astype(BF16)
        lo = (r1 - mid.astype(F32)).astype(BF16)
        b = (jnp.dot(tri, hi, preferred_element_type=F32) + jnp.dot(tri, mid, preferred_element_type=F32)
             + jnp.dot(tri, lo, preferred_element_type=F32))
        qs, ks = {}, {}
        for m in halves:
            if m == 1:
                t = jnp.where(upper[m], q * f, k)
            else:
                nblk = CHUNK // (2 * m)
                r = jnp.concatenate([jnp.broadcast_to(b[2 * m * j + m - 1:2 * m * j + m, :], (2 * m, WIDTH))
                                     for j in range(nblk)], axis=0)
                t = jnp.where(upper[m], q, k) * jnp.exp(-jnp.abs(b - r))
            qs[m] = jnp.where(upper[m], t, 0.0).astype(BF16)
            ks[m] = jnp.where(upper[m], 0.0, t).astype(BF16)
        b_last = b[CHUNK - 1:CHUNK, :]
        qe = q * jnp.exp(b)
        kd = k * jnp.exp(b_last - b)
        e_last = jnp.exp(b_last)
        outs = []
        for h in range(HEADS):
            sl = slice(h * HEAD_DIM, (h + 1) * HEAD_DIM)
            st = st_s[h]
            a = jnp.where(sr == sc, _dot_nt(q[:, sl], k[:, sl]), 0.0)
            for m in halves:
                a = a + jnp.where(same_blk[m], _dot_nt(qs[m][:, sl], ks[m][:, sl]), 0.0)
            oh = _dot_nt(qe[:, sl], st) + _dot(a, v[:, sl])
            st_s[h] = e_last[:, sl] * st + _dot_tn(v[:, sl], kd[:, sl])
            ms = jnp.sum(oh * oh, axis=1, keepdims=True) * (1.0 / HEAD_DIM)
            outs.append(oh * lax.rsqrt(ms + NORM_EPS))
        on = jnp.concatenate(outs, axis=1)
        g = g_ref[pl.ds(r0, CHUNK), :].astype(F32)
        o_ref[pl.ds(r0, CHUNK), :] = (on * nw * _silu(g)).astype(o_ref.dtype)
        return carry

    lax.fori_loop(0, ncb, chunk_body, 0, unroll=4)


def _hgrn(z1, z2, lb, norm_w, tb):
    t = z1.shape[0]
    tril = jnp.tril(jnp.ones((CHUNK, CHUNK), F32)).astype(BF16)
    full = lambda shape: pl.BlockSpec(shape, lambda i: (0,) * len(shape))
    return pl.pallas_call(
        functools.partial(_hgrn_kernel, tb=tb),
        grid=(t // tb,),
        in_specs=[pl.BlockSpec((tb, WIDTH), lambda i: (i, Z1_QH)),
                  pl.BlockSpec((tb, WIDTH), lambda i: (i, 0)),
                  pl.BlockSpec((tb, WIDTH), lambda i: (i, Z1_IH)),
                  pl.BlockSpec((tb, WIDTH), lambda i: (i, Z1_GH)),
                  full((1, WIDTH)), full((1, WIDTH)), full((CHUNK, CHUNK))],
        out_specs=pl.BlockSpec((tb, WIDTH), lambda i: (i, 0)),
        out_shape=jax.ShapeDtypeStruct((t, WIDTH), F32),
        scratch_shapes=[pltpu.VMEM((HEADS, HEAD_DIM, HEAD_DIM), F32)],
        compiler_params=_cparams(("arbitrary",)),
        name="hgrn2",
    )(z1, z2, z1, z1, lb.reshape(1, WIDTH).astype(F32), norm_w.reshape(1, WIDTH).astype(F32), tril)


S5_PB = 2 * S5_GROUP
S5_ROWS = S5_CHUNK * S5_PB
S5_SLAB = LANES // S5_PB


def _s5_toeplitz_kernel(k_ref, m_ref):
    krow = k_ref[0]
    lane = lax.broadcasted_iota(jnp.int32, krow.shape, 1)
    for s in range(S5_CHUNK):
        blk = krow if s == 0 else jnp.where(lane >= s * S5_PB, pltpu.roll(krow, s * S5_PB, 1), 0.0)
        m_ref[0, s * S5_PB:(s + 1) * S5_PB, :] = blk.astype(m_ref.dtype)


def _s5_toeplitz(krow):
    return pl.pallas_call(
        _s5_toeplitz_kernel,
        grid=(S5_PAIRS,),
        in_specs=[pl.BlockSpec((1, S5_PB, S5_ROWS), lambda j: (j, 0, 0))],
        out_specs=pl.BlockSpec((1, S5_ROWS, S5_ROWS), lambda j: (j, 0, 0)),
        out_shape=jax.ShapeDtypeStruct((S5_PAIRS, S5_ROWS, S5_ROWS), BF16),
        compiler_params=_cparams(("parallel",)),
        name="s5_toeplitz",
    )(krow)


def _s5_tables(lam_re, lam_im, log_dt, b_re, b_im, c_re, c_im):
    ln = S5_CHUNK
    lr = jnp.minimum(lam_re.astype(F32), S5_MAX_REAL)
    li = lam_im.astype(F32)
    dt = jnp.exp(log_dt.astype(F32))[:, None]
    mag = jnp.exp(lr * dt)
    ab_re = mag * jnp.cos(li * dt)
    ab_im = mag * jnp.sin(li * dt)
    nr = ab_re - 1.0
    den = lr * lr + li * li
    cr = (nr * lr + ab_im * li) / den
    ci = (ab_im * lr - nr * li) / den
    bb_re = cr[..., None] * b_re - ci[..., None] * b_im
    bb_im = cr[..., None] * b_im + ci[..., None] * b_re
    tau = jnp.arange(ln + 1, dtype=F32)[:, None, None]
    pm = jnp.exp(lr * dt * tau)
    pr = pm * jnp.cos(li * dt * tau)
    pi = pm * jnp.sin(li * dt * tau)
    def block_diag(a):
        a0, a1 = a[0::2], a[1::2]
        z = jnp.zeros_like(a0)
        return jnp.concatenate([jnp.concatenate([a0, z], axis=2), jnp.concatenate([z, a1], axis=2)], axis=1)

    def pair_pow(p):
        return p.reshape(p.shape[0], S5_PAIRS, 2 * S5_STATE).transpose(1, 0, 2)

    bbr = block_diag(bb_re.transpose(0, 2, 1))
    bbi = block_diag(bb_im.transpose(0, 2, 1))
    ccr = block_diag(c_re.transpose(0, 2, 1))
    cci = block_diag(c_im.transpose(0, 2, 1))
    col = jnp.arange(S5_ROWS)
    rep_t = (col[None, :] // S5_PB == jnp.arange(ln)[:, None]).astype(F32)
    rep_c = (col[None, :] % S5_PB == jnp.arange(S5_PB)[:, None]).astype(F32)
    lanes_c = lambda a: jnp.einsum('jrq,qc->jrc', a, rep_c, precision=HIGHEST)
    lanes_t = lambda p: jnp.einsum('jtr,tc->jrc', pair_pow(p), rep_t, precision=HIGHEST)
    ccr_l, cci_l = lanes_c(ccr), lanes_c(cci)

    def out_tables(p_r, p_i):
        pr_l, pi_l = lanes_t(p_r), lanes_t(p_i)
        return ccr_l * pr_l - cci_l * pi_l, -(ccr_l * pi_l + cci_l * pr_l)

    fr_pair, fi_pair = out_tables(pr[1:], pi[1:])
    fk_re, fk_im = out_tables(pr[:ln], pi[:ln])
    krow = jnp.einsum('jrk,jkc->jrc', jnp.concatenate([bbr, bbi], axis=2), jnp.concatenate([fk_re, fk_im], axis=1),
                      precision=HIGHEST)
    m = _s5_toeplitz(krow)
    rows_of = lambda a: jnp.broadcast_to(a[:, :, None, :], (S5_PAIRS, a.shape[1], S5_PB, a.shape[2])).reshape(
        S5_PAIRS, a.shape[1] * S5_PB, a.shape[2])
    pe_r = rows_of(pair_pow(pr[ln - 1 - jnp.arange(ln)]))
    pe_i = rows_of(pair_pow(pi[ln - 1 - jnp.arange(ln)]))
    tile_s = lambda a: jnp.broadcast_to(a[:, None], (S5_PAIRS, ln) + a.shape[1:]).reshape(S5_PAIRS, S5_ROWS, a.shape[2])
    bbr_s, bbi_s = tile_s(bbr), tile_s(bbi)
    e_pair = jnp.concatenate([pe_r * bbr_s - pe_i * bbi_s, pe_r * bbi_s + pe_i * bbr_s], axis=2)
    al = jnp.zeros((S5_PAIRS, SUBLANES, LANES), F32)
    al = al.at[:, 0, :].set(pr[ln].reshape(S5_PAIRS, 2 * S5_STATE))
    al = al.at[:, 1, :].set(pi[ln].reshape(S5_PAIRS, 2 * S5_STATE))
    return m.astype(BF16), e_pair.astype(BF16), fr_pair.astype(BF16), fi_pair.astype(BF16), al


def _s5_kernel(u_ref, m_ref, e_ref, fr_ref, fi_ref, al_ref, o_ref, ustage, ystage, sloc, xpr, xpi, *, nc):
    jj = pl.program_id(1)

    @pl.when(jj == 0)
    def _():
        for b in range(S5_ROWS // LANES):
            slabs = [u_ref[pl.ds(S5_SLAB * b + a, nc, stride=S5_CHUNK), :] for a in range(S5_SLAB)]
            for q in range(S5_SLAB):
                ustage[q, :, LANES * b:LANES * (b + 1)] = jnp.concatenate(
                    [sl[:, S5_PB * q:S5_PB * (q + 1)] for sl in slabs], axis=1).astype(BF16)

    u = ustage[jj]
    sloc[...] = jnp.dot(u, e_ref[0], preferred_element_type=F32)
    ar = al_ref[0, 0:1, :]
    ai = al_ref[0, 1:2, :]

    row = lax.broadcasted_iota(jnp.int32, (SUBLANES, LANES), 0)

    def body(ti, carry):
        xr, xi = carry
        r0 = pl.multiple_of(ti * SUBLANES, SUBLANES)
        sr_t = sloc[pl.ds(r0, SUBLANES), 0:LANES]
        si_t = sloc[pl.ds(r0, SUBLANES), LANES:2 * LANES]
        pr_t = jnp.zeros((SUBLANES, LANES), F32)
        pi_t = jnp.zeros((SUBLANES, LANES), F32)
        for r in range(SUBLANES):
            pr_t = jnp.where(row == r, xr, pr_t)
            pi_t = jnp.where(row == r, xi, pi_t)
            xr, xi = ar * xr - ai * xi + sr_t[r:r + 1, :], ar * xi + ai * xr + si_t[r:r + 1, :]
        xpr[pl.ds(r0, SUBLANES), :] = pr_t
        xpi[pl.ds(r0, SUBLANES), :] = pi_t
        return xr, xi

    zero = jnp.zeros((1, LANES), F32)
    lax.fori_loop(0, nc // SUBLANES, body, (zero, zero))
    ystage[jj] = (jnp.dot(xpr[...].astype(BF16), fr_ref[0], preferred_element_type=F32)
                  + jnp.dot(xpi[...].astype(BF16), fi_ref[0], preferred_element_type=F32)
                  + jnp.dot(u, m_ref[0], preferred_element_type=F32))

    @pl.when(jj == S5_SLAB - 1)
    def _():
        for t in range(S5_CHUNK):
            o_ref[pl.ds(t, nc, stride=S5_CHUNK), :] = jnp.concatenate(
                [ystage[q, :, S5_PB * t:S5_PB * (t + 1)] for q in range(S5_SLAB)], axis=1)


def _s5(z2, lam_re, lam_im, log_dt, b_re, b_im, c_re, c_im):
    t = z2.shape[0]
    nc = t // S5_CHUNK
    m, e_pair, fr_pair, fi_pair, al = _s5_tables(lam_re, lam_im, log_dt, b_re, b_im, c_re, c_im)
    pair = lambda a, b: pl.BlockSpec((1, a, b), lambda k, jj: (S5_SLAB * k + jj, 0, 0))
    return pl.pallas_call(
        functools.partial(_s5_kernel, nc=nc),
        grid=(WIDTH // LANES, S5_SLAB),
        in_specs=[pl.BlockSpec((t, LANES), lambda k, jj: (0, Z2_US * WIDTH // LANES + k), pipeline_mode=pl.Buffered(1)),
                  pair(S5_ROWS, S5_ROWS), pair(S5_ROWS, 2 * LANES), pair(LANES, S5_ROWS), pair(LANES, S5_ROWS),
                  pair(SUBLANES, LANES)],
        out_specs=pl.BlockSpec((t, LANES), lambda k, jj: (0, k), pipeline_mode=pl.Buffered(1)),
        out_shape=jax.ShapeDtypeStruct((t, WIDTH), F32),
        scratch_shapes=[pltpu.VMEM((S5_SLAB, nc, S5_ROWS), BF16), pltpu.VMEM((S5_SLAB, nc, S5_ROWS), F32),
                        pltpu.VMEM((nc, 2 * LANES), F32), pltpu.VMEM((nc, LANES), F32), pltpu.VMEM((nc, LANES), F32)],
        compiler_params=_cparams(("arbitrary", "arbitrary")),
        name="s5_scan",
    )(z2, m, e_pair, fr_pair, fi_pair, al)


def _layer_norm(x, g, b):
    mu = jnp.mean(x, axis=-1, keepdims=True)
    xc = x - mu
    var = jnp.mean(xc * xc, axis=-1, keepdims=True)
    return xc * lax.rsqrt(var + LN_EPS) * g + b


def _merge_rows(sl, x_ref, g0_ref, g1_ref, g2_ref, ym_ref, ys_ref, us_ref, yh_ref, wm_ref, ws_ref, wh_ref, wglu_ref,
                d_ref, wo_ref, lg_ref, lb_ref, rw_ref, rb_ref, o_ref):
    ys = _gelu_tanh(ys_ref[sl, :] + d_ref[...] * us_ref[sl, :].astype(F32))
    ys = ys * _sigmoid(_dot(ys, wglu_ref[...]))
    merged = (_sigmoid(g0_ref[sl, :].astype(F32)) * _dot(ym_ref[sl, :], wm_ref[...])
              + _sigmoid(g1_ref[sl, :].astype(F32)) * _dot(ys, ws_ref[...])
              + _sigmoid(g2_ref[sl, :].astype(F32)) * _dot(yh_ref[sl, :], wh_ref[...]))
    x1 = _layer_norm(DN_ALPHA * x_ref[sl, :] + _dot(merged, wo_ref[...]), lg_ref[...], lb_ref[...])
    o_ref[sl, 0:D_MODEL] = x1

    xh = x1.astype(BF16)
    xl = (x1 - xh.astype(F32)).astype(BF16)
    logits = (jnp.dot(xh, rw_ref[0], preferred_element_type=F32) + jnp.dot(xl, rw_ref[0], preferred_element_type=F32)
              + jnp.dot(xh, rw_ref[1], preferred_element_type=F32))
    s0 = _sigmoid(logits)
    sb0 = s0 + rb_ref[...]
    shift = lambda a, j: a if j == 0 else pltpu.roll(a, LANES - N_GROUPS * j, 1)
    s = [shift(s0, j) for j in range(EXPERTS_PER_GROUP)]
    sb = [shift(sb0, j) for j in range(EXPERTS_PER_GROUP)]
    hi1, lo1 = jnp.maximum(sb[0], sb[1]), jnp.minimum(sb[0], sb[1])
    hi2, lo2 = jnp.maximum(sb[2], sb[3]), jnp.minimum(sb[2], sb[3])
    top2 = jnp.maximum(hi1, hi2) + jnp.maximum(jnp.minimum(hi1, hi2), jnp.maximum(lo1, lo2))
    lane = lax.broadcasted_iota(jnp.int32, top2.shape, 1)
    top2 = jnp.where(lane < N_GROUPS, top2, -jnp.inf)
    gmax = jnp.max(top2, axis=1, keepdims=True)
    g_idx = jnp.min(jnp.where(top2 == gmax, lane, LANES), axis=1, keepdims=True)
    sel = lane == g_idx
    v = [jnp.sum(jnp.where(sel, sb[j], 0.0), axis=1, keepdims=True) for j in range(EXPERTS_PER_GROUP)]
    sv = [jnp.sum(jnp.where(sel, s[j], 0.0), axis=1, keepdims=True) for j in range(EXPERTS_PER_GROUP)]

    def first_max(vals):
        m = jnp.maximum(jnp.maximum(vals[0], vals[1]), jnp.maximum(vals[2], vals[3]))
        return jnp.where(vals[0] == m, 0, jnp.where(vals[1] == m, 1, jnp.where(vals[2] == m, 2, 3)))

    e1 = first_max(v)
    e2 = first_max([jnp.where(e1 == j, -jnp.inf, v[j]) for j in range(EXPERTS_PER_GROUP)])
    s1 = sum(jnp.where(e1 == j, sv[j], 0.0) for j in range(EXPERTS_PER_GROUP))
    s2 = sum(jnp.where(e2 == j, sv[j], 0.0) for j in range(EXPERTS_PER_GROUP))
    tot = s1 + s2
    meta = jnp.where(lane == 0, g_idx.astype(F32), 0.0)
    for j in range(EXPERTS_PER_GROUP):
        cw = jnp.where(e1 == j, s1 / tot, 0.0) + jnp.where(e2 == j, s2 / tot, 0.0)
        meta = jnp.where(lane == 1 + j, cw, meta)
    o_ref[sl, D_MODEL:D_MODEL + META] = meta


MERGE_ROWS = 256


def _merge_kernel(*refs):
    for lo in range(0, refs[0].shape[0], MERGE_ROWS):
        _merge_rows(slice(lo, lo + MERGE_ROWS), *refs)


def _merge(x, z1, z2, ym, ys, yh, wm, ws, wh, wglu, d, wo, lg, lb, router_w, router_bias, tm):
    t = x.shape[0]
    rw = router_w.astype(F32).reshape(D_MODEL, N_GROUPS, EXPERTS_PER_GROUP).transpose(0, 2, 1)
    rw = jnp.pad(rw.reshape(D_MODEL, N_EXPERTS), ((0, 0), (0, LANES - N_EXPERTS)))
    rw_hi = rw.astype(BF16)
    rw = jnp.stack([rw_hi, (rw - rw_hi.astype(F32)).astype(BF16)])
    rb = jnp.pad(router_bias.astype(F32).reshape(N_GROUPS, EXPERTS_PER_GROUP).T.reshape(1, N_EXPERTS),
                 ((0, 0), (0, LANES - N_EXPERTS)))
    full = lambda shape: pl.BlockSpec(shape, lambda i: (0,) * len(shape))
    row = lambda w: pl.BlockSpec((tm, w), lambda i: (i, 0))
    return pl.pallas_call(
        _merge_kernel,
        grid=(t // tm,),
        in_specs=[row(D_MODEL),
                  pl.BlockSpec((tm, D_MODEL), lambda i: (i, 0)),
                  pl.BlockSpec((tm, D_MODEL), lambda i: (i, 1)),
                  pl.BlockSpec((tm, D_MODEL), lambda i: (i, 2)),
                  row(WIDTH), row(WIDTH), pl.BlockSpec((tm, WIDTH), lambda i: (i, Z2_US)), row(WIDTH),
                  full((WIDTH, D_MODEL)), full((WIDTH, D_MODEL)), full((WIDTH, D_MODEL)), full((WIDTH, WIDTH)),
                  full((1, WIDTH)), full((D_MODEL, D_MODEL)), full((1, D_MODEL)), full((1, D_MODEL)),
                  full((2, D_MODEL, LANES)), full((1, LANES))],
        out_specs=pl.BlockSpec((tm, D_MODEL + META), lambda i: (i, 0)),
        out_shape=jax.ShapeDtypeStruct((t, D_MODEL + META), F32),
        compiler_params=_cparams(("parallel",)),
        name="merge_router",
    )(x, z1, z1, z1, ym, ys, z2, yh, wm.astype(BF16), ws.astype(BF16), wh.astype(BF16), wglu.astype(BF16),
      d.reshape(1, WIDTH).astype(F32), wo.astype(BF16), lg.reshape(1, D_MODEL).astype(F32),
      lb.reshape(1, D_MODEL).astype(F32), rw, rb)


def _start_row_gather(idx_ref, base, src_hbm, dst, sem, rows):
    for r in range(rows):
        pltpu.make_async_copy(src_hbm.at[pl.ds(idx_ref[base + r], 1)], dst.at[pl.ds(r, 1)], sem).start(priority=r % 2)


def _wait_row_gather(src_hbm, dst, sem, rows):
    pltpu.make_async_copy(src_hbm.at[pl.ds(0, rows)], dst, sem).wait()


def _dispatch_kernel(dst_ref, x_ref, init_hbm, o_hbm, sem, *, tm):
    del init_hbm
    base = pl.program_id(0) * tm
    for r in range(tm):
        pltpu.make_async_copy(x_ref.at[pl.ds(r, 1)], o_hbm.at[pl.ds(dst_ref[base + r], 1)], sem).start(priority=r % 2)
    pltpu.make_async_copy(x_ref, o_hbm.at[pl.ds(0, tm)], sem).wait()


def _moe_kernel(tg_ref, x_ref, wg_ref, wu_ref, wd_ref, o_ref, wg_s, wu_s, wd_s, *, tm, ntiles):
    i = pl.program_id(0)

    @pl.when(jnp.logical_or(i == 0, tg_ref[i] != tg_ref[jnp.maximum(i - 1, 0)]))
    def _():
        for e in range(EXPERTS_PER_GROUP):
            wg_s[e] = wg_ref[0, e].astype(BF16)
            wu_s[e] = wu_ref[0, e].astype(BF16)
            wd_s[e] = wd_ref[0, e].astype(BF16)

    used = i < tg_ref[ntiles]

    @pl.when(used)
    def _():
        xb = x_ref[:, 0:D_MODEL].astype(BF16)
        meta = x_ref[:, D_MODEL:D_MODEL + META]
        y = jnp.zeros((tm, D_MODEL), F32)
        for e in range(EXPERTS_PER_GROUP):
            hg = jnp.dot(xb, wg_s[e], preferred_element_type=F32)
            hu = jnp.dot(xb, wu_s[e], preferred_element_type=F32)
            hh = _silu(hg) * hu * _lane_col(meta, 1 + e)
            y = y + jnp.dot(hh.astype(BF16), wd_s[e], preferred_element_type=F32)
        o_ref[...] = y

    @pl.when(jnp.logical_not(used))
    def _():
        o_ref[...] = jnp.zeros_like(o_ref)


def _ln2_kernel(dst_ref, x_ref, y_hbm, lg_ref, lb_ref, o_ref, buf, sem, *, tm, ntiles):
    i = pl.program_id(0)
    slot = i % 2

    @pl.when(i == 0)
    def _():
        _start_row_gather(dst_ref, 0, y_hbm, buf.at[0], sem.at[0], tm)

    @pl.when(i + 1 < ntiles)
    def _():
        _start_row_gather(dst_ref, (i + 1) * tm, y_hbm, buf.at[1 - slot], sem.at[1 - slot], tm)

    _wait_row_gather(y_hbm, buf.at[slot], sem.at[slot], tm)
    o_ref[...] = _layer_norm(DN_ALPHA * x_ref[...] + buf[slot], lg_ref[...], lb_ref[...])


def _moe_ln(x1e, wg, wu, wd, layer, lg, lb, tm, tt):
    t = x1e.shape[0]
    ntiles = t // tm + N_GROUPS
    p = ntiles * tm
    key = x1e[:, D_MODEL].astype(jnp.int32)
    onehot = (key[:, None] == jnp.arange(N_GROUPS)[None, :]).astype(jnp.int32)
    csum = jnp.cumsum(onehot, axis=0)
    counts = csum[-1]
    rank = jnp.sum(onehot * csum, axis=1) - 1
    pcount = ((counts + tm - 1) // tm) * tm
    pend = jnp.cumsum(pcount)
    dest = (pend - pcount)[key] + rank
    dest = dest.astype(jnp.int32)
    tile_start = jnp.arange(ntiles, dtype=jnp.int32) * tm
    tile_group = jnp.minimum(jnp.sum((tile_start[:, None] >= pend[None, :]).astype(jnp.int32), axis=1),
                             N_GROUPS - 1)
    tile_group = jnp.concatenate([tile_group, (pend[-1:] // tm).astype(jnp.int32)])
    nt2 = t // tt
    x_sorted = pl.pallas_call(
        functools.partial(_dispatch_kernel, tm=tt),
        grid_spec=pltpu.PrefetchScalarGridSpec(
            num_scalar_prefetch=1,
            grid=(nt2,),
            in_specs=[pl.BlockSpec((tt, D_MODEL + META), lambda i, d: (i, 0)), pl.BlockSpec(memory_space=pl.ANY)],
            out_specs=pl.BlockSpec(memory_space=pl.ANY),
            scratch_shapes=[pltpu.SemaphoreType.DMA(())]),
        out_shape=jax.ShapeDtypeStruct((p, D_MODEL + META), F32),
        input_output_aliases={2: 0},
        compiler_params=_cparams(("arbitrary",)),
        name="moe_dispatch",
    )(dest, x1e, jnp.zeros((p, D_MODEL + META), F32))
    wspec = lambda a, b: pl.BlockSpec((1, EXPERTS_PER_GROUP, a, b), lambda i, tg: (layer, tg[i], 0, 0))
    wscr = lambda a, b: pltpu.VMEM((EXPERTS_PER_GROUP, a, b), BF16)
    y_sorted = pl.pallas_call(
        functools.partial(_moe_kernel, tm=tm, ntiles=ntiles),
        grid_spec=pltpu.PrefetchScalarGridSpec(
            num_scalar_prefetch=1,
            grid=(ntiles,),
            in_specs=[pl.BlockSpec((tm, D_MODEL + META), lambda i, tg: (i, 0)), wspec(D_MODEL, D_EXPERT),
                      wspec(D_MODEL, D_EXPERT), wspec(D_EXPERT, D_MODEL)],
            out_specs=pl.BlockSpec((tm, D_MODEL), lambda i, tg: (i, 0)),
            scratch_shapes=[wscr(D_MODEL, D_EXPERT), wscr(D_MODEL, D_EXPERT), wscr(D_EXPERT, D_MODEL)]),
        out_shape=jax.ShapeDtypeStruct((p, D_MODEL), F32),
        compiler_params=_cparams(("arbitrary",)),
        name="moe_experts",
    )(tile_group, x_sorted, wg, wu, wd)
    return pl.pallas_call(
        functools.partial(_ln2_kernel, tm=tt, ntiles=nt2),
        grid_spec=pltpu.PrefetchScalarGridSpec(
            num_scalar_prefetch=1,
            grid=(nt2,),
            in_specs=[pl.BlockSpec((tt, D_MODEL), lambda i, d: (i, 0)), pl.BlockSpec(memory_space=pl.ANY),
                      pl.BlockSpec((1, D_MODEL), lambda i, d: (0, 0)), pl.BlockSpec((1, D_MODEL), lambda i, d: (0, 0))],
            out_specs=pl.BlockSpec((tt, D_MODEL), lambda i, d: (i, 0)),
            scratch_shapes=[pltpu.VMEM((2, tt, D_MODEL), F32), pltpu.SemaphoreType.DMA((2,))]),
        out_shape=jax.ShapeDtypeStruct((t, D_MODEL), F32),
        compiler_params=_cparams(("arbitrary",)),
        name="moe_combine_ln",
    )(dest.astype(jnp.int32), x1e, y_sorted, lg.reshape(1, D_MODEL).astype(F32), lb.reshape(1, D_MODEL).astype(F32))


def _layer(x, l, p, lb_l, cfg):
    t = x.shape[0]
    w1, w2, wt = _split_w_in(p['w_in'][l])
    xb = x.astype(BF16)
    z1 = _matmul(xb, w1, Z1_DTYPE, cfg['tm_in'], cfg['tn_in'])
    z2, gt = _inproj_gate(xb, w2, wt, cfg['tm_in'])
    gt3 = gt.reshape(SUBLANES, t // CHUNK, CHUNK).transpose(1, 0, 2)
    ym = _mlstm(z1, z2, gt3, p['mlstm_conv_w'][l], p['mlstm_conv_b'][l], p['mlstm_wq'][l], p['mlstm_wk'][l],
                p['mlstm_wv'][l], p['mlstm_b_i'][l], p['mlstm_b_f'][l], p['mlstm_skip'][l], cfg['tb'])
    ys = _s5(z2, p['s5_lambda_re'][l], p['s5_lambda_im'][l], p['s5_log_dt'][l], p['s5_b_re'][l], p['s5_b_im'][l],
             p['s5_c_re'][l], p['s5_c_im'][l])
    yh = _hgrn(z1, z2, lb_l, p['hgrn_norm_w'][l], cfg['tb'])
    x1e = _merge(x, z1, z2, ym, ys, yh, p['w_branch_mlstm'][l], p['w_branch_s5'][l], p['w_branch_hgrn'][l],
                 p['s5_w_glu'][l], p['s5_d'][l], p['w_out'][l], p['ln1_g'][l], p['ln1_b'][l], p['router_w'],
                 p['router_bias'], cfg['tm_merge'])
    return _moe_ln(x1e, p['exp_w_gate'], p['exp_w_up'], p['exp_w_down'], l, p['ln2_g'][l], p['ln2_b'][l],
                   cfg['tm_moe'], cfg['tt_moe'])


_CFG = dict(tm_in=1024, tn_in=Z1_COLS // 4, tb=1024, tm_merge=2 * MERGE_ROWS, tm_moe=256, tt_moe=512)


def kernel(x, w_in, mlstm_conv_w, mlstm_conv_b, mlstm_wq, mlstm_wk, mlstm_wv, mlstm_b_i, mlstm_b_f, mlstm_skip, s5_lambda_re, s5_lambda_im, s5_log_dt, s5_b_re, s5_b_im, s5_c_re, s5_c_im, s5_d, s5_w_glu, hgrn_lower_bounds, hgrn_norm_w, w_branch_mlstm, w_branch_s5, w_branch_hgrn, w_out, ln1_g, ln1_b, ln2_g, ln2_b, router_w, router_bias, exp_w_gate, exp_w_up, exp_w_down):
    p = dict(w_in=w_in, mlstm_conv_w=mlstm_conv_w, mlstm_conv_b=mlstm_conv_b, mlstm_wq=mlstm_wq, mlstm_wk=mlstm_wk,
             mlstm_wv=mlstm_wv, mlstm_b_i=mlstm_b_i, mlstm_b_f=mlstm_b_f, mlstm_skip=mlstm_skip,
             s5_lambda_re=s5_lambda_re, s5_lambda_im=s5_lambda_im, s5_log_dt=s5_log_dt, s5_b_re=s5_b_re,
             s5_b_im=s5_b_im, s5_c_re=s5_c_re, s5_c_im=s5_c_im, s5_d=s5_d, s5_w_glu=s5_w_glu,
             hgrn_norm_w=hgrn_norm_w, w_branch_mlstm=w_branch_mlstm, w_branch_s5=w_branch_s5,
             w_branch_hgrn=w_branch_hgrn, w_out=w_out, ln1_g=ln1_g, ln1_b=ln1_b, ln2_g=ln2_g, ln2_b=ln2_b,
             router_w=router_w, router_bias=router_bias, exp_w_gate=exp_w_gate, exp_w_up=exp_w_up,
             exp_w_down=exp_w_down)
    lb_cum = jnp.cumsum(jax.nn.softmax(hgrn_lower_bounds.astype(F32), axis=0), axis=0)
    lb_layers = lb_cum - lb_cum[0]
    bsz, seq, d = x.shape
    h = x.reshape(bsz * seq, d)
    for l in range(DEPTH):
        h = _layer(h, l, p, lb_layers[l], _CFG)
    return h.reshape(bsz, seq, d)
```

```python
import functools
import math

import jax
import jax.numpy as jnp
from jax import lax
from jax.experimental import pallas as pl
from jax.experimental.pallas import tpu as pltpu

F32 = jnp.float32
BF16 = jnp.bfloat16
HIGHEST = lax.Precision.HIGHEST

D_MODEL = 1024
DEPTH = 2
HEADS = 4
HEAD_DIM = 128
WIDTH = HEADS * HEAD_DIM
MLSTM_CONV = 4
CHUNK = 64
S5_GROUP = 16
S5_GROUPS = 32
S5_STATE = 64
S5_PAIRS = S5_GROUPS // 2
S5_CHUNK = 32
S5_MAX_REAL = -1e-4
N_EXPERTS = 32
N_GROUPS = 8
EXPERTS_PER_GROUP = 4
D_EXPERT = 256
DN_ALPHA = (2 * DEPTH) ** 0.25
LN_EPS = 1e-5
NORM_EPS = 1e-6

LANES = 128
SUBLANES = 8
META = LANES
V7X_VMEM_BYTES = 64 * 1024 * 1024
VMEM_LIMIT = V7X_VMEM_BYTES * 7 // 8

Z1_DTYPE = BF16
Z1_GATE, Z1_XM, Z1_OM, Z1_QH, Z1_IH, Z1_GH = 0, 6, 7, 8, 9, 10
Z1_COLS = 11 * WIDTH
Z2_FH, Z2_US = 0, 1
Z2_COLS = 2 * WIDTH + LANES


def _cparams(sem):
    return pltpu.CompilerParams(dimension_semantics=sem, vmem_limit_bytes=VMEM_LIMIT)


def _sigmoid(x):
    return 0.5 * (1.0 + jnp.tanh(0.5 * x))


def _silu(x):
    return x * _sigmoid(x)


def _log_sigmoid(x):
    return jnp.minimum(x, 0.0) - jnp.log(1.0 + jnp.exp(-jnp.abs(x)))


def _gelu_tanh(x):
    return 0.5 * x * (1.0 + jnp.tanh(math.sqrt(2.0 / math.pi) * (x + 0.044715 * (x * x * x))))


def _cumsum_rows(x):
    n = x.shape[0]
    row = lax.broadcasted_iota(jnp.int32, x.shape, 0)
    s = 1
    while s < n:
        x = x + jnp.where(row >= s, pltpu.roll(x, s, 0), 0.0)
        s *= 2
    return x


def _lane_col(x, idx):
    lane = lax.broadcasted_iota(jnp.int32, x.shape, 1)
    return jnp.sum(jnp.where(lane == idx, x, 0.0), axis=1, keepdims=True)


def _dot(a, b):
    return jnp.dot(a.astype(BF16), b.astype(BF16), preferred_element_type=F32)


def _dot_nt(a, b):
    return lax.dot_general(a.astype(BF16), b.astype(BF16), (((1,), (1,)), ((), ())), preferred_element_type=F32)


def _dot_tn(a, b):
    return lax.dot_general(a.astype(BF16), b.astype(BF16), (((0,), (0,)), ((), ())), preferred_element_type=F32)


def _mm_kernel(x_ref, w_ref, o_ref):
    o_ref[...] = jnp.dot(x_ref[...], w_ref[...], preferred_element_type=F32).astype(o_ref.dtype)


def _matmul(x, w, out_dtype, tm, tn):
    m, k = x.shape
    n = w.shape[1]
    return pl.pallas_call(
        _mm_kernel,
        grid=(m // tm, n // tn),
        in_specs=[pl.BlockSpec((tm, k), lambda i, j: (i, 0)), pl.BlockSpec((k, tn), lambda i, j: (0, j))],
        out_specs=pl.BlockSpec((tm, tn), lambda i, j: (i, j)),
        out_shape=jax.ShapeDtypeStruct((m, n), out_dtype),
        compiler_params=_cparams(("parallel", "parallel")),
        name="in_proj_wide",
    )(x, w)


def _inproj_gate_kernel(x_ref, w_ref, wt_ref, o_ref, gt_ref):
    x = x_ref[...]
    o_ref[...] = jnp.dot(x, w_ref[...], preferred_element_type=F32)
    gt_ref[...] = lax.dot_general(wt_ref[...], x, (((1,), (1,)), ((), ())), preferred_element_type=F32)


def _inproj_gate(x, w2, wt, tm):
    m, k = x.shape
    return pl.pallas_call(
        _inproj_gate_kernel,
        grid=(m // tm,),
        in_specs=[pl.BlockSpec((tm, k), lambda i: (i, 0)),
                  pl.BlockSpec((k, Z2_COLS), lambda i: (0, 0)),
                  pl.BlockSpec((SUBLANES, k), lambda i: (0, 0))],
        out_specs=[pl.BlockSpec((tm, Z2_COLS), lambda i: (i, 0)), pl.BlockSpec((SUBLANES, tm), lambda i: (0, i))],
        out_shape=[jax.ShapeDtypeStruct((m, Z2_COLS), F32), jax.ShapeDtypeStruct((SUBLANES, m), F32)],
        compiler_params=_cparams(("parallel",)),
        name="in_proj_gates",
    )(x, w2, wt)


def _split_w_in(w):
    offs, o = [], 0
    for s in (WIDTH, WIDTH, HEADS, HEADS, WIDTH, WIDTH, WIDTH, WIDTH, WIDTH, 3 * D_MODEL):
        offs.append((o, o + s))
        o += s
    seg = [w[:, a:b] for a, b in offs]
    xm, om, im, fm, us, qh, fh, ih, gh, gate = seg
    w1 = jnp.concatenate([gate, xm, om, qh, ih, gh], axis=1).astype(BF16)
    pad = jnp.zeros((w.shape[0], LANES - 2 * HEADS), w.dtype)
    w2 = jnp.concatenate([fh, us, im, fm, pad], axis=1).astype(BF16)
    wt = jnp.concatenate([im, fm], axis=1).T.astype(BF16)
    return w1, w2, wt


def _mlstm_kernel(xm_ref, om_ref, gc_ref, gr_ref, cw_ref, cb_ref, wq_ref, wk_ref, wv_ref, bcol_ref, brow_ref,
                  skip_ref, triu_ref, o_ref, xpad, q_s, k_s, v_s, xc_s, h_s, gcol_s, grow_s, brow_s, ct_s, n_s, m_s,
                  *, tb):
    ncb = tb // CHUNK

    @pl.when(pl.program_id(0) == 0)
    def _():
        xpad[0:SUBLANES, :] = jnp.zeros((SUBLANES, WIDTH), F32)
        ct_s[...] = jnp.zeros_like(ct_s)
        n_s[...] = jnp.zeros_like(n_s)
        m_s[...] = jnp.zeros_like(m_s)

    xm = xm_ref[...].astype(F32)
    xpad[SUBLANES:SUBLANES + tb, :] = xm
    cw = cw_ref[...]
    conv = cb_ref[...] + cw[3:4, :] * xm
    for d in range(1, MLSTM_CONV):
        conv = conv + cw[3 - d:4 - d, :] * xpad[SUBLANES - d:SUBLANES - d + tb, :]
    xpad[0:SUBLANES, :] = xpad[tb:tb + SUBLANES, :]
    xc = _silu(conv)
    xc_s[...] = xc
    for h in range(HEADS):
        sl = slice(h * HEAD_DIM, (h + 1) * HEAD_DIM)
        xch = xc[:, sl].astype(BF16)
        q_s[:, sl] = jnp.dot(xch, wq_ref[h], preferred_element_type=F32) * (HEAD_DIM ** -0.5)
        k_s[:, sl] = jnp.dot(xch, wk_ref[h], preferred_element_type=F32)
        v_s[:, sl] = jnp.dot(xm[:, sl].astype(BF16), wv_ref[h], preferred_element_type=F32)

    gc = gc_ref[...] + bcol_ref[...]
    lane = lax.broadcasted_iota(jnp.int32, gc.shape, 1)
    gcol_s[...] = jnp.where(lane < HEADS, gc, _log_sigmoid(gc))
    gr = gr_ref[...] + brow_ref[...]
    sub = lax.broadcasted_iota(jnp.int32, gr.shape, 1)
    gr = jnp.where(sub < HEADS, gr, _log_sigmoid(gr)).reshape(ncb * SUBLANES, CHUNK)
    grow_s[...] = gr
    brow_s[...] = jnp.dot(gr, triu_ref[...], precision=HIGHEST, preferred_element_type=F32)

    rowi = lax.broadcasted_iota(jnp.int32, (CHUNK, CHUNK), 0)
    coli = lax.broadcasted_iota(jnp.int32, (CHUNK, CHUNK), 1)
    causal = rowi >= coli

    def chunk_body(c, carry):
        r0 = pl.multiple_of(c * CHUNK, CHUNK)
        g0 = pl.multiple_of(c * SUBLANES, SUBLANES)
        gcol = gcol_s[pl.ds(r0, CHUNK), :]
        bcol = _cumsum_rows(gcol)
        grow = grow_s[pl.ds(g0, SUBLANES), :]
        brow = brow_s[pl.ds(g0, SUBLANES), :]
        for h in range(HEADS):
            sl = slice(h * HEAD_DIM, (h + 1) * HEAD_DIM)
            li_c = _lane_col(gcol, h)
            b_c = _lane_col(bcol, HEADS + h)
            li_r = grow[h:h + 1, :]
            b_r = brow[HEADS + h:HEADS + h + 1, :]
            b_tot = b_c[CHUNK - 1:CHUNK, :]
            a_c = b_tot - b_c + li_c
            a_max = jnp.max(a_c, axis=0, keepdims=True)
            q = q_s[pl.ds(r0, CHUNK), sl]
            k = k_s[pl.ds(r0, CHUNK), sl]
            v = v_s[pl.ds(r0, CHUNK), sl]
            wk = jnp.exp(a_c - a_max) * k
            c_loc_t = _dot_tn(v, wk)
            n_loc = jnp.sum(wk, axis=0, keepdims=True)
            ct_prev = ct_s[h]
            n_prev = n_s[h:h + 1, :]
            m_prev = m_s[h:h + 1, 0:1]
            d_mat = jnp.where(causal, b_c - b_r + li_r, -jnp.inf)
            m_inter = b_c + m_prev
            m_j = jnp.maximum(m_inter, jnp.max(d_mat, axis=1, keepdims=True))
            sc = _dot_nt(q, k) * jnp.exp(d_mat - m_j)
            g_inter = jnp.exp(m_inter - m_j)
            num = g_inter * _dot_nt(q, ct_prev) + _dot(sc, v)
            den = g_inter * jnp.sum(q * n_prev, axis=1, keepdims=True) + jnp.sum(sc, axis=1, keepdims=True)
            h_s[pl.ds(r0, CHUNK), sl] = num / jnp.maximum(jnp.abs(den), jnp.exp(-m_j))
            m_new = jnp.maximum(b_tot + m_prev, a_max)
            g_old = jnp.exp(b_tot + m_prev - m_new)
            g_loc = jnp.exp(a_max - m_new)
            ct_s[h] = g_old * ct_prev + g_loc * c_loc_t
            n_s[h:h + 1, :] = g_old * n_prev + g_loc * n_loc
            m_s[h:h + 1, :] = jnp.broadcast_to(m_new, (1, LANES))
        return carry

    lax.fori_loop(0, ncb, chunk_body, 0, unroll=4)
    o_ref[...] = (_sigmoid(om_ref[...].astype(F32)) * h_s[...] + skip_ref[...] * xc_s[...]).astype(o_ref.dtype)


def _mlstm(z1, z2, gt3, conv_w, conv_b, wq, wk, wv, b_i, b_f, skip, tb):
    t = z1.shape[0]
    ncb = tb // CHUNK
    bias = jnp.concatenate([b_i, b_f]).astype(F32)
    bcol = jnp.zeros((1, LANES), F32).at[0, :2 * HEADS].set(bias)
    brow = bias.reshape(2 * HEADS, 1)
    triu = jnp.triu(jnp.ones((CHUNK, CHUNK), F32))
    full = lambda shape: pl.BlockSpec(shape, lambda i: (0,) * len(shape))
    return pl.pallas_call(
        functools.partial(_mlstm_kernel, tb=tb),
        grid=(t // tb,),
        in_specs=[pl.BlockSpec((tb, WIDTH), lambda i: (i, Z1_XM)),
                  pl.BlockSpec((tb, WIDTH), lambda i: (i, Z1_OM)),
                  pl.BlockSpec((tb, LANES), lambda i: (i, 2 * WIDTH // LANES)),
                  pl.BlockSpec((ncb, SUBLANES, CHUNK), lambda i: (i, 0, 0)),
                  full((MLSTM_CONV, WIDTH)), full((1, WIDTH)),
                  full((HEADS, HEAD_DIM, HEAD_DIM)), full((HEADS, HEAD_DIM, HEAD_DIM)),
                  full((HEADS, HEAD_DIM, HEAD_DIM)),
                  full((1, LANES)), full((2 * HEADS, 1)), full((1, WIDTH)), full((CHUNK, CHUNK))],
        out_specs=pl.BlockSpec((tb, WIDTH), lambda i: (i, 0)),
        out_shape=jax.ShapeDtypeStruct((t, WIDTH), F32),
        scratch_shapes=[pltpu.VMEM((tb + SUBLANES, WIDTH), F32)]
        + [pltpu.VMEM((tb, WIDTH), F32) for _ in range(5)]
        + [pltpu.VMEM((tb, LANES), F32),
           pltpu.VMEM((ncb * SUBLANES, CHUNK), F32), pltpu.VMEM((ncb * SUBLANES, CHUNK), F32),
           pltpu.VMEM((HEADS, HEAD_DIM, HEAD_DIM), F32), pltpu.VMEM((SUBLANES, LANES), F32),
           pltpu.VMEM((SUBLANES, LANES), F32)],
        compiler_params=_cparams(("arbitrary",)),
        name="mlstm",
    )(z1, z1, z2, gt3, conv_w.astype(F32), conv_b.reshape(1, WIDTH).astype(F32), wq.astype(BF16), wk.astype(BF16),
      wv.astype(BF16), bcol, brow, skip.reshape(1, WIDTH).astype(F32), triu)


def _hgrn_kernel(q_ref, f_ref, i_ref, g_ref, lb_ref, nw_ref, tri_ref, o_ref, st_s, *, tb):
    ncb = tb // CHUNK

    @pl.when(pl.program_id(0) == 0)
    def _():
        st_s[...] = jnp.zeros_like(st_s)

    lb = lb_ref[...]
    nw = nw_ref[...]
    tri = tri_ref[...]
    rowi = lax.broadcasted_iota(jnp.int32, (CHUNK, WIDTH), 0)
    sr = lax.broadcasted_iota(jnp.int32, (CHUNK, CHUNK), 0)
    sc = lax.broadcasted_iota(jnp.int32, (CHUNK, CHUNK), 1)
    halves = [1 << p for p in range(CHUNK.bit_length() - 1)]
    upper = {m: (rowi & m) != 0 for m in halves}
    same_blk = {m: (sr // (2 * m)) == (sc // (2 * m)) for m in halves}

    def chunk_body(c, carry):
        r0 = pl.multiple_of(c * CHUNK, CHUNK)
        f = lb + (1.0 - lb) * _sigmoid(f_ref[pl.ds(r0, CHUNK), :].astype(F32))
        k = 1.0 - f
        q = _silu(q_ref[pl.ds(r0, CHUNK), :].astype(F32))
        v = i_ref[pl.ds(r0, CHUNK), :].astype(F32)
        lf = jnp.log(f)
        hi = lf.astype(BF16)
        r1 = lf - hi.astype(F32)
        mid = r1.astype(BF16)
        lo = (r1 - mid.astype(F32)).astype(BF16)
        b = (jnp.dot(tri, hi, preferred_element_type=F32) + jnp.dot(tri, mid, preferred_element_type=F32)
             + jnp.dot(tri, lo, preferred_element_type=F32))
        qs, ks = {}, {}
        for m in halves:
            if m == 1:
                t = jnp.where(upper[m], q * f, k)
            else:
                nblk = CHUNK // (2 * m)
                r = jnp.concatenate([jnp.broadcast_to(b[2 * m * j + m - 1:2 * m * j + m, :], (2 * m, WIDTH))
                                     for j in range(nblk)], axis=0)
                t = jnp.where(upper[m], q, k) * jnp.exp(-jnp.abs(b - r))
            qs[m] = jnp.where(upper[m], t, 0.0).astype(BF16)
            ks[m] = jnp.where(upper[m], 0.0, t).astype(BF16)
        b_last = b[CHUNK - 1:CHUNK, :]
        qe = q * jnp.exp(b)
        kd = k * jnp.exp(b_last - b)
        e_last = jnp.exp(b_last)
        outs = []
        for h in range(HEADS):
            sl = slice(h * HEAD_DIM, (h + 1) * HEAD_DIM)
            st = st_s[h]
            a = jnp.where(sr == sc, _dot_nt(q[:, sl], k[:, sl]), 0.0)
            for m in halves:
                a = a + jnp.where(same_blk[m], _dot_nt(qs[m][:, sl], ks[m][:, sl]), 0.0)
            oh = _dot_nt(qe[:, sl], st) + _dot(a, v[:, sl])
            st_s[h] = e_last[:, sl] * st + _dot_tn(v[:, sl], kd[:, sl])
            ms = jnp.sum(oh * oh, axis=1, keepdims=True) * (1.0 / HEAD_DIM)
            outs.append(oh * lax.rsqrt(ms + NORM_EPS))
        on = jnp.concatenate(outs, axis=1)
        g = g_ref[pl.ds(r0, CHUNK), :].astype(F32)
        o_ref[pl.ds(r0, CHUNK), :] = (on * nw * _silu(g)).astype(o_ref.dtype)
        return carry

    lax.fori_loop(0, ncb, chunk_body, 0, unroll=4)


def _hgrn(z1, z2, lb, norm_w, tb):
    t = z1.shape[0]
    tril = jnp.tril(jnp.ones((CHUNK, CHUNK), F32)).astype(BF16)
    full = lambda shape: pl.BlockSpec(shape, lambda i: (0,) * len(shape))
    return pl.pallas_call(
        functools.partial(_hgrn_kernel, tb=tb),
        grid=(t // tb,),
        in_specs=[pl.BlockSpec((tb, WIDTH), lambda i: (i, Z1_QH)),
                  pl.BlockSpec((tb, WIDTH), lambda i: (i, 0)),
                  pl.BlockSpec((tb, WIDTH), lambda i: (i, Z1_IH)),
                  pl.BlockSpec((tb, WIDTH), lambda i: (i, Z1_GH)),
                  full((1, WIDTH)), full((1, WIDTH)), full((CHUNK, CHUNK))],
        out_specs=pl.BlockSpec((tb, WIDTH), lambda i: (i, 0)),
        out_shape=jax.ShapeDtypeStruct((t, WIDTH), F32),
        scratch_shapes=[pltpu.VMEM((HEADS, HEAD_DIM, HEAD_DIM), F32)],
        compiler_params=_cparams(("arbitrary",)),
        name="hgrn2",
    )(z1, z2, z1, z1, lb.reshape(1, WIDTH).astype(F32), norm_w.reshape(1, WIDTH).astype(F32), tril)


S5_PB = 2 * S5_GROUP
S5_ROWS = S5_CHUNK * S5_PB
S5_SLAB = LANES // S5_PB


def _s5_toeplitz_kernel(k_ref, m_ref):
    krow = k_ref[0]
    lane = lax.broadcasted_iota(jnp.int32, krow.shape, 1)
    for s in range(S5_CHUNK):
        blk = krow if s == 0 else jnp.where(lane >= s * S5_PB, pltpu.roll(krow, s * S5_PB, 1), 0.0)
        m_ref[0, s * S5_PB:(s + 1) * S5_PB, :] = blk.astype(m_ref.dtype)


def _s5_toeplitz(krow):
    return pl.pallas_call(
        _s5_toeplitz_kernel,
        grid=(S5_PAIRS,),
        in_specs=[pl.BlockSpec((1, S5_PB, S5_ROWS), lambda j: (j, 0, 0))],
        out_specs=pl.BlockSpec((1, S5_ROWS, S5_ROWS), lambda j: (j, 0, 0)),
        out_shape=jax.ShapeDtypeStruct((S5_PAIRS, S5_ROWS, S5_ROWS), BF16),
        compiler_params=_cparams(("parallel",)),
        name="s5_toeplitz",
    )(krow)


def _s5_tables(lam_re, lam_im, log_dt, b_re, b_im, c_re, c_im):
    ln = S5_CHUNK
    lr = jnp.minimum(lam_re.astype(F32), S5_MAX_REAL)
    li = lam_im.astype(F32)
    dt = jnp.exp(log_dt.astype(F32))[:, None]
    mag = jnp.exp(lr * dt)
    ab_re = mag * jnp.cos(li * dt)
    ab_im = mag * jnp.sin(li * dt)
    nr = ab_re - 1.0
    den = lr * lr + li * li
    cr = (nr * lr + ab_im * li) / den
    ci = (ab_im * lr - nr * li) / den
    bb_re = cr[..., None] * b_re - ci[..., None] * b_im
    bb_im = cr[..., None] * b_im + ci[..., None] * b_re
    tau = jnp.arange(ln + 1, dtype=F32)[:, None, None]
    pm = jnp.exp(lr * dt * tau)
    pr = pm * jnp.cos(li * dt * tau)
    pi = pm * jnp.sin(li * dt * tau)
    def block_diag(a):
        a0, a1 = a[0::2], a[1::2]
        z = jnp.zeros_like(a0)
        return jnp.concatenate([jnp.concatenate([a0, z], axis=2), jnp.concatenate([z, a1], axis=2)], axis=1)

    def pair_pow(p):
        return p.reshape(p.shape[0], S5_PAIRS, 2 * S5_STATE).transpose(1, 0, 2)

    bbr = block_diag(bb_re.transpose(0, 2, 1))
    bbi = block_diag(bb_im.transpose(0, 2, 1))
    ccr = block_diag(c_re.transpose(0, 2, 1))
    cci = block_diag(c_im.transpose(0, 2, 1))
    col = jnp.arange(S5_ROWS)
    rep_t = (col[None, :] // S5_PB == jnp.arange(ln)[:, None]).astype(F32)
    rep_c = (col[None, :] % S5_PB == jnp.arange(S5_PB)[:, None]).astype(F32)
    lanes_c = lambda a: jnp.einsum('jrq,qc->jrc', a, rep_c, precision=HIGHEST)
    lanes_t = lambda p: jnp.einsum('jtr,tc->jrc', pair_pow(p), rep_t, precision=HIGHEST)
    ccr_l, cci_l = lanes_c(ccr), lanes_c(cci)

    def out_tables(p_r, p_i):
        pr_l, pi_l = lanes_t(p_r), lanes_t(p_i)
        return ccr_l * pr_l - cci_l * pi_l, -(ccr_l * pi_l + cci_l * pr_l)

    fr_pair, fi_pair = out_tables(pr[1:], pi[1:])
    fk_re, fk_im = out_tables(pr[:ln], pi[:ln])
    krow = jnp.einsum('jrk,jkc->jrc', jnp.concatenate([bbr, bbi], axis=2), jnp.concatenate([fk_re, fk_im], axis=1),
                      precision=HIGHEST)
    m = _s5_toeplitz(krow)
    rows_of = lambda a: jnp.broadcast_to(a[:, :, None, :], (S5_PAIRS, a.shape[1], S5_PB, a.shape[2])).reshape(
        S5_PAIRS, a.shape[1] * S5_PB, a.shape[2])
    pe_r = rows_of(pair_pow(pr[ln - 1 - jnp.arange(ln)]))
    pe_i = rows_of(pair_pow(pi[ln - 1 - jnp.arange(ln)]))
    tile_s = lambda a: jnp.broadcast_to(a[:, None], (S5_PAIRS, ln) + a.shape[1:]).reshape(S5_PAIRS, S5_ROWS, a.shape[2])
    bbr_s, bbi_s = tile_s(bbr), tile_s(bbi)
    e_pair = jnp.concatenate([pe_r * bbr_s - pe_i * bbi_s, pe_r * bbi_s + pe_i * bbr_s], axis=2)
    al = jnp.zeros((S5_PAIRS, SUBLANES, LANES), F32)
    al = al.at[:, 0, :].set(pr[ln].reshape(S5_PAIRS, 2 * S5_STATE))
    al = al.at[:, 1, :].set(pi[ln].reshape(S5_PAIRS, 2 * S5_STATE))
    return m.astype(BF16), e_pair.astype(BF16), fr_pair.astype(BF16), fi_pair.astype(BF16), al


def _s5_kernel(u_ref, m_ref, e_ref, fr_ref, fi_ref, al_ref, o_ref, ustage, ystage, sloc, xpr, xpi, *, nc):
    jj = pl.program_id(1)

    @pl.when(jj == 0)
    def _():
        for b in range(S5_ROWS // LANES):
            slabs = [u_ref[pl.ds(S5_SLAB * b + a, nc, stride=S5_CHUNK), :] for a in range(S5_SLAB)]
            for q in range(S5_SLAB):
                ustage[q, :, LANES * b:LANES * (b + 1)] = jnp.concatenate(
                    [sl[:, S5_PB * q:S5_PB * (q + 1)] for sl in slabs], axis=1).astype(BF16)

    u = ustage[jj]
    sloc[...] = jnp.dot(u, e_ref[0], preferred_element_type=F32)
    ar = al_ref[0, 0:1, :]
    ai = al_ref[0, 1:2, :]

    row = lax.broadcasted_iota(jnp.int32, (SUBLANES, LANES), 0)

    def body(ti, carry):
        xr, xi = carry
        r0 = pl.multiple_of(ti * SUBLANES, SUBLANES)
        sr_t = sloc[pl.ds(r0, SUBLANES), 0:LANES]
        si_t = sloc[pl.ds(r0, SUBLANES), LANES:2 * LANES]
        pr_t = jnp.zeros((SUBLANES, LANES), F32)
        pi_t = jnp.zeros((SUBLANES, LANES), F32)
        for r in range(SUBLANES):
            pr_t = jnp.where(row == r, xr, pr_t)
            pi_t = jnp.where(row == r, xi, pi_t)
            xr, xi = ar * xr - ai * xi + sr_t[r:r + 1, :], ar * xi + ai * xr + si_t[r:r + 1, :]
        xpr[pl.ds(r0, SUBLANES), :] = pr_t
        xpi[pl.ds(r0, SUBLANES), :] = pi_t
        return xr, xi

    zero = jnp.zeros((1, LANES), F32)
    lax.fori_loop(0, nc // SUBLANES, body, (zero, zero))
    ystage[jj] = (jnp.dot(xpr[...].astype(BF16), fr_ref[0], preferred_element_type=F32)
                  + jnp.dot(xpi[...].astype(BF16), fi_ref[0], preferred_element_type=F32)
                  + jnp.dot(u, m_ref[0], preferred_element_type=F32))

    @pl.when(jj == S5_SLAB - 1)
    def _():
        for t in range(S5_CHUNK):
            o_ref[pl.ds(t, nc, stride=S5_CHUNK), :] = jnp.concatenate(
                [ystage[q, :, S5_PB * t:S5_PB * (t + 1)] for q in range(S5_SLAB)], axis=1)


def _s5(z2, lam_re, lam_im, log_dt, b_re, b_im, c_re, c_im):
    t = z2.shape[0]
    nc = t // S5_CHUNK
    m, e_pair, fr_pair, fi_pair, al = _s5_tables(lam_re, lam_im, log_dt, b_re, b_im, c_re, c_im)
    pair = lambda a, b: pl.BlockSpec((1, a, b), lambda k, jj: (S5_SLAB * k + jj, 0, 0))
    return pl.pallas_call(
        functools.partial(_s5_kernel, nc=nc),
        grid=(WIDTH // LANES, S5_SLAB),
        in_specs=[pl.BlockSpec((t, LANES), lambda k, jj: (0, Z2_US * WIDTH // LANES + k), pipeline_mode=pl.Buffered(1)),
                  pair(S5_ROWS, S5_ROWS), pair(S5_ROWS, 2 * LANES), pair(LANES, S5_ROWS), pair(LANES, S5_ROWS),
                  pair(SUBLANES, LANES)],
        out_specs=pl.BlockSpec((t, LANES), lambda k, jj: (0, k), pipeline_mode=pl.Buffered(1)),
        out_shape=jax.ShapeDtypeStruct((t, WIDTH), F32),
        scratch_shapes=[pltpu.VMEM((S5_SLAB, nc, S5_ROWS), BF16), pltpu.VMEM((S5_SLAB, nc, S5_ROWS), F32),
                        pltpu.VMEM((nc, 2 * LANES), F32), pltpu.VMEM((nc, LANES), F32), pltpu.VMEM((nc, LANES), F32)],
        compiler_params=_cparams(("arbitrary", "arbitrary")),
        name="s5_scan",
    )(z2, m, e_pair, fr_pair, fi_pair, al)


def _layer_norm(x, g, b):
    mu = jnp.mean(x, axis=-1, keepdims=True)
    xc = x - mu
    var = jnp.mean(xc * xc, axis=-1, keepdims=True)
    return xc * lax.rsqrt(var + LN_EPS) * g + b


def _merge_rows(sl, x_ref, g0_ref, g1_ref, g2_ref, ym_ref, ys_ref, us_ref, yh_ref, wm_ref, ws_ref, wh_ref, wglu_ref,
                d_ref, wo_ref, lg_ref, lb_ref, rw_ref, rb_ref, o_ref):
    ys = _gelu_tanh(ys_ref[sl, :] + d_ref[...] * us_ref[sl, :].astype(F32))
    ys = ys * _sigmoid(_dot(ys, wglu_ref[...]))
    merged = (_sigmoid(g0_ref[sl, :].astype(F32)) * _dot(ym_ref[sl, :], wm_ref[...])
              + _sigmoid(g1_ref[sl, :].astype(F32)) * _dot(ys, ws_ref[...])
              + _sigmoid(g2_ref[sl, :].astype(F32)) * _dot(yh_ref[sl, :], wh_ref[...]))
    x1 = _layer_norm(DN_ALPHA * x_ref[sl, :] + _dot(merged, wo_ref[...]), lg_ref[...], lb_ref[...])
    o_ref[sl, 0:D_MODEL] = x1

    xh = x1.astype(BF16)
    xl = (x1 - xh.astype(F32)).astype(BF16)
    logits = (jnp.dot(xh, rw_ref[0], preferred_element_type=F32) + jnp.dot(xl, rw_ref[0], preferred_element_type=F32)
              + jnp.dot(xh, rw_ref[1], preferred_element_type=F32))
    s0 = _sigmoid(logits)
    sb0 = s0 + rb_ref[...]
    shift = lambda a, j: a if j == 0 else pltpu.roll(a, LANES - N_GROUPS * j, 1)
    s = [shift(s0, j) for j in range(EXPERTS_PER_GROUP)]
    sb = [shift(sb0, j) for j in range(EXPERTS_PER_GROUP)]
    hi1, lo1 = jnp.maximum(sb[0], sb[1]), jnp.minimum(sb[0], sb[1])
    hi2, lo2 = jnp.maximum(sb[2], sb[3]), jnp.minimum(sb[2], sb[3])
    top2 = jnp.maximum(hi1, hi2) + jnp.maximum(jnp.minimum(hi1, hi2), jnp.maximum(lo1, lo2))
    lane = lax.broadcasted_iota(jnp.int32, top2.shape, 1)
    top2 = jnp.where(lane < N_GROUPS, top2, -jnp.inf)
    gmax = jnp.max(top2, axis=1, keepdims=True)
    g_idx = jnp.min(jnp.where(top2 == gmax, lane, LANES), axis=1, keepdims=True)
    sel = lane == g_idx
    v = [jnp.sum(jnp.where(sel, sb[j], 0.0), axis=1, keepdims=True) for j in range(EXPERTS_PER_GROUP)]
    sv = [jnp.sum(jnp.where(sel, s[j], 0.0), axis=1, keepdims=True) for j in range(EXPERTS_PER_GROUP)]

    def first_max(vals):
        m = jnp.maximum(jnp.maximum(vals[0], vals[1]), jnp.maximum(vals[2], vals[3]))
        return jnp.where(vals[0] == m, 0, jnp.where(vals[1] == m, 1, jnp.where(vals[2] == m, 2, 3)))

    e1 = first_max(v)
    e2 = first_max([jnp.where(e1 == j, -jnp.inf, v[j]) for j in range(EXPERTS_PER_GROUP)])
    s1 = sum(jnp.where(e1 == j, sv[j], 0.0) for j in range(EXPERTS_PER_GROUP))
    s2 = sum(jnp.where(e2 == j, sv[j], 0.0) for j in range(EXPERTS_PER_GROUP))
    tot = s1 + s2
    meta = jnp.where(lane == 0, g_idx.astype(F32), 0.0)
    for j in range(EXPERTS_PER_GROUP):
        cw = jnp.where(e1 == j, s1 / tot, 0.0) + jnp.where(e2 == j, s2 / tot, 0.0)
        meta = jnp.where(lane == 1 + j, cw, meta)
    o_ref[sl, D_MODEL:D_MODEL + META] = meta


MERGE_ROWS = 256


def _merge_kernel(*refs):
    for lo in range(0, refs[0].shape[0], MERGE_ROWS):
        _merge_rows(slice(lo, lo + MERGE_ROWS), *refs)


def _merge(x, z1, z2, ym, ys, yh, wm, ws, wh, wglu, d, wo, lg, lb, router_w, router_bias, tm):
    t = x.shape[0]
    rw = router_w.astype(F32).reshape(D_MODEL, N_GROUPS, EXPERTS_PER_GROUP).transpose(0, 2, 1)
    rw = jnp.pad(rw.reshape(D_MODEL, N_EXPERTS), ((0, 0), (0, LANES - N_EXPERTS)))
    rw_hi = rw.astype(BF16)
    rw = jnp.stack([rw_hi, (rw - rw_hi.astype(F32)).astype(BF16)])
    rb = jnp.pad(router_bias.astype(F32).reshape(N_GROUPS, EXPERTS_PER_GROUP).T.reshape(1, N_EXPERTS),
                 ((0, 0), (0, LANES - N_EXPERTS)))
    full = lambda shape: pl.BlockSpec(shape, lambda i: (0,) * len(shape))
    row = lambda w: pl.BlockSpec((tm, w), lambda i: (i, 0))
    return pl.pallas_call(
        _merge_kernel,
        grid=(t // tm,),
        in_specs=[row(D_MODEL),
                  pl.BlockSpec((tm, D_MODEL), lambda i: (i, 0)),
                  pl.BlockSpec((tm, D_MODEL), lambda i: (i, 1)),
                  pl.BlockSpec((tm, D_MODEL), lambda i: (i, 2)),
                  row(WIDTH), row(WIDTH), pl.BlockSpec((tm, WIDTH), lambda i: (i, Z2_US)), row(WIDTH),
                  full((WIDTH, D_MODEL)), full((WIDTH, D_MODEL)), full((WIDTH, D_MODEL)), full((WIDTH, WIDTH)),
                  full((1, WIDTH)), full((D_MODEL, D_MODEL)), full((1, D_MODEL)), full((1, D_MODEL)),
                  full((2, D_MODEL, LANES)), full((1, LANES))],
        out_specs=pl.BlockSpec((tm, D_MODEL + META), lambda i: (i, 0)),
        out_shape=jax.ShapeDtypeStruct((t, D_MODEL + META), F32),
        compiler_params=_cparams(("parallel",)),
        name="merge_router",
    )(x, z1, z1, z1, ym, ys, z2, yh, wm.astype(BF16), ws.astype(BF16), wh.astype(BF16), wglu.astype(BF16),
      d.reshape(1, WIDTH).astype(F32), wo.astype(BF16), lg.reshape(1, D_MODEL).astype(F32),
      lb.reshape(1, D_MODEL).astype(F32), rw, rb)


def _start_row_gather(idx_ref, base, src_hbm, dst, sem, rows):
    for r in range(rows):
        pltpu.make_async_copy(src_hbm.at[pl.ds(idx_ref[base + r], 1)], dst.at[pl.ds(r, 1)], sem).start(priority=r % 2)


def _wait_row_gather(src_hbm, dst, sem, rows):
    pltpu.make_async_copy(src_hbm.at[pl.ds(0, rows)], dst, sem).wait()


def _dispatch_kernel(dst_ref, x_ref, init_hbm, o_hbm, sem, *, tm):
    del init_hbm
    base = pl.program_id(0) * tm
    for r in range(tm):
        pltpu.make_async_copy(x_ref.at[pl.ds(r, 1)], o_hbm.at[pl.ds(dst_ref[base + r], 1)], sem).start(priority=r % 2)
    pltpu.make_async_copy(x_ref, o_hbm.at[pl.ds(0, tm)], sem).wait()


def _moe_kernel(tg_ref, x_ref, wg_ref, wu_ref, wd_ref, o_ref, wg_s, wu_s, wd_s, *, tm, ntiles):
    i = pl.program_id(0)

    @pl.when(jnp.logical_or(i == 0, tg_ref[i] != tg_ref[jnp.maximum(i - 1, 0)]))
    def _():
        for e in range(EXPERTS_PER_GROUP):
            wg_s[e] = wg_ref[0, e].astype(BF16)
            wu_s[e] = wu_ref[0, e].astype(BF16)
            wd_s[e] = wd_ref[0, e].astype(BF16)

    used = i < tg_ref[ntiles]

    @pl.when(used)
    def _():
        xb = x_ref[:, 0:D_MODEL].astype(BF16)
        meta = x_ref[:, D_MODEL:D_MODEL + META]
        y = jnp.zeros((tm, D_MODEL), F32)
        for e in range(EXPERTS_PER_GROUP):
            hg = jnp.dot(xb, wg_s[e], preferred_element_type=F32)
            hu = jnp.dot(xb, wu_s[e], preferred_element_type=F32)
            hh = _silu(hg) * hu * _lane_col(meta, 1 + e)
            y = y + jnp.dot(hh.astype(BF16), wd_s[e], preferred_element_type=F32)
        o_ref[...] = y

    @pl.when(jnp.logical_not(used))
    def _():
        o_ref[...] = jnp.zeros_like(o_ref)


def _ln2_kernel(dst_ref, x_ref, y_hbm, lg_ref, lb_ref, o_ref, buf, sem, *, tm, ntiles):
    i = pl.program_id(0)
    slot = i % 2

    @pl.when(i == 0)
    def _():
        _start_row_gather(dst_ref, 0, y_hbm, buf.at[0], sem.at[0], tm)

    @pl.when(i + 1 < ntiles)
    def _():
        _start_row_gather(dst_ref, (i + 1) * tm, y_hbm, buf.at[1 - slot], sem.at[1 - slot], tm)

    _wait_row_gather(y_hbm, buf.at[slot], sem.at[slot], tm)
    o_ref[...] = _layer_norm(DN_ALPHA * x_ref[...] + buf[slot], lg_ref[...], lb_ref[...])


def _moe_ln(x1e, wg, wu, wd, layer, lg, lb, tm, tt):
    t = x1e.shape[0]
    ntiles = t // tm + N_GROUPS
    p = ntiles * tm
    key = x1e[:, D_MODEL].astype(jnp.int32)
    onehot = (key[:, None] == jnp.arange(N_GROUPS)[None, :]).astype(jnp.int32)
    csum = jnp.cumsum(onehot, axis=0)
    counts = csum[-1]
    rank = jnp.sum(onehot * csum, axis=1) - 1
    pcount = ((counts + tm - 1) // tm) * tm
    pend = jnp.cumsum(pcount)
    dest = (pend - pcount)[key] + rank
    dest = dest.astype(jnp.int32)
    tile_start = jnp.arange(ntiles, dtype=jnp.int32) * tm
    tile_group = jnp.minimum(jnp.sum((tile_start[:, None] >= pend[None, :]).astype(jnp.int32), axis=1),
                             N_GROUPS - 1)
    tile_group = jnp.concatenate([tile_group, (pend[-1:] // tm).astype(jnp.int32)])
    nt2 = t // tt
    x_sorted = pl.pallas_call(
        functools.partial(_dispatch_kernel, tm=tt),
        grid_spec=pltpu.PrefetchScalarGridSpec(
            num_scalar_prefetch=1,
            grid=(nt2,),
            in_specs=[pl.BlockSpec((tt, D_MODEL + META), lambda i, d: (i, 0)), pl.BlockSpec(memory_space=pl.ANY)],
            out_specs=pl.BlockSpec(memory_space=pl.ANY),
            scratch_shapes=[pltpu.SemaphoreType.DMA(())]),
        out_shape=jax.ShapeDtypeStruct((p, D_MODEL + META), F32),
        input_output_aliases={2: 0},
        compiler_params=_cparams(("arbitrary",)),
        name="moe_dispatch",
    )(dest, x1e, jnp.zeros((p, D_MODEL + META), F32))
    wspec = lambda a, b: pl.BlockSpec((1, EXPERTS_PER_GROUP, a, b), lambda i, tg: (layer, tg[i], 0, 0))
    wscr = lambda a, b: pltpu.VMEM((EXPERTS_PER_GROUP, a, b), BF16)
    y_sorted = pl.pallas_call(
        functools.partial(_moe_kernel, tm=tm, ntiles=ntiles),
        grid_spec=pltpu.PrefetchScalarGridSpec(
            num_scalar_prefetch=1,
            grid=(ntiles,),
            in_specs=[pl.BlockSpec((tm, D_MODEL + META), lambda i, tg: (i, 0)), wspec(D_MODEL, D_EXPERT),
                      wspec(D_MODEL, D_EXPERT), wspec(D_EXPERT, D_MODEL)],
            out_specs=pl.BlockSpec((tm, D_MODEL), lambda i, tg: (i, 0)),
            scratch_shapes=[wscr(D_MODEL, D_EXPERT), wscr(D_MODEL, D_EXPERT), wscr(D_EXPERT, D_MODEL)]),
        out_shape=jax.ShapeDtypeStruct((p, D_MODEL), F32),
        compiler_params=_cparams(("arbitrary",)),
        name="moe_experts",
    )(tile_group, x_sorted, wg, wu, wd)
    return pl.pallas_call(
        functools.partial(_ln2_kernel, tm=tt, ntiles=nt2),
        grid_spec=pltpu.PrefetchScalarGridSpec(
            num_scalar_prefetch=1,
            grid=(nt2,),
            in_specs=[pl.BlockSpec((tt, D_MODEL), lambda i, d: (i, 0)), pl.BlockSpec(memory_space=pl.ANY),
                      pl.BlockSpec((1, D_MODEL), lambda i, d: (0, 0)), pl.BlockSpec((1, D_MODEL), lambda i, d: (0, 0))],
            out_specs=pl.BlockSpec((tt, D_MODEL), lambda i, d: (i, 0)),
            scratch_shapes=[pltpu.VMEM((2, tt, D_MODEL), F32), pltpu.SemaphoreType.DMA((2,))]),
        out_shape=jax.ShapeDtypeStruct((t, D_MODEL), F32),
        compiler_params=_cparams(("arbitrary",)),
        name="moe_combine_ln",
    )(dest.astype(jnp.int32), x1e, y_sorted, lg.reshape(1, D_MODEL).astype(F32), lb.reshape(1, D_MODEL).astype(F32))


def _layer(x, l, p, lb_l, cfg):
    t = x.shape[0]
    w1, w2, wt = _split_w_in(p['w_in'][l])
    xb = x.astype(BF16)
    z1 = _matmul(xb, w1, Z1_DTYPE, cfg['tm_in'], cfg['tn_in'])
    z2, gt = _inproj_gate(xb, w2, wt, cfg['tm_in'])
    gt3 = gt.reshape(SUBLANES, t // CHUNK, CHUNK).transpose(1, 0, 2)
    ym = _mlstm(z1, z2, gt3, p['mlstm_conv_w'][l], p['mlstm_conv_b'][l], p['mlstm_wq'][l], p['mlstm_wk'][l],
                p['mlstm_wv'][l], p['mlstm_b_i'][l], p['mlstm_b_f'][l], p['mlstm_skip'][l], cfg['tb'])
    ys = _s5(z2, p['s5_lambda_re'][l], p['s5_lambda_im'][l], p['s5_log_dt'][l], p['s5_b_re'][l], p['s5_b_im'][l],
             p['s5_c_re'][l], p['s5_c_im'][l])
    yh = _hgrn(z1, z2, lb_l, p['hgrn_norm_w'][l], cfg['tb'])
    x1e = _merge(x, z1, z2, ym, ys, yh, p['w_branch_mlstm'][l], p['w_branch_s5'][l], p['w_branch_hgrn'][l],
                 p['s5_w_glu'][l], p['s5_d'][l], p['w_out'][l], p['ln1_g'][l], p['ln1_b'][l], p['router_w'],
                 p['router_bias'], cfg['tm_merge'])
    return _moe_ln(x1e, p['exp_w_gate'], p['exp_w_up'], p['exp_w_down'], l, p['ln2_g'][l], p['ln2_b'][l],
                   cfg['tm_moe'], cfg['tt_moe'])


_CFG = dict(tm_in=1024, tn_in=Z1_COLS // 4, tb=1024, tm_merge=2 * MERGE_ROWS, tm_moe=256, tt_moe=1024)


def kernel(x, w_in, mlstm_conv_w, mlstm_conv_b, mlstm_wq, mlstm_wk, mlstm_wv, mlstm_b_i, mlstm_b_f, mlstm_skip, s5_lambda_re, s5_lambda_im, s5_log_dt, s5_b_re, s5_b_im, s5_c_re, s5_c_im, s5_d, s5_w_glu, hgrn_lower_bounds, hgrn_norm_w, w_branch_mlstm, w_branch_s5, w_branch_hgrn, w_out, ln1_g, ln1_b, ln2_g, ln2_b, router_w, router_bias, exp_w_gate, exp_w_up, exp_w_down):
    p = dict(w_in=w_in, mlstm_conv_w=mlstm_conv_w, mlstm_conv_b=mlstm_conv_b, mlstm_wq=mlstm_wq, mlstm_wk=mlstm_wk,
             mlstm_wv=mlstm_wv, mlstm_b_i=mlstm_b_i, mlstm_b_f=mlstm_b_f, mlstm_skip=mlstm_skip,
             s5_lambda_re=s5_lambda_re, s5_lambda_im=s5_lambda_im, s5_log_dt=s5_log_dt, s5_b_re=s5_b_re,
             s5_b_im=s5_b_im, s5_c_re=s5_c_re, s5_c_im=s5_c_im, s5_d=s5_d, s5_w_glu=s5_w_glu,
             hgrn_norm_w=hgrn_norm_w, w_branch_mlstm=w_branch_mlstm, w_branch_s5=w_branch_s5,
             w_branch_hgrn=w_branch_hgrn, w_out=w_out, ln1_g=ln1_g, ln1_b=ln1_b, ln2_g=ln2_g, ln2_b=ln2_b,
             router_w=router_w, router_bias=router_bias, exp_w_gate=exp_w_gate, exp_w_up=exp_w_up,
             exp_w_down=exp_w_down)
    lb_cum = jnp.cumsum(jax.nn.softmax(hgrn_lower_bounds.astype(F32), axis=0), axis=0)
    lb_layers = lb_cum - lb_cum[0]
    bsz, seq, d = x.shape
    h = x.reshape(bsz * seq, d)
    for l in range(DEPTH):
        h = _layer(h, l, p, lb_layers[l], _CFG)
    return h.reshape(bsz, seq, d)
```

```python
import functools
import math

import jax
import jax.numpy as jnp
from jax import lax
from jax.experimental import pallas as pl
from jax.experimental.pallas import tpu as pltpu

F32 = jnp.float32
BF16 = jnp.bfloat16
HIGHEST = lax.Precision.HIGHEST

D_MODEL = 1024
DEPTH = 2
HEADS = 4
HEAD_DIM = 128
WIDTH = HEADS * HEAD_DIM
MLSTM_CONV = 4
CHUNK = 64
S5_GROUP = 16
S5_GROUPS = 32
S5_STATE = 64
S5_PAIRS = S5_GROUPS // 2
S5_CHUNK = 32
S5_MAX_REAL = -1e-4
N_EXPERTS = 32
N_GROUPS = 8
EXPERTS_PER_GROUP = 4
D_EXPERT = 256
DN_ALPHA = (2 * DEPTH) ** 0.25
LN_EPS = 1e-5
NORM_EPS = 1e-6

LANES = 128
SUBLANES = 8
META = LANES
V7X_VMEM_BYTES = 64 * 1024 * 1024
VMEM_LIMIT = V7X_VMEM_BYTES * 7 // 8

Z1_DTYPE = BF16
Z1_GATE, Z1_XM, Z1_OM, Z1_QH, Z1_IH, Z1_GH = 0, 6, 7, 8, 9, 10
Z1_COLS = 11 * WIDTH
Z2_FH, Z2_US = 0, 1
Z2_COLS = 2 * WIDTH + LANES


def _cparams(sem):
    return pltpu.CompilerParams(dimension_semantics=sem, vmem_limit_bytes=VMEM_LIMIT)


def _sigmoid(x):
    return 0.5 * (1.0 + jnp.tanh(0.5 * x))


def _silu(x):
    return x * _sigmoid(x)


def _log_sigmoid(x):
    return jnp.minimum(x, 0.0) - jnp.log(1.0 + jnp.exp(-jnp.abs(x)))


def _gelu_tanh(x):
    return 0.5 * x * (1.0 + jnp.tanh(math.sqrt(2.0 / math.pi) * (x + 0.044715 * (x * x * x))))


def _cumsum_rows(x):
    n = x.shape[0]
    row = lax.broadcasted_iota(jnp.int32, x.shape, 0)
    s = 1
    while s < n:
        x = x + jnp.where(row >= s, pltpu.roll(x, s, 0), 0.0)
        s *= 2
    return x


def _lane_col(x, idx):
    lane = lax.broadcasted_iota(jnp.int32, x.shape, 1)
    return jnp.sum(jnp.where(lane == idx, x, 0.0), axis=1, keepdims=True)


def _dot(a, b):
    return jnp.dot(a.astype(BF16), b.astype(BF16), preferred_element_type=F32)


def _dot_nt(a, b):
    return lax.dot_general(a.astype(BF16), b.astype(BF16), (((1,), (1,)), ((), ())), preferred_element_type=F32)


def _dot_tn(a, b):
    return lax.dot_general(a.astype(BF16), b.astype(BF16), (((0,), (0,)), ((), ())), preferred_element_type=F32)


def _mm_kernel(x_ref, w_ref, o_ref):
    o_ref[...] = jnp.dot(x_ref[...], w_ref[...], preferred_element_type=F32).astype(o_ref.dtype)


def _matmul(x, w, out_dtype, tm, tn):
    m, k = x.shape
    n = w.shape[1]
    return pl.pallas_call(
        _mm_kernel,
        grid=(m // tm, n // tn),
        in_specs=[pl.BlockSpec((tm, k), lambda i, j: (i, 0)), pl.BlockSpec((k, tn), lambda i, j: (0, j))],
        out_specs=pl.BlockSpec((tm, tn), lambda i, j: (i, j)),
        out_shape=jax.ShapeDtypeStruct((m, n), out_dtype),
        compiler_params=_cparams(("parallel", "parallel")),
        name="in_proj_wide",
    )(x, w)


def _inproj_gate_kernel(x_ref, w_ref, wt_ref, o_ref, gt_ref):
    x = x_ref[...]
    o_ref[...] = jnp.dot(x, w_ref[...], preferred_element_type=F32)
    gt_ref[...] = lax.dot_general(wt_ref[...], x, (((1,), (1,)), ((), ())), preferred_element_type=F32)


def _inproj_gate(x, w2, wt, tm):
    m, k = x.shape
    return pl.pallas_call(
        _inproj_gate_kernel,
        grid=(m // tm,),
        in_specs=[pl.BlockSpec((tm, k), lambda i: (i, 0)),
                  pl.BlockSpec((k, Z2_COLS), lambda i: (0, 0)),
                  pl.BlockSpec((SUBLANES, k), lambda i: (0, 0))],
        out_specs=[pl.BlockSpec((tm, Z2_COLS), lambda i: (i, 0)), pl.BlockSpec((SUBLANES, tm), lambda i: (0, i))],
        out_shape=[jax.ShapeDtypeStruct((m, Z2_COLS), F32), jax.ShapeDtypeStruct((SUBLANES, m), F32)],
        compiler_params=_cparams(("parallel",)),
        name="in_proj_gates",
    )(x, w2, wt)


def _split_w_in(w):
    offs, o = [], 0
    for s in (WIDTH, WIDTH, HEADS, HEADS, WIDTH, WIDTH, WIDTH, WIDTH, WIDTH, 3 * D_MODEL):
        offs.append((o, o + s))
        o += s
    seg = [w[:, a:b] for a, b in offs]
    xm, om, im, fm, us, qh, fh, ih, gh, gate = seg
    w1 = jnp.concatenate([gate, xm, om, qh, ih, gh], axis=1).astype(BF16)
    pad = jnp.zeros((w.shape[0], LANES - 2 * HEADS), w.dtype)
    w2 = jnp.concatenate([fh, us, im, fm, pad], axis=1).astype(BF16)
    wt = jnp.concatenate([im, fm], axis=1).T.astype(BF16)
    return w1, w2, wt


def _mlstm_kernel(xm_ref, om_ref, gc_ref, gr_ref, cw_ref, cb_ref, wq_ref, wk_ref, wv_ref, bcol_ref, brow_ref,
                  skip_ref, triu_ref, o_ref, xpad, q_s, k_s, v_s, xc_s, h_s, gcol_s, grow_s, brow_s, ct_s, n_s, m_s,
                  *, tb):
    ncb = tb // CHUNK

    @pl.when(pl.program_id(0) == 0)
    def _():
        xpad[0:SUBLANES, :] = jnp.zeros((SUBLANES, WIDTH), F32)
        ct_s[...] = jnp.zeros_like(ct_s)
        n_s[...] = jnp.zeros_like(n_s)
        m_s[...] = jnp.zeros_like(m_s)

    xm = xm_ref[...].astype(F32)
    xpad[SUBLANES:SUBLANES + tb, :] = xm
    cw = cw_ref[...]
    conv = cb_ref[...] + cw[3:4, :] * xm
    for d in range(1, MLSTM_CONV):
        conv = conv + cw[3 - d:4 - d, :] * xpad[SUBLANES - d:SUBLANES - d + tb, :]
    xpad[0:SUBLANES, :] = xpad[tb:tb + SUBLANES, :]
    xc = _silu(conv)
    xc_s[...] = xc
    for h in range(HEADS):
        sl = slice(h * HEAD_DIM, (h + 1) * HEAD_DIM)
        xch = xc[:, sl].astype(BF16)
        q_s[:, sl] = jnp.dot(xch, wq_ref[h], preferred_element_type=F32) * (HEAD_DIM ** -0.5)
        k_s[:, sl] = jnp.dot(xch, wk_ref[h], preferred_element_type=F32)
        v_s[:, sl] = jnp.dot(xm[:, sl].astype(BF16), wv_ref[h], preferred_element_type=F32)

    gc = gc_ref[...] + bcol_ref[...]
    lane = lax.broadcasted_iota(jnp.int32, gc.shape, 1)
    gcol_s[...] = jnp.where(lane < HEADS, gc, _log_sigmoid(gc))
    gr = gr_ref[...] + brow_ref[...]
    sub = lax.broadcasted_iota(jnp.int32, gr.shape, 1)
    gr = jnp.where(sub < HEADS, gr, _log_sigmoid(gr)).reshape(ncb * SUBLANES, CHUNK)
    grow_s[...] = gr
    brow_s[...] = jnp.dot(gr, triu_ref[...], precision=HIGHEST, preferred_element_type=F32)

    rowi = lax.broadcasted_iota(jnp.int32, (CHUNK, CHUNK), 0)
    coli = lax.broadcasted_iota(jnp.int32, (CHUNK, CHUNK), 1)
    causal = rowi >= coli

    def chunk_body(c, carry):
        r0 = pl.multiple_of(c * CHUNK, CHUNK)
        g0 = pl.multiple_of(c * SUBLANES, SUBLANES)
        gcol = gcol_s[pl.ds(r0, CHUNK), :]
        bcol = _cumsum_rows(gcol)
        grow = grow_s[pl.ds(g0, SUBLANES), :]
        brow = brow_s[pl.ds(g0, SUBLANES), :]
        for h in range(HEADS):
            sl = slice(h * HEAD_DIM, (h + 1) * HEAD_DIM)
            li_c = _lane_col(gcol, h)
            b_c = _lane_col(bcol, HEADS + h)
            li_r = grow[h:h + 1, :]
            b_r = brow[HEADS + h:HEADS + h + 1, :]
            b_tot = b_c[CHUNK - 1:CHUNK, :]
            a_c = b_tot - b_c + li_c
            a_max = jnp.max(a_c, axis=0, keepdims=True)
            q = q_s[pl.ds(r0, CHUNK), sl]
            k = k_s[pl.ds(r0, CHUNK), sl]
            v = v_s[pl.ds(r0, CHUNK), sl]
            wk = jnp.exp(a_c - a_max) * k
            c_loc_t = _dot_tn(v, wk)
            n_loc = jnp.sum(wk, axis=0, keepdims=True)
            ct_prev = ct_s[h]
            n_prev = n_s[h:h + 1, :]
            m_prev = m_s[h:h + 1, 0:1]
            d_mat = jnp.where(causal, b_c - b_r + li_r, -jnp.inf)
            m_inter = b_c + m_prev
            m_j = jnp.maximum(m_inter, jnp.max(d_mat, axis=1, keepdims=True))
            sc = _dot_nt(q, k) * jnp.exp(d_mat - m_j)
            g_inter = jnp.exp(m_inter - m_j)
            num = g_inter * _dot_nt(q, ct_prev) + _dot(sc, v)
            den = g_inter * jnp.sum(q * n_prev, axis=1, keepdims=True) + jnp.sum(sc, axis=1, keepdims=True)
            h_s[pl.ds(r0, CHUNK), sl] = num / jnp.maximum(jnp.abs(den), jnp.exp(-m_j))
            m_new = jnp.maximum(b_tot + m_prev, a_max)
            g_old = jnp.exp(b_tot + m_prev - m_new)
            g_loc = jnp.exp(a_max - m_new)
            ct_s[h] = g_old * ct_prev + g_loc * c_loc_t
            n_s[h:h + 1, :] = g_old * n_prev + g_loc * n_loc
            m_s[h:h + 1, :] = jnp.broadcast_to(m_new, (1, LANES))
        return carry

    lax.fori_loop(0, ncb, chunk_body, 0, unroll=4)
    o_ref[...] = (_sigmoid(om_ref[...].astype(F32)) * h_s[...] + skip_ref[...] * xc_s[...]).astype(o_ref.dtype)


def _mlstm(z1, z2, gt3, conv_w, conv_b, wq, wk, wv, b_i, b_f, skip, tb):
    t = z1.shape[0]
    ncb = tb // CHUNK
    bias = jnp.concatenate([b_i, b_f]).astype(F32)
    bcol = jnp.zeros((1, LANES), F32).at[0, :2 * HEADS].set(bias)
    brow = bias.reshape(2 * HEADS, 1)
    triu = jnp.triu(jnp.ones((CHUNK, CHUNK), F32))
    full = lambda shape: pl.BlockSpec(shape, lambda i: (0,) * len(shape))
    return pl.pallas_call(
        functools.partial(_mlstm_kernel, tb=tb),
        grid=(t // tb,),
        in_specs=[pl.BlockSpec((tb, WIDTH), lambda i: (i, Z1_XM)),
                  pl.BlockSpec((tb, WIDTH), lambda i: (i, Z1_OM)),
                  pl.BlockSpec((tb, LANES), lambda i: (i, 2 * WIDTH // LANES)),
                  pl.BlockSpec((ncb, SUBLANES, CHUNK), lambda i: (i, 0, 0)),
                  full((MLSTM_CONV, WIDTH)), full((1, WIDTH)),
                  full((HEADS, HEAD_DIM, HEAD_DIM)), full((HEADS, HEAD_DIM, HEAD_DIM)),
                  full((HEADS, HEAD_DIM, HEAD_DIM)),
                  full((1, LANES)), full((2 * HEADS, 1)), full((1, WIDTH)), full((CHUNK, CHUNK))],
        out_specs=pl.BlockSpec((tb, WIDTH), lambda i: (i, 0)),
        out_shape=jax.ShapeDtypeStruct((t, WIDTH), F32),
        scratch_shapes=[pltpu.VMEM((tb + SUBLANES, WIDTH), F32)]
        + [pltpu.VMEM((tb, WIDTH), F32) for _ in range(5)]
        + [pltpu.VMEM((tb, LANES), F32),
           pltpu.VMEM((ncb * SUBLANES, CHUNK), F32), pltpu.VMEM((ncb * SUBLANES, CHUNK), F32),
           pltpu.VMEM((HEADS, HEAD_DIM, HEAD_DIM), F32), pltpu.VMEM((SUBLANES, LANES), F32),
           pltpu.VMEM((SUBLANES, LANES), F32)],
        compiler_params=_cparams(("arbitrary",)),
        name="mlstm",
    )(z1, z1, z2, gt3, conv_w.astype(F32), conv_b.reshape(1, WIDTH).astype(F32), wq.astype(BF16), wk.astype(BF16),
      wv.astype(BF16), bcol, brow, skip.reshape(1, WIDTH).astype(F32), triu)


def _hgrn_kernel(q_ref, f_ref, i_ref, g_ref, lb_ref, nw_ref, tri_ref, o_ref, st_s, *, tb):
    ncb = tb // CHUNK

    @pl.when(pl.program_id(0) == 0)
    def _():
        st_s[...] = jnp.zeros_like(st_s)

    lb = lb_ref[...]
    nw = nw_ref[...]
    tri = tri_ref[...]
    rowi = lax.broadcasted_iota(jnp.int32, (CHUNK, WIDTH), 0)
    sr = lax.broadcasted_iota(jnp.int32, (CHUNK, CHUNK), 0)
    sc = lax.broadcasted_iota(jnp.int32, (CHUNK, CHUNK), 1)
    halves = [1 << p for p in range(CHUNK.bit_length() - 1)]
    upper = {m: (rowi & m) != 0 for m in halves}
    same_blk = {m: (sr // (2 * m)) == (sc // (2 * m)) for m in halves}

    def chunk_body(c, carry):
        r0 = pl.multiple_of(c * CHUNK, CHUNK)
        f = lb + (1.0 - lb) * _sigmoid(f_ref[pl.ds(r0, CHUNK), :].astype(F32))
        k = 1.0 - f
        q = _silu(q_ref[pl.ds(r0, CHUNK), :].astype(F32))
        v = i_ref[pl.ds(r0, CHUNK), :].astype(F32)
        lf = jnp.log(f)
        hi = lf.astype(BF16)
        r1 = lf - hi.astype(F32)
        mid = r1.astype(BF16)
        lo = (r1 - mid.astype(F32)).astype(BF16)
        b = (jnp.dot(tri, hi, preferred_element_type=F32) + jnp.dot(tri, mid, preferred_element_type=F32)
             + jnp.dot(tri, lo, preferred_element_type=F32))
        qs, ks = {}, {}
        for m in halves:
            if m == 1:
                t = jnp.where(upper[m], q * f, k)
            else:
                nblk = CHUNK // (2 * m)
                r = jnp.concatenate([jnp.broadcast_to(b[2 * m * j + m - 1:2 * m * j + m, :], (2 * m, WIDTH))
                                     for j in range(nblk)], axis=0)
                t = jnp.where(upper[m], q, k) * jnp.exp(-jnp.abs(b - r))
            qs[m] = jnp.where(upper[m], t, 0.0).astype(BF16)
            ks[m] = jnp.where(upper[m], 0.0, t).astype(BF16)
        b_last = b[CHUNK - 1:CHUNK, :]
        qe = q * jnp.exp(b)
        kd = k * jnp.exp(b_last - b)
        e_last = jnp.exp(b_last)
        outs = []
        for h in range(HEADS):
            sl = slice(h * HEAD_DIM, (h + 1) * HEAD_DIM)
            st = st_s[h]
            a = jnp.where(sr == sc, _dot_nt(q[:, sl], k[:, sl]), 0.0)
            for m in halves:
                a = a + jnp.where(same_blk[m], _dot_nt(qs[m][:, sl], ks[m][:, sl]), 0.0)
            oh = _dot_nt(qe[:, sl], st) + _dot(a, v[:, sl])
            st_s[h] = e_last[:, sl] * st + _dot_tn(v[:, sl], kd[:, sl])
            ms = jnp.sum(oh * oh, axis=1, keepdims=True) * (1.0 / HEAD_DIM)
            outs.append(oh * lax.rsqrt(ms + NORM_EPS))
        on = jnp.concatenate(outs, axis=1)
        g = g_ref[pl.ds(r0, CHUNK), :].astype(F32)
        o_ref[pl.ds(r0, CHUNK), :] = (on * nw * _silu(g)).astype(o_ref.dtype)
        return carry

    lax.fori_loop(0, ncb, chunk_body, 0, unroll=4)


def _hgrn(z1, z2, lb, norm_w, tb):
    t = z1.shape[0]
    tril = jnp.tril(jnp.ones((CHUNK, CHUNK), F32)).astype(BF16)
    full = lambda shape: pl.BlockSpec(shape, lambda i: (0,) * len(shape))
    return pl.pallas_call(
        functools.partial(_hgrn_kernel, tb=tb),
        grid=(t // tb,),
        in_specs=[pl.BlockSpec((tb, WIDTH), lambda i: (i, Z1_QH)),
                  pl.BlockSpec((tb, WIDTH), lambda i: (i, 0)),
                  pl.BlockSpec((tb, WIDTH), lambda i: (i, Z1_IH)),
                  pl.BlockSpec((tb, WIDTH), lambda i: (i, Z1_GH)),
                  full((1, WIDTH)), full((1, WIDTH)), full((CHUNK, CHUNK))],
        out_specs=pl.BlockSpec((tb, WIDTH), lambda i: (i, 0)),
        out_shape=jax.ShapeDtypeStruct((t, WIDTH), F32),
        scratch_shapes=[pltpu.VMEM((HEADS, HEAD_DIM, HEAD_DIM), F32)],
        compiler_params=_cparams(("arbitrary",)),
        name="hgrn2",
    )(z1, z2, z1, z1, lb.reshape(1, WIDTH).astype(F32), norm_w.reshape(1, WIDTH).astype(F32), tril)


S5_PB = 2 * S5_GROUP
S5_ROWS = S5_CHUNK * S5_PB
S5_SLAB = LANES // S5_PB


def _s5_toeplitz_kernel(k_ref, m_ref):
    krow = k_ref[0]
    lane = lax.broadcasted_iota(jnp.int32, krow.shape, 1)
    for s in range(S5_CHUNK):
        blk = krow if s == 0 else jnp.where(lane >= s * S5_PB, pltpu.roll(krow, s * S5_PB, 1), 0.0)
        m_ref[0, s * S5_PB:(s + 1) * S5_PB, :] = blk.astype(m_ref.dtype)


def _s5_toeplitz(krow):
    return pl.pallas_call(
        _s5_toeplitz_kernel,
        grid=(S5_PAIRS,),
        in_specs=[pl.BlockSpec((1, S5_PB, S5_ROWS), lambda j: (j, 0, 0))],
        out_specs=pl.BlockSpec((1, S5_ROWS, S5_ROWS), lambda j: (j, 0, 0)),
        out_shape=jax.ShapeDtypeStruct((S5_PAIRS, S5_ROWS, S5_ROWS), BF16),
        compiler_params=_cparams(("parallel",)),
        name="s5_toeplitz",
    )(krow)


def _s5_tables(lam_re, lam_im, log_dt, b_re, b_im, c_re, c_im):
    ln = S5_CHUNK
    lr = jnp.minimum(lam_re.astype(F32), S5_MAX_REAL)
    li = lam_im.astype(F32)
    dt = jnp.exp(log_dt.astype(F32))[:, None]
    mag = jnp.exp(lr * dt)
    ab_re = mag * jnp.cos(li * dt)
    ab_im = mag * jnp.sin(li * dt)
    nr = ab_re - 1.0
    den = lr * lr + li * li
    cr = (nr * lr + ab_im * li) / den
    ci = (ab_im * lr - nr * li) / den
    bb_re = cr[..., None] * b_re - ci[..., None] * b_im
    bb_im = cr[..., None] * b_im + ci[..., None] * b_re
    tau = jnp.arange(ln + 1, dtype=F32)[:, None, None]
    pm = jnp.exp(lr * dt * tau)
    pr = pm * jnp.cos(li * dt * tau)
    pi = pm * jnp.sin(li * dt * tau)
    def block_diag(a):
        a0, a1 = a[0::2], a[1::2]
        z = jnp.zeros_like(a0)
        return jnp.concatenate([jnp.concatenate([a0, z], axis=2), jnp.concatenate([z, a1], axis=2)], axis=1)

    def pair_pow(p):
        return p.reshape(p.shape[0], S5_PAIRS, 2 * S5_STATE).transpose(1, 0, 2)

    bbr = block_diag(bb_re.transpose(0, 2, 1))
    bbi = block_diag(bb_im.transpose(0, 2, 1))
    ccr = block_diag(c_re.transpose(0, 2, 1))
    cci = block_diag(c_im.transpose(0, 2, 1))
    col = jnp.arange(S5_ROWS)
    rep_t = (col[None, :] // S5_PB == jnp.arange(ln)[:, None]).astype(F32)
    rep_c = (col[None, :] % S5_PB == jnp.arange(S5_PB)[:, None]).astype(F32)
    lanes_c = lambda a: jnp.einsum('jrq,qc->jrc', a, rep_c, precision=HIGHEST)
    lanes_t = lambda p: jnp.einsum('jtr,tc->jrc', pair_pow(p), rep_t, precision=HIGHEST)
    ccr_l, cci_l = lanes_c(ccr), lanes_c(cci)

    def out_tables(p_r, p_i):
        pr_l, pi_l = lanes_t(p_r), lanes_t(p_i)
        return ccr_l * pr_l - cci_l * pi_l, -(ccr_l * pi_l + cci_l * pr_l)

    fr_pair, fi_pair = out_tables(pr[1:], pi[1:])
    fk_re, fk_im = out_tables(pr[:ln], pi[:ln])
    krow = jnp.einsum('jrk,jkc->jrc', jnp.concatenate([bbr, bbi], axis=2), jnp.concatenate([fk_re, fk_im], axis=1),
                      precision=HIGHEST)
    m = _s5_toeplitz(krow)
    rows_of = lambda a: jnp.broadcast_to(a[:, :, None, :], (S5_PAIRS, a.shape[1], S5_PB, a.shape[2])).reshape(
        S5_PAIRS, a.shape[1] * S5_PB, a.shape[2])
    pe_r = rows_of(pair_pow(pr[ln - 1 - jnp.arange(ln)]))
    pe_i = rows_of(pair_pow(pi[ln - 1 - jnp.arange(ln)]))
    tile_s = lambda a: jnp.broadcast_to(a[:, None], (S5_PAIRS, ln) + a.shape[1:]).reshape(S5_PAIRS, S5_ROWS, a.shape[2])
    bbr_s, bbi_s = tile_s(bbr), tile_s(bbi)
    e_pair = jnp.concatenate([pe_r * bbr_s - pe_i * bbi_s, pe_r * bbi_s + pe_i * bbr_s], axis=2)
    al = jnp.zeros((S5_PAIRS, SUBLANES, LANES), F32)
    al = al.at[:, 0, :].set(pr[ln].reshape(S5_PAIRS, 2 * S5_STATE))
    al = al.at[:, 1, :].set(pi[ln].reshape(S5_PAIRS, 2 * S5_STATE))
    return m.astype(BF16), e_pair.astype(BF16), fr_pair.astype(BF16), fi_pair.astype(BF16), al


def _s5_kernel(u_ref, m_ref, e_ref, fr_ref, fi_ref, al_ref, o_ref, ustage, ystage, sloc, xpr, xpi, *, nc):
    jj = pl.program_id(1)

    @pl.when(jj == 0)
    def _():
        for b in range(S5_ROWS // LANES):
            slabs = [u_ref[pl.ds(S5_SLAB * b + a, nc, stride=S5_CHUNK), :] for a in range(S5_SLAB)]
            for q in range(S5_SLAB):
                ustage[q, :, LANES * b:LANES * (b + 1)] = jnp.concatenate(
                    [sl[:, S5_PB * q:S5_PB * (q + 1)] for sl in slabs], axis=1).astype(BF16)

    u = ustage[jj]
    sloc[...] = jnp.dot(u, e_ref[0], preferred_element_type=F32)
    ar = al_ref[0, 0:1, :]
    ai = al_ref[0, 1:2, :]

    row = lax.broadcasted_iota(jnp.int32, (SUBLANES, LANES), 0)

    def body(ti, carry):
        xr, xi = carry
        r0 = pl.multiple_of(ti * SUBLANES, SUBLANES)
        sr_t = sloc[pl.ds(r0, SUBLANES), 0:LANES]
        si_t = sloc[pl.ds(r0, SUBLANES), LANES:2 * LANES]
        pr_t = jnp.zeros((SUBLANES, LANES), F32)
        pi_t = jnp.zeros((SUBLANES, LANES), F32)
        for r in range(SUBLANES):
            pr_t = jnp.where(row == r, xr, pr_t)
            pi_t = jnp.where(row == r, xi, pi_t)
            xr, xi = ar * xr - ai * xi + sr_t[r:r + 1, :], ar * xi + ai * xr + si_t[r:r + 1, :]
        xpr[pl.ds(r0, SUBLANES), :] = pr_t
        xpi[pl.ds(r0, SUBLANES), :] = pi_t
        return xr, xi

    zero = jnp.zeros((1, LANES), F32)
    lax.fori_loop(0, nc // SUBLANES, body, (zero, zero))
    ystage[jj] = (jnp.dot(xpr[...].astype(BF16), fr_ref[0], preferred_element_type=F32)
                  + jnp.dot(xpi[...].astype(BF16), fi_ref[0], preferred_element_type=F32)
                  + jnp.dot(u, m_ref[0], preferred_element_type=F32))

    @pl.when(jj == S5_SLAB - 1)
    def _():
        for t in range(S5_CHUNK):
            o_ref[pl.ds(t, nc, stride=S5_CHUNK), :] = jnp.concatenate(
                [ystage[q, :, S5_PB * t:S5_PB * (t + 1)] for q in range(S5_SLAB)], axis=1)


def _s5(z2, lam_re, lam_im, log_dt, b_re, b_im, c_re, c_im):
    t = z2.shape[0]
    nc = t // S5_CHUNK
    m, e_pair, fr_pair, fi_pair, al = _s5_tables(lam_re, lam_im, log_dt, b_re, b_im, c_re, c_im)
    pair = lambda a, b: pl.BlockSpec((1, a, b), lambda k, jj: (S5_SLAB * k + jj, 0, 0))
    return pl.pallas_call(
        functools.partial(_s5_kernel, nc=nc),
        grid=(WIDTH // LANES, S5_SLAB),
        in_specs=[pl.BlockSpec((t, LANES), lambda k, jj: (0, Z2_US * WIDTH // LANES + k), pipeline_mode=pl.Buffered(1)),
                  pair(S5_ROWS, S5_ROWS), pair(S5_ROWS, 2 * LANES), pair(LANES, S5_ROWS), pair(LANES, S5_ROWS),
                  pair(SUBLANES, LANES)],
        out_specs=pl.BlockSpec((t, LANES), lambda k, jj: (0, k), pipeline_mode=pl.Buffered(1)),
        out_shape=jax.ShapeDtypeStruct((t, WIDTH), F32),
        scratch_shapes=[pltpu.VMEM((S5_SLAB, nc, S5_ROWS), BF16), pltpu.VMEM((S5_SLAB, nc, S5_ROWS), F32),
                        pltpu.VMEM((nc, 2 * LANES), F32), pltpu.VMEM((nc, LANES), F32), pltpu.VMEM((nc, LANES), F32)],
        compiler_params=_cparams(("arbitrary", "arbitrary")),
        name="s5_scan",
    )(z2, m, e_pair, fr_pair, fi_pair, al)


def _layer_norm(x, g, b):
    mu = jnp.mean(x, axis=-1, keepdims=True)
    xc = x - mu
    var = jnp.mean(xc * xc, axis=-1, keepdims=True)
    return xc * lax.rsqrt(var + LN_EPS) * g + b


def _merge_rows(sl, x_ref, g0_ref, g1_ref, g2_ref, ym_ref, ys_ref, us_ref, yh_ref, wm_ref, ws_ref, wh_ref, wglu_ref,
                d_ref, wo_ref, lg_ref, lb_ref, rw_ref, rb_ref, o_ref):
    ys = _gelu_tanh(ys_ref[sl, :] + d_ref[...] * us_ref[sl, :].astype(F32))
    ys = ys * _sigmoid(_dot(ys, wglu_ref[...]))
    merged = (_sigmoid(g0_ref[sl, :].astype(F32)) * _dot(ym_ref[sl, :], wm_ref[...])
              + _sigmoid(g1_ref[sl, :].astype(F32)) * _dot(ys, ws_ref[...])
              + _sigmoid(g2_ref[sl, :].astype(F32)) * _dot(yh_ref[sl, :], wh_ref[...]))
    x1 = _layer_norm(DN_ALPHA * x_ref[sl, :] + _dot(merged, wo_ref[...]), lg_ref[...], lb_ref[...])
    o_ref[sl, 0:D_MODEL] = x1

    xh = x1.astype(BF16)
    xl = (x1 - xh.astype(F32)).astype(BF16)
    logits = (jnp.dot(xh, rw_ref[0], preferred_element_type=F32) + jnp.dot(xl, rw_ref[0], preferred_element_type=F32)
              + jnp.dot(xh, rw_ref[1], preferred_element_type=F32))
    s0 = _sigmoid(logits)
    sb0 = s0 + rb_ref[...]
    shift = lambda a, j: a if j == 0 else pltpu.roll(a, LANES - N_GROUPS * j, 1)
    s = [shift(s0, j) for j in range(EXPERTS_PER_GROUP)]
    sb = [shift(sb0, j) for j in range(EXPERTS_PER_GROUP)]
    hi1, lo1 = jnp.maximum(sb[0], sb[1]), jnp.minimum(sb[0], sb[1])
    hi2, lo2 = jnp.maximum(sb[2], sb[3]), jnp.minimum(sb[2], sb[3])
    top2 = jnp.maximum(hi1, hi2) + jnp.maximum(jnp.minimum(hi1, hi2), jnp.maximum(lo1, lo2))
    lane = lax.broadcasted_iota(jnp.int32, top2.shape, 1)
    top2 = jnp.where(lane < N_GROUPS, top2, -jnp.inf)
    gmax = jnp.max(top2, axis=1, keepdims=True)
    g_idx = jnp.min(jnp.where(top2 == gmax, lane, LANES), axis=1, keepdims=True)
    sel = lane == g_idx
    v = [jnp.sum(jnp.where(sel, sb[j], 0.0), axis=1, keepdims=True) for j in range(EXPERTS_PER_GROUP)]
    sv = [jnp.sum(jnp.where(sel, s[j], 0.0), axis=1, keepdims=True) for j in range(EXPERTS_PER_GROUP)]

    def first_max(vals):
        m = jnp.maximum(jnp.maximum(vals[0], vals[1]), jnp.maximum(vals[2], vals[3]))
        return jnp.where(vals[0] == m, 0, jnp.where(vals[1] == m, 1, jnp.where(vals[2] == m, 2, 3)))

    e1 = first_max(v)
    e2 = first_max([jnp.where(e1 == j, -jnp.inf, v[j]) for j in range(EXPERTS_PER_GROUP)])
    s1 = sum(jnp.where(e1 == j, sv[j], 0.0) for j in range(EXPERTS_PER_GROUP))
    s2 = sum(jnp.where(e2 == j, sv[j], 0.0) for j in range(EXPERTS_PER_GROUP))
    tot = s1 + s2
    meta = jnp.where(lane == 0, g_idx.astype(F32), 0.0)
    for j in range(EXPERTS_PER_GROUP):
        cw = jnp.where(e1 == j, s1 / tot, 0.0) + jnp.where(e2 == j, s2 / tot, 0.0)
        meta = jnp.where(lane == 1 + j, cw, meta)
    o_ref[sl, D_MODEL:D_MODEL + META] = meta


MERGE_ROWS = 256


def _merge_kernel(*refs):
    for lo in range(0, refs[0].shape[0], MERGE_ROWS):
        _merge_rows(slice(lo, lo + MERGE_ROWS), *refs)


def _merge(x, z1, z2, ym, ys, yh, wm, ws, wh, wglu, d, wo, lg, lb, router_w, router_bias, tm):
    t = x.shape[0]
    rw = router_w.astype(F32).reshape(D_MODEL, N_GROUPS, EXPERTS_PER_GROUP).transpose(0, 2, 1)
    rw = jnp.pad(rw.reshape(D_MODEL, N_EXPERTS), ((0, 0), (0, LANES - N_EXPERTS)))
    rw_hi = rw.astype(BF16)
    rw = jnp.stack([rw_hi, (rw - rw_hi.astype(F32)).astype(BF16)])
    rb = jnp.pad(router_bias.astype(F32).reshape(N_GROUPS, EXPERTS_PER_GROUP).T.reshape(1, N_EXPERTS),
                 ((0, 0), (0, LANES - N_EXPERTS)))
    full = lambda shape: pl.BlockSpec(shape, lambda i: (0,) * len(shape))
    row = lambda w: pl.BlockSpec((tm, w), lambda i: (i, 0))
    return pl.pallas_call(
        _merge_kernel,
        grid=(t // tm,),
        in_specs=[row(D_MODEL),
                  pl.BlockSpec((tm, D_MODEL), lambda i: (i, 0)),
                  pl.BlockSpec((tm, D_MODEL), lambda i: (i, 1)),
                  pl.BlockSpec((tm, D_MODEL), lambda i: (i, 2)),
                  row(WIDTH), row(WIDTH), pl.BlockSpec((tm, WIDTH), lambda i: (i, Z2_US)), row(WIDTH),
                  full((WIDTH, D_MODEL)), full((WIDTH, D_MODEL)), full((WIDTH, D_MODEL)), full((WIDTH, WIDTH)),
                  full((1, WIDTH)), full((D_MODEL, D_MODEL)), full((1, D_MODEL)), full((1, D_MODEL)),
                  full((2, D_MODEL, LANES)), full((1, LANES))],
        out_specs=pl.BlockSpec((tm, D_MODEL + META), lambda i: (i, 0)),
        out_shape=jax.ShapeDtypeStruct((t, D_MODEL + META), F32),
        compiler_params=_cparams(("parallel",)),
        name="merge_router",
    )(x, z1, z1, z1, ym, ys, z2, yh, wm.astype(BF16), ws.astype(BF16), wh.astype(BF16), wglu.astype(BF16),
      d.reshape(1, WIDTH).astype(F32), wo.astype(BF16), lg.reshape(1, D_MODEL).astype(F32),
      lb.reshape(1, D_MODEL).astype(F32), rw, rb)


def _start_row_gather(idx_ref, base, src_hbm, dst, sem, rows):
    for r in range(rows):
        pltpu.make_async_copy(src_hbm.at[pl.ds(idx_ref[base + r], 1)], dst.at[pl.ds(r, 1)], sem).start(priority=r % 2)


def _wait_row_gather(src_hbm, dst, sem, rows):
    pltpu.make_async_copy(src_hbm.at[pl.ds(0, rows)], dst, sem).wait()


def _dispatch_kernel(dst_ref, x_ref, init_hbm, o_hbm, sem, *, tm):
    del init_hbm
    base = pl.program_id(0) * tm
    for r in range(tm):
        pltpu.make_async_copy(x_ref.at[pl.ds(r, 1)], o_hbm.at[pl.ds(dst_ref[base + r], 1)], sem).start(priority=r % 2)
    pltpu.make_async_copy(x_ref, o_hbm.at[pl.ds(0, tm)], sem).wait()


def _moe_kernel(tg_ref, x_ref, wg_ref, wu_ref, wd_ref, o_ref, wg_s, wu_s, wd_s, *, tm, ntiles):
    i = pl.program_id(0)

    @pl.when(jnp.logical_or(i == 0, tg_ref[i] != tg_ref[jnp.maximum(i - 1, 0)]))
    def _():
        for e in range(EXPERTS_PER_GROUP):
            wg_s[e] = wg_ref[0, e].astype(BF16)
            wu_s[e] = wu_ref[0, e].astype(BF16)
            wd_s[e] = wd_ref[0, e].astype(BF16)

    used = i < tg_ref[ntiles]

    @pl.when(used)
    def _():
        xb = x_ref[:, 0:D_MODEL].astype(BF16)
        meta = x_ref[:, D_MODEL:D_MODEL + META]
        y = jnp.zeros((tm, D_MODEL), F32)
        for e in range(EXPERTS_PER_GROUP):
            hg = jnp.dot(xb, wg_s[e], preferred_element_type=F32)
            hu = jnp.dot(xb, wu_s[e], preferred_element_type=F32)
            hh = _silu(hg) * hu * _lane_col(meta, 1 + e)
            y = y + jnp.dot(hh.astype(BF16), wd_s[e], preferred_element_type=F32)
        o_ref[...] = y

    @pl.when(jnp.logical_not(used))
    def _():
        o_ref[...] = jnp.zeros_like(o_ref)


def _ln2_kernel(dst_ref, x_ref, y_hbm, lg_ref, lb_ref, o_ref, buf, sem, *, tm, ntiles):
    i = pl.program_id(0)
    slot = i % 2

    @pl.when(i == 0)
    def _():
        _start_row_gather(dst_ref, 0, y_hbm, buf.at[0], sem.at[0], tm)

    @pl.when(i + 1 < ntiles)
    def _():
        _start_row_gather(dst_ref, (i + 1) * tm, y_hbm, buf.at[1 - slot], sem.at[1 - slot], tm)

    _wait_row_gather(y_hbm, buf.at[slot], sem.at[slot], tm)
    o_ref[...] = _layer_norm(DN_ALPHA * x_ref[...] + buf[slot], lg_ref[...], lb_ref[...])


def _moe_ln(x1e, wg, wu, wd, layer, lg, lb, tm, tt, tc):
    t = x1e.shape[0]
    ntiles = t // tm + N_GROUPS
    p = ntiles * tm
    key = x1e[:, D_MODEL].astype(jnp.int32)
    onehot = (key[:, None] == jnp.arange(N_GROUPS)[None, :]).astype(jnp.int32)
    csum = jnp.cumsum(onehot, axis=0)
    counts = csum[-1]
    rank = jnp.sum(onehot * csum, axis=1) - 1
    pcount = ((counts + tm - 1) // tm) * tm
    pend = jnp.cumsum(pcount)
    dest = (pend - pcount)[key] + rank
    dest = dest.astype(jnp.int32)
    tile_start = jnp.arange(ntiles, dtype=jnp.int32) * tm
    tile_group = jnp.minimum(jnp.sum((tile_start[:, None] >= pend[None, :]).astype(jnp.int32), axis=1),
                             N_GROUPS - 1)
    tile_group = jnp.concatenate([tile_group, (pend[-1:] // tm).astype(jnp.int32)])
    nt2 = t // tt
    x_sorted = pl.pallas_call(
        functools.partial(_dispatch_kernel, tm=tt),
        grid_spec=pltpu.PrefetchScalarGridSpec(
            num_scalar_prefetch=1,
            grid=(nt2,),
            in_specs=[pl.BlockSpec((tt, D_MODEL + META), lambda i, d: (i, 0)), pl.BlockSpec(memory_space=pl.ANY)],
            out_specs=pl.BlockSpec(memory_space=pl.ANY),
            scratch_shapes=[pltpu.SemaphoreType.DMA(())]),
        out_shape=jax.ShapeDtypeStruct((p, D_MODEL + META), F32),
        input_output_aliases={2: 0},
        compiler_params=_cparams(("arbitrary",)),
        name="moe_dispatch",
    )(dest, x1e, jnp.zeros((p, D_MODEL + META), F32))
    wspec = lambda a, b: pl.BlockSpec((1, EXPERTS_PER_GROUP, a, b), lambda i, tg: (layer, tg[i], 0, 0))
    wscr = lambda a, b: pltpu.VMEM((EXPERTS_PER_GROUP, a, b), BF16)
    y_sorted = pl.pallas_call(
        functools.partial(_moe_kernel, tm=tm, ntiles=ntiles),
        grid_spec=pltpu.PrefetchScalarGridSpec(
            num_scalar_prefetch=1,
            grid=(ntiles,),
            in_specs=[pl.BlockSpec((tm, D_MODEL + META), lambda i, tg: (i, 0)), wspec(D_MODEL, D_EXPERT),
                      wspec(D_MODEL, D_EXPERT), wspec(D_EXPERT, D_MODEL)],
            out_specs=pl.BlockSpec((tm, D_MODEL), lambda i, tg: (i, 0)),
            scratch_shapes=[wscr(D_MODEL, D_EXPERT), wscr(D_MODEL, D_EXPERT), wscr(D_EXPERT, D_MODEL)]),
        out_shape=jax.ShapeDtypeStruct((p, D_MODEL), F32),
        compiler_params=_cparams(("arbitrary",)),
        name="moe_experts",
    )(tile_group, x_sorted, wg, wu, wd)
    return pl.pallas_call(
        functools.partial(_ln2_kernel, tm=tc, ntiles=t // tc),
        grid_spec=pltpu.PrefetchScalarGridSpec(
            num_scalar_prefetch=1,
            grid=(t // tc,),
            in_specs=[pl.BlockSpec((tc, D_MODEL), lambda i, d: (i, 0)), pl.BlockSpec(memory_space=pl.ANY),
                      pl.BlockSpec((1, D_MODEL), lambda i, d: (0, 0)), pl.BlockSpec((1, D_MODEL), lambda i, d: (0, 0))],
            out_specs=pl.BlockSpec((tc, D_MODEL), lambda i, d: (i, 0)),
            scratch_shapes=[pltpu.VMEM((2, tc, D_MODEL), F32), pltpu.SemaphoreType.DMA((2,))]),
        out_shape=jax.ShapeDtypeStruct((t, D_MODEL), F32),
        compiler_params=_cparams(("arbitrary",)),
        name="moe_combine_ln",
    )(dest.astype(jnp.int32), x1e, y_sorted, lg.reshape(1, D_MODEL).astype(F32), lb.reshape(1, D_MODEL).astype(F32))


def _layer(x, l, p, lb_l, cfg):
    t = x.shape[0]
    w1, w2, wt = _split_w_in(p['w_in'][l])
    xb = x.astype(BF16)
    z1 = _matmul(xb, w1, Z1_DTYPE, cfg['tm_in'], cfg['tn_in'])
    z2, gt = _inproj_gate(xb, w2, wt, cfg['tm_in'])
    gt3 = gt.reshape(SUBLANES, t // CHUNK, CHUNK).transpose(1, 0, 2)
    ym = _mlstm(z1, z2, gt3, p['mlstm_conv_w'][l], p['mlstm_conv_b'][l], p['mlstm_wq'][l], p['mlstm_wk'][l],
                p['mlstm_wv'][l], p['mlstm_b_i'][l], p['mlstm_b_f'][l], p['mlstm_skip'][l], cfg['tb'])
    ys = _s5(z2, p['s5_lambda_re'][l], p['s5_lambda_im'][l], p['s5_log_dt'][l], p['s5_b_re'][l], p['s5_b_im'][l],
             p['s5_c_re'][l], p['s5_c_im'][l])
    yh = _hgrn(z1, z2, lb_l, p['hgrn_norm_w'][l], cfg['tb'])
    x1e = _merge(x, z1, z2, ym, ys, yh, p['w_branch_mlstm'][l], p['w_branch_s5'][l], p['w_branch_hgrn'][l],
                 p['s5_w_glu'][l], p['s5_d'][l], p['w_out'][l], p['ln1_g'][l], p['ln1_b'][l], p['router_w'],
                 p['router_bias'], cfg['tm_merge'])
    return _moe_ln(x1e, p['exp_w_gate'], p['exp_w_up'], p['exp_w_down'], l, p['ln2_g'][l], p['ln2_b'][l],
                   cfg['tm_moe'], cfg['tt_dispatch'], cfg['tt_combine'])


_CFG = dict(tm_in=1024, tn_in=Z1_COLS // 4, tb=1024, tm_merge=2 * MERGE_ROWS, tm_moe=256, tt_dispatch=1024, tt_combine=512)


def kernel(x, w_in, mlstm_conv_w, mlstm_conv_b, mlstm_wq, mlstm_wk, mlstm_wv, mlstm_b_i, mlstm_b_f, mlstm_skip, s5_lambda_re, s5_lambda_im, s5_log_dt, s5_b_re, s5_b_im, s5_c_re, s5_c_im, s5_d, s5_w_glu, hgrn_lower_bounds, hgrn_norm_w, w_branch_mlstm, w_branch_s5, w_branch_hgrn, w_out, ln1_g, ln1_b, ln2_g, ln2_b, router_w, router_bias, exp_w_gate, exp_w_up, exp_w_down):
    p = dict(w_in=w_in, mlstm_conv_w=mlstm_conv_w, mlstm_conv_b=mlstm_conv_b, mlstm_wq=mlstm_wq, mlstm_wk=mlstm_wk,
             mlstm_wv=mlstm_wv, mlstm_b_i=mlstm_b_i, mlstm_b_f=mlstm_b_f, mlstm_skip=mlstm_skip,
             s5_lambda_re=s5_lambda_re, s5_lambda_im=s5_lambda_im, s5_log_dt=s5_log_dt, s5_b_re=s5_b_re,
             s5_b_im=s5_b_im, s5_c_re=s5_c_re, s5_c_im=s5_c_im, s5_d=s5_d, s5_w_glu=s5_w_glu,
             hgrn_norm_w=hgrn_norm_w, w_branch_mlstm=w_branch_mlstm, w_branch_s5=w_branch_s5,
             w_branch_hgrn=w_branch_hgrn, w_out=w_out, ln1_g=ln1_g, ln1_b=ln1_b, ln2_g=ln2_g, ln2_b=ln2_b,
             router_w=router_w, router_bias=router_bias, exp_w_gate=exp_w_gate, exp_w_up=exp_w_up,
             exp_w_down=exp_w_down)
    lb_cum = jnp.cumsum(jax.nn.softmax(hgrn_lower_bounds.astype(F32), axis=0), axis=0)
    lb_layers = lb_cum - lb_cum[0]
    bsz, seq, d = x.shape
    h = x.reshape(bsz * seq, d)
    for l in range(DEPTH):
        h = _layer(h, l, p, lb_layers[l], _CFG)
    return h.reshape(bsz, seq, d)
```

```python
import functools
import math

import jax
import jax.numpy as jnp
from jax import lax
from jax.experimental import pallas as pl
from jax.experimental.pallas import tpu as pltpu

F32 = jnp.float32
BF16 = jnp.bfloat16
HIGHEST = lax.Precision.HIGHEST

D_MODEL = 1024
DEPTH = 2
HEADS = 4
HEAD_DIM = 128
WIDTH = HEADS * HEAD_DIM
MLSTM_CONV = 4
CHUNK = 64
S5_GROUP = 16
S5_GROUPS = 32
S5_STATE = 64
S5_PAIRS = S5_GROUPS // 2
S5_CHUNK = 32
S5_MAX_REAL = -1e-4
N_EXPERTS = 32
N_GROUPS = 8
EXPERTS_PER_GROUP = 4
D_EXPERT = 256
DN_ALPHA = (2 * DEPTH) ** 0.25
LN_EPS = 1e-5
NORM_EPS = 1e-6

LANES = 128
SUBLANES = 8
ROUTE_LANES = LANES
V7X_VMEM_BYTES = 64 * 1024 * 1024
VMEM_LIMIT = V7X_VMEM_BYTES * 7 // 8

Z1_DTYPE = BF16
Z1_GATE, Z1_XM, Z1_OM, Z1_QH, Z1_IH, Z1_GH = 0, 6, 7, 8, 9, 10
Z1_COLS = 11 * WIDTH
Z2_FH, Z2_US = 0, 1
Z2_COLS = 2 * WIDTH + LANES


def _cparams(sem):
    return pltpu.CompilerParams(dimension_semantics=sem, vmem_limit_bytes=VMEM_LIMIT)


def _sigmoid(x):
    return 0.5 * (1.0 + jnp.tanh(0.5 * x))


def _silu(x):
    return x * _sigmoid(x)


def _log_sigmoid(x):
    return jnp.minimum(x, 0.0) - jnp.log(1.0 + jnp.exp(-jnp.abs(x)))


def _gelu_tanh(x):
    return 0.5 * x * (1.0 + jnp.tanh(math.sqrt(2.0 / math.pi) * (x + 0.044715 * (x * x * x))))


def _cumsum_rows(x):
    n = x.shape[0]
    row = lax.broadcasted_iota(jnp.int32, x.shape, 0)
    s = 1
    while s < n:
        x = x + jnp.where(row >= s, pltpu.roll(x, s, 0), 0.0)
        s *= 2
    return x


def _lane_col(x, idx):
    lane = lax.broadcasted_iota(jnp.int32, x.shape, 1)
    return jnp.sum(jnp.where(lane == idx, x, 0.0), axis=1, keepdims=True)


def _dot(a, b):
    return jnp.dot(a.astype(BF16), b.astype(BF16), preferred_element_type=F32)


def _dot_nt(a, b):
    return lax.dot_general(a.astype(BF16), b.astype(BF16), (((1,), (1,)), ((), ())), preferred_element_type=F32)


def _dot_tn(a, b):
    return lax.dot_general(a.astype(BF16), b.astype(BF16), (((0,), (0,)), ((), ())), preferred_element_type=F32)


def _mm_kernel(x_ref, w_ref, o_ref):
    o_ref[...] = jnp.dot(x_ref[...], w_ref[...], preferred_element_type=F32).astype(o_ref.dtype)


def _matmul(x, w, out_dtype, tm, tn):
    m, k = x.shape
    n = w.shape[1]
    return pl.pallas_call(
        _mm_kernel,
        grid=(m // tm, n // tn),
        in_specs=[pl.BlockSpec((tm, k), lambda i, j: (i, 0)), pl.BlockSpec((k, tn), lambda i, j: (0, j))],
        out_specs=pl.BlockSpec((tm, tn), lambda i, j: (i, j)),
        out_shape=jax.ShapeDtypeStruct((m, n), out_dtype),
        compiler_params=_cparams(("parallel", "parallel")),
        name="in_proj_wide",
    )(x, w)


def _inproj_gate_kernel(x_ref, w_ref, wt_ref, o_ref, gt_ref):
    x = x_ref[...]
    o_ref[...] = jnp.dot(x, w_ref[...], preferred_element_type=F32)
    gt_ref[...] = lax.dot_general(wt_ref[...], x, (((1,), (1,)), ((), ())), preferred_element_type=F32)


def _inproj_gate(x, w2, wt, tm):
    m, k = x.shape
    return pl.pallas_call(
        _inproj_gate_kernel,
        grid=(m // tm,),
        in_specs=[pl.BlockSpec((tm, k), lambda i: (i, 0)),
                  pl.BlockSpec((k, Z2_COLS), lambda i: (0, 0)),
                  pl.BlockSpec((SUBLANES, k), lambda i: (0, 0))],
        out_specs=[pl.BlockSpec((tm, Z2_COLS), lambda i: (i, 0)), pl.BlockSpec((SUBLANES, tm), lambda i: (0, i))],
        out_shape=[jax.ShapeDtypeStruct((m, Z2_COLS), F32), jax.ShapeDtypeStruct((SUBLANES, m), F32)],
        compiler_params=_cparams(("parallel",)),
        name="in_proj_gates",
    )(x, w2, wt)


def _split_w_in(w):
    offs, o = [], 0
    for s in (WIDTH, WIDTH, HEADS, HEADS, WIDTH, WIDTH, WIDTH, WIDTH, WIDTH, 3 * D_MODEL):
        offs.append((o, o + s))
        o += s
    seg = [w[:, a:b] for a, b in offs]
    xm, om, im, fm, us, qh, fh, ih, gh, gate = seg
    w1 = jnp.concatenate([gate, xm, om, qh, ih, gh], axis=1).astype(BF16)
    pad = jnp.zeros((w.shape[0], LANES - 2 * HEADS), w.dtype)
    w2 = jnp.concatenate([fh, us, im, fm, pad], axis=1).astype(BF16)
    wt = jnp.concatenate([im, fm], axis=1).T.astype(BF16)
    return w1, w2, wt


def _mlstm_kernel(xm_ref, om_ref, gc_ref, gr_ref, cw_ref, cb_ref, wq_ref, wk_ref, wv_ref, bcol_ref, brow_ref,
                  skip_ref, triu_ref, o_ref, xpad, q_s, k_s, v_s, xc_s, h_s, gcol_s, grow_s, brow_s, ct_s, n_s, m_s,
                  *, tb):
    ncb = tb // CHUNK

    @pl.when(pl.program_id(0) == 0)
    def _():
        xpad[0:SUBLANES, :] = jnp.zeros((SUBLANES, WIDTH), F32)
        ct_s[...] = jnp.zeros_like(ct_s)
        n_s[...] = jnp.zeros_like(n_s)
        m_s[...] = jnp.zeros_like(m_s)

    xm = xm_ref[...].astype(F32)
    xpad[SUBLANES:SUBLANES + tb, :] = xm
    cw = cw_ref[...]
    conv = cb_ref[...] + cw[3:4, :] * xm
    for d in range(1, MLSTM_CONV):
        conv = conv + cw[3 - d:4 - d, :] * xpad[SUBLANES - d:SUBLANES - d + tb, :]
    xpad[0:SUBLANES, :] = xpad[tb:tb + SUBLANES, :]
    xc = _silu(conv)
    xc_s[...] = xc
    for h in range(HEADS):
        sl = slice(h * HEAD_DIM, (h + 1) * HEAD_DIM)
        xch = xc[:, sl].astype(BF16)
        q_s[:, sl] = jnp.dot(xch, wq_ref[h], preferred_element_type=F32) * (HEAD_DIM ** -0.5)
        k_s[:, sl] = jnp.dot(xch, wk_ref[h], preferred_element_type=F32)
        v_s[:, sl] = jnp.dot(xm[:, sl].astype(BF16), wv_ref[h], preferred_element_type=F32)

    gc = gc_ref[...] + bcol_ref[...]
    lane = lax.broadcasted_iota(jnp.int32, gc.shape, 1)
    gcol_s[...] = jnp.where(lane < HEADS, gc, _log_sigmoid(gc))
    gr = gr_ref[...] + brow_ref[...]
    sub = lax.broadcasted_iota(jnp.int32, gr.shape, 1)
    gr = jnp.where(sub < HEADS, gr, _log_sigmoid(gr)).reshape(ncb * SUBLANES, CHUNK)
    grow_s[...] = gr
    brow_s[...] = jnp.dot(gr, triu_ref[...], precision=HIGHEST, preferred_element_type=F32)

    rowi = lax.broadcasted_iota(jnp.int32, (CHUNK, CHUNK), 0)
    coli = lax.broadcasted_iota(jnp.int32, (CHUNK, CHUNK), 1)
    causal = rowi >= coli

    def chunk_body(c, carry):
        r0 = pl.multiple_of(c * CHUNK, CHUNK)
        g0 = pl.multiple_of(c * SUBLANES, SUBLANES)
        gcol = gcol_s[pl.ds(r0, CHUNK), :]
        bcol = _cumsum_rows(gcol)
        grow = grow_s[pl.ds(g0, SUBLANES), :]
        brow = brow_s[pl.ds(g0, SUBLANES), :]
        for h in range(HEADS):
            sl = slice(h * HEAD_DIM, (h + 1) * HEAD_DIM)
            li_c = _lane_col(gcol, h)
            b_c = _lane_col(bcol, HEADS + h)
            li_r = grow[h:h + 1, :]
            b_r = brow[HEADS + h:HEADS + h + 1, :]
            b_tot = b_c[CHUNK - 1:CHUNK, :]
            a_c = b_tot - b_c + li_c
            a_max = jnp.max(a_c, axis=0, keepdims=True)
            q = q_s[pl.ds(r0, CHUNK), sl]
            k = k_s[pl.ds(r0, CHUNK), sl]
            v = v_s[pl.ds(r0, CHUNK), sl]
            wk = jnp.exp(a_c - a_max) * k
            c_loc_t = _dot_tn(v, wk)
            n_loc = jnp.sum(wk, axis=0, keepdims=True)
            ct_prev = ct_s[h]
            n_prev = n_s[h:h + 1, :]
            m_prev = m_s[h:h + 1, 0:1]
            d_mat = jnp.where(causal, b_c - b_r + li_r, -jnp.inf)
            m_inter = b_c + m_prev
            m_j = jnp.maximum(m_inter, jnp.max(d_mat, axis=1, keepdims=True))
            sc = _dot_nt(q, k) * jnp.exp(d_mat - m_j)
            g_inter = jnp.exp(m_inter - m_j)
            num = g_inter * _dot_nt(q, ct_prev) + _dot(sc, v)
            den = g_inter * jnp.sum(q * n_prev, axis=1, keepdims=True) + jnp.sum(sc, axis=1, keepdims=True)
            h_s[pl.ds(r0, CHUNK), sl] = num / jnp.maximum(jnp.abs(den), jnp.exp(-m_j))
            m_new = jnp.maximum(b_tot + m_prev, a_max)
            g_old = jnp.exp(b_tot + m_prev - m_new)
            g_loc = jnp.exp(a_max - m_new)
            ct_s[h] = g_old * ct_prev + g_loc * c_loc_t
            n_s[h:h + 1, :] = g_old * n_prev + g_loc * n_loc
            m_s[h:h + 1, :] = jnp.broadcast_to(m_new, (1, LANES))
        return carry

    lax.fori_loop(0, ncb, chunk_body, 0, unroll=4)
    o_ref[...] = (_sigmoid(om_ref[...].astype(F32)) * h_s[...] + skip_ref[...] * xc_s[...]).astype(o_ref.dtype)


def _mlstm(z1, z2, gt3, conv_w, conv_b, wq, wk, wv, b_i, b_f, skip, tb):
    t = z1.shape[0]
    ncb = tb // CHUNK
    bias = jnp.concatenate([b_i, b_f]).astype(F32)
    bcol = jnp.zeros((1, LANES), F32).at[0, :2 * HEADS].set(bias)
    brow = bias.reshape(2 * HEADS, 1)
    triu = jnp.triu(jnp.ones((CHUNK, CHUNK), F32))
    full = lambda shape: pl.BlockSpec(shape, lambda i: (0,) * len(shape))
    return pl.pallas_call(
        functools.partial(_mlstm_kernel, tb=tb),
        grid=(t // tb,),
        in_specs=[pl.BlockSpec((tb, WIDTH), lambda i: (i, Z1_XM)),
                  pl.BlockSpec((tb, WIDTH), lambda i: (i, Z1_OM)),
                  pl.BlockSpec((tb, LANES), lambda i: (i, 2 * WIDTH // LANES)),
                  pl.BlockSpec((ncb, SUBLANES, CHUNK), lambda i: (i, 0, 0)),
                  full((MLSTM_CONV, WIDTH)), full((1, WIDTH)),
                  full((HEADS, HEAD_DIM, HEAD_DIM)), full((HEADS, HEAD_DIM, HEAD_DIM)),
                  full((HEADS, HEAD_DIM, HEAD_DIM)),
                  full((1, LANES)), full((2 * HEADS, 1)), full((1, WIDTH)), full((CHUNK, CHUNK))],
        out_specs=pl.BlockSpec((tb, WIDTH), lambda i: (i, 0)),
        out_shape=jax.ShapeDtypeStruct((t, WIDTH), F32),
        scratch_shapes=[pltpu.VMEM((tb + SUBLANES, WIDTH), F32)]
        + [pltpu.VMEM((tb, WIDTH), F32) for _ in range(5)]
        + [pltpu.VMEM((tb, LANES), F32),
           pltpu.VMEM((ncb * SUBLANES, CHUNK), F32), pltpu.VMEM((ncb * SUBLANES, CHUNK), F32),
           pltpu.VMEM((HEADS, HEAD_DIM, HEAD_DIM), F32), pltpu.VMEM((SUBLANES, LANES), F32),
           pltpu.VMEM((SUBLANES, LANES), F32)],
        compiler_params=_cparams(("arbitrary",)),
        name="mlstm",
    )(z1, z1, z2, gt3, conv_w.astype(F32), conv_b.reshape(1, WIDTH).astype(F32), wq.astype(BF16), wk.astype(BF16),
      wv.astype(BF16), bcol, brow, skip.reshape(1, WIDTH).astype(F32), triu)


def _hgrn_kernel(q_ref, f_ref, i_ref, g_ref, lb_ref, nw_ref, tri_ref, o_ref, st_s, *, tb):
    ncb = tb // CHUNK

    @pl.when(pl.program_id(0) == 0)
    def _():
        st_s[...] = jnp.zeros_like(st_s)

    lb = lb_ref[...]
    nw = nw_ref[...]
    tri = tri_ref[...]
    rowi = lax.broadcasted_iota(jnp.int32, (CHUNK, WIDTH), 0)
    sr = lax.broadcasted_iota(jnp.int32, (CHUNK, CHUNK), 0)
    sc = lax.broadcasted_iota(jnp.int32, (CHUNK, CHUNK), 1)
    halves = [1 << p for p in range(CHUNK.bit_length() - 1)]
    upper = {m: (rowi & m) != 0 for m in halves}
    same_blk = {m: (sr // (2 * m)) == (sc // (2 * m)) for m in halves}

    def chunk_body(c, carry):
        r0 = pl.multiple_of(c * CHUNK, CHUNK)
        f = lb + (1.0 - lb) * _sigmoid(f_ref[pl.ds(r0, CHUNK), :].astype(F32))
        k = 1.0 - f
        q = _silu(q_ref[pl.ds(r0, CHUNK), :].astype(F32))
        v = i_ref[pl.ds(r0, CHUNK), :].astype(F32)
        lf = jnp.log(f)
        hi = lf.astype(BF16)
        r1 = lf - hi.astype(F32)
        mid = r1.astype(BF16)
        lo = (r1 - mid.astype(F32)).astype(BF16)
        b = (jnp.dot(tri, hi, preferred_element_type=F32) + jnp.dot(tri, mid, preferred_element_type=F32)
             + jnp.dot(tri, lo, preferred_element_type=F32))
        qs, ks = {}, {}
        for m in halves:
            if m == 1:
                t = jnp.where(upper[m], q * f, k)
            else:
                nblk = CHUNK // (2 * m)
                r = jnp.concatenate([jnp.broadcast_to(b[2 * m * j + m - 1:2 * m * j + m, :], (2 * m, WIDTH))
                                     for j in range(nblk)], axis=0)
                t = jnp.where(upper[m], q, k) * jnp.exp(-jnp.abs(b - r))
            qs[m] = jnp.where(upper[m], t, 0.0).astype(BF16)
            ks[m] = jnp.where(upper[m], 0.0, t).astype(BF16)
        b_last = b[CHUNK - 1:CHUNK, :]
        qe = q * jnp.exp(b)
        kd = k * jnp.exp(b_last - b)
        e_last = jnp.exp(b_last)
        outs = []
        for h in range(HEADS):
            sl = slice(h * HEAD_DIM, (h + 1) * HEAD_DIM)
            st = st_s[h]
            a = jnp.where(sr == sc, _dot_nt(q[:, sl], k[:, sl]), 0.0)
            for m in halves:
                a = a + jnp.where(same_blk[m], _dot_nt(qs[m][:, sl], ks[m][:, sl]), 0.0)
            oh = _dot_nt(qe[:, sl], st) + _dot(a, v[:, sl])
            st_s[h] = e_last[:, sl] * st + _dot_tn(v[:, sl], kd[:, sl])
            ms = jnp.sum(oh * oh, axis=1, keepdims=True) * (1.0 / HEAD_DIM)
            outs.append(oh * lax.rsqrt(ms + NORM_EPS))
        on = jnp.concatenate(outs, axis=1)
        g = g_ref[pl.ds(r0, CHUNK), :].astype(F32)
        o_ref[pl.ds(r0, CHUNK), :] = (on * nw * _silu(g)).astype(o_ref.dtype)
        return carry

    lax.fori_loop(0, ncb, chunk_body, 0, unroll=4)


def _hgrn(z1, z2, lb, norm_w, tb):
    t = z1.shape[0]
    tril = jnp.tril(jnp.ones((CHUNK, CHUNK), F32)).astype(BF16)
    full = lambda shape: pl.BlockSpec(shape, lambda i: (0,) * len(shape))
    return pl.pallas_call(
        functools.partial(_hgrn_kernel, tb=tb),
        grid=(t // tb,),
        in_specs=[pl.BlockSpec((tb, WIDTH), lambda i: (i, Z1_QH)),
                  pl.BlockSpec((tb, WIDTH), lambda i: (i, 0)),
                  pl.BlockSpec((tb, WIDTH), lambda i: (i, Z1_IH)),
                  pl.BlockSpec((tb, WIDTH), lambda i: (i, Z1_GH)),
                  full((1, WIDTH)), full((1, WIDTH)), full((CHUNK, CHUNK))],
        out_specs=pl.BlockSpec((tb, WIDTH), lambda i: (i, 0)),
        out_shape=jax.ShapeDtypeStruct((t, WIDTH), F32),
        scratch_shapes=[pltpu.VMEM((HEADS, HEAD_DIM, HEAD_DIM), F32)],
        compiler_params=_cparams(("arbitrary",)),
        name="hgrn2",
    )(z1, z2, z1, z1, lb.reshape(1, WIDTH).astype(F32), norm_w.reshape(1, WIDTH).astype(F32), tril)


S5_PB = 2 * S5_GROUP
S5_ROWS = S5_CHUNK * S5_PB
S5_SLAB = LANES // S5_PB


def _s5_toeplitz_kernel(k_ref, m_ref):
    krow = k_ref[0]
    lane = lax.broadcasted_iota(jnp.int32, krow.shape, 1)
    for s in range(S5_CHUNK):
        blk = krow if s == 0 else jnp.where(lane >= s * S5_PB, pltpu.roll(krow, s * S5_PB, 1), 0.0)
        m_ref[0, s * S5_PB:(s + 1) * S5_PB, :] = blk.astype(m_ref.dtype)


def _s5_toeplitz(krow):
    return pl.pallas_call(
        _s5_toeplitz_kernel,
        grid=(S5_PAIRS,),
        in_specs=[pl.BlockSpec((1, S5_PB, S5_ROWS), lambda j: (j, 0, 0))],
        out_specs=pl.BlockSpec((1, S5_ROWS, S5_ROWS), lambda j: (j, 0, 0)),
        out_shape=jax.ShapeDtypeStruct((S5_PAIRS, S5_ROWS, S5_ROWS), BF16),
        compiler_params=_cparams(("parallel",)),
        name="s5_toeplitz",
    )(krow)


def _s5_tables(lam_re, lam_im, log_dt, b_re, b_im, c_re, c_im):
    ln = S5_CHUNK
    lr = jnp.minimum(lam_re.astype(F32), S5_MAX_REAL)
    li = lam_im.astype(F32)
    dt = jnp.exp(log_dt.astype(F32))[:, None]
    mag = jnp.exp(lr * dt)
    ab_re = mag * jnp.cos(li * dt)
    ab_im = mag * jnp.sin(li * dt)
    nr = ab_re - 1.0
    den = lr * lr + li * li
    cr = (nr * lr + ab_im * li) / den
    ci = (ab_im * lr - nr * li) / den
    bb_re = cr[..., None] * b_re - ci[..., None] * b_im
    bb_im = cr[..., None] * b_im + ci[..., None] * b_re
    tau = jnp.arange(ln + 1, dtype=F32)[:, None, None]
    pm = jnp.exp(lr * dt * tau)
    pr = pm * jnp.cos(li * dt * tau)
    pi = pm * jnp.sin(li * dt * tau)
    def block_diag(a):
        a0, a1 = a[0::2], a[1::2]
        z = jnp.zeros_like(a0)
        return jnp.concatenate([jnp.concatenate([a0, z], axis=2), jnp.concatenate([z, a1], axis=2)], axis=1)

    def pair_pow(p):
        return p.reshape(p.shape[0], S5_PAIRS, 2 * S5_STATE).transpose(1, 0, 2)

    bbr = block_diag(bb_re.transpose(0, 2, 1))
    bbi = block_diag(bb_im.transpose(0, 2, 1))
    ccr = block_diag(c_re.transpose(0, 2, 1))
    cci = block_diag(c_im.transpose(0, 2, 1))
    col = jnp.arange(S5_ROWS)
    rep_t = (col[None, :] // S5_PB == jnp.arange(ln)[:, None]).astype(F32)
    rep_c = (col[None, :] % S5_PB == jnp.arange(S5_PB)[:, None]).astype(F32)
    lanes_c = lambda a: jnp.einsum('jrq,qc->jrc', a, rep_c, precision=HIGHEST)
    lanes_t = lambda p: jnp.einsum('jtr,tc->jrc', pair_pow(p), rep_t, precision=HIGHEST)
    ccr_l, cci_l = lanes_c(ccr), lanes_c(cci)

    def out_tables(p_r, p_i):
        pr_l, pi_l = lanes_t(p_r), lanes_t(p_i)
        return ccr_l * pr_l - cci_l * pi_l, -(ccr_l * pi_l + cci_l * pr_l)

    fr_pair, fi_pair = out_tables(pr[1:], pi[1:])
    fk_re, fk_im = out_tables(pr[:ln], pi[:ln])
    krow = jnp.einsum('jrk,jkc->jrc', jnp.concatenate([bbr, bbi], axis=2), jnp.concatenate([fk_re, fk_im], axis=1),
                      precision=HIGHEST)
    m = _s5_toeplitz(krow)
    rows_of = lambda a: jnp.broadcast_to(a[:, :, None, :], (S5_PAIRS, a.shape[1], S5_PB, a.shape[2])).reshape(
        S5_PAIRS, a.shape[1] * S5_PB, a.shape[2])
    pe_r = rows_of(pair_pow(pr[ln - 1 - jnp.arange(ln)]))
    pe_i = rows_of(pair_pow(pi[ln - 1 - jnp.arange(ln)]))
    tile_s = lambda a: jnp.broadcast_to(a[:, None], (S5_PAIRS, ln) + a.shape[1:]).reshape(S5_PAIRS, S5_ROWS, a.shape[2])
    bbr_s, bbi_s = tile_s(bbr), tile_s(bbi)
    e_pair = jnp.concatenate([pe_r * bbr_s - pe_i * bbi_s, pe_r * bbi_s + pe_i * bbr_s], axis=2)
    al = jnp.zeros((S5_PAIRS, SUBLANES, LANES), F32)
    al = al.at[:, 0, :].set(pr[ln].reshape(S5_PAIRS, 2 * S5_STATE))
    al = al.at[:, 1, :].set(pi[ln].reshape(S5_PAIRS, 2 * S5_STATE))
    return m.astype(BF16), e_pair.astype(BF16), fr_pair.astype(BF16), fi_pair.astype(BF16), al


def _s5_kernel(u_ref, m_ref, e_ref, fr_ref, fi_ref, al_ref, o_ref, ustage, ystage, sloc, xpr, xpi, *, nc):
    jj = pl.program_id(1)

    @pl.when(jj == 0)
    def _():
        for b in range(S5_ROWS // LANES):
            slabs = [u_ref[pl.ds(S5_SLAB * b + a, nc, stride=S5_CHUNK), :] for a in range(S5_SLAB)]
            for q in range(S5_SLAB):
                ustage[q, :, LANES * b:LANES * (b + 1)] = jnp.concatenate(
                    [sl[:, S5_PB * q:S5_PB * (q + 1)] for sl in slabs], axis=1).astype(BF16)

    u = ustage[jj]
    sloc[...] = jnp.dot(u, e_ref[0], preferred_element_type=F32)
    ar = al_ref[0, 0:1, :]
    ai = al_ref[0, 1:2, :]

    row = lax.broadcasted_iota(jnp.int32, (SUBLANES, LANES), 0)

    def body(ti, carry):
        xr, xi = carry
        r0 = pl.multiple_of(ti * SUBLANES, SUBLANES)
        sr_t = sloc[pl.ds(r0, SUBLANES), 0:LANES]
        si_t = sloc[pl.ds(r0, SUBLANES), LANES:2 * LANES]
        pr_t = jnp.zeros((SUBLANES, LANES), F32)
        pi_t = jnp.zeros((SUBLANES, LANES), F32)
        for r in range(SUBLANES):
            pr_t = jnp.where(row == r, xr, pr_t)
            pi_t = jnp.where(row == r, xi, pi_t)
            xr, xi = ar * xr - ai * xi + sr_t[r:r + 1, :], ar * xi + ai * xr + si_t[r:r + 1, :]
        xpr[pl.ds(r0, SUBLANES), :] = pr_t
        xpi[pl.ds(r0, SUBLANES), :] = pi_t
        return xr, xi

    zero = jnp.zeros((1, LANES), F32)
    lax.fori_loop(0, nc // SUBLANES, body, (zero, zero))
    ystage[jj] = (jnp.dot(xpr[...].astype(BF16), fr_ref[0], preferred_element_type=F32)
                  + jnp.dot(xpi[...].astype(BF16), fi_ref[0], preferred_element_type=F32)
                  + jnp.dot(u, m_ref[0], preferred_element_type=F32))

    @pl.when(jj == S5_SLAB - 1)
    def _():
        for t in range(S5_CHUNK):
            o_ref[pl.ds(t, nc, stride=S5_CHUNK), :] = jnp.concatenate(
                [ystage[q, :, S5_PB * t:S5_PB * (t + 1)] for q in range(S5_SLAB)], axis=1)


def _s5(z2, lam_re, lam_im, log_dt, b_re, b_im, c_re, c_im):
    t = z2.shape[0]
    nc = t // S5_CHUNK
    m, e_pair, fr_pair, fi_pair, al = _s5_tables(lam_re, lam_im, log_dt, b_re, b_im, c_re, c_im)
    pair = lambda a, b: pl.BlockSpec((1, a, b), lambda k, jj: (S5_SLAB * k + jj, 0, 0))
    return pl.pallas_call(
        functools.partial(_s5_kernel, nc=nc),
        grid=(WIDTH // LANES, S5_SLAB),
        in_specs=[pl.BlockSpec((t, LANES), lambda k, jj: (0, Z2_US * WIDTH // LANES + k), pipeline_mode=pl.Buffered(1)),
                  pair(S5_ROWS, S5_ROWS), pair(S5_ROWS, 2 * LANES), pair(LANES, S5_ROWS), pair(LANES, S5_ROWS),
                  pair(SUBLANES, LANES)],
        out_specs=pl.BlockSpec((t, LANES), lambda k, jj: (0, k), pipeline_mode=pl.Buffered(1)),
        out_shape=jax.ShapeDtypeStruct((t, WIDTH), F32),
        scratch_shapes=[pltpu.VMEM((S5_SLAB, nc, S5_ROWS), BF16), pltpu.VMEM((S5_SLAB, nc, S5_ROWS), F32),
                        pltpu.VMEM((nc, 2 * LANES), F32), pltpu.VMEM((nc, LANES), F32), pltpu.VMEM((nc, LANES), F32)],
        compiler_params=_cparams(("arbitrary", "arbitrary")),
        name="s5_scan",
    )(z2, m, e_pair, fr_pair, fi_pair, al)


def _layer_norm(x, g, b):
    mu = jnp.mean(x, axis=-1, keepdims=True)
    xc = x - mu
    var = jnp.mean(xc * xc, axis=-1, keepdims=True)
    return xc * lax.rsqrt(var + LN_EPS) * g + b


def _merge_rows(sl, x_ref, g0_ref, g1_ref, g2_ref, ym_ref, ys_ref, us_ref, yh_ref, wm_ref, ws_ref, wh_ref, wglu_ref,
                d_ref, wo_ref, lg_ref, lb_ref, rw_ref, rb_ref, o_ref):
    ys = _gelu_tanh(ys_ref[sl, :] + d_ref[...] * us_ref[sl, :].astype(F32))
    ys = ys * _sigmoid(_dot(ys, wglu_ref[...]))
    merged = (_sigmoid(g0_ref[sl, :].astype(F32)) * _dot(ym_ref[sl, :], wm_ref[...])
              + _sigmoid(g1_ref[sl, :].astype(F32)) * _dot(ys, ws_ref[...])
              + _sigmoid(g2_ref[sl, :].astype(F32)) * _dot(yh_ref[sl, :], wh_ref[...]))
    x1 = _layer_norm(DN_ALPHA * x_ref[sl, :] + _dot(merged, wo_ref[...]), lg_ref[...], lb_ref[...])
    o_ref[sl, 0:D_MODEL] = x1

    xh = x1.astype(BF16)
    xl = (x1 - xh.astype(F32)).astype(BF16)
    logits = (jnp.dot(xh, rw_ref[0], preferred_element_type=F32) + jnp.dot(xl, rw_ref[0], preferred_element_type=F32)
              + jnp.dot(xh, rw_ref[1], preferred_element_type=F32))
    s0 = _sigmoid(logits)
    sb0 = s0 + rb_ref[...]
    shift = lambda a, j: a if j == 0 else pltpu.roll(a, LANES - N_GROUPS * j, 1)
    s = [shift(s0, j) for j in range(EXPERTS_PER_GROUP)]
    sb = [shift(sb0, j) for j in range(EXPERTS_PER_GROUP)]
    hi1, lo1 = jnp.maximum(sb[0], sb[1]), jnp.minimum(sb[0], sb[1])
    hi2, lo2 = jnp.maximum(sb[2], sb[3]), jnp.minimum(sb[2], sb[3])
    top2 = jnp.maximum(hi1, hi2) + jnp.maximum(jnp.minimum(hi1, hi2), jnp.maximum(lo1, lo2))
    lane = lax.broadcasted_iota(jnp.int32, top2.shape, 1)
    top2 = jnp.where(lane < N_GROUPS, top2, -jnp.inf)
    gmax = jnp.max(top2, axis=1, keepdims=True)
    g_idx = jnp.min(jnp.where(top2 == gmax, lane, LANES), axis=1, keepdims=True)
    sel = lane == g_idx
    v = [jnp.sum(jnp.where(sel, sb[j], 0.0), axis=1, keepdims=True) for j in range(EXPERTS_PER_GROUP)]
    sv = [jnp.sum(jnp.where(sel, s[j], 0.0), axis=1, keepdims=True) for j in range(EXPERTS_PER_GROUP)]

    def first_max(vals):
        m = jnp.maximum(jnp.maximum(vals[0], vals[1]), jnp.maximum(vals[2], vals[3]))
        return jnp.where(vals[0] == m, 0, jnp.where(vals[1] == m, 1, jnp.where(vals[2] == m, 2, 3)))

    e1 = first_max(v)
    e2 = first_max([jnp.where(e1 == j, -jnp.inf, v[j]) for j in range(EXPERTS_PER_GROUP)])
    s1 = sum(jnp.where(e1 == j, sv[j], 0.0) for j in range(EXPERTS_PER_GROUP))
    s2 = sum(jnp.where(e2 == j, sv[j], 0.0) for j in range(EXPERTS_PER_GROUP))
    tot = s1 + s2
    meta = jnp.where(lane == 0, g_idx.astype(F32), 0.0)
    for j in range(EXPERTS_PER_GROUP):
        cw = jnp.where(e1 == j, s1 / tot, 0.0) + jnp.where(e2 == j, s2 / tot, 0.0)
        meta = jnp.where(lane == 1 + j, cw, meta)
    o_ref[sl, D_MODEL:D_MODEL + ROUTE_LANES] = meta


MERGE_ROWS = 256


def _merge_kernel(*refs):
    for lo in range(0, refs[0].shape[0], MERGE_ROWS):
        _merge_rows(slice(lo, lo + MERGE_ROWS), *refs)


def _merge(x, z1, z2, ym, ys, yh, wm, ws, wh, wglu, d, wo, lg, lb, router_w, router_bias, tm):
    t = x.shape[0]
    rw = router_w.astype(F32).reshape(D_MODEL, N_GROUPS, EXPERTS_PER_GROUP).transpose(0, 2, 1)
    rw = jnp.pad(rw.reshape(D_MODEL, N_EXPERTS), ((0, 0), (0, LANES - N_EXPERTS)))
    rw_hi = rw.astype(BF16)
    rw = jnp.stack([rw_hi, (rw - rw_hi.astype(F32)).astype(BF16)])
    rb = jnp.pad(router_bias.astype(F32).reshape(N_GROUPS, EXPERTS_PER_GROUP).T.reshape(1, N_EXPERTS),
                 ((0, 0), (0, LANES - N_EXPERTS)))
    full = lambda shape: pl.BlockSpec(shape, lambda i: (0,) * len(shape))
    row = lambda w: pl.BlockSpec((tm, w), lambda i: (i, 0))
    return pl.pallas_call(
        _merge_kernel,
        grid=(t // tm,),
        in_specs=[row(D_MODEL),
                  pl.BlockSpec((tm, D_MODEL), lambda i: (i, 0)),
                  pl.BlockSpec((tm, D_MODEL), lambda i: (i, 1)),
                  pl.BlockSpec((tm, D_MODEL), lambda i: (i, 2)),
                  row(WIDTH), row(WIDTH), pl.BlockSpec((tm, WIDTH), lambda i: (i, Z2_US)), row(WIDTH),
                  full((WIDTH, D_MODEL)), full((WIDTH, D_MODEL)), full((WIDTH, D_MODEL)), full((WIDTH, WIDTH)),
                  full((1, WIDTH)), full((D_MODEL, D_MODEL)), full((1, D_MODEL)), full((1, D_MODEL)),
                  full((2, D_MODEL, LANES)), full((1, LANES))],
        out_specs=pl.BlockSpec((tm, D_MODEL + ROUTE_LANES), lambda i: (i, 0)),
        out_shape=jax.ShapeDtypeStruct((t, D_MODEL + ROUTE_LANES), F32),
        compiler_params=_cparams(("parallel",)),
        name="merge_router",
    )(x, z1, z1, z1, ym, ys, z2, yh, wm.astype(BF16), ws.astype(BF16), wh.astype(BF16), wglu.astype(BF16),
      d.reshape(1, WIDTH).astype(F32), wo.astype(BF16), lg.reshape(1, D_MODEL).astype(F32),
      lb.reshape(1, D_MODEL).astype(F32), rw, rb)


def _start_row_gather(idx_ref, base, src_hbm, dst, sem, rows):
    for r in range(rows):
        pltpu.make_async_copy(src_hbm.at[pl.ds(idx_ref[base + r], 1)], dst.at[pl.ds(r, 1)], sem).start(priority=r % 2)


def _wait_row_gather(src_hbm, dst, sem, rows):
    pltpu.make_async_copy(src_hbm.at[pl.ds(0, rows)], dst, sem).wait()


def _dispatch_kernel(dst_ref, x_ref, init_hbm, o_hbm, sem, *, tm):
    del init_hbm
    base = pl.program_id(0) * tm
    for r in range(tm):
        pltpu.make_async_copy(x_ref.at[pl.ds(r, 1)], o_hbm.at[pl.ds(dst_ref[base + r], 1)], sem).start(priority=r % 2)
    pltpu.make_async_copy(x_ref, o_hbm.at[pl.ds(0, tm)], sem).wait()


def _moe_kernel(tg_ref, x_ref, wg_ref, wu_ref, wd_ref, o_ref, wg_s, wu_s, wd_s, *, tm, ntiles):
    i = pl.program_id(0)

    @pl.when(jnp.logical_or(i == 0, tg_ref[i] != tg_ref[jnp.maximum(i - 1, 0)]))
    def _():
        for e in range(EXPERTS_PER_GROUP):
            wg_s[e] = wg_ref[0, e].astype(BF16)
            wu_s[e] = wu_ref[0, e].astype(BF16)
            wd_s[e] = wd_ref[0, e].astype(BF16)

    used = i < tg_ref[ntiles]

    @pl.when(used)
    def _():
        xb = x_ref[:, 0:D_MODEL].astype(BF16)
        meta = x_ref[:, D_MODEL:D_MODEL + ROUTE_LANES]
        y = jnp.zeros((tm, D_MODEL), F32)
        for e in range(EXPERTS_PER_GROUP):
            hg = jnp.dot(xb, wg_s[e], preferred_element_type=F32)
            hu = jnp.dot(xb, wu_s[e], preferred_element_type=F32)
            hh = _silu(hg) * hu * _lane_col(meta, 1 + e)
            y = y + jnp.dot(hh.astype(BF16), wd_s[e], preferred_element_type=F32)
        o_ref[...] = y

    @pl.when(jnp.logical_not(used))
    def _():
        o_ref[...] = jnp.zeros_like(o_ref)


def _ln2_kernel(dst_ref, x_ref, y_hbm, lg_ref, lb_ref, o_ref, buf, sem, *, tm, ntiles):
    i = pl.program_id(0)
    slot = i % 2

    @pl.when(i == 0)
    def _():
        _start_row_gather(dst_ref, 0, y_hbm, buf.at[0], sem.at[0], tm)

    @pl.when(i + 1 < ntiles)
    def _():
        _start_row_gather(dst_ref, (i + 1) * tm, y_hbm, buf.at[1 - slot], sem.at[1 - slot], tm)

    _wait_row_gather(y_hbm, buf.at[slot], sem.at[slot], tm)
    o_ref[...] = _layer_norm(DN_ALPHA * x_ref[...] + buf[slot], lg_ref[...], lb_ref[...])


def _moe_ln(x1e, wg, wu, wd, layer, lg, lb, tm, tt, tc):
    t = x1e.shape[0]
    ntiles = t // tm + N_GROUPS
    p = ntiles * tm
    key = x1e[:, D_MODEL].astype(jnp.int32)
    onehot = (key[:, None] == jnp.arange(N_GROUPS)[None, :]).astype(jnp.int32)
    csum = jnp.cumsum(onehot, axis=0)
    counts = csum[-1]
    rank = jnp.sum(onehot * csum, axis=1) - 1
    pcount = ((counts + tm - 1) // tm) * tm
    pend = jnp.cumsum(pcount)
    dest = (pend - pcount)[key] + rank
    dest = dest.astype(jnp.int32)
    tile_start = jnp.arange(ntiles, dtype=jnp.int32) * tm
    tile_group = jnp.minimum(jnp.sum((tile_start[:, None] >= pend[None, :]).astype(jnp.int32), axis=1),
                             N_GROUPS - 1)
    tile_group = jnp.concatenate([tile_group, (pend[-1:] // tm).astype(jnp.int32)])
    nt2 = t // tt
    x_sorted = pl.pallas_call(
        functools.partial(_dispatch_kernel, tm=tt),
        grid_spec=pltpu.PrefetchScalarGridSpec(
            num_scalar_prefetch=1,
            grid=(nt2,),
            in_specs=[pl.BlockSpec((tt, D_MODEL + ROUTE_LANES), lambda i, d: (i, 0)), pl.BlockSpec(memory_space=pl.ANY)],
            out_specs=pl.BlockSpec(memory_space=pl.ANY),
            scratch_shapes=[pltpu.SemaphoreType.DMA(())]),
        out_shape=jax.ShapeDtypeStruct((p, D_MODEL + ROUTE_LANES), F32),
        input_output_aliases={2: 0},
        compiler_params=_cparams(("arbitrary",)),
        name="moe_dispatch",
    )(dest, x1e, jnp.zeros((p, D_MODEL + ROUTE_LANES), F32))
    wspec = lambda a, b: pl.BlockSpec((1, EXPERTS_PER_GROUP, a, b), lambda i, tg: (layer, tg[i], 0, 0))
    wscr = lambda a, b: pltpu.VMEM((EXPERTS_PER_GROUP, a, b), BF16)
    y_sorted = pl.pallas_call(
        functools.partial(_moe_kernel, tm=tm, ntiles=ntiles),
        grid_spec=pltpu.PrefetchScalarGridSpec(
            num_scalar_prefetch=1,
            grid=(ntiles,),
            in_specs=[pl.BlockSpec((tm, D_MODEL + ROUTE_LANES), lambda i, tg: (i, 0)), wspec(D_MODEL, D_EXPERT),
                      wspec(D_MODEL, D_EXPERT), wspec(D_EXPERT, D_MODEL)],
            out_specs=pl.BlockSpec((tm, D_MODEL), lambda i, tg: (i, 0)),
            scratch_shapes=[wscr(D_MODEL, D_EXPERT), wscr(D_MODEL, D_EXPERT), wscr(D_EXPERT, D_MODEL)]),
        out_shape=jax.ShapeDtypeStruct((p, D_MODEL), F32),
        compiler_params=_cparams(("arbitrary",)),
        name="moe_experts",
    )(tile_group, x_sorted, wg, wu, wd)
    return pl.pallas_call(
        functools.partial(_ln2_kernel, tm=tc, ntiles=t // tc),
        grid_spec=pltpu.PrefetchScalarGridSpec(
            num_scalar_prefetch=1,
            grid=(t // tc,),
            in_specs=[pl.BlockSpec((tc, D_MODEL), lambda i, d: (i, 0)), pl.BlockSpec(memory_space=pl.ANY),
                      pl.BlockSpec((1, D_MODEL), lambda i, d: (0, 0)), pl.BlockSpec((1, D_MODEL), lambda i, d: (0, 0))],
            out_specs=pl.BlockSpec((tc, D_MODEL), lambda i, d: (i, 0)),
            scratch_shapes=[pltpu.VMEM((2, tc, D_MODEL), F32), pltpu.SemaphoreType.DMA((2,))]),
        out_shape=jax.ShapeDtypeStruct((t, D_MODEL), F32),
        compiler_params=_cparams(("arbitrary",)),
        name="moe_combine_ln",
    )(dest.astype(jnp.int32), x1e, y_sorted, lg.reshape(1, D_MODEL).astype(F32), lb.reshape(1, D_MODEL).astype(F32))


def _layer(x, l, p, lb_l, cfg):
    t = x.shape[0]
    w1, w2, wt = _split_w_in(p['w_in'][l])
    xb = x.astype(BF16)
    z1 = _matmul(xb, w1, Z1_DTYPE, cfg['tm_in'], cfg['tn_in'])
    z2, gt = _inproj_gate(xb, w2, wt, cfg['tm_in'])
    gt3 = gt.reshape(SUBLANES, t // CHUNK, CHUNK).transpose(1, 0, 2)
    ym = _mlstm(z1, z2, gt3, p['mlstm_conv_w'][l], p['mlstm_conv_b'][l], p['mlstm_wq'][l], p['mlstm_wk'][l],
                p['mlstm_wv'][l], p['mlstm_b_i'][l], p['mlstm_b_f'][l], p['mlstm_skip'][l], cfg['tb'])
    ys = _s5(z2, p['s5_lambda_re'][l], p['s5_lambda_im'][l], p['s5_log_dt'][l], p['s5_b_re'][l], p['s5_b_im'][l],
             p['s5_c_re'][l], p['s5_c_im'][l])
    yh = _hgrn(z1, z2, lb_l, p['hgrn_norm_w'][l], cfg['tb'])
    x1e = _merge(x, z1, z2, ym, ys, yh, p['w_branch_mlstm'][l], p['w_branch_s5'][l], p['w_branch_hgrn'][l],
                 p['s5_w_glu'][l], p['s5_d'][l], p['w_out'][l], p['ln1_g'][l], p['ln1_b'][l], p['router_w'],
                 p['router_bias'], cfg['tm_merge'])
    return _moe_ln(x1e, p['exp_w_gate'], p['exp_w_up'], p['exp_w_down'], l, p['ln2_g'][l], p['ln2_b'][l],
                   cfg['tm_moe'], cfg['tt_dispatch'], cfg['tt_combine'])


_CFG = dict(tm_in=1024, tn_in=Z1_COLS // 4, tb=1024, tm_merge=2 * MERGE_ROWS, tm_moe=256, tt_dispatch=1024, tt_combine=512)


def kernel(x, w_in, mlstm_conv_w, mlstm_conv_b, mlstm_wq, mlstm_wk, mlstm_wv, mlstm_b_i, mlstm_b_f, mlstm_skip, s5_lambda_re, s5_lambda_im, s5_log_dt, s5_b_re, s5_b_im, s5_c_re, s5_c_im, s5_d, s5_w_glu, hgrn_lower_bounds, hgrn_norm_w, w_branch_mlstm, w_branch_s5, w_branch_hgrn, w_out, ln1_g, ln1_b, ln2_g, ln2_b, router_w, router_bias, exp_w_gate, exp_w_up, exp_w_down):
    p = dict(w_in=w_in, mlstm_conv_w=mlstm_conv_w, mlstm_conv_b=mlstm_conv_b, mlstm_wq=mlstm_wq, mlstm_wk=mlstm_wk,
             mlstm_wv=mlstm_wv, mlstm_b_i=mlstm_b_i, mlstm_b_f=mlstm_b_f, mlstm_skip=mlstm_skip,
             s5_lambda_re=s5_lambda_re, s5_lambda_im=s5_lambda_im, s5_log_dt=s5_log_dt, s5_b_re=s5_b_re,
             s5_b_im=s5_b_im, s5_c_re=s5_c_re, s5_c_im=s5_c_im, s5_d=s5_d, s5_w_glu=s5_w_glu,
             hgrn_norm_w=hgrn_norm_w, w_branch_mlstm=w_branch_mlstm, w_branch_s5=w_branch_s5,
             w_branch_hgrn=w_branch_hgrn, w_out=w_out, ln1_g=ln1_g, ln1_b=ln1_b, ln2_g=ln2_g, ln2_b=ln2_b,
             router_w=router_w, router_bias=router_bias, exp_w_gate=exp_w_gate, exp_w_up=exp_w_up,
             exp_w_down=exp_w_down)
    lb_cum = jnp.cumsum(jax.nn.softmax(hgrn_lower_bounds.astype(F32), axis=0), axis=0)
    lb_layers = lb_cum - lb_cum[0]
    bsz, seq, d = x.shape
    h = x.reshape(bsz * seq, d)
    for l in range(DEPTH):
        h = _layer(h, l, p, lb_layers[l], _CFG)
    return h.reshape(bsz, seq, d)
```

```python
import functools
import math

import jax
import jax.numpy as jnp
from jax import lax
from jax.experimental import pallas as pl
from jax.experimental.pallas import tpu as pltpu

F32 = jnp.float32
BF16 = jnp.bfloat16
HIGHEST = lax.Precision.HIGHEST

D_MODEL = 1024
DEPTH = 2
HEADS = 4
HEAD_DIM = 128
WIDTH = HEADS * HEAD_DIM
MLSTM_CONV = 4
CHUNK = 64
S5_GROUP = 16
S5_GROUPS = 32
S5_STATE = 64
S5_PAIRS = S5_GROUPS // 2
S5_CHUNK = 32
S5_MAX_REAL = -1e-4
N_EXPERTS = 32
N_GROUPS = 8
EXPERTS_PER_GROUP = 4
D_EXPERT = 256
DN_ALPHA = (2 * DEPTH) ** 0.25
LN_EPS = 1e-5
NORM_EPS = 1e-6

LANES = 128
SUBLANES = 8
ROUTE_LANES = LANES
V7X_VMEM_BYTES = 64 * 1024 * 1024
VMEM_LIMIT = V7X_VMEM_BYTES * 7 // 8

Z1_DTYPE = BF16
Z1_GATE, Z1_XM, Z1_OM, Z1_QH, Z1_IH, Z1_GH = 0, 6, 7, 8, 9, 10
Z1_COLS = 11 * WIDTH
Z2_FH, Z2_US = 0, 1
Z2_COLS = 2 * WIDTH + LANES


def _cparams(sem):
    return pltpu.CompilerParams(dimension_semantics=sem, vmem_limit_bytes=VMEM_LIMIT)


def _sigmoid(x):
    return 0.5 * (1.0 + jnp.tanh(0.5 * x))


def _silu(x):
    return x * _sigmoid(x)


def _log_sigmoid(x):
    return jnp.minimum(x, 0.0) - jnp.log(1.0 + jnp.exp(-jnp.abs(x)))


def _gelu_tanh(x):
    return 0.5 * x * (1.0 + jnp.tanh(math.sqrt(2.0 / math.pi) * (x + 0.044715 * (x * x * x))))


def _cumsum_rows(x):
    n = x.shape[0]
    row = lax.broadcasted_iota(jnp.int32, x.shape, 0)
    s = 1
    while s < n:
        x = x + jnp.where(row >= s, pltpu.roll(x, s, 0), 0.0)
        s *= 2
    return x


def _lane_col(x, idx):
    lane = lax.broadcasted_iota(jnp.int32, x.shape, 1)
    return jnp.sum(jnp.where(lane == idx, x, 0.0), axis=1, keepdims=True)


def _dot(a, b):
    return jnp.dot(a.astype(BF16), b.astype(BF16), preferred_element_type=F32)


def _dot_nt(a, b):
    return lax.dot_general(a.astype(BF16), b.astype(BF16), (((1,), (1,)), ((), ())), preferred_element_type=F32)


def _dot_tn(a, b):
    return lax.dot_general(a.astype(BF16), b.astype(BF16), (((0,), (0,)), ((), ())), preferred_element_type=F32)


def _mm_kernel(x_ref, w_ref, o_ref):
    o_ref[...] = jnp.dot(x_ref[...], w_ref[...], preferred_element_type=F32).astype(o_ref.dtype)


def _matmul(x, w, out_dtype, tm, tn):
    m, k = x.shape
    n = w.shape[1]
    return pl.pallas_call(
        _mm_kernel,
        grid=(m // tm, n // tn),
        in_specs=[pl.BlockSpec((tm, k), lambda i, j: (i, 0)), pl.BlockSpec((k, tn), lambda i, j: (0, j))],
        out_specs=pl.BlockSpec((tm, tn), lambda i, j: (i, j)),
        out_shape=jax.ShapeDtypeStruct((m, n), out_dtype),
        compiler_params=_cparams(("parallel", "parallel")),
        name="in_proj_wide",
    )(x, w)


def _inproj_gate_kernel(x_ref, w_ref, wt_ref, o_ref, gt_ref):
    x = x_ref[...]
    o_ref[...] = jnp.dot(x, w_ref[...], preferred_element_type=F32)
    gt_ref[...] = lax.dot_general(wt_ref[...], x, (((1,), (1,)), ((), ())), preferred_element_type=F32)


def _inproj_gate(x, w2, wt, tm):
    m, k = x.shape
    return pl.pallas_call(
        _inproj_gate_kernel,
        grid=(m // tm,),
        in_specs=[pl.BlockSpec((tm, k), lambda i: (i, 0)),
                  pl.BlockSpec((k, Z2_COLS), lambda i: (0, 0)),
                  pl.BlockSpec((SUBLANES, k), lambda i: (0, 0))],
        out_specs=[pl.BlockSpec((tm, Z2_COLS), lambda i: (i, 0)), pl.BlockSpec((SUBLANES, tm), lambda i: (0, i))],
        out_shape=[jax.ShapeDtypeStruct((m, Z2_COLS), F32), jax.ShapeDtypeStruct((SUBLANES, m), F32)],
        compiler_params=_cparams(("parallel",)),
        name="in_proj_gates",
    )(x, w2, wt)


def _split_w_in(w):
    offs, o = [], 0
    for s in (WIDTH, WIDTH, HEADS, HEADS, WIDTH, WIDTH, WIDTH, WIDTH, WIDTH, 3 * D_MODEL):
        offs.append((o, o + s))
        o += s
    seg = [w[:, a:b] for a, b in offs]
    xm, om, im, fm, us, qh, fh, ih, gh, gate = seg
    w1 = jnp.concatenate([gate, xm, om, qh, ih, gh], axis=1).astype(BF16)
    pad = jnp.zeros((w.shape[0], LANES - 2 * HEADS), w.dtype)
    w2 = jnp.concatenate([fh, us, im, fm, pad], axis=1).astype(BF16)
    wt = jnp.concatenate([im, fm], axis=1).T.astype(BF16)
    return w1, w2, wt


def _mlstm_kernel(xm_ref, om_ref, gc_ref, gr_ref, cw_ref, cb_ref, wq_ref, wk_ref, wv_ref, bcol_ref, brow_ref,
                  skip_ref, triu_ref, o_ref, xpad, q_s, k_s, v_s, xc_s, h_s, gcol_s, grow_s, brow_s, ct_s, n_s, m_s,
                  *, tb):
    ncb = tb // CHUNK

    @pl.when(pl.program_id(0) == 0)
    def _():
        xpad[0:SUBLANES, :] = jnp.zeros((SUBLANES, WIDTH), F32)
        ct_s[...] = jnp.zeros_like(ct_s)
        n_s[...] = jnp.zeros_like(n_s)
        m_s[...] = jnp.zeros_like(m_s)

    xm = xm_ref[...].astype(F32)
    xpad[SUBLANES:SUBLANES + tb, :] = xm
    cw = cw_ref[...]
    conv = cb_ref[...] + cw[3:4, :] * xm
    for d in range(1, MLSTM_CONV):
        conv = conv + cw[3 - d:4 - d, :] * xpad[SUBLANES - d:SUBLANES - d + tb, :]
    xpad[0:SUBLANES, :] = xpad[tb:tb + SUBLANES, :]
    xc = _silu(conv)
    xc_s[...] = xc
    for h in range(HEADS):
        sl = slice(h * HEAD_DIM, (h + 1) * HEAD_DIM)
        xch = xc[:, sl].astype(BF16)
        q_s[:, sl] = jnp.dot(xch, wq_ref[h], preferred_element_type=F32) * (HEAD_DIM ** -0.5)
        k_s[:, sl] = jnp.dot(xch, wk_ref[h], preferred_element_type=F32)
        v_s[:, sl] = jnp.dot(xm[:, sl].astype(BF16), wv_ref[h], preferred_element_type=F32)

    gc = gc_ref[...] + bcol_ref[...]
    lane = lax.broadcasted_iota(jnp.int32, gc.shape, 1)
    gcol_s[...] = jnp.where(lane < HEADS, gc, _log_sigmoid(gc))
    gr = gr_ref[...] + brow_ref[...]
    sub = lax.broadcasted_iota(jnp.int32, gr.shape, 1)
    gr = jnp.where(sub < HEADS, gr, _log_sigmoid(gr)).reshape(ncb * SUBLANES, CHUNK)
    grow_s[...] = gr
    brow_s[...] = jnp.dot(gr, triu_ref[...], precision=HIGHEST, preferred_element_type=F32)

    rowi = lax.broadcasted_iota(jnp.int32, (CHUNK, CHUNK), 0)
    coli = lax.broadcasted_iota(jnp.int32, (CHUNK, CHUNK), 1)
    causal = rowi >= coli

    def chunk_body(c, carry):
        r0 = pl.multiple_of(c * CHUNK, CHUNK)
        g0 = pl.multiple_of(c * SUBLANES, SUBLANES)
        gcol = gcol_s[pl.ds(r0, CHUNK), :]
        bcol = _cumsum_rows(gcol)
        grow = grow_s[pl.ds(g0, SUBLANES), :]
        brow = brow_s[pl.ds(g0, SUBLANES), :]
        for h in range(HEADS):
            sl = slice(h * HEAD_DIM, (h + 1) * HEAD_DIM)
            li_c = _lane_col(gcol, h)
            b_c = _lane_col(bcol, HEADS + h)
            li_r = grow[h:h + 1, :]
            b_r = brow[HEADS + h:HEADS + h + 1, :]
            b_tot = b_c[CHUNK - 1:CHUNK, :]
            a_c = b_tot - b_c + li_c
            a_max = jnp.max(a_c, axis=0, keepdims=True)
            q = q_s[pl.ds(r0, CHUNK), sl]
            k = k_s[pl.ds(r0, CHUNK), sl]
            v = v_s[pl.ds(r0, CHUNK), sl]
            wk = jnp.exp(a_c - a_max) * k
            c_loc_t = _dot_tn(v, wk)
            n_loc = jnp.sum(wk, axis=0, keepdims=True)
            ct_prev = ct_s[h]
            n_prev = n_s[h:h + 1, :]
            m_prev = m_s[h:h + 1, 0:1]
            d_mat = jnp.where(causal, b_c - b_r + li_r, -jnp.inf)
            m_inter = b_c + m_prev
            m_j = jnp.maximum(m_inter, jnp.max(d_mat, axis=1, keepdims=True))
            sc = _dot_nt(q, k) * jnp.exp(d_mat - m_j)
            g_inter = jnp.exp(m_inter - m_j)
            num = g_inter * _dot_nt(q, ct_prev) + _dot(sc, v)
            den = g_inter * jnp.sum(q * n_prev, axis=1, keepdims=True) + jnp.sum(sc, axis=1, keepdims=True)
            h_s[pl.ds(r0, CHUNK), sl] = num / jnp.maximum(jnp.abs(den), jnp.exp(-m_j))
            m_new = jnp.maximum(b_tot + m_prev, a_max)
            g_old = jnp.exp(b_tot + m_prev - m_new)
            g_loc = jnp.exp(a_max - m_new)
            ct_s[h] = g_old * ct_prev + g_loc * c_loc_t
            n_s[h:h + 1, :] = g_old * n_prev + g_loc * n_loc
            m_s[h:h + 1, :] = jnp.broadcast_to(m_new, (1, LANES))
        return carry

    lax.fori_loop(0, ncb, chunk_body, 0, unroll=4)
    o_ref[...] = (_sigmoid(om_ref[...].astype(F32)) * h_s[...] + skip_ref[...] * xc_s[...]).astype(o_ref.dtype)


def _mlstm(z1, z2, gt3, conv_w, conv_b, wq, wk, wv, b_i, b_f, skip, tb):
    t = z1.shape[0]
    ncb = tb // CHUNK
    bias = jnp.concatenate([b_i, b_f]).astype(F32)
    bcol = jnp.zeros((1, LANES), F32).at[0, :2 * HEADS].set(bias)
    brow = bias.reshape(2 * HEADS, 1)
    triu = jnp.triu(jnp.ones((CHUNK, CHUNK), F32))
    full = lambda shape: pl.BlockSpec(shape, lambda i: (0,) * len(shape))
    return pl.pallas_call(
        functools.partial(_mlstm_kernel, tb=tb),
        grid=(t // tb,),
        in_specs=[pl.BlockSpec((tb, WIDTH), lambda i: (i, Z1_XM)),
                  pl.BlockSpec((tb, WIDTH), lambda i: (i, Z1_OM)),
                  pl.BlockSpec((tb, LANES), lambda i: (i, 2 * WIDTH // LANES)),
                  pl.BlockSpec((ncb, SUBLANES, CHUNK), lambda i: (i, 0, 0)),
                  full((MLSTM_CONV, WIDTH)), full((1, WIDTH)),
                  full((HEADS, HEAD_DIM, HEAD_DIM)), full((HEADS, HEAD_DIM, HEAD_DIM)),
                  full((HEADS, HEAD_DIM, HEAD_DIM)),
                  full((1, LANES)), full((2 * HEADS, 1)), full((1, WIDTH)), full((CHUNK, CHUNK))],
        out_specs=pl.BlockSpec((tb, WIDTH), lambda i: (i, 0)),
        out_shape=jax.ShapeDtypeStruct((t, WIDTH), F32),
        scratch_shapes=[pltpu.VMEM((tb + SUBLANES, WIDTH), F32)]
        + [pltpu.VMEM((tb, WIDTH), F32) for _ in range(5)]
        + [pltpu.VMEM((tb, LANES), F32),
           pltpu.VMEM((ncb * SUBLANES, CHUNK), F32), pltpu.VMEM((ncb * SUBLANES, CHUNK), F32),
           pltpu.VMEM((HEADS, HEAD_DIM, HEAD_DIM), F32), pltpu.VMEM((SUBLANES, LANES), F32),
           pltpu.VMEM((SUBLANES, LANES), F32)],
        compiler_params=_cparams(("arbitrary",)),
        name="mlstm",
    )(z1, z1, z2, gt3, conv_w.astype(F32), conv_b.reshape(1, WIDTH).astype(F32), wq.astype(BF16), wk.astype(BF16),
      wv.astype(BF16), bcol, brow, skip.reshape(1, WIDTH).astype(F32), triu)


def _hgrn_kernel(q_ref, f_ref, i_ref, g_ref, lb_ref, nw_ref, tri_ref, o_ref, st_s, *, tb):
    ncb = tb // CHUNK

    @pl.when(pl.program_id(0) == 0)
    def _():
        st_s[...] = jnp.zeros_like(st_s)

    lb = lb_ref[...]
    nw = nw_ref[...]
    tri = tri_ref[...]
    rowi = lax.broadcasted_iota(jnp.int32, (CHUNK, WIDTH), 0)
    sr = lax.broadcasted_iota(jnp.int32, (CHUNK, CHUNK), 0)
    sc = lax.broadcasted_iota(jnp.int32, (CHUNK, CHUNK), 1)
    halves = [1 << p for p in range(CHUNK.bit_length() - 1)]
    upper = {m: (rowi & m) != 0 for m in halves}
    same_blk = {m: (sr // (2 * m)) == (sc // (2 * m)) for m in halves}

    def chunk_body(c, carry):
        r0 = pl.multiple_of(c * CHUNK, CHUNK)
        f = lb + (1.0 - lb) * _sigmoid(f_ref[pl.ds(r0, CHUNK), :].astype(F32))
        k = 1.0 - f
        q = _silu(q_ref[pl.ds(r0, CHUNK), :].astype(F32))
        v = i_ref[pl.ds(r0, CHUNK), :].astype(F32)
        lf = jnp.log(f)
        hi = lf.astype(BF16)
        r1 = lf - hi.astype(F32)
        mid = r1.astype(BF16)
        lo = (r1 - mid.astype(F32)).astype(BF16)
        b = (jnp.dot(tri, hi, preferred_element_type=F32) + jnp.dot(tri, mid, preferred_element_type=F32)
             + jnp.dot(tri, lo, preferred_element_type=F32))
        qs, ks = {}, {}
        for m in halves:
            if m == 1:
                t = jnp.where(upper[m], q * f, k)
            else:
                nblk = CHUNK // (2 * m)
                r = jnp.concatenate([jnp.broadcast_to(b[2 * m * j + m - 1:2 * m * j + m, :], (2 * m, WIDTH))
                                     for j in range(nblk)], axis=0)
                t = jnp.where(upper[m], q, k) * jnp.exp(-jnp.abs(b - r))
            qs[m] = jnp.where(upper[m], t, 0.0).astype(BF16)
            ks[m] = jnp.where(upper[m], 0.0, t).astype(BF16)
        b_last = b[CHUNK - 1:CHUNK, :]
        qe = q * jnp.exp(b)
        kd = k * jnp.exp(b_last - b)
        e_last = jnp.exp(b_last)
        outs = []
        for h in range(HEADS):
            sl = slice(h * HEAD_DIM, (h + 1) * HEAD_DIM)
            st = st_s[h]
            a = jnp.where(sr == sc, _dot_nt(q[:, sl], k[:, sl]), 0.0)
            for m in halves:
                a = a + jnp.where(same_blk[m], _dot_nt(qs[m][:, sl], ks[m][:, sl]), 0.0)
            oh = _dot_nt(qe[:, sl], st) + _dot(a, v[:, sl])
            st_s[h] = e_last[:, sl] * st + _dot_tn(v[:, sl], kd[:, sl])
            ms = jnp.sum(oh * oh, axis=1, keepdims=True) * (1.0 / HEAD_DIM)
            outs.append(oh * lax.rsqrt(ms + NORM_EPS))
        on = jnp.concatenate(outs, axis=1)
        g = g_ref[pl.ds(r0, CHUNK), :].astype(F32)
        o_ref[pl.ds(r0, CHUNK), :] = (on * nw * _silu(g)).astype(o_ref.dtype)
        return carry

    lax.fori_loop(0, ncb, chunk_body, 0, unroll=8)


def _hgrn(z1, z2, lb, norm_w, tb):
    t = z1.shape[0]
    tril = jnp.tril(jnp.ones((CHUNK, CHUNK), F32)).astype(BF16)
    full = lambda shape: pl.BlockSpec(shape, lambda i: (0,) * len(shape))
    return pl.pallas_call(
        functools.partial(_hgrn_kernel, tb=tb),
        grid=(t // tb,),
        in_specs=[pl.BlockSpec((tb, WIDTH), lambda i: (i, Z1_QH)),
                  pl.BlockSpec((tb, WIDTH), lambda i: (i, 0)),
                  pl.BlockSpec((tb, WIDTH), lambda i: (i, Z1_IH)),
                  pl.BlockSpec((tb, WIDTH), lambda i: (i, Z1_GH)),
                  full((1, WIDTH)), full((1, WIDTH)), full((CHUNK, CHUNK))],
        out_specs=pl.BlockSpec((tb, WIDTH), lambda i: (i, 0)),
        out_shape=jax.ShapeDtypeStruct((t, WIDTH), F32),
        scratch_shapes=[pltpu.VMEM((HEADS, HEAD_DIM, HEAD_DIM), F32)],
        compiler_params=_cparams(("arbitrary",)),
        name="hgrn2",
    )(z1, z2, z1, z1, lb.reshape(1, WIDTH).astype(F32), norm_w.reshape(1, WIDTH).astype(F32), tril)


S5_PB = 2 * S5_GROUP
S5_ROWS = S5_CHUNK * S5_PB
S5_SLAB = LANES // S5_PB


def _s5_toeplitz_kernel(k_ref, m_ref):
    krow = k_ref[0]
    lane = lax.broadcasted_iota(jnp.int32, krow.shape, 1)
    for s in range(S5_CHUNK):
        blk = krow if s == 0 else jnp.where(lane >= s * S5_PB, pltpu.roll(krow, s * S5_PB, 1), 0.0)
        m_ref[0, s * S5_PB:(s + 1) * S5_PB, :] = blk.astype(m_ref.dtype)


def _s5_toeplitz(krow):
    return pl.pallas_call(
        _s5_toeplitz_kernel,
        grid=(S5_PAIRS,),
        in_specs=[pl.BlockSpec((1, S5_PB, S5_ROWS), lambda j: (j, 0, 0))],
        out_specs=pl.BlockSpec((1, S5_ROWS, S5_ROWS), lambda j: (j, 0, 0)),
        out_shape=jax.ShapeDtypeStruct((S5_PAIRS, S5_ROWS, S5_ROWS), BF16),
        compiler_params=_cparams(("parallel",)),
        name="s5_toeplitz",
    )(krow)


def _s5_tables(lam_re, lam_im, log_dt, b_re, b_im, c_re, c_im):
    ln = S5_CHUNK
    lr = jnp.minimum(lam_re.astype(F32), S5_MAX_REAL)
    li = lam_im.astype(F32)
    dt = jnp.exp(log_dt.astype(F32))[:, None]
    mag = jnp.exp(lr * dt)
    ab_re = mag * jnp.cos(li * dt)
    ab_im = mag * jnp.sin(li * dt)
    nr = ab_re - 1.0
    den = lr * lr + li * li
    cr = (nr * lr + ab_im * li) / den
    ci = (ab_im * lr - nr * li) / den
    bb_re = cr[..., None] * b_re - ci[..., None] * b_im
    bb_im = cr[..., None] * b_im + ci[..., None] * b_re
    tau = jnp.arange(ln + 1, dtype=F32)[:, None, None]
    pm = jnp.exp(lr * dt * tau)
    pr = pm * jnp.cos(li * dt * tau)
    pi = pm * jnp.sin(li * dt * tau)
    def block_diag(a):
        a0, a1 = a[0::2], a[1::2]
        z = jnp.zeros_like(a0)
        return jnp.concatenate([jnp.concatenate([a0, z], axis=2), jnp.concatenate([z, a1], axis=2)], axis=1)

    def pair_pow(p):
        return p.reshape(p.shape[0], S5_PAIRS, 2 * S5_STATE).transpose(1, 0, 2)

    bbr = block_diag(bb_re.transpose(0, 2, 1))
    bbi = block_diag(bb_im.transpose(0, 2, 1))
    ccr = block_diag(c_re.transpose(0, 2, 1))
    cci = block_diag(c_im.transpose(0, 2, 1))
    col = jnp.arange(S5_ROWS)
    rep_t = (col[None, :] // S5_PB == jnp.arange(ln)[:, None]).astype(F32)
    rep_c = (col[None, :] % S5_PB == jnp.arange(S5_PB)[:, None]).astype(F32)
    lanes_c = lambda a: jnp.einsum('jrq,qc->jrc', a, rep_c, precision=HIGHEST)
    lanes_t = lambda p: jnp.einsum('jtr,tc->jrc', pair_pow(p), rep_t, precision=HIGHEST)
    ccr_l, cci_l = lanes_c(ccr), lanes_c(cci)

    def out_tables(p_r, p_i):
        pr_l, pi_l = lanes_t(p_r), lanes_t(p_i)
        return ccr_l * pr_l - cci_l * pi_l, -(ccr_l * pi_l + cci_l * pr_l)

    fr_pair, fi_pair = out_tables(pr[1:], pi[1:])
    fk_re, fk_im = out_tables(pr[:ln], pi[:ln])
    krow = jnp.einsum('jrk,jkc->jrc', jnp.concatenate([bbr, bbi], axis=2), jnp.concatenate([fk_re, fk_im], axis=1),
                      precision=HIGHEST)
    m = _s5_toeplitz(krow)
    rows_of = lambda a: jnp.broadcast_to(a[:, :, None, :], (S5_PAIRS, a.shape[1], S5_PB, a.shape[2])).reshape(
        S5_PAIRS, a.shape[1] * S5_PB, a.shape[2])
    pe_r = rows_of(pair_pow(pr[ln - 1 - jnp.arange(ln)]))
    pe_i = rows_of(pair_pow(pi[ln - 1 - jnp.arange(ln)]))
    tile_s = lambda a: jnp.broadcast_to(a[:, None], (S5_PAIRS, ln) + a.shape[1:]).reshape(S5_PAIRS, S5_ROWS, a.shape[2])
    bbr_s, bbi_s = tile_s(bbr), tile_s(bbi)
    e_pair = jnp.concatenate([pe_r * bbr_s - pe_i * bbi_s, pe_r * bbi_s + pe_i * bbr_s], axis=2)
    al = jnp.zeros((S5_PAIRS, SUBLANES, LANES), F32)
    al = al.at[:, 0, :].set(pr[ln].reshape(S5_PAIRS, 2 * S5_STATE))
    al = al.at[:, 1, :].set(pi[ln].reshape(S5_PAIRS, 2 * S5_STATE))
    return m.astype(BF16), e_pair.astype(BF16), fr_pair.astype(BF16), fi_pair.astype(BF16), al


def _s5_kernel(u_ref, m_ref, e_ref, fr_ref, fi_ref, al_ref, o_ref, ustage, ystage, sloc, xpr, xpi, *, nc):
    jj = pl.program_id(1)

    @pl.when(jj == 0)
    def _():
        for b in range(S5_ROWS // LANES):
            slabs = [u_ref[pl.ds(S5_SLAB * b + a, nc, stride=S5_CHUNK), :] for a in range(S5_SLAB)]
            for q in range(S5_SLAB):
                ustage[q, :, LANES * b:LANES * (b + 1)] = jnp.concatenate(
                    [sl[:, S5_PB * q:S5_PB * (q + 1)] for sl in slabs], axis=1).astype(BF16)

    u = ustage[jj]
    sloc[...] = jnp.dot(u, e_ref[0], preferred_element_type=F32)
    ar = al_ref[0, 0:1, :]
    ai = al_ref[0, 1:2, :]

    row = lax.broadcasted_iota(jnp.int32, (SUBLANES, LANES), 0)

    def body(ti, carry):
        xr, xi = carry
        r0 = pl.multiple_of(ti * SUBLANES, SUBLANES)
        sr_t = sloc[pl.ds(r0, SUBLANES), 0:LANES]
        si_t = sloc[pl.ds(r0, SUBLANES), LANES:2 * LANES]
        pr_t = jnp.zeros((SUBLANES, LANES), F32)
        pi_t = jnp.zeros((SUBLANES, LANES), F32)
        for r in range(SUBLANES):
            pr_t = jnp.where(row == r, xr, pr_t)
            pi_t = jnp.where(row == r, xi, pi_t)
            xr, xi = ar * xr - ai * xi + sr_t[r:r + 1, :], ar * xi + ai * xr + si_t[r:r + 1, :]
        xpr[pl.ds(r0, SUBLANES), :] = pr_t
        xpi[pl.ds(r0, SUBLANES), :] = pi_t
        return xr, xi

    zero = jnp.zeros((1, LANES), F32)
    lax.fori_loop(0, nc // SUBLANES, body, (zero, zero))
    ystage[jj] = (jnp.dot(xpr[...].astype(BF16), fr_ref[0], preferred_element_type=F32)
                  + jnp.dot(xpi[...].astype(BF16), fi_ref[0], preferred_element_type=F32)
                  + jnp.dot(u, m_ref[0], preferred_element_type=F32))

    @pl.when(jj == S5_SLAB - 1)
    def _():
        for t in range(S5_CHUNK):
            o_ref[pl.ds(t, nc, stride=S5_CHUNK), :] = jnp.concatenate(
                [ystage[q, :, S5_PB * t:S5_PB * (t + 1)] for q in range(S5_SLAB)], axis=1)


def _s5(z2, lam_re, lam_im, log_dt, b_re, b_im, c_re, c_im):
    t = z2.shape[0]
    nc = t // S5_CHUNK
    m, e_pair, fr_pair, fi_pair, al = _s5_tables(lam_re, lam_im, log_dt, b_re, b_im, c_re, c_im)
    pair = lambda a, b: pl.BlockSpec((1, a, b), lambda k, jj: (S5_SLAB * k + jj, 0, 0))
    return pl.pallas_call(
        functools.partial(_s5_kernel, nc=nc),
        grid=(WIDTH // LANES, S5_SLAB),
        in_specs=[pl.BlockSpec((t, LANES), lambda k, jj: (0, Z2_US * WIDTH // LANES + k), pipeline_mode=pl.Buffered(1)),
                  pair(S5_ROWS, S5_ROWS), pair(S5_ROWS, 2 * LANES), pair(LANES, S5_ROWS), pair(LANES, S5_ROWS),
                  pair(SUBLANES, LANES)],
        out_specs=pl.BlockSpec((t, LANES), lambda k, jj: (0, k), pipeline_mode=pl.Buffered(1)),
        out_shape=jax.ShapeDtypeStruct((t, WIDTH), F32),
        scratch_shapes=[pltpu.VMEM((S5_SLAB, nc, S5_ROWS), BF16), pltpu.VMEM((S5_SLAB, nc, S5_ROWS), F32),
                        pltpu.VMEM((nc, 2 * LANES), F32), pltpu.VMEM((nc, LANES), F32), pltpu.VMEM((nc, LANES), F32)],
        compiler_params=_cparams(("arbitrary", "arbitrary")),
        name="s5_scan",
    )(z2, m, e_pair, fr_pair, fi_pair, al)


def _layer_norm(x, g, b):
    mu = jnp.mean(x, axis=-1, keepdims=True)
    xc = x - mu
    var = jnp.mean(xc * xc, axis=-1, keepdims=True)
    return xc * lax.rsqrt(var + LN_EPS) * g + b


def _merge_rows(sl, x_ref, g0_ref, g1_ref, g2_ref, ym_ref, ys_ref, us_ref, yh_ref, wm_ref, ws_ref, wh_ref, wglu_ref,
                d_ref, wo_ref, lg_ref, lb_ref, rw_ref, rb_ref, o_ref):
    ys = _gelu_tanh(ys_ref[sl, :] + d_ref[...] * us_ref[sl, :].astype(F32))
    ys = ys * _sigmoid(_dot(ys, wglu_ref[...]))
    merged = (_sigmoid(g0_ref[sl, :].astype(F32)) * _dot(ym_ref[sl, :], wm_ref[...])
              + _sigmoid(g1_ref[sl, :].astype(F32)) * _dot(ys, ws_ref[...])
              + _sigmoid(g2_ref[sl, :].astype(F32)) * _dot(yh_ref[sl, :], wh_ref[...]))
    x1 = _layer_norm(DN_ALPHA * x_ref[sl, :] + _dot(merged, wo_ref[...]), lg_ref[...], lb_ref[...])
    o_ref[sl, 0:D_MODEL] = x1

    xh = x1.astype(BF16)
    xl = (x1 - xh.astype(F32)).astype(BF16)
    logits = (jnp.dot(xh, rw_ref[0], preferred_element_type=F32) + jnp.dot(xl, rw_ref[0], preferred_element_type=F32)
              + jnp.dot(xh, rw_ref[1], preferred_element_type=F32))
    s0 = _sigmoid(logits)
    sb0 = s0 + rb_ref[...]
    shift = lambda a, j: a if j == 0 else pltpu.roll(a, LANES - N_GROUPS * j, 1)
    s = [shift(s0, j) for j in range(EXPERTS_PER_GROUP)]
    sb = [shift(sb0, j) for j in range(EXPERTS_PER_GROUP)]
    hi1, lo1 = jnp.maximum(sb[0], sb[1]), jnp.minimum(sb[0], sb[1])
    hi2, lo2 = jnp.maximum(sb[2], sb[3]), jnp.minimum(sb[2], sb[3])
    top2 = jnp.maximum(hi1, hi2) + jnp.maximum(jnp.minimum(hi1, hi2), jnp.maximum(lo1, lo2))
    lane = lax.broadcasted_iota(jnp.int32, top2.shape, 1)
    top2 = jnp.where(lane < N_GROUPS, top2, -jnp.inf)
    gmax = jnp.max(top2, axis=1, keepdims=True)
    g_idx = jnp.min(jnp.where(top2 == gmax, lane, LANES), axis=1, keepdims=True)
    sel = lane == g_idx
    v = [jnp.sum(jnp.where(sel, sb[j], 0.0), axis=1, keepdims=True) for j in range(EXPERTS_PER_GROUP)]
    sv = [jnp.sum(jnp.where(sel, s[j], 0.0), axis=1, keepdims=True) for j in range(EXPERTS_PER_GROUP)]

    def first_max(vals):
        m = jnp.maximum(jnp.maximum(vals[0], vals[1]), jnp.maximum(vals[2], vals[3]))
        return jnp.where(vals[0] == m, 0, jnp.where(vals[1] == m, 1, jnp.where(vals[2] == m, 2, 3)))

    e1 = first_max(v)
    e2 = first_max([jnp.where(e1 == j, -jnp.inf, v[j]) for j in range(EXPERTS_PER_GROUP)])
    s1 = sum(jnp.where(e1 == j, sv[j], 0.0) for j in range(EXPERTS_PER_GROUP))
    s2 = sum(jnp.where(e2 == j, sv[j], 0.0) for j in range(EXPERTS_PER_GROUP))
    tot = s1 + s2
    meta = jnp.where(lane == 0, g_idx.astype(F32), 0.0)
    for j in range(EXPERTS_PER_GROUP):
        cw = jnp.where(e1 == j, s1 / tot, 0.0) + jnp.where(e2 == j, s2 / tot, 0.0)
        meta = jnp.where(lane == 1 + j, cw, meta)
    o_ref[sl, D_MODEL:D_MODEL + ROUTE_LANES] = meta


MERGE_ROWS = 256


def _merge_kernel(*refs):
    for lo in range(0, refs[0].shape[0], MERGE_ROWS):
        _merge_rows(slice(lo, lo + MERGE_ROWS), *refs)


def _merge(x, z1, z2, ym, ys, yh, wm, ws, wh, wglu, d, wo, lg, lb, router_w, router_bias, tm):
    t = x.shape[0]
    rw = router_w.astype(F32).reshape(D_MODEL, N_GROUPS, EXPERTS_PER_GROUP).transpose(0, 2, 1)
    rw = jnp.pad(rw.reshape(D_MODEL, N_EXPERTS), ((0, 0), (0, LANES - N_EXPERTS)))
    rw_hi = rw.astype(BF16)
    rw = jnp.stack([rw_hi, (rw - rw_hi.astype(F32)).astype(BF16)])
    rb = jnp.pad(router_bias.astype(F32).reshape(N_GROUPS, EXPERTS_PER_GROUP).T.reshape(1, N_EXPERTS),
                 ((0, 0), (0, LANES - N_EXPERTS)))
    full = lambda shape: pl.BlockSpec(shape, lambda i: (0,) * len(shape))
    row = lambda w: pl.BlockSpec((tm, w), lambda i: (i, 0))
    return pl.pallas_call(
        _merge_kernel,
        grid=(t // tm,),
        in_specs=[row(D_MODEL),
                  pl.BlockSpec((tm, D_MODEL), lambda i: (i, 0)),
                  pl.BlockSpec((tm, D_MODEL), lambda i: (i, 1)),
                  pl.BlockSpec((tm, D_MODEL), lambda i: (i, 2)),
                  row(WIDTH), row(WIDTH), pl.BlockSpec((tm, WIDTH), lambda i: (i, Z2_US)), row(WIDTH),
                  full((WIDTH, D_MODEL)), full((WIDTH, D_MODEL)), full((WIDTH, D_MODEL)), full((WIDTH, WIDTH)),
                  full((1, WIDTH)), full((D_MODEL, D_MODEL)), full((1, D_MODEL)), full((1, D_MODEL)),
                  full((2, D_MODEL, LANES)), full((1, LANES))],
        out_specs=pl.BlockSpec((tm, D_MODEL + ROUTE_LANES), lambda i: (i, 0)),
        out_shape=jax.ShapeDtypeStruct((t, D_MODEL + ROUTE_LANES), F32),
        compiler_params=_cparams(("parallel",)),
        name="merge_router",
    )(x, z1, z1, z1, ym, ys, z2, yh, wm.astype(BF16), ws.astype(BF16), wh.astype(BF16), wglu.astype(BF16),
      d.reshape(1, WIDTH).astype(F32), wo.astype(BF16), lg.reshape(1, D_MODEL).astype(F32),
      lb.reshape(1, D_MODEL).astype(F32), rw, rb)


def _start_row_gather(idx_ref, base, src_hbm, dst, sem, rows):
    for r in range(rows):
        pltpu.make_async_copy(src_hbm.at[pl.ds(idx_ref[base + r], 1)], dst.at[pl.ds(r, 1)], sem).start(priority=r % 2)


def _wait_row_gather(src_hbm, dst, sem, rows):
    pltpu.make_async_copy(src_hbm.at[pl.ds(0, rows)], dst, sem).wait()


def _dispatch_kernel(dst_ref, x_ref, init_hbm, o_hbm, sem, *, tm):
    del init_hbm
    base = pl.program_id(0) * tm
    for r in range(tm):
        pltpu.make_async_copy(x_ref.at[pl.ds(r, 1)], o_hbm.at[pl.ds(dst_ref[base + r], 1)], sem).start(priority=r % 2)
    pltpu.make_async_copy(x_ref, o_hbm.at[pl.ds(0, tm)], sem).wait()


def _moe_kernel(tg_ref, x_ref, wg_ref, wu_ref, wd_ref, o_ref, wg_s, wu_s, wd_s, *, tm, ntiles):
    i = pl.program_id(0)

    @pl.when(jnp.logical_or(i == 0, tg_ref[i] != tg_ref[jnp.maximum(i - 1, 0)]))
    def _():
        for e in range(EXPERTS_PER_GROUP):
            wg_s[e] = wg_ref[0, e].astype(BF16)
            wu_s[e] = wu_ref[0, e].astype(BF16)
            wd_s[e] = wd_ref[0, e].astype(BF16)

    used = i < tg_ref[ntiles]

    @pl.when(used)
    def _():
        xb = x_ref[:, 0:D_MODEL].astype(BF16)
        meta = x_ref[:, D_MODEL:D_MODEL + ROUTE_LANES]
        y = jnp.zeros((tm, D_MODEL), F32)
        for e in range(EXPERTS_PER_GROUP):
            hg = jnp.dot(xb, wg_s[e], preferred_element_type=F32)
            hu = jnp.dot(xb, wu_s[e], preferred_element_type=F32)
            hh = _silu(hg) * hu * _lane_col(meta, 1 + e)
            y = y + jnp.dot(hh.astype(BF16), wd_s[e], preferred_element_type=F32)
        o_ref[...] = y

    @pl.when(jnp.logical_not(used))
    def _():
        o_ref[...] = jnp.zeros_like(o_ref)


def _ln2_kernel(dst_ref, x_ref, y_hbm, lg_ref, lb_ref, o_ref, buf, sem, *, tm, ntiles):
    i = pl.program_id(0)
    slot = i % 2

    @pl.when(i == 0)
    def _():
        _start_row_gather(dst_ref, 0, y_hbm, buf.at[0], sem.at[0], tm)

    @pl.when(i + 1 < ntiles)
    def _():
        _start_row_gather(dst_ref, (i + 1) * tm, y_hbm, buf.at[1 - slot], sem.at[1 - slot], tm)

    _wait_row_gather(y_hbm, buf.at[slot], sem.at[slot], tm)
    o_ref[...] = _layer_norm(DN_ALPHA * x_ref[...] + buf[slot], lg_ref[...], lb_ref[...])


def _moe_ln(x1e, wg, wu, wd, layer, lg, lb, tm, tt, tc):
    t = x1e.shape[0]
    ntiles = t // tm + N_GROUPS
    p = ntiles * tm
    key = x1e[:, D_MODEL].astype(jnp.int32)
    onehot = (key[:, None] == jnp.arange(N_GROUPS)[None, :]).astype(jnp.int32)
    csum = jnp.cumsum(onehot, axis=0)
    counts = csum[-1]
    rank = jnp.sum(onehot * csum, axis=1) - 1
    pcount = ((counts + tm - 1) // tm) * tm
    pend = jnp.cumsum(pcount)
    dest = (pend - pcount)[key] + rank
    dest = dest.astype(jnp.int32)
    tile_start = jnp.arange(ntiles, dtype=jnp.int32) * tm
    tile_group = jnp.minimum(jnp.sum((tile_start[:, None] >= pend[None, :]).astype(jnp.int32), axis=1),
                             N_GROUPS - 1)
    tile_group = jnp.concatenate([tile_group, (pend[-1:] // tm).astype(jnp.int32)])
    nt2 = t // tt
    x_sorted = pl.pallas_call(
        functools.partial(_dispatch_kernel, tm=tt),
        grid_spec=pltpu.PrefetchScalarGridSpec(
            num_scalar_prefetch=1,
            grid=(nt2,),
            in_specs=[pl.BlockSpec((tt, D_MODEL + ROUTE_LANES), lambda i, d: (i, 0)), pl.BlockSpec(memory_space=pl.ANY)],
            out_specs=pl.BlockSpec(memory_space=pl.ANY),
            scratch_shapes=[pltpu.SemaphoreType.DMA(())]),
        out_shape=jax.ShapeDtypeStruct((p, D_MODEL + ROUTE_LANES), F32),
        input_output_aliases={2: 0},
        compiler_params=_cparams(("arbitrary",)),
        name="moe_dispatch",
    )(dest, x1e, jnp.zeros((p, D_MODEL + ROUTE_LANES), F32))
    wspec = lambda a, b: pl.BlockSpec((1, EXPERTS_PER_GROUP, a, b), lambda i, tg: (layer, tg[i], 0, 0))
    wscr = lambda a, b: pltpu.VMEM((EXPERTS_PER_GROUP, a, b), BF16)
    y_sorted = pl.pallas_call(
        functools.partial(_moe_kernel, tm=tm, ntiles=ntiles),
        grid_spec=pltpu.PrefetchScalarGridSpec(
            num_scalar_prefetch=1,
            grid=(ntiles,),
            in_specs=[pl.BlockSpec((tm, D_MODEL + ROUTE_LANES), lambda i, tg: (i, 0)), wspec(D_MODEL, D_EXPERT),
                      wspec(D_MODEL, D_EXPERT), wspec(D_EXPERT, D_MODEL)],
            out_specs=pl.BlockSpec((tm, D_MODEL), lambda i, tg: (i, 0)),
            scratch_shapes=[wscr(D_MODEL, D_EXPERT), wscr(D_MODEL, D_EXPERT), wscr(D_EXPERT, D_MODEL)]),
        out_shape=jax.ShapeDtypeStruct((p, D_MODEL), F32),
        compiler_params=_cparams(("arbitrary",)),
        name="moe_experts",
    )(tile_group, x_sorted, wg, wu, wd)
    return pl.pallas_call(
        functools.partial(_ln2_kernel, tm=tc, ntiles=t // tc),
        grid_spec=pltpu.PrefetchScalarGridSpec(
            num_scalar_prefetch=1,
            grid=(t // tc,),
            in_specs=[pl.BlockSpec((tc, D_MODEL), lambda i, d: (i, 0)), pl.BlockSpec(memory_space=pl.ANY),
                      pl.BlockSpec((1, D_MODEL), lambda i, d: (0, 0)), pl.BlockSpec((1, D_MODEL), lambda i, d: (0, 0))],
            out_specs=pl.BlockSpec((tc, D_MODEL), lambda i, d: (i, 0)),
            scratch_shapes=[pltpu.VMEM((2, tc, D_MODEL), F32), pltpu.SemaphoreType.DMA((2,))]),
        out_shape=jax.ShapeDtypeStruct((t, D_MODEL), F32),
        compiler_params=_cparams(("arbitrary",)),
        name="moe_combine_ln",
    )(dest.astype(jnp.int32), x1e, y_sorted, lg.reshape(1, D_MODEL).astype(F32), lb.reshape(1, D_MODEL).astype(F32))


def _layer(x, l, p, lb_l, cfg):
    t = x.shape[0]
    w1, w2, wt = _split_w_in(p['w_in'][l])
    xb = x.astype(BF16)
    z1 = _matmul(xb, w1, Z1_DTYPE, cfg['tm_in'], cfg['tn_in'])
    z2, gt = _inproj_gate(xb, w2, wt, cfg['tm_in'])
    gt3 = gt.reshape(SUBLANES, t // CHUNK, CHUNK).transpose(1, 0, 2)
    ym = _mlstm(z1, z2, gt3, p['mlstm_conv_w'][l], p['mlstm_conv_b'][l], p['mlstm_wq'][l], p['mlstm_wk'][l],
                p['mlstm_wv'][l], p['mlstm_b_i'][l], p['mlstm_b_f'][l], p['mlstm_skip'][l], cfg['tb'])
    ys = _s5(z2, p['s5_lambda_re'][l], p['s5_lambda_im'][l], p['s5_log_dt'][l], p['s5_b_re'][l], p['s5_b_im'][l],
             p['s5_c_re'][l], p['s5_c_im'][l])
    yh = _hgrn(z1, z2, lb_l, p['hgrn_norm_w'][l], cfg['tb'])
    x1e = _merge(x, z1, z2, ym, ys, yh, p['w_branch_mlstm'][l], p['w_branch_s5'][l], p['w_branch_hgrn'][l],
                 p['s5_w_glu'][l], p['s5_d'][l], p['w_out'][l], p['ln1_g'][l], p['ln1_b'][l], p['router_w'],
                 p['router_bias'], cfg['tm_merge'])
    return _moe_ln(x1e, p['exp_w_gate'], p['exp_w_up'], p['exp_w_down'], l, p['ln2_g'][l], p['ln2_b'][l],
                   cfg['tm_moe'], cfg['tt_dispatch'], cfg['tt_combine'])


_CFG = dict(tm_in=1024, tn_in=Z1_COLS // 4, tb=1024, tm_merge=2 * MERGE_ROWS, tm_moe=256, tt_dispatch=1024, tt_combine=512)


def kernel(x, w_in, mlstm_conv_w, mlstm_conv_b, mlstm_wq, mlstm_wk, mlstm_wv, mlstm_b_i, mlstm_b_f, mlstm_skip, s5_lambda_re, s5_lambda_im, s5_log_dt, s5_b_re, s5_b_im, s5_c_re, s5_c_im, s5_d, s5_w_glu, hgrn_lower_bounds, hgrn_norm_w, w_branch_mlstm, w_branch_s5, w_branch_hgrn, w_out, ln1_g, ln1_b, ln2_g, ln2_b, router_w, router_bias, exp_w_gate, exp_w_up, exp_w_down):
    p = dict(w_in=w_in, mlstm_conv_w=mlstm_conv_w, mlstm_conv_b=mlstm_conv_b, mlstm_wq=mlstm_wq, mlstm_wk=mlstm_wk,
             mlstm_wv=mlstm_wv, mlstm_b_i=mlstm_b_i, mlstm_b_f=mlstm_b_f, mlstm_skip=mlstm_skip,
             s5_lambda_re=s5_lambda_re, s5_lambda_im=s5_lambda_im, s5_log_dt=s5_log_dt, s5_b_re=s5_b_re,
             s5_b_im=s5_b_im, s5_c_re=s5_c_re, s5_c_im=s5_c_im, s5_d=s5_d, s5_w_glu=s5_w_glu,
             hgrn_norm_w=hgrn_norm_w, w_branch_mlstm=w_branch_mlstm, w_branch_s5=w_branch_s5,
             w_branch_hgrn=w_branch_hgrn, w_out=w_out, ln1_g=ln1_g, ln1_b=ln1_b, ln2_g=ln2_g, ln2_b=ln2_b,
             router_w=router_w, router_bias=router_bias, exp_w_gate=exp_w_gate, exp_w_up=exp_w_up,
             exp_w_down=exp_w_down)
    lb_cum = jnp.cumsum(jax.nn.softmax(hgrn_lower_bounds.astype(F32), axis=0), axis=0)
    lb_layers = lb_cum - lb_cum[0]
    bsz, seq, d = x.shape
    h = x.reshape(bsz * seq, d)
    for l in range(DEPTH):
        h = _layer(h, l, p, lb_layers[l], _CFG)
    return h.reshape(bsz, seq, d)
```
